```python
import jax, jax.numpy as jnp
from jax import lax
import numpy as np

D_MODEL = 1024
BATCH = 8
SEQ = 16384
DEPTH = 2

D_MIX = D_MODEL
W_A = D_MIX // 4
W_B = D_MIX // 4
W_C = D_MIX // 4
W_D = D_MIX // 4
SHORT_CONV = 3
POOL_WINDOWS = (2, 4, 8, 16)
N_POOL_GROUPS = len(POOL_WINDOWS)
POOL_GROUP_DIM = W_B // N_POOL_GROUPS
CONF_WIDTH = 31
SGU_CHUNK = 128
N_HEADS_D = 4
SGU_HEAD_DIM = W_D // N_HEADS_D
IN_SIZES = (W_A, W_A, W_A, W_A,
            W_B, W_B,
            W_C, W_C, W_C,
            W_D, W_D, W_D)
D_IN = sum(IN_SIZES)
EPS = 1e-6

kernel_name = "hybrid_parallel_conv_pool_conformer_sgu"


def rms_norm(x, g):
    xf = x.astype(jnp.float32)
    y = xf * lax.rsqrt(jnp.mean(xf * xf, axis=-1, keepdims=True) + EPS)
    return (y * g.astype(jnp.float32)).astype(x.dtype)


def layer_norm(x, g, b):
    xf = x.astype(jnp.float32)
    mu = jnp.mean(xf, axis=-1, keepdims=True)
    var = jnp.mean(jnp.square(xf - mu), axis=-1, keepdims=True)
    y = (xf - mu) * lax.rsqrt(var + EPS)
    return (y * g.astype(jnp.float32) + b.astype(jnp.float32)).astype(x.dtype)


def causal_dwconv(x, w):
    k = w.shape[0]
    xp = jnp.pad(x, ((0, 0), (k - 1, 0), (0, 0)))
    return lax.conv_general_dilated(
        xp, w[:, None, :].astype(x.dtype), window_strides=(1,), padding='VALID',
        dimension_numbers=('NWC', 'WIO', 'NWC'), feature_group_count=x.shape[-1])


def short_conv_mixer(b_gate, c_gate, xt, w_conv):
    return b_gate * causal_dwconv(c_gate * xt, w_conv)


def multiscale_pool_mixer(p, w_pool, pool_scale):
    bsz, t_len, _ = p.shape
    pf = p.astype(jnp.float32)
    t_idx = jnp.arange(t_len)
    outs = []
    for grp, w in zip(jnp.split(pf, N_POOL_GROUPS, axis=-1), POOL_WINDOWS):
        cs = jnp.pad(jnp.cumsum(grp, axis=1), ((0, 0), (w, 0), (0, 0)))
        wsum = cs[:, w:] - cs[:, :t_len]
        count = jnp.minimum(t_idx + 1, w).astype(jnp.float32)[None, :, None]
        outs.append(wsum / count - grp)
    pooled = jnp.stack(outs, axis=2).astype(p.dtype)
    y = jnp.einsum('btgc,gcd->btgd', pooled, w_pool).reshape(bsz, t_len, W_B)
    return y * pool_scale


def conformer_conv_mixer(a, gl, w_dw, b_dw, ln_g, ln_b, w_pw2, b_pw2):
    h = a * jax.nn.sigmoid(gl)
    h = causal_dwconv(h, w_dw) + b_dw
    h = jax.nn.silu(layer_norm(h, ln_g, ln_b))
    return h @ w_pw2 + b_pw2


def chunked_sgu_mixer(u, v, ln_g, ln_b, w_s, b_s):
    u = jax.nn.gelu(u)
    v = layer_norm(jax.nn.gelu(v), ln_g, ln_b)
    bsz, t_len, _ = v.shape
    n_chunks = t_len // SGU_CHUNK
    vc = v.reshape(bsz, n_chunks, SGU_CHUNK, N_HEADS_D, SGU_HEAD_DIM)
    mask = jnp.tril(jnp.ones((SGU_CHUNK, SGU_CHUNK), dtype=bool))
    ws = jnp.where(mask[None], w_s, 0)
    mixed = jnp.einsum('hst,bnthc->bnshc', ws, vc) + b_s.T[None, None, :, :, None]
    return u * mixed.reshape(bsz, t_len, W_D)


def _fwd_setup_inputs(seed: int = 0) -> dict:
    key = jax.random.key(seed)
    ks = jax.random.split(key, 24)
    f32 = jnp.float32
    nrm = lambda k, shape, s: jax.random.normal(k, shape, f32) * s
    return {
        "x": nrm(ks[0], (BATCH, SEQ, D_MODEL), 1.0),
        "c": nrm(ks[1], (BATCH, D_MODEL), 1.0),
        "norm_g": 1.0 + nrm(ks[2], (DEPTH, D_MODEL), 0.02),
        "w_ada": nrm(ks[3], (DEPTH, D_MODEL, 3 * D_MODEL), 0.5 * D_MODEL ** -0.5),
        "b_ada": nrm(ks[4], (DEPTH, 3 * D_MODEL), 0.01),
        "w_in": nrm(ks[5], (DEPTH, D_MODEL, D_IN), D_MODEL ** -0.5),
        "w_conv_a": nrm(ks[6], (DEPTH, SHORT_CONV, W_A), SHORT_CONV ** -0.5),
        "w_pool": nrm(ks[7], (DEPTH, N_POOL_GROUPS, POOL_GROUP_DIM, POOL_GROUP_DIM), POOL_GROUP_DIM ** -0.5),
        "pool_scale": 1.0 + nrm(ks[8], (DEPTH, W_B), 0.02),
        "w_dw_c": nrm(ks[9], (DEPTH, CONF_WIDTH, W_C), CONF_WIDTH ** -0.5),
        "b_dw_c": nrm(ks[10], (DEPTH, W_C), 0.01),
        "ln_g_c": 1.0 + nrm(ks[11], (DEPTH, W_C), 0.02),
        "ln_b_c": nrm(ks[12], (DEPTH, W_C), 0.01),
        "w_pw2_c": nrm(ks[13], (DEPTH, W_C, W_C), W_C ** -0.5),
        "b_pw2_c": nrm(ks[14], (DEPTH, W_C), 0.01),
        "ln_g_d": 1.0 + nrm(ks[15], (DEPTH, W_D), 0.02),
        "ln_b_d": nrm(ks[16], (DEPTH, W_D), 0.01),
        "w_s_d": nrm(ks[17], (DEPTH, N_HEADS_D, SGU_CHUNK, SGU_CHUNK), 0.5 * SGU_CHUNK ** -0.5),
        "b_s_d": 1.0 + nrm(ks[18], (DEPTH, N_HEADS_D, SGU_CHUNK), 0.02),
        "w_out": nrm(ks[19], (DEPTH, D_MIX, D_MODEL), D_MIX ** -0.5),
        "final_g": 1.0 + nrm(ks[20], (D_MODEL,), 0.02),
    }


def _fwd_reference(x, c, norm_g, w_ada, b_ada, w_in, w_conv_a, w_pool, pool_scale,
              w_dw_c, b_dw_c, ln_g_c, ln_b_c, w_pw2_c, b_pw2_c,
              ln_g_d, ln_b_d, w_s_d, b_s_d, w_out, final_g):
    split_points = list(np.cumsum(IN_SIZES)[:-1])
    c_act = jax.nn.silu(c)
    for l in range(DEPTH):
        mod = c_act @ w_ada[l] + b_ada[l]
        shift, scale, gate = jnp.split(mod, 3, axis=-1)
        h = rms_norm(x, norm_g[l]) * (1.0 + scale[:, None, :]) + shift[:, None, :]
        z = h @ w_in[l]
        (a_b, a_c, a_x, a_g, b_p, b_g, c_a, c_gl, c_g,
         d_u, d_v, d_g) = jnp.split(z, split_points, axis=-1)
        y_a = short_conv_mixer(a_b, a_c, a_x, w_conv_a[l]) * jax.nn.silu(a_g)
        y_b = multiscale_pool_mixer(b_p, w_pool[l], pool_scale[l]) * jax.nn.silu(b_g)
        y_c = conformer_conv_mixer(c_a, c_gl, w_dw_c[l], b_dw_c[l], ln_g_c[l], ln_b_c[l],
                                   w_pw2_c[l], b_pw2_c[l]) * jax.nn.silu(c_g)
        y_d = chunked_sgu_mixer(d_u, d_v, ln_g_d[l], ln_b_d[l], w_s_d[l], b_s_d[l]) * jax.nn.silu(d_g)
        y = jnp.concatenate([y_a, y_b, y_c, y_d], axis=-1) @ w_out[l]
        x = x + gate[:, None, :] * y
    return rms_norm(x, final_g)


import jax as _jax
import jax.numpy as _jnp

TWIN_FORMAT = 'train_step'
FWD_PARAMS = ['x', 'c', 'norm_g', 'w_ada', 'b_ada', 'w_in', 'w_conv_a', 'w_pool', 'pool_scale', 'w_dw_c', 'b_dw_c', 'ln_g_c', 'ln_b_c', 'w_pw2_c', 'b_pw2_c', 'ln_g_d', 'ln_b_d', 'w_s_d', 'b_s_d', 'w_out', 'final_g']
TWIN_WEIGHTS = ['norm_g', 'w_ada', 'b_ada', 'w_in', 'w_conv_a', 'w_pool', 'pool_scale', 'w_dw_c', 'b_dw_c', 'ln_g_c', 'ln_b_c', 'w_pw2_c', 'b_pw2_c', 'ln_g_d', 'ln_b_d', 'w_s_d', 'b_s_d', 'w_out', 'final_g']
TWIN_DIFF_INPUT = 'x'
TWIN_INPUTS = ['x', 'c', 'norm_g', 'w_ada', 'b_ada', 'w_in', 'w_conv_a', 'w_pool', 'pool_scale', 'w_dw_c', 'b_dw_c', 'ln_g_c', 'ln_b_c', 'w_pw2_c', 'b_pw2_c', 'ln_g_d', 'ln_b_d', 'w_s_d', 'b_s_d', 'w_out', 'final_g', 'loss_target', 'm_norm_g', 'm_w_ada', 'm_b_ada', 'm_w_in', 'm_w_conv_a', 'm_w_pool', 'm_pool_scale', 'm_w_dw_c', 'm_b_dw_c', 'm_ln_g_c', 'm_ln_b_c', 'm_w_pw2_c', 'm_b_pw2_c', 'm_ln_g_d', 'm_ln_b_d', 'm_w_s_d', 'm_b_s_d', 'm_w_out', 'm_final_g', 'v_norm_g', 'v_w_ada', 'v_b_ada', 'v_w_in', 'v_w_conv_a', 'v_w_pool', 'v_pool_scale', 'v_w_dw_c', 'v_b_dw_c', 'v_ln_g_c', 'v_ln_b_c', 'v_w_pw2_c', 'v_b_pw2_c', 'v_ln_g_d', 'v_ln_b_d', 'v_w_s_d', 'v_b_s_d', 'v_w_out', 'v_final_g']
TWIN_OUTPUTS = ['loss', 'grad_x', 'grad_norm_g', 'grad_w_ada', 'grad_b_ada', 'grad_w_in', 'grad_w_conv_a', 'grad_w_pool', 'grad_pool_scale', 'grad_w_dw_c', 'grad_b_dw_c', 'grad_ln_g_c', 'grad_ln_b_c', 'grad_w_pw2_c', 'grad_b_pw2_c', 'grad_ln_g_d', 'grad_ln_b_d', 'grad_w_s_d', 'grad_b_s_d', 'grad_w_out', 'grad_final_g', 'delta_norm_g', 'delta_w_ada', 'delta_b_ada', 'delta_w_in', 'delta_w_conv_a', 'delta_w_pool', 'delta_pool_scale', 'delta_w_dw_c', 'delta_b_dw_c', 'delta_ln_g_c', 'delta_ln_b_c', 'delta_w_pw2_c', 'delta_b_pw2_c', 'delta_ln_g_d', 'delta_ln_b_d', 'delta_w_s_d', 'delta_b_s_d', 'delta_w_out', 'delta_final_g', 'new_m_norm_g', 'new_m_w_ada', 'new_m_b_ada', 'new_m_w_in', 'new_m_w_conv_a', 'new_m_w_pool', 'new_m_pool_scale', 'new_m_w_dw_c', 'new_m_b_dw_c', 'new_m_ln_g_c', 'new_m_ln_b_c', 'new_m_w_pw2_c', 'new_m_b_pw2_c', 'new_m_ln_g_d', 'new_m_ln_b_d', 'new_m_w_s_d', 'new_m_b_s_d', 'new_m_w_out', 'new_m_final_g', 'new_v_norm_g', 'new_v_w_ada', 'new_v_b_ada', 'new_v_w_in', 'new_v_w_conv_a', 'new_v_w_pool', 'new_v_pool_scale', 'new_v_w_dw_c', 'new_v_b_dw_c', 'new_v_ln_g_c', 'new_v_ln_b_c', 'new_v_w_pw2_c', 'new_v_b_pw2_c', 'new_v_ln_g_d', 'new_v_ln_b_d', 'new_v_w_s_d', 'new_v_b_s_d', 'new_v_w_out', 'new_v_final_g']
TWIN_LEAF_KINDS = {'loss': 'loss', 'grad_x': 'grad_x', 'grad_norm_g': 'grad_w', 'grad_w_ada': 'grad_w', 'grad_b_ada': 'grad_w', 'grad_w_in': 'grad_w', 'grad_w_conv_a': 'grad_w', 'grad_w_pool': 'grad_w', 'grad_pool_scale': 'grad_w', 'grad_w_dw_c': 'grad_w', 'grad_b_dw_c': 'grad_w', 'grad_ln_g_c': 'grad_w', 'grad_ln_b_c': 'grad_w', 'grad_w_pw2_c': 'grad_w', 'grad_b_pw2_c': 'grad_w', 'grad_ln_g_d': 'grad_w', 'grad_ln_b_d': 'grad_w', 'grad_w_s_d': 'grad_w', 'grad_b_s_d': 'grad_w', 'grad_w_out': 'grad_w', 'grad_final_g': 'grad_w', 'delta_norm_g': 'delta_w', 'delta_w_ada': 'delta_w', 'delta_b_ada': 'delta_w', 'delta_w_in': 'delta_w', 'delta_w_conv_a': 'delta_w', 'delta_w_pool': 'delta_w', 'delta_pool_scale': 'delta_w', 'delta_w_dw_c': 'delta_w', 'delta_b_dw_c': 'delta_w', 'delta_ln_g_c': 'delta_w', 'delta_ln_b_c': 'delta_w', 'delta_w_pw2_c': 'delta_w', 'delta_b_pw2_c': 'delta_w', 'delta_ln_g_d': 'delta_w', 'delta_ln_b_d': 'delta_w', 'delta_w_s_d': 'delta_w', 'delta_b_s_d': 'delta_w', 'delta_w_out': 'delta_w', 'delta_final_g': 'delta_w', 'new_m_norm_g': 'new_m', 'new_m_w_ada': 'new_m', 'new_m_b_ada': 'new_m', 'new_m_w_in': 'new_m', 'new_m_w_conv_a': 'new_m', 'new_m_w_pool': 'new_m', 'new_m_pool_scale': 'new_m', 'new_m_w_dw_c': 'new_m', 'new_m_b_dw_c': 'new_m', 'new_m_ln_g_c': 'new_m', 'new_m_ln_b_c': 'new_m', 'new_m_w_pw2_c': 'new_m', 'new_m_b_pw2_c': 'new_m', 'new_m_ln_g_d': 'new_m', 'new_m_ln_b_d': 'new_m', 'new_m_w_s_d': 'new_m', 'new_m_b_s_d': 'new_m', 'new_m_w_out': 'new_m', 'new_m_final_g': 'new_m', 'new_v_norm_g': 'new_v', 'new_v_w_ada': 'new_v', 'new_v_b_ada': 'new_v', 'new_v_w_in': 'new_v', 'new_v_w_conv_a': 'new_v', 'new_v_w_pool': 'new_v', 'new_v_pool_scale': 'new_v', 'new_v_w_dw_c': 'new_v', 'new_v_b_dw_c': 'new_v', 'new_v_ln_g_c': 'new_v', 'new_v_ln_b_c': 'new_v', 'new_v_w_pw2_c': 'new_v', 'new_v_b_pw2_c': 'new_v', 'new_v_ln_g_d': 'new_v', 'new_v_ln_b_d': 'new_v', 'new_v_w_s_d': 'new_v', 'new_v_b_s_d': 'new_v', 'new_v_w_out': 'new_v', 'new_v_final_g': 'new_v'}


def _forward(args):
    return _fwd_reference(*[args[k] for k in FWD_PARAMS])


def _output_shape():
    def fwd():
        inp = _fwd_setup_inputs(0)
        return _fwd_reference(*[inp[k] for k in FWD_PARAMS])
    out = _jax.eval_shape(fwd)
    return out.shape, out.dtype

N_MICROBATCH = 1
ADAM_LR = 0.001
ADAM_B1 = 0.9
ADAM_B2 = 0.999
ADAM_EPS = 1e-08
ADAM_WD = 0.01
ADAM_STEP = 10
PER_EXAMPLE_BATCH_AXIS = {'x': 0, 'c': 0, 'loss_target': 0}
SHARED_INPUTS = []
_WEIGHT_DTYPES = {'norm_g': _jnp.float32, 'w_ada': _jnp.float32, 'b_ada': _jnp.float32, 'w_in': _jnp.float32, 'w_conv_a': _jnp.float32, 'w_pool': _jnp.float32, 'pool_scale': _jnp.float32, 'w_dw_c': _jnp.float32, 'b_dw_c': _jnp.float32, 'ln_g_c': _jnp.float32, 'ln_b_c': _jnp.float32, 'w_pw2_c': _jnp.float32, 'b_pw2_c': _jnp.float32, 'ln_g_d': _jnp.float32, 'ln_b_d': _jnp.float32, 'w_s_d': _jnp.float32, 'b_s_d': _jnp.float32, 'w_out': _jnp.float32, 'final_g': _jnp.float32}
MOMENT_SCALE = {'norm_g': 1.154620e-01, 'w_ada': 9.596599e-02, 'b_ada': 1.667386e-01, 'w_in': 6.678487e-02, 'w_conv_a': 9.069626e-02, 'w_pool': 6.575192e-02, 'pool_scale': 7.175191e-02, 'w_dw_c': 4.474803e-02, 'b_dw_c': 8.406754e-02, 'ln_g_c': 5.239712e-02, 'ln_b_c': 4.760108e-02, 'w_pw2_c': 4.316269e-02, 'b_pw2_c': 7.463881e-02, 'ln_g_d': 1.812462e-02, 'ln_b_d': 2.009148e-02, 'w_s_d': 2.660725e-02, 'b_s_d': 3.859810e-02, 'w_out': 6.557005e-02, 'final_g': 1.280552e+02}


def _to_microbatches(a, axis):
    t = _jnp.moveaxis(a, axis, 0)
    t = t.reshape((N_MICROBATCH, t.shape[0] // N_MICROBATCH) + t.shape[1:])
    return _jnp.moveaxis(t, 1, axis + 1)


def setup_inputs(seed: int = 0) -> dict:
    inp = _fwd_setup_inputs(seed)
    key = _jax.random.fold_in(_jax.random.key(seed), 7919)
    shape, _ = _output_shape()
    out = dict(inp)
    out["loss_target"] = _jax.random.normal(_jax.random.fold_in(key, 0), shape, _jnp.float32)
    for i, name in enumerate(TWIN_WEIGHTS):
        w = inp[name].astype(_jnp.float32)
        if MOMENT_SCALE is None:
            s = _jnp.sqrt(_jnp.mean(_jnp.square(w)) + 1e-30)
        else:
            s = MOMENT_SCALE[name]
        km, kv = _jax.random.split(_jax.random.fold_in(key, i + 1))
        out[name] = w
        out["m_" + name] = s * _jax.random.normal(km, w.shape, _jnp.float32)
        out["v_" + name] = (s * s) * _jax.random.uniform(kv, w.shape, _jnp.float32, 0.5, 1.5)
    if N_MICROBATCH > 1:
        for name, axis in PER_EXAMPLE_BATCH_AXIS.items():
            out[name] = _to_microbatches(out[name], axis)
    return {'x': out['x'], 'c': out['c'], 'norm_g': out['norm_g'], 'w_ada': out['w_ada'], 'b_ada': out['b_ada'], 'w_in': out['w_in'], 'w_conv_a': out['w_conv_a'], 'w_pool': out['w_pool'], 'pool_scale': out['pool_scale'], 'w_dw_c': out['w_dw_c'], 'b_dw_c': out['b_dw_c'], 'ln_g_c': out['ln_g_c'], 'ln_b_c': out['ln_b_c'], 'w_pw2_c': out['w_pw2_c'], 'b_pw2_c': out['b_pw2_c'], 'ln_g_d': out['ln_g_d'], 'ln_b_d': out['ln_b_d'], 'w_s_d': out['w_s_d'], 'b_s_d': out['b_s_d'], 'w_out': out['w_out'], 'final_g': out['final_g'], 'loss_target': out['loss_target'], 'm_norm_g': out['m_norm_g'], 'm_w_ada': out['m_w_ada'], 'm_b_ada': out['m_b_ada'], 'm_w_in': out['m_w_in'], 'm_w_conv_a': out['m_w_conv_a'], 'm_w_pool': out['m_w_pool'], 'm_pool_scale': out['m_pool_scale'], 'm_w_dw_c': out['m_w_dw_c'], 'm_b_dw_c': out['m_b_dw_c'], 'm_ln_g_c': out['m_ln_g_c'], 'm_ln_b_c': out['m_ln_b_c'], 'm_w_pw2_c': out['m_w_pw2_c'], 'm_b_pw2_c': out['m_b_pw2_c'], 'm_ln_g_d': out['m_ln_g_d'], 'm_ln_b_d': out['m_ln_b_d'], 'm_w_s_d': out['m_w_s_d'], 'm_b_s_d': out['m_b_s_d'], 'm_w_out': out['m_w_out'], 'm_final_g': out['m_final_g'], 'v_norm_g': out['v_norm_g'], 'v_w_ada': out['v_w_ada'], 'v_b_ada': out['v_b_ada'], 'v_w_in': out['v_w_in'], 'v_w_conv_a': out['v_w_conv_a'], 'v_w_pool': out['v_w_pool'], 'v_pool_scale': out['v_pool_scale'], 'v_w_dw_c': out['v_w_dw_c'], 'v_b_dw_c': out['v_b_dw_c'], 'v_ln_g_c': out['v_ln_g_c'], 'v_ln_b_c': out['v_ln_b_c'], 'v_w_pw2_c': out['v_w_pw2_c'], 'v_b_pw2_c': out['v_b_pw2_c'], 'v_ln_g_d': out['v_ln_g_d'], 'v_ln_b_d': out['v_ln_b_d'], 'v_w_s_d': out['v_w_s_d'], 'v_b_s_d': out['v_b_s_d'], 'v_w_out': out['v_w_out'], 'v_final_g': out['v_final_g']}


def _loss(weights, diff, rest, loss_target):
    with _jax.named_scope("forward"):
        args = {**rest, TWIN_DIFF_INPUT: diff, **{k: w.astype(_WEIGHT_DTYPES[k]) for k, w in weights.items()}}
        y = _forward(args)
    with _jax.named_scope("loss_head"):
        err = _jnp.square(y.astype(_jnp.float32) - loss_target)
        return 0.5 * _jnp.sum(_jnp.mean(err, axis=-1)) if err.ndim else 0.5 * err


def _adamw(w, g, m, v):
    m = ADAM_B1 * m + (1.0 - ADAM_B1) * g
    v = ADAM_B2 * v + (1.0 - ADAM_B2) * _jnp.square(g)
    m_hat = m / (1.0 - ADAM_B1 ** ADAM_STEP)
    v_hat = v / (1.0 - ADAM_B2 ** ADAM_STEP)
    delta = -ADAM_LR * (m_hat / (_jnp.sqrt(v_hat) + ADAM_EPS) + ADAM_WD * w)
    return delta, m, v


def reference(x, c, norm_g, w_ada, b_ada, w_in, w_conv_a, w_pool, pool_scale, w_dw_c, b_dw_c, ln_g_c, ln_b_c, w_pw2_c, b_pw2_c, ln_g_d, ln_b_d, w_s_d, b_s_d, w_out, final_g, loss_target, m_norm_g, m_w_ada, m_b_ada, m_w_in, m_w_conv_a, m_w_pool, m_pool_scale, m_w_dw_c, m_b_dw_c, m_ln_g_c, m_ln_b_c, m_w_pw2_c, m_b_pw2_c, m_ln_g_d, m_ln_b_d, m_w_s_d, m_b_s_d, m_w_out, m_final_g, v_norm_g, v_w_ada, v_b_ada, v_w_in, v_w_conv_a, v_w_pool, v_pool_scale, v_w_dw_c, v_b_dw_c, v_ln_g_c, v_ln_b_c, v_w_pw2_c, v_b_pw2_c, v_ln_g_d, v_ln_b_d, v_w_s_d, v_b_s_d, v_w_out, v_final_g):
    given = dict(x=x, c=c, norm_g=norm_g, w_ada=w_ada, b_ada=b_ada, w_in=w_in, w_conv_a=w_conv_a, w_pool=w_pool, pool_scale=pool_scale, w_dw_c=w_dw_c, b_dw_c=b_dw_c, ln_g_c=ln_g_c, ln_b_c=ln_b_c, w_pw2_c=w_pw2_c, b_pw2_c=b_pw2_c, ln_g_d=ln_g_d, ln_b_d=ln_b_d, w_s_d=w_s_d, b_s_d=b_s_d, w_out=w_out, final_g=final_g, loss_target=loss_target, m_norm_g=m_norm_g, m_w_ada=m_w_ada, m_b_ada=m_b_ada, m_w_in=m_w_in, m_w_conv_a=m_w_conv_a, m_w_pool=m_w_pool, m_pool_scale=m_pool_scale, m_w_dw_c=m_w_dw_c, m_b_dw_c=m_b_dw_c, m_ln_g_c=m_ln_g_c, m_ln_b_c=m_ln_b_c, m_w_pw2_c=m_w_pw2_c, m_b_pw2_c=m_b_pw2_c, m_ln_g_d=m_ln_g_d, m_ln_b_d=m_ln_b_d, m_w_s_d=m_w_s_d, m_b_s_d=m_b_s_d, m_w_out=m_w_out, m_final_g=m_final_g, v_norm_g=v_norm_g, v_w_ada=v_w_ada, v_b_ada=v_b_ada, v_w_in=v_w_in, v_w_conv_a=v_w_conv_a, v_w_pool=v_w_pool, v_pool_scale=v_pool_scale, v_w_dw_c=v_w_dw_c, v_b_dw_c=v_b_dw_c, v_ln_g_c=v_ln_g_c, v_ln_b_c=v_ln_b_c, v_w_pw2_c=v_w_pw2_c, v_b_pw2_c=v_b_pw2_c, v_ln_g_d=v_ln_g_d, v_ln_b_d=v_ln_b_d, v_w_s_d=v_w_s_d, v_b_s_d=v_b_s_d, v_w_out=v_w_out, v_final_g=v_final_g)
    weights = {n: given[n] for n in TWIN_WEIGHTS}
    shared = {n: given[n] for n in SHARED_INPUTS}
    per_example = {n: given[n] for n in ['x', 'c']}
    grad_fn = _jax.value_and_grad(_loss, argnums=(0, 1))

    def one_microbatch(ex, loss_target):
        ex = dict(ex)
        diff = ex.pop(TWIN_DIFF_INPUT)
        return grad_fn(weights, diff, {**shared, **ex}, loss_target)

    if N_MICROBATCH == 1:
        loss, (grad_w, grad_x) = one_microbatch(per_example, given["loss_target"])
    else:
        def body(carry, xs):
            loss_sum, grad_sum = carry
            l_k, (gw_k, gx_k) = one_microbatch(xs[0], xs[1])
            with _jax.named_scope("update"):
                return (loss_sum + l_k, _jax.tree.map(_jnp.add, grad_sum, gw_k)), gx_k

        init = (_jnp.zeros((), _jnp.float32), _jax.tree.map(_jnp.zeros_like, weights))
        (loss, grad_w), grad_x = _jax.lax.scan(body, init, (per_example, given["loss_target"]))
    with _jax.named_scope("update"):
        delta_w, new_m, new_v = {}, {}, {}
        for n in TWIN_WEIGHTS:
            delta_w[n], new_m[n], new_v[n] = _adamw(weights[n], grad_w[n], given["m_" + n], given["v_" + n])
    return (loss, grad_x, *[grad_w[n] for n in TWIN_WEIGHTS], *[delta_w[n] for n in TWIN_WEIGHTS],
            *[new_m[n] for n in TWIN_WEIGHTS], *[new_v[n] for n in TWIN_WEIGHTS])
```

```python
import functools
import math

import jax
import jax.numpy as jnp
import numpy as np
from jax import lax
from jax.experimental import pallas as pl
from jax.experimental.pallas import tpu as pltpu

F32 = jnp.float32
BF16 = jnp.bfloat16
MESH = pl.DeviceIdType.MESH

D = 1024
DIN = 3072
GW = 256
PIECE = 768
N_CHIP = 4
N_DEV = 8
HALO = 32
CONV_A = 3
CONV_C = 31
CHUNK = 128
CONV_ROWS = 64
EPS = 1e-6
VMEM_LIMIT = 56 * 1024 * 1024

ADAM_LR, ADAM_B1, ADAM_B2, ADAM_EPS, ADAM_WD, ADAM_STEP = 0.001, 0.9, 0.999, 1e-08, 0.01, 10
GELU_K = math.sqrt(2.0 / math.pi)
GELU_C = 0.044715

WEIGHTS = ['norm_g', 'w_ada', 'b_ada', 'w_in', 'w_conv_a', 'w_pool', 'pool_scale', 'w_dw_c', 'b_dw_c', 'ln_g_c',
           'ln_b_c', 'w_pw2_c', 'b_pw2_c', 'ln_g_d', 'ln_b_d', 'w_s_d', 'b_s_d', 'w_out', 'final_g']
BIG = ('w_ada', 'w_in', 'w_out')


def _tiles(t):
    if t % 512 == 0 and t >= 2048:
        return 512, 256
    return 128, 128


def _params(sem, limit=VMEM_LIMIT):
    return pltpu.CompilerParams(dimension_semantics=sem, vmem_limit_bytes=limit)


def _sig(x):
    return jax.nn.sigmoid(x)


def _gelu(x):
    th = jnp.tanh(GELU_K * (x + GELU_C * x * x * x))
    return 0.5 * x * (1.0 + th), th


def _gelu_grad(x, th):
    return 0.5 * (1.0 + th) + 0.5 * x * (1.0 - th * th) * GELU_K * (1.0 + 3.0 * GELU_C * x * x)


def _dot(a, b):
    return jnp.dot(a, b, preferred_element_type=F32)


def _dot_nt(a, b):
    return lax.dot_general(a, b, (((1,), (1,)), ((), ())), preferred_element_type=F32)


def _dot_tn(a, b):
    return lax.dot_general(a, b, (((0,), (0,)), ((), ())), preferred_element_type=F32)


def _rowsum(x):
    return jnp.sum(x, axis=0, keepdims=True)


def _allgather8(v, name, reduce=False):
    m_per, n = v.shape

    def body(x_ref, out_ref, *rest):
        if reduce:
            sum_ref, send_sems, recv_sems, local_sem = rest
        else:
            send_sems, recv_sems, local_sem = rest
        x, y, c = lax.axis_index("x"), lax.axis_index("y"), lax.axis_index("c")
        me, sibling = (x, y, c), (x, y, 1 - c)
        chips = [(1 - x, y), (x, 1 - y), (1 - x, 1 - y)]

        def rows(px, py, pc):
            return out_ref.at[pl.ds((4 * px + 2 * py + pc) * m_per, m_per), :]

        def copy(k, block, to, src=None):
            return pltpu.make_async_remote_copy(
                src_ref=rows(*block) if src is None else src, dst_ref=rows(*block),
                send_sem=send_sems.at[k], recv_sem=recv_sems.at[k], device_id=to, device_id_type=MESH)

        mine = pltpu.make_async_copy(x_ref, rows(*me), local_sem)
        mine.start()
        first = [copy(0, me, sibling, src=x_ref)]
        first += [copy(1 + j, me, (*chip, c), src=x_ref) for j, chip in enumerate(chips)]
        for cp in first:
            cp.start()
        passed = [copy(4 + j, (*chip, c), sibling) for j, chip in enumerate(chips)]
        for j, chip in enumerate(chips):
            copy(1 + j, (*chip, c), me).wait_recv()
            passed[j].start()
        copy(0, sibling, me).wait_recv()
        for j, chip in enumerate(chips):
            copy(4 + j, (*chip, 1 - c), me).wait_recv()
        for cp in first + passed:
            cp.wait_send()
        mine.wait()
        if reduce:
            acc = out_ref[0:m_per, :]
            for d in range(1, N_DEV):
                acc = acc + out_ref[d * m_per:(d + 1) * m_per, :]
            sum_ref[...] = acc

    out_shape = [jax.ShapeDtypeStruct((N_DEV * m_per, n), v.dtype)]
    out_specs = [pl.BlockSpec(memory_space=pltpu.VMEM)]
    if reduce:
        out_shape.append(jax.ShapeDtypeStruct((m_per, n), v.dtype))
        out_specs.append(pl.BlockSpec(memory_space=pltpu.VMEM))
    res = pl.pallas_call(
        body, name=name, out_shape=tuple(out_shape),
        in_specs=[pl.BlockSpec(memory_space=pltpu.VMEM)], out_specs=tuple(out_specs),
        scratch_shapes=[pltpu.SemaphoreType.DMA((7,)), pltpu.SemaphoreType.DMA((7,)), pltpu.SemaphoreType.DMA],
        compiler_params=pltpu.CompilerParams(vmem_limit_bytes=VMEM_LIMIT),
    )(v)
    return res if reduce else res[0]


def _gather_weights(w_in_s, w_out_s):
    nl = w_in_s.shape[0]
    shapes = ((nl, N_CHIP) + w_in_s.shape[1:], (nl, N_CHIP) + w_out_s.shape[1:])
    n_sem = 3 * 2 * nl

    def body(win_ref, wout_ref, win_o, wout_o, send_sems, recv_sems):
        x, y, c = lax.axis_index("x"), lax.axis_index("y"), lax.axis_index("c")
        s = 2 * x + y
        chips = [(1 - x, y), (x, 1 - y), (1 - x, 1 - y)]
        for src, dst in ((win_ref, win_o), (wout_ref, wout_o)):
            rows = src.shape[1]
            step = min(rows, 256)
            for l in range(nl):
                for r in range(0, rows, step):
                    dst[l, s, r:r + step, :] = src[l, r:r + step, :].astype(BF16)

        def copy(k, ref, l, piece, to):
            return pltpu.make_async_remote_copy(
                src_ref=ref.at[l, piece], dst_ref=ref.at[l, piece], send_sem=send_sems.at[k],
                recv_sem=recv_sems.at[k], device_id=to, device_id_type=MESH)

        sends, recvs = [], []
        for j, (px, py) in enumerate(chips):
            for a, ref in enumerate((win_o, wout_o)):
                for l in range(nl):
                    k = (j * 2 + a) * nl + l
                    sends.append(copy(k, ref, l, s, (px, py, c)))
                    recvs.append(copy(k, ref, l, 2 * px + py, (px, py, c)))
        for cp in sends:
            cp.start()
        for cp in recvs:
            cp.wait_recv()
        for cp in sends:
            cp.wait_send()

    return pl.pallas_call(
        body, name="gather_weights",
        out_shape=tuple(jax.ShapeDtypeStruct(s, BF16) for s in shapes),
        in_specs=[pl.BlockSpec(memory_space=pltpu.VMEM)] * 2,
        out_specs=tuple(pl.BlockSpec(memory_space=pltpu.VMEM) for _ in shapes),
        scratch_shapes=[pltpu.SemaphoreType.DMA((n_sem,)), pltpu.SemaphoreType.DMA((n_sem,))],
        compiler_params=pltpu.CompilerParams(vmem_limit_bytes=VMEM_LIMIT),
    )(w_in_s, w_out_s)


def _send_half_to_sibling(g, tag):
    nl, nc, r, cdim = g.shape
    half = r // 2

    def body(g_ref, recv_ref, send_sem, recv_sem):
        x, y, c = lax.axis_index("x"), lax.axis_index("y"), lax.axis_index("c")
        cp = pltpu.make_async_remote_copy(
            src_ref=g_ref.at[:, :, pl.ds((1 - c) * half, half), :], dst_ref=recv_ref,
            send_sem=send_sem, recv_sem=recv_sem, device_id=(x, y, 1 - c), device_id_type=MESH)
        cp.start()
        cp.wait()

    return pl.pallas_call(
        body, name=f"rs_sibling_{tag}",
        out_shape=jax.ShapeDtypeStruct((nl, nc, half, cdim), g.dtype),
        in_specs=[pl.BlockSpec(memory_space=pl.ANY)], out_specs=pl.BlockSpec(memory_space=pl.ANY),
        scratch_shapes=[pltpu.SemaphoreType.DMA, pltpu.SemaphoreType.DMA],
    )(g)


def _add_half(g, recv, cidx, tag):
    nl, nc, r, cdim = g.shape
    half = r // 2
    rt = min(half, 128)
    nrt = half // rt

    def body(c_ref, a_ref, b_ref, o_ref):
        c = c_ref[0]
        first = jnp.where(c == 0, a_ref[...], b_ref[...])
        second = jnp.where(c == 0, b_ref[...], a_ref[...])
        o_ref[...] = first + second

    grid_spec = pltpu.PrefetchScalarGridSpec(
        num_scalar_prefetch=1, grid=(nl, nc, nrt),
        in_specs=[pl.BlockSpec((None, None, rt, cdim), lambda l, j, i, c: (l, j, c[0] * nrt + i, 0)),
                  pl.BlockSpec((None, None, rt, cdim), lambda l, j, i, c: (l, j, i, 0))],
        out_specs=pl.BlockSpec((None, None, rt, cdim), lambda l, j, i, c: (l, j, i, 0)))
    return pl.pallas_call(
        body, name=f"rs_add_sibling_{tag}", grid_spec=grid_spec,
        out_shape=jax.ShapeDtypeStruct((nl, nc, half, cdim), g.dtype),
        compiler_params=_params(("parallel", "parallel", "parallel")),
    )(cidx, g, recv)


def _exchange_chips(h, tag):
    nl, nc, r2, cdim = h.shape

    def body(h_ref, recv_ref, send_sems, recv_sems):
        x, y, c = lax.axis_index("x"), lax.axis_index("y"), lax.axis_index("c")
        chips = [(1 - x, y), (x, 1 - y), (1 - x, 1 - y)]
        cps = []
        for k, (px, py) in enumerate(chips):
            cps.append(pltpu.make_async_remote_copy(
                src_ref=h_ref.at[:, 2 * px + py], dst_ref=recv_ref.at[k], send_sem=send_sems.at[k],
                recv_sem=recv_sems.at[k], device_id=(px, py, c), device_id_type=MESH))
        for cp in cps:
            cp.start()
        for cp in cps:
            cp.wait()

    return pl.pallas_call(
        body, name=f"rs_chips_{tag}",
        out_shape=jax.ShapeDtypeStruct((3, nl, r2, cdim), h.dtype),
        in_specs=[pl.BlockSpec(memory_space=pl.ANY)], out_specs=pl.BlockSpec(memory_space=pl.ANY),
        scratch_shapes=[pltpu.SemaphoreType.DMA((3,)), pltpu.SemaphoreType.DMA((3,))],
    )(h)


def _add_chips(h, recv, sidx, tag):
    nl, nc, r2, cdim = h.shape
    rt = min(r2, 128)

    def body(s_ref, own_ref, r_ref, o_ref):
        s = s_ref[0]
        total = None
        for chip in range(N_CHIP):
            term = jnp.where(s == chip, own_ref[...],
                             jnp.where((s ^ 2) == chip, r_ref[0],
                                       jnp.where((s ^ 1) == chip, r_ref[1], r_ref[2])))
            total = term if total is None else total + term
        o_ref[...] = total

    grid_spec = pltpu.PrefetchScalarGridSpec(
        num_scalar_prefetch=1, grid=(nl, r2 // rt),
        in_specs=[pl.BlockSpec((None, None, rt, cdim), lambda l, i, s: (l, s[0], i, 0)),
                  pl.BlockSpec((3, None, rt, cdim), lambda l, i, s: (0, l, i, 0))],
        out_specs=pl.BlockSpec((None, rt, cdim), lambda l, i, s: (l, i, 0)))
    return pl.pallas_call(
        body, name=f"rs_add_chips_{tag}", grid_spec=grid_spec,
        out_shape=jax.ShapeDtypeStruct((nl, r2, cdim), h.dtype),
        compiler_params=_params(("parallel", "parallel")),
    )(sidx, h, recv)


def _join_halves(red, tag):
    nl, r2, cdim = red.shape

    def body(red_ref, out_ref, send_sem, recv_sem, local_sem):
        x, y, c = lax.axis_index("x"), lax.axis_index("y"), lax.axis_index("c")
        mine = out_ref.at[:, pl.ds(c * r2, r2), :]
        local = pltpu.make_async_copy(red_ref, mine, local_sem)
        local.start()
        cp = pltpu.make_async_remote_copy(
            src_ref=red_ref, dst_ref=mine, send_sem=send_sem, recv_sem=recv_sem,
            device_id=(x, y, 1 - c), device_id_type=MESH)
        cp.start()
        theirs = pltpu.make_async_remote_copy(
            src_ref=red_ref, dst_ref=out_ref.at[:, pl.ds((1 - c) * r2, r2), :], send_sem=send_sem,
            recv_sem=recv_sem, device_id=(x, y, 1 - c), device_id_type=MESH)
        theirs.wait_recv()
        cp.wait_send()
        local.wait()

    return pl.pallas_call(
        body, name=f"rs_join_{tag}",
        out_shape=jax.ShapeDtypeStruct((nl, 2 * r2, cdim), red.dtype),
        in_specs=[pl.BlockSpec(memory_space=pl.ANY)], out_specs=pl.BlockSpec(memory_space=pl.ANY),
        scratch_shapes=[pltpu.SemaphoreType.DMA, pltpu.SemaphoreType.DMA, pltpu.SemaphoreType.DMA],
    )(red)


def _reduce_scatter(g, cidx, sidx, tag):
    recv = _send_half_to_sibling(g, tag)
    chip_sum = _add_half(g, recv, cidx, tag)
    got = _exchange_chips(chip_sum, tag)
    red = _add_chips(chip_sum, got, sidx, tag)
    return _join_halves(red, tag)


def _ada_forward(c_all, w_ada_s, b_s):
    nl = w_ada_s.shape[0]

    def body(c_ref, w_ref, b_ref, o_ref):
        cv = c_ref[...]
        ca = (cv * _sig(cv)).astype(BF16)
        for l in range(nl):
            o_ref[l] = _dot(ca, w_ref[l].astype(BF16)) + b_ref[l]

    return pl.pallas_call(
        body, name="ada_forward", out_shape=jax.ShapeDtypeStruct((nl, N_DEV, PIECE), F32),
        compiler_params=pltpu.CompilerParams(vmem_limit_bytes=VMEM_LIMIT),
    )(c_all, w_ada_s, b_s)


def _ada_backward(c_all, dmod_s):
    nl = dmod_s.shape[0]

    def body(c_ref, d_ref, o_ref):
        cv = c_ref[...]
        ca = (cv * _sig(cv)).astype(BF16)
        for l in range(nl):
            o_ref[l] = _dot_tn(ca, d_ref[l].astype(BF16))

    return pl.pallas_call(
        body, name="ada_backward", out_shape=jax.ShapeDtypeStruct((nl, D, PIECE), F32),
        compiler_params=pltpu.CompilerParams(vmem_limit_bytes=VMEM_LIMIT),
    )(c_all, dmod_s)


def _normed(xv, vecs):
    r = lax.rsqrt(jnp.mean(xv * xv, axis=-1, keepdims=True) + EPS)
    xhat = xv * r
    hn = xhat * vecs[3:4, :]
    return r, xhat, hn


def _fwd_in(x, vecs, w, l, tt):
    t = x.shape[0]

    def body(x_ref, v_ref, w_ref, z_ref):
        vecs_v = v_ref[...]
        _, _, hn = _normed(x_ref[...], vecs_v)
        hb = (hn * (1.0 + vecs_v[1:2, :]) + vecs_v[0:1, :]).astype(BF16)
        for j in range(N_CHIP):
            z_ref[:, PIECE * j:PIECE * (j + 1)] = _dot(hb, w_ref[j])

    return pl.pallas_call(
        body, name=f"fwd_in_{l}", grid=(t // tt,),
        in_specs=[pl.BlockSpec((tt, D), lambda i: (i, 0)), pl.BlockSpec((8, D), lambda i: (0, 0)),
                  pl.BlockSpec((None, N_CHIP, D, PIECE), lambda i: (l, 0, 0, 0))],
        out_specs=pl.BlockSpec((tt, DIN), lambda i: (i, 0)),
        out_shape=jax.ShapeDtypeStruct((t, DIN), F32),
        compiler_params=_params(("parallel",)),
    )(x, vecs, w)


def _bwd_in(dz, x, dxo, vecs, w, l, tt):
    t = x.shape[0]
    nt = t // tt

    def body(dz_ref, x_ref, dxo_ref, v_ref, w_any, dx_ref, gw_any, sums_ref, w_vmem, gw_acc, sem):
        i = pl.program_id(0)

        @pl.when(i == 0)
        def _():
            cp = pltpu.make_async_copy(w_any.at[l], w_vmem, sem)
            cp.start()
            cp.wait()
            gw_acc[...] = jnp.zeros_like(gw_acc)
            sums_ref[...] = jnp.zeros_like(sums_ref)

        vecs_v = v_ref[...]
        r, xhat, hn = _normed(x_ref[...], vecs_v)
        one_scale = 1.0 + vecs_v[1:2, :]
        hb = (hn * one_scale + vecs_v[0:1, :]).astype(BF16)
        dh = None
        for j in range(N_CHIP):
            dzj = dz_ref[:, PIECE * j:PIECE * (j + 1)]
            part = _dot_nt(dzj, w_vmem[j])
            dh = part if dh is None else dh + part
            gw_acc[j] += _dot_tn(hb, dzj)
        sums_ref[0:1, :] += _rowsum(dh)
        sums_ref[1:2, :] += _rowsum(dh * hn)
        sums_ref[2:3, :] += _rowsum(dh * one_scale * xhat)
        dxh = dh * (vecs_v[3:4, :] * one_scale)
        dx_ref[...] = dxo_ref[...] + r * (dxh - xhat * jnp.mean(dxh * xhat, axis=-1, keepdims=True))

        @pl.when(i == nt - 1)
        def _():
            cp = pltpu.make_async_copy(gw_acc, gw_any, sem)
            cp.start()
            cp.wait()

    return pl.pallas_call(
        body, name=f"bwd_in_{l}", grid=(nt,),
        in_specs=[pl.BlockSpec((tt, DIN), lambda i: (i, 0)), pl.BlockSpec((tt, D), lambda i: (i, 0)),
                  pl.BlockSpec((tt, D), lambda i: (i, 0)), pl.BlockSpec((8, D), lambda i: (0, 0)),
                  pl.BlockSpec(memory_space=pl.ANY)],
        out_specs=(pl.BlockSpec((tt, D), lambda i: (i, 0)), pl.BlockSpec(memory_space=pl.ANY),
                   pl.BlockSpec((8, D), lambda i: (0, 0))),
        out_shape=(jax.ShapeDtypeStruct((t, D), F32), jax.ShapeDtypeStruct((N_CHIP, D, PIECE), F32),
                   jax.ShapeDtypeStruct((8, D), F32)),
        scratch_shapes=[pltpu.VMEM((N_CHIP, D, PIECE), BF16), pltpu.VMEM((N_CHIP, D, PIECE), F32),
                        pltpu.SemaphoreType.DMA],
        compiler_params=_params(("arbitrary",)),
    )(dz, x, dxo, vecs, w)


def _col(ref, k):
    return ref[:, GW * k:GW * (k + 1)]


def _lane_group():
    return lax.broadcasted_iota(jnp.int32, (1, GW), 1) // (GW // 4)


def _pool_window(group):
    return jnp.where(group == 0, 2.0, jnp.where(group == 1, 4.0, jnp.where(group == 2, 8.0, 16.0)))


def _by_group(group, a, b, c, d):
    return jnp.where(group == 0, a, jnp.where(group == 1, b, jnp.where(group == 2, c, d)))


def _layer_norm(x, g, b):
    mu = jnp.mean(x, axis=-1, keepdims=True)
    xc = x - mu
    rstd = lax.rsqrt(jnp.mean(xc * xc, axis=-1, keepdims=True) + EPS)
    xh = xc * rstd
    return xh * g + b, xh, rstd


def _layer_norm_grad(d_out, xh, rstd, g):
    dxh = d_out * g
    return rstd * (dxh - jnp.mean(dxh, axis=-1, keepdims=True) - xh * jnp.mean(dxh * xh, axis=-1, keepdims=True))


def _conv_taps(ext_ref, w_ref, n_taps, first_row, tm, out_ref, flip=False):
    for rb in range(0, tm, CONV_ROWS):
        acc = None
        for k in range(n_taps):
            wk = n_taps - 1 - k if flip else k
            term = w_ref[wk:wk + 1, :] * ext_ref[first_row + rb + k:first_row + rb + k + CONV_ROWS, :]
            acc = term if acc is None else acc + term
        out_ref[rb:rb + CONV_ROWS, :] = acc


def _mix_forward(zc, zp, first, row0, tm, p, s):
    f = {}
    keep = jnp.where(first, 0.0, 1.0)
    group = _lane_group()
    a_b, a_c, a_x, a_g = _col(zc, 0), _col(zc, 1), _col(zc, 2), _col(zc, 3)
    s['ua'][0:HALO, :] = _col(zp, 1) * _col(zp, 2) * keep
    s['ua'][HALO:HALO + tm, :] = a_c * a_x
    _conv_taps(s['ua'], p['wa'], CONV_A, HALO - (CONV_A - 1), tm, s['cv'])
    cv = s['cv'][...]
    sg_a = _sig(a_g)
    f.update(a_b=a_b, a_c=a_c, a_x=a_x, a_g=a_g, cv=cv, sg_a=sg_a)
    y_a = a_b * cv * (a_g * sg_a)
    b_p, b_g = _col(zc, 4), _col(zc, 5)
    s['p0'][0:HALO, :] = _col(zp, 4) * keep
    s['p0'][HALO:HALO + tm, :] = b_p
    n = HALO + tm
    s['p1'][8:n, :] = s['p0'][8:n, :] + s['p0'][7:n - 1, :]
    w2 = s['p1'][HALO:n, :]
    s['p0'][16:n, :] = s['p1'][16:n, :] + s['p1'][14:n - 2, :]
    w4 = s['p0'][HALO:n, :]
    s['p1'][24:n, :] = s['p0'][24:n, :] + s['p0'][20:n - 4, :]
    w8 = s['p1'][HALO:n, :]
    w16 = w8 + s['p1'][HALO - 8:n - 8, :]
    tpos = (row0 + lax.broadcasted_iota(jnp.int32, (tm, 1), 0) + 1).astype(F32)
    inv_cnt = 1.0 / jnp.minimum(tpos, _pool_window(group))
    pooled = (_by_group(group, w2, w4, w8, w16) * inv_cnt - b_p).astype(BF16)
    q = _dot(pooled, p['wbd'][...])
    sg_b = _sig(b_g)
    f.update(b_g=b_g, pooled=pooled, q=q, sg_b=sg_b, inv_cnt=inv_cnt)
    y_b = q * p['v256'][0:1, :] * (b_g * sg_b)
    c_a, c_gl, c_g = _col(zc, 6), _col(zc, 7), _col(zc, 8)
    sg_gl = _sig(c_gl)
    zp_gl = _col(zp, 7)
    s['hc'][0:HALO, :] = _col(zp, 6) * _sig(zp_gl) * keep
    s['hc'][HALO:HALO + tm, :] = c_a * sg_gl
    _conv_taps(s['hc'], p['wdw'], CONV_C, HALO - (CONV_C - 1), tm, s['co'])
    co = s['co'][...] + p['v256'][1:2, :]
    ln_c, xh_c, rstd_c = _layer_norm(co, p['v256'][2:3, :], p['v256'][3:4, :])
    sg_ln = _sig(ln_c)
    sc = (ln_c * sg_ln).astype(BF16)
    pw = _dot(sc, p['wpw'][...]) + p['v256'][4:5, :]
    sg_c = _sig(c_g)
    f.update(c_a=c_a, c_g=c_g, sg_gl=sg_gl, ln_c=ln_c, xh_c=xh_c, rstd_c=rstd_c, sg_ln=sg_ln, sc=sc, pw=pw,
             sg_c=sg_c)
    y_c = pw * (c_g * sg_c)
    d_u, d_v, d_g = _col(zc, 9), _col(zc, 10), _col(zc, 11)
    gu, th_u = _gelu(d_u)
    gv, th_v = _gelu(d_v)
    v, xh_d, rstd_d = _layer_norm(gv, p['v256'][5:6, :], p['v256'][6:7, :])
    vb = v.astype(BF16)
    parts = []
    for ch in range(tm // CHUNK):
        vc = vb[ch * CHUNK:(ch + 1) * CHUNK, :]
        mixed = p['bsd'][...]
        for h in range(4):
            mixed = mixed + jnp.where(group == h, _dot(p['ws'][h], vc), 0.0)
        parts.append(mixed)
    mx = parts[0] if len(parts) == 1 else jnp.concatenate(parts, axis=0)
    sg_d = _sig(d_g)
    f.update(d_u=d_u, d_v=d_v, d_g=d_g, gu=gu, th_u=th_u, th_v=th_v, xh_d=xh_d, rstd_d=rstd_d, vb=vb, mx=mx,
             sg_d=sg_d)
    y_d = gu * mx * (d_g * sg_d)
    f['ys'] = [y.astype(BF16) for y in (y_a, y_b, y_c, y_d)]
    return f


def _mix_scratch(tm):
    ext = pltpu.VMEM((HALO + tm, GW), F32)
    tile = pltpu.VMEM((tm, GW), F32)
    return dict(ua=ext, cv=tile, p0=ext, p1=ext, hc=ext, co=tile)


_MIX_PARAMS = ('wa', 'wdw', 'v256', 'wbd', 'wpw', 'ws', 'bsd', 'wout')


def _mix_param_specs(l):
    def whole(*shape):
        nd = len(shape)
        return pl.BlockSpec((None,) + shape, lambda i, _nd=nd: (l,) + (0,) * _nd)
    return [whole(8, GW), whole(32, GW), whole(16, GW), whole(GW, GW), whole(GW, GW), whole(4, CHUNK, CHUNK),
            whole(CHUNK, GW), whole(N_CHIP, GW, D)]


def _fwd_mix(z, x, vecs, mp, l, tm):
    t = x.shape[0]
    per_halo = tm // HALO
    names = list(_mix_scratch(tm))

    def body(zc, zp, x_ref, v_ref, *rest):
        p = dict(zip(_MIX_PARAMS, rest[:len(_MIX_PARAMS)]))
        xn_ref, y_ref = rest[len(_MIX_PARAMS):len(_MIX_PARAMS) + 2]
        s = dict(zip(names, rest[len(_MIX_PARAMS) + 2:]))
        i = pl.program_id(0)
        f = _mix_forward(zc, zp, i == 0, i * tm, tm, p, s)
        y = None
        for j in range(4):
            part = _dot(f['ys'][j], p['wout'][j])
            y = part if y is None else y + part
        y_ref[...] = y
        xn_ref[...] = x_ref[...] + v_ref[2:3, :] * y

    return pl.pallas_call(
        body, name=f"fwd_mix_{l}", grid=(t // tm,),
        in_specs=[pl.BlockSpec((tm, DIN), lambda i: (i, 0)),
                  pl.BlockSpec((HALO, DIN), lambda i: (jnp.maximum(i * per_halo - 1, 0), 0)),
                  pl.BlockSpec((tm, D), lambda i: (i, 0)), pl.BlockSpec((8, D), lambda i: (0, 0))]
        + _mix_param_specs(l),
        out_specs=(pl.BlockSpec((tm, D), lambda i: (i, 0)), pl.BlockSpec((tm, D), lambda i: (i, 0))),
        out_shape=(jax.ShapeDtypeStruct((t, D), F32), jax.ShapeDtypeStruct((t, D), F32)),
        scratch_shapes=list(_mix_scratch(tm).values()),
        compiler_params=_params(("parallel",)),
    )(z, z, x, vecs, *[mp[k] for k in _MIX_PARAMS])


def _bwd_mix(dxo, y, z, vecs, mp, wst, l, tm):
    t = dxo.shape[0]
    nt = t // tm
    per_halo = tm // HALO
    fwd_names = list(_mix_scratch(tm))
    ext = pltpu.VMEM((tm + HALO, GW), F32)
    carry = pltpu.VMEM((HALO, GW), F32)
    tile = pltpu.VMEM((tm, GW), F32)
    bwd_scratch = dict(ea=ext, eb=ext, eb2=ext, ec=ext, ca=carry, cb=carry, cc=carry, dua=tile, dhc=tile,
                       gbacc=pltpu.VMEM((CHUNK, GW), F32))
    bwd_names = list(bwd_scratch)
    n_p = len(_MIX_PARAMS)

    def body(dxo_ref, y_ref, zc, zp, v_ref, *rest):
        p = dict(zip(_MIX_PARAMS, rest[:n_p]))
        wst_ref = rest[n_p]
        outs = rest[n_p + 1:n_p + 10]
        dz_ref, gwout, gwbd, gwpw, gws, gbs, gwdw, g256, g1024 = outs
        s = dict(zip(fwd_names + bwd_names, rest[n_p + 10:]))
        i = pl.program_id(0)
        ti = nt - 1 - i
        group = _lane_group()

        @pl.when(i == 0)
        def _():
            for ref in (gwout, gwbd, gwpw, gws, gbs, gwdw, g256, g1024, s['ca'], s['cb'], s['cc'], s['gbacc']):
                ref[...] = jnp.zeros_like(ref)

        f = _mix_forward(zc, zp, ti == 0, ti * tm, tm, p, s)
        dxo_v = dxo_ref[...]
        g1024[0:1, :] += _rowsum(dxo_v * y_ref[...])
        dy = (dxo_v * v_ref[2:3, :]).astype(BF16)
        dys = []
        for j in range(4):
            gwout[j] += _dot_tn(f['ys'][j], dy)
            dys.append(_dot_nt(dy, p['wout'][j]))
        dya, dyb, dyc, dyd = dys

        def put(k, val):
            dz_ref[:, GW * k:GW * (k + 1)] = val.astype(BF16)

        a_b, a_c, a_x, a_g, cv, sg_a = f['a_b'], f['a_c'], f['a_x'], f['a_g'], f['cv'], f['sg_a']
        silu_a = a_g * sg_a
        put(0, dya * cv * silu_a)
        put(3, dya * a_b * cv * (sg_a * (1.0 + a_g * (1.0 - sg_a))))
        d_cv = dya * a_b * silu_a
        for k in range(CONV_A):
            g256[8 + k:9 + k, :] += _rowsum(d_cv * s['ua'][HALO - (CONV_A - 1) + k:HALO - (CONV_A - 1) + k + tm, :])
        s['ea'][0:tm, :] = d_cv
        s['ea'][tm:tm + HALO, :] = s['ca'][...]
        s['ca'][...] = d_cv[0:HALO, :]
        _conv_taps(s['ea'], p['wa'], CONV_A, 0, tm, s['dua'], flip=True)
        d_u = s['dua'][...]
        put(1, d_u * a_x)
        put(2, d_u * a_c)
        b_g, q, sg_b, inv_cnt = f['b_g'], f['q'], f['sg_b'], f['inv_cnt']
        scale_b = p['v256'][0:1, :]
        silu_b = b_g * sg_b
        g256[0:1, :] += _rowsum(dyb * q * silu_b)
        put(5, dyb * q * scale_b * (sg_b * (1.0 + b_g * (1.0 - sg_b))))
        d_q = (dyb * scale_b * silu_b).astype(BF16)
        gwbd[...] += _dot_tn(f['pooled'], d_q)
        d_pooled = _dot_nt(d_q, p['wbd'][...])
        gp = d_pooled * inv_cnt
        n = tm + HALO
        s['eb'][0:tm, :] = gp
        s['eb'][tm:n, :] = s['cb'][...]
        s['cb'][...] = gp[0:HALO, :]
        s['eb2'][0:n - 8, :] = s['eb'][0:n - 8, :] + s['eb'][1:n - 7, :]
        r2 = s['eb2'][0:tm, :]
        s['eb'][0:n - 16, :] = s['eb2'][0:n - 16, :] + s['eb2'][2:n - 14, :]
        r4 = s['eb'][0:tm, :]
        s['eb2'][0:n - 24, :] = s['eb'][0:n - 24, :] + s['eb'][4:n - 20, :]
        r8 = s['eb2'][0:tm, :]
        r16 = r8 + s['eb2'][8:tm + 8, :]
        put(4, _by_group(group, r2, r4, r8, r16) - d_pooled)
        c_a, c_g, sg_gl, ln_c, xh_c, rstd_c, sg_ln, pw, sg_c = (
            f['c_a'], f['c_g'], f['sg_gl'], f['ln_c'], f['xh_c'], f['rstd_c'], f['sg_ln'], f['pw'], f['sg_c'])
        put(8, dyc * pw * (sg_c * (1.0 + c_g * (1.0 - sg_c))))
        d_pw = dyc * (c_g * sg_c)
        g256[4:5, :] += _rowsum(d_pw)
        d_pwb = d_pw.astype(BF16)
        gwpw[...] += _dot_tn(f['sc'], d_pwb)
        d_ln = _dot_nt(d_pwb, p['wpw'][...]) * (sg_ln * (1.0 + ln_c * (1.0 - sg_ln)))
        g256[2:3, :] += _rowsum(d_ln * xh_c)
        g256[3:4, :] += _rowsum(d_ln)
        d_co = _layer_norm_grad(d_ln, xh_c, rstd_c, p['v256'][2:3, :])
        g256[1:2, :] += _rowsum(d_co)
        for k in range(CONV_C):
            gwdw[k:k + 1, :] += _rowsum(d_co * s['hc'][HALO - (CONV_C - 1) + k:HALO - (CONV_C - 1) + k + tm, :])
        s['ec'][0:tm, :] = d_co
        s['ec'][tm:tm + HALO, :] = s['cc'][...]
        s['cc'][...] = d_co[0:HALO, :]
        _conv_taps(s['ec'], p['wdw'], CONV_C, 0, tm, s['dhc'], flip=True)
        d_hc = s['dhc'][...]
        put(6, d_hc * sg_gl)
        put(7, d_hc * c_a * sg_gl * (1.0 - sg_gl))
        d_uu, d_vv, d_g, gu, th_u, th_v, xh_d, rstd_d, vb, mx, sg_d = (
            f['d_u'], f['d_v'], f['d_g'], f['gu'], f['th_u'], f['th_v'], f['xh_d'], f['rstd_d'], f['vb'], f['mx'],
            f['sg_d'])
        silu_d = d_g * sg_d
        put(9, dyd * mx * silu_d * _gelu_grad(d_uu, th_u))
        put(11, dyd * gu * mx * (sg_d * (1.0 + d_g * (1.0 - sg_d))))
        d_mx = dyd * gu * silu_d
        dv_parts = []
        gb = None
        for ch in range(tm // CHUNK):
            dmc = d_mx[ch * CHUNK:(ch + 1) * CHUNK, :]
            gb = dmc if gb is None else gb + dmc
            dmb = dmc.astype(BF16)
            vc = vb[ch * CHUNK:(ch + 1) * CHUNK, :]
            dvc = None
            for h in range(4):
                gws[h] += _dot_nt(jnp.where(group == h, dmb, jnp.zeros_like(dmb)), vc)
                part = jnp.where(group == h, _dot(wst_ref[h], dmb), 0.0)
                dvc = part if dvc is None else dvc + part
            dv_parts.append(dvc)
        s['gbacc'][...] += gb
        d_v = dv_parts[0] if len(dv_parts) == 1 else jnp.concatenate(dv_parts, axis=0)
        g256[5:6, :] += _rowsum(d_v * xh_d)
        g256[6:7, :] += _rowsum(d_v)
        d_gv = _layer_norm_grad(d_v, xh_d, rstd_d, p['v256'][5:6, :])
        put(10, d_gv * _gelu_grad(d_vv, th_v))

        @pl.when(i == nt - 1)
        def _():
            row = lax.broadcasted_iota(jnp.int32, (CHUNK, CHUNK), 0)
            col = lax.broadcasted_iota(jnp.int32, (CHUNK, CHUNK), 1)
            for h in range(4):
                gws[h] = jnp.where(col <= row, gws[h], 0.0)
            head_of_lane = lax.broadcasted_iota(jnp.int32, (GW, CHUNK), 0) // (GW // 4)
            sel = jnp.where(head_of_lane == lax.broadcasted_iota(jnp.int32, (GW, CHUNK), 1), 1.0, 0.0)
            gbs[...] = jnp.dot(s['gbacc'][...], sel, preferred_element_type=F32, precision=lax.Precision.HIGHEST)

    def const(*shape):
        nd = len(shape)
        return pl.BlockSpec(shape, lambda i, _nd=nd: (0,) * _nd)

    def rev(i):
        return nt - 1 - i

    out_shapes = (jax.ShapeDtypeStruct((t, DIN), BF16), jax.ShapeDtypeStruct((N_CHIP, GW, D), F32),
                  jax.ShapeDtypeStruct((GW, GW), F32), jax.ShapeDtypeStruct((GW, GW), F32),
                  jax.ShapeDtypeStruct((4, CHUNK, CHUNK), F32), jax.ShapeDtypeStruct((CHUNK, CHUNK), F32),
                  jax.ShapeDtypeStruct((32, GW), F32), jax.ShapeDtypeStruct((16, GW), F32),
                  jax.ShapeDtypeStruct((8, D), F32))
    out_specs = (pl.BlockSpec((tm, DIN), lambda i: (rev(i), 0)), const(N_CHIP, GW, D), const(GW, GW),
                 const(GW, GW), const(4, CHUNK, CHUNK), const(CHUNK, CHUNK), const(32, GW), const(16, GW),
                 const(8, D))
    scratch = list(_mix_scratch(tm).values()) + list(bwd_scratch.values())
    return pl.pallas_call(
        body, name=f"bwd_mix_{l}", grid=(nt,),
        in_specs=[pl.BlockSpec((tm, D), lambda i: (rev(i), 0)), pl.BlockSpec((tm, D), lambda i: (rev(i), 0)),
                  pl.BlockSpec((tm, DIN), lambda i: (rev(i), 0)),
                  pl.BlockSpec((HALO, DIN), lambda i: (jnp.maximum(rev(i) * per_halo - 1, 0), 0)),
                  pl.BlockSpec((8, D), lambda i: (0, 0))]
        + _mix_param_specs(l)
        + [pl.BlockSpec((None, 4, CHUNK, CHUNK), lambda i: (l, 0, 0, 0))],
        out_specs=out_specs, out_shape=out_shapes, scratch_shapes=scratch,
        compiler_params=_params(("arbitrary",)),
    )(dxo, y, z, z, vecs, *[mp[k] for k in _MIX_PARAMS], wst)


def _final(x, target, fg, tt):
    t = x.shape[0]
    nt = t // tt

    def body(x_ref, t_ref, g_ref, dx_ref, sums_ref):
        i = pl.program_id(0)

        @pl.when(i == 0)
        def _():
            sums_ref[...] = jnp.zeros_like(sums_ref)

        xv = x_ref[...]
        g = g_ref[0:1, :]
        r = lax.rsqrt(jnp.mean(xv * xv, axis=-1, keepdims=True) + EPS)
        xhat = xv * r
        err = xhat * g - t_ref[...]
        sums_ref[1:2, :] += _rowsum(err * err) * (0.5 / D)
        dyv = err * (1.0 / D)
        sums_ref[0:1, :] += _rowsum(dyv * xhat)
        dxh = dyv * g
        dx_ref[...] = r * (dxh - xhat * jnp.mean(dxh * xhat, axis=-1, keepdims=True))

        @pl.when(i == nt - 1)
        def _():
            sums_ref[2:3, :] = jnp.broadcast_to(jnp.sum(sums_ref[1:2, :], axis=-1, keepdims=True), (1, D))

    return pl.pallas_call(
        body, name="final_loss", grid=(nt,),
        in_specs=[pl.BlockSpec((tt, D), lambda i: (i, 0)), pl.BlockSpec((tt, D), lambda i: (i, 0)),
                  pl.BlockSpec((8, D), lambda i: (0, 0))],
        out_specs=(pl.BlockSpec((tt, D), lambda i: (i, 0)), pl.BlockSpec((8, D), lambda i: (0, 0))),
        out_shape=(jax.ShapeDtypeStruct((t, D), F32), jax.ShapeDtypeStruct((8, D), F32)),
        compiler_params=_params(("arbitrary",)),
    )(x, target, fg)


def _adamw(w, g, m, v, tag):
    rows, cols = w.shape
    rt = rows
    for cand in (512, 256, 128, 64, 32, 16, 8):
        if rows % cand == 0:
            rt = cand
            break

    def body(w_ref, g_ref, m_ref, v_ref, d_ref, nm_ref, nv_ref):
        gv = g_ref[...]
        nm = ADAM_B1 * m_ref[...] + (1.0 - ADAM_B1) * gv
        nv = ADAM_B2 * v_ref[...] + (1.0 - ADAM_B2) * (gv * gv)
        m_hat = nm / (1.0 - ADAM_B1 ** ADAM_STEP)
        v_hat = nv / (1.0 - ADAM_B2 ** ADAM_STEP)
        d_ref[...] = -ADAM_LR * (m_hat / (jnp.sqrt(v_hat) + ADAM_EPS) + ADAM_WD * w_ref[...])
        nm_ref[...] = nm
        nv_ref[...] = nv

    spec = pl.BlockSpec((rt, cols), lambda i: (i, 0))
    return pl.pallas_call(
        body, name=f"adamw_{tag}", grid=(rows // rt,), in_specs=[spec] * 4, out_specs=(spec,) * 3,
        out_shape=(jax.ShapeDtypeStruct(w.shape, F32),) * 3, compiler_params=_params(("parallel",)),
    )(w, g, m, v)


def _pack(arrays):
    parts, offsets, off = [], [], 0
    for a in arrays:
        n = int(np.prod(a.shape))
        padded = -(-n // 1024) * 1024
        flat = a.reshape(-1).astype(F32)
        if padded != n:
            flat = jnp.concatenate([flat, jnp.zeros((padded - n,), F32)])
        parts.append(flat)
        offsets.append((off, n, a.shape))
        off += padded
    return jnp.concatenate(parts).reshape(-1, 128), offsets


def _unpack(buf, offsets):
    flat = buf.reshape(-1)
    return [flat[off:off + n].reshape(shape) for off, n, shape in offsets]


def _pad_rows(a, rows):
    return jnp.concatenate([a, jnp.zeros((rows - a.shape[0],) + a.shape[1:], a.dtype)], axis=0)


def kernel(x, c, norm_g, w_ada, b_ada, w_in, w_conv_a, w_pool, pool_scale, w_dw_c, b_dw_c, ln_g_c, ln_b_c, w_pw2_c, b_pw2_c, ln_g_d, ln_b_d, w_s_d, b_s_d, w_out, final_g, loss_target, m_norm_g, m_w_ada, m_b_ada, m_w_in, m_w_conv_a, m_w_pool, m_pool_scale, m_w_dw_c, m_b_dw_c, m_ln_g_c, m_ln_b_c, m_w_pw2_c, m_b_pw2_c, m_ln_g_d, m_ln_b_d, m_w_s_d, m_b_s_d, m_w_out, m_final_g, v_norm_g, v_w_ada, v_b_ada, v_w_in, v_w_conv_a, v_w_pool, v_pool_scale, v_w_dw_c, v_b_dw_c, v_ln_g_c, v_ln_b_c, v_w_pw2_c, v_b_pw2_c, v_ln_g_d, v_ln_b_d, v_w_s_d, v_b_s_d, v_w_out, v_final_g):
    env = dict(locals())
    weights = {n: env[n] for n in WEIGHTS}
    mom_m = {n: env["m_" + n] for n in WEIGHTS}
    mom_v = {n: env["v_" + n] for n in WEIGHTS}
    nl = norm_g.shape[0]
    t = x.shape[1]
    tt, tm = _tiles(t)
    ax, ay, ac = lax.axis_index("x"), lax.axis_index("y"), lax.axis_index("c")
    chip = 2 * ax + ay
    dev = 2 * chip + ac
    cidx = jnp.reshape(ac, (1,)).astype(jnp.int32)
    sidx = jnp.reshape(chip, (1,)).astype(jnp.int32)
    xs, target = x[0], loss_target[0]

    shard_small, shard_off = _pack([c, w_conv_a, w_dw_c, w_pw2_c])
    gathered = _allgather8(shard_small, "gather_small").reshape(N_DEV, -1, 128)
    per_dev = [_unpack(gathered[d], shard_off) for d in range(0, N_DEV, 2)]
    c_all = jnp.concatenate([u[0] for d in range(N_DEV)
                             for u in [_unpack(gathered[d], shard_off[:1])]], axis=0)
    w_conv_full = jnp.concatenate([u[1] for u in per_dev], axis=-1)
    w_dw_full = jnp.concatenate([u[2] for u in per_dev], axis=-1)
    w_pw2_full = jnp.concatenate([u[3] for u in per_dev], axis=1)

    b_ada_s = lax.dynamic_slice_in_dim(b_ada, chip * PIECE, PIECE, axis=1)[:, None, :]
    mod_s = _ada_forward(c_all, w_ada, b_ada_s)
    mod_all = _allgather8(mod_s.reshape(nl * N_DEV, PIECE), "gather_mod").reshape(N_DEV, nl, N_DEV, PIECE)
    mod_rows = lax.dynamic_index_in_dim(mod_all, dev, axis=2, keepdims=False)
    mod = jnp.concatenate([mod_rows[2 * s] for s in range(N_CHIP)], axis=-1)

    w_in_full, w_out_full = _gather_weights(w_in, w_out)

    tril = jnp.tril(jnp.ones((CHUNK, CHUNK), bool))
    ws_masked = jnp.where(tril[None, None], w_s_d, 0.0)
    wbd = jnp.zeros((nl, GW, GW), F32)
    for g in range(4):
        wbd = wbd.at[:, 64 * g:64 * (g + 1), 64 * g:64 * (g + 1)].set(w_pool[:, g])
    v256 = jnp.stack([pool_scale, b_dw_c, ln_g_c, ln_b_c, b_pw2_c, ln_g_d, ln_b_d], axis=1)
    mp = dict(
        wa=_pad_rows(jnp.swapaxes(w_conv_full, 0, 1), 8).swapaxes(0, 1),
        wdw=_pad_rows(jnp.swapaxes(w_dw_full, 0, 1), 32).swapaxes(0, 1),
        v256=_pad_rows(jnp.swapaxes(v256, 0, 1), 16).swapaxes(0, 1),
        wbd=wbd.astype(BF16), wpw=w_pw2_full.astype(BF16), ws=ws_masked.astype(BF16),
        bsd=jnp.repeat(jnp.swapaxes(b_s_d, 1, 2), GW // 4, axis=2),
        wout=w_out_full)
    wst = jnp.swapaxes(ws_masked, 2, 3).astype(BF16)

    def vec_rows(l):
        rows = jnp.stack([mod[l, 0:D], mod[l, D:2 * D], mod[l, 2 * D:3 * D], norm_g[l]], axis=0)
        return _pad_rows(rows, 8)

    xin, zs, ys, vecs = [xs], [], [], []
    for l in range(nl):
        vecs.append(vec_rows(l))
        zs.append(_fwd_in(xin[l], vecs[l], w_in_full, l, tt))
        xn, yl = _fwd_mix(zs[l], xin[l], vecs[l], mp, l, tm)
        xin.append(xn)
        ys.append(yl)
    dx, fin_sums = _final(xin[nl], target, _pad_rows(final_g[None, :], 8), tt)

    g_in, g_out, small, dmod = [None] * nl, [None] * nl, [None] * nl, [None] * nl
    for l in reversed(range(nl)):
        dz, gwout, gwbd, gwpw, gws, gbs, gwdw, g256, g1024 = _bwd_mix(dx, ys[l], zs[l], vecs[l], mp, wst, l, tm)
        dx, g_in[l], sums = _bwd_in(dz, xin[l], dx, vecs[l], w_in_full, l, tt)
        g_out[l] = gwout
        dmod[l] = jnp.concatenate([sums[0], sums[1], g1024[0]])
        small[l] = dict(
            norm_g=sums[2], w_conv_a=g256[8:8 + CONV_A],
            w_pool=jnp.stack([gwbd[64 * g:64 * (g + 1), 64 * g:64 * (g + 1)] for g in range(4)]),
            pool_scale=g256[0], w_dw_c=gwdw[:CONV_C], b_dw_c=g256[1], ln_g_c=g256[2], ln_b_c=g256[3],
            w_pw2_c=gwpw, b_pw2_c=g256[4], ln_g_d=g256[5], ln_b_d=g256[6], w_s_d=gws, b_s_d=gbs[:, :4].T)
    grad_x = dx[None]

    dmod_all, dmod_sum = _allgather8(jnp.stack(dmod).reshape(nl * 3 * D // 128, 128), "gather_dmod", reduce=True)
    dmod_all = dmod_all.reshape(N_DEV, nl, 3 * D)
    grad_b_ada = dmod_sum.reshape(nl, 3 * D)
    dmod_s = jnp.swapaxes(lax.dynamic_slice_in_dim(dmod_all, chip * PIECE, PIECE, axis=2), 0, 1)
    grad_w_ada = _ada_backward(c_all, dmod_s)

    small_names = [n for n in WEIGHTS if n not in BIG and n not in ('b_ada', 'final_g')]
    stacked = [jnp.stack([small[l][n] for l in range(nl)]) for n in small_names]
    small_buf, small_off = _pack(stacked + [fin_sums[0], fin_sums[2, 0:1]])
    reduced = _unpack(_allgather8(small_buf, "allreduce_small", reduce=True)[1], small_off)
    grads = dict(zip(small_names, reduced[:len(small_names)]))
    grads['final_g'] = reduced[-2]
    loss = reduced[-1][0]
    grads['b_ada'] = grad_b_ada
    grads['w_ada'] = grad_w_ada
    grads['w_conv_a'] = lax.dynamic_slice_in_dim(grads['w_conv_a'], chip * 64, 64, axis=2)
    grads['w_dw_c'] = lax.dynamic_slice_in_dim(grads['w_dw_c'], chip * 64, 64, axis=2)
    grads['w_pw2_c'] = lax.dynamic_slice_in_dim(grads['w_pw2_c'], chip * 64, 64, axis=1)

    grads['w_in'] = jnp.concatenate(
        [_reduce_scatter(g_in[l][None], cidx, sidx, f"in{l}") for l in range(nl)], axis=0)
    grads['w_out'] = jnp.concatenate(
        [_reduce_scatter(g_out[l][None], cidx, sidx, f"out{l}") for l in range(nl)], axis=0)

    delta, new_m, new_v = {}, {}, {}
    for n in BIG:
        shape = weights[n].shape
        as2d = lambda a: a.reshape(-1, shape[-1])
        d2, m2, v2 = _adamw(as2d(weights[n]), as2d(grads[n]), as2d(mom_m[n]), as2d(mom_v[n]), n)
        delta[n], new_m[n], new_v[n] = d2.reshape(shape), m2.reshape(shape), v2.reshape(shape)
    rest = [n for n in WEIGHTS if n not in BIG]
    wb, off = _pack([weights[n] for n in rest])
    gb, _ = _pack([grads[n] for n in rest])
    mb, _ = _pack([mom_m[n] for n in rest])
    vb, _ = _pack([mom_v[n] for n in rest])
    d2, m2, v2 = _adamw(wb, gb, mb, vb, "small")
    for n, dn, mn, vn in zip(rest, _unpack(d2, off), _unpack(m2, off), _unpack(v2, off)):
        delta[n], new_m[n], new_v[n] = dn, mn, vn

    return (loss, grad_x, *[grads[n] for n in WEIGHTS], *[delta[n] for n in WEIGHTS],
            *[new_m[n] for n in WEIGHTS], *[new_v[n] for n in WEIGHTS])
```

```python
import functools
import math

import jax
import jax.numpy as jnp
import numpy as np
from jax import lax
from jax.experimental import pallas as pl
from jax.experimental.pallas import tpu as pltpu

F32 = jnp.float32
BF16 = jnp.bfloat16
MESH = pl.DeviceIdType.MESH

D = 1024
DIN = 3072
GW = 256
PIECE = 768
N_CHIP = 4
N_DEV = 8
HALO = 32
CONV_A = 3
CONV_C = 31
CHUNK = 128
CONV_ROWS = 64
EPS = 1e-6
VMEM_LIMIT = 56 * 1024 * 1024

ADAM_LR, ADAM_B1, ADAM_B2, ADAM_EPS, ADAM_WD, ADAM_STEP = 0.001, 0.9, 0.999, 1e-08, 0.01, 10
GELU_K = math.sqrt(2.0 / math.pi)
GELU_C = 0.044715

WEIGHTS = ['norm_g', 'w_ada', 'b_ada', 'w_in', 'w_conv_a', 'w_pool', 'pool_scale', 'w_dw_c', 'b_dw_c', 'ln_g_c',
           'ln_b_c', 'w_pw2_c', 'b_pw2_c', 'ln_g_d', 'ln_b_d', 'w_s_d', 'b_s_d', 'w_out', 'final_g']
BIG = ('w_ada', 'w_in', 'w_out')


def _tiles(t):
    if t % 512 == 0 and t >= 2048:
        return 512, 256
    return 128, 128


def _params(sem, limit=VMEM_LIMIT):
    return pltpu.CompilerParams(dimension_semantics=sem, vmem_limit_bytes=limit)


def _sig(x):
    return jax.nn.sigmoid(x)


def _gelu(x):
    th = jnp.tanh(GELU_K * (x + GELU_C * x * x * x))
    return 0.5 * x * (1.0 + th), th


def _gelu_grad(x, th):
    return 0.5 * (1.0 + th) + 0.5 * x * (1.0 - th * th) * GELU_K * (1.0 + 3.0 * GELU_C * x * x)


def _dot(a, b):
    return jnp.dot(a, b, preferred_element_type=F32)


def _dot_nt(a, b):
    return lax.dot_general(a, b, (((1,), (1,)), ((), ())), preferred_element_type=F32)


def _dot_tn(a, b):
    return lax.dot_general(a, b, (((0,), (0,)), ((), ())), preferred_element_type=F32)


def _rowsum(x):
    return jnp.sum(x, axis=0, keepdims=True)


def _allgather8(v, name, reduce=False):
    m_per, n = v.shape

    def body(x_ref, out_ref, *rest):
        if reduce:
            sum_ref, send_sems, recv_sems, local_sem = rest
        else:
            send_sems, recv_sems, local_sem = rest
        x, y, c = lax.axis_index("x"), lax.axis_index("y"), lax.axis_index("c")
        me, sibling = (x, y, c), (x, y, 1 - c)
        chips = [(1 - x, y), (x, 1 - y), (1 - x, 1 - y)]

        def rows(px, py, pc):
            return out_ref.at[pl.ds((4 * px + 2 * py + pc) * m_per, m_per), :]

        def copy(k, block, to, src=None):
            return pltpu.make_async_remote_copy(
                src_ref=rows(*block) if src is None else src, dst_ref=rows(*block),
                send_sem=send_sems.at[k], recv_sem=recv_sems.at[k], device_id=to, device_id_type=MESH)

        mine = pltpu.make_async_copy(x_ref, rows(*me), local_sem)
        mine.start()
        first = [copy(0, me, sibling, src=x_ref)]
        first += [copy(1 + j, me, (*chip, c), src=x_ref) for j, chip in enumerate(chips)]
        for cp in first:
            cp.start()
        passed = [copy(4 + j, (*chip, c), sibling) for j, chip in enumerate(chips)]
        for j, chip in enumerate(chips):
            copy(1 + j, (*chip, c), me).wait_recv()
            passed[j].start()
        copy(0, sibling, me).wait_recv()
        for j, chip in enumerate(chips):
            copy(4 + j, (*chip, 1 - c), me).wait_recv()
        for cp in first + passed:
            cp.wait_send()
        mine.wait()
        if reduce:
            acc = out_ref[0:m_per, :]
            for d in range(1, N_DEV):
                acc = acc + out_ref[d * m_per:(d + 1) * m_per, :]
            sum_ref[...] = acc

    out_shape = [jax.ShapeDtypeStruct((N_DEV * m_per, n), v.dtype)]
    out_specs = [pl.BlockSpec(memory_space=pltpu.VMEM)]
    if reduce:
        out_shape.append(jax.ShapeDtypeStruct((m_per, n), v.dtype))
        out_specs.append(pl.BlockSpec(memory_space=pltpu.VMEM))
    res = pl.pallas_call(
        body, name=name, out_shape=tuple(out_shape),
        in_specs=[pl.BlockSpec(memory_space=pltpu.VMEM)], out_specs=tuple(out_specs),
        scratch_shapes=[pltpu.SemaphoreType.DMA((7,)), pltpu.SemaphoreType.DMA((7,)), pltpu.SemaphoreType.DMA],
        compiler_params=pltpu.CompilerParams(vmem_limit_bytes=VMEM_LIMIT),
    )(v)
    return res if reduce else res[0]


def _gather_weights(w_in_s, w_out_s):
    nl = w_in_s.shape[0]
    shapes = ((nl, N_CHIP) + w_in_s.shape[1:], (nl, N_CHIP) + w_out_s.shape[1:])
    n_sem = 2 * 3 * 2

    def body(win_ref, wout_ref, win_o, wout_o, send_sems, recv_sems):
        x, y, c = lax.axis_index("x"), lax.axis_index("y"), lax.axis_index("c")
        s = 2 * x + y
        sibling = (x, y, 1 - c)
        chips = [(1 - x, y), (x, 1 - y), (1 - x, 1 - y)]
        outs = (win_o, wout_o)
        for src, dst in ((win_ref, win_o), (wout_ref, wout_o)):
            rows = src.shape[1]
            step = min(rows, 256)
            for l in range(nl):
                for r in range(0, rows, step):
                    dst[l, s, r:r + step, :] = src[l, r:r + step, :].astype(BF16)

        def copy(k, ref, piece, half, to):
            r2 = ref.shape[2] // 2
            rows = ref.at[:, piece, pl.ds(half * r2, r2), :]
            return pltpu.make_async_remote_copy(
                src_ref=rows, dst_ref=rows, send_sem=send_sems.at[k], recv_sem=recv_sems.at[k],
                device_id=to, device_id_type=MESH)

        sends = [copy(2 * j + a, ref, s, c, (px, py, c)) for j, (px, py) in enumerate(chips)
                 for a, ref in enumerate(outs)]
        for cp in sends:
            cp.start()
        passed = []
        for j, (px, py) in enumerate(chips):
            for a, ref in enumerate(outs):
                copy(2 * j + a, ref, 2 * px + py, c, (px, py, c)).wait_recv()
                passed.append(copy(6 + 2 * j + a, ref, 2 * px + py, c, sibling))
                passed[-1].start()
        for j, (px, py) in enumerate(chips):
            for a, ref in enumerate(outs):
                copy(6 + 2 * j + a, ref, 2 * px + py, 1 - c, sibling).wait_recv()
        for cp in sends + passed:
            cp.wait_send()

    return pl.pallas_call(
        body, name="gather_weights",
        out_shape=tuple(jax.ShapeDtypeStruct(s, BF16) for s in shapes),
        in_specs=[pl.BlockSpec(memory_space=pltpu.VMEM)] * 2,
        out_specs=tuple(pl.BlockSpec(memory_space=pltpu.VMEM) for _ in shapes),
        scratch_shapes=[pltpu.SemaphoreType.DMA((n_sem,)), pltpu.SemaphoreType.DMA((n_sem,))],
        compiler_params=pltpu.CompilerParams(vmem_limit_bytes=VMEM_LIMIT),
    )(w_in_s, w_out_s)


def _send_half_to_sibling(g, tag):
    nl, nc, r, cdim = g.shape
    half = r // 2

    def body(g_ref, recv_ref, send_sem, recv_sem):
        x, y, c = lax.axis_index("x"), lax.axis_index("y"), lax.axis_index("c")
        cp = pltpu.make_async_remote_copy(
            src_ref=g_ref.at[:, :, pl.ds((1 - c) * half, half), :], dst_ref=recv_ref,
            send_sem=send_sem, recv_sem=recv_sem, device_id=(x, y, 1 - c), device_id_type=MESH)
        cp.start()
        cp.wait()

    return pl.pallas_call(
        body, name=f"rs_sibling_{tag}",
        out_shape=jax.ShapeDtypeStruct((nl, nc, half, cdim), g.dtype),
        in_specs=[pl.BlockSpec(memory_space=pl.ANY)], out_specs=pl.BlockSpec(memory_space=pl.ANY),
        scratch_shapes=[pltpu.SemaphoreType.DMA, pltpu.SemaphoreType.DMA],
    )(g)


def _add_half(g, recv, cidx, tag):
    nl, nc, r, cdim = g.shape
    half = r // 2
    rt = min(half, 128)
    nrt = half // rt

    def body(c_ref, a_ref, b_ref, o_ref, ob_ref):
        c = c_ref[0]
        first = jnp.where(c == 0, a_ref[...], b_ref[...])
        second = jnp.where(c == 0, b_ref[...], a_ref[...])
        total = first + second
        o_ref[...] = total
        ob_ref[...] = total.astype(BF16)

    out_spec = pl.BlockSpec((None, None, rt, cdim), lambda l, j, i, c: (l, j, i, 0))
    grid_spec = pltpu.PrefetchScalarGridSpec(
        num_scalar_prefetch=1, grid=(nl, nc, nrt),
        in_specs=[pl.BlockSpec((None, None, rt, cdim), lambda l, j, i, c: (l, j, c[0] * nrt + i, 0)),
                  pl.BlockSpec((None, None, rt, cdim), lambda l, j, i, c: (l, j, i, 0))],
        out_specs=(out_spec, out_spec))
    return pl.pallas_call(
        body, name=f"rs_add_sibling_{tag}", grid_spec=grid_spec,
        out_shape=(jax.ShapeDtypeStruct((nl, nc, half, cdim), g.dtype),
                   jax.ShapeDtypeStruct((nl, nc, half, cdim), BF16)),
        compiler_params=_params(("parallel", "parallel", "parallel")),
    )(cidx, g, recv)


def _exchange_chips(h, tag):
    nl, nc, r2, cdim = h.shape

    def body(h_ref, recv_ref, send_sems, recv_sems):
        x, y, c = lax.axis_index("x"), lax.axis_index("y"), lax.axis_index("c")
        chips = [(1 - x, y), (x, 1 - y), (1 - x, 1 - y)]
        cps = []
        for k, (px, py) in enumerate(chips):
            cps.append(pltpu.make_async_remote_copy(
                src_ref=h_ref.at[:, 2 * px + py], dst_ref=recv_ref.at[k], send_sem=send_sems.at[k],
                recv_sem=recv_sems.at[k], device_id=(px, py, c), device_id_type=MESH))
        for cp in cps:
            cp.start()
        for cp in cps:
            cp.wait()

    return pl.pallas_call(
        body, name=f"rs_chips_{tag}",
        out_shape=jax.ShapeDtypeStruct((3, nl, r2, cdim), h.dtype),
        in_specs=[pl.BlockSpec(memory_space=pl.ANY)], out_specs=pl.BlockSpec(memory_space=pl.ANY),
        scratch_shapes=[pltpu.SemaphoreType.DMA((3,)), pltpu.SemaphoreType.DMA((3,))],
    )(h)


def _add_chips(h, recv, sidx, tag):
    nl, nc, r2, cdim = h.shape
    rt = min(r2, 128)

    def body(s_ref, own_ref, r_ref, o_ref):
        s = s_ref[0]
        got = [r_ref[k].astype(F32) for k in range(3)]
        total = None
        for chip in range(N_CHIP):
            term = jnp.where(s == chip, own_ref[...],
                             jnp.where((s ^ 2) == chip, got[0], jnp.where((s ^ 1) == chip, got[1], got[2])))
            total = term if total is None else total + term
        o_ref[...] = total

    grid_spec = pltpu.PrefetchScalarGridSpec(
        num_scalar_prefetch=1, grid=(nl, r2 // rt),
        in_specs=[pl.BlockSpec((None, None, rt, cdim), lambda l, i, s: (l, s[0], i, 0)),
                  pl.BlockSpec((3, None, rt, cdim), lambda l, i, s: (0, l, i, 0))],
        out_specs=pl.BlockSpec((None, rt, cdim), lambda l, i, s: (l, i, 0)))
    return pl.pallas_call(
        body, name=f"rs_add_chips_{tag}", grid_spec=grid_spec,
        out_shape=jax.ShapeDtypeStruct((nl, r2, cdim), h.dtype),
        compiler_params=_params(("parallel", "parallel")),
    )(sidx, h, recv)


def _join_halves(red, tag):
    nl, r2, cdim = red.shape

    def body(red_ref, out_ref, send_sem, recv_sem, local_sem):
        x, y, c = lax.axis_index("x"), lax.axis_index("y"), lax.axis_index("c")
        mine = out_ref.at[:, pl.ds(c * r2, r2), :]
        local = pltpu.make_async_copy(red_ref, mine, local_sem)
        local.start()
        cp = pltpu.make_async_remote_copy(
            src_ref=red_ref, dst_ref=mine, send_sem=send_sem, recv_sem=recv_sem,
            device_id=(x, y, 1 - c), device_id_type=MESH)
        cp.start()
        theirs = pltpu.make_async_remote_copy(
            src_ref=red_ref, dst_ref=out_ref.at[:, pl.ds((1 - c) * r2, r2), :], send_sem=send_sem,
            recv_sem=recv_sem, device_id=(x, y, 1 - c), device_id_type=MESH)
        theirs.wait_recv()
        cp.wait_send()
        local.wait()

    return pl.pallas_call(
        body, name=f"rs_join_{tag}",
        out_shape=jax.ShapeDtypeStruct((nl, 2 * r2, cdim), red.dtype),
        in_specs=[pl.BlockSpec(memory_space=pl.ANY)], out_specs=pl.BlockSpec(memory_space=pl.ANY),
        scratch_shapes=[pltpu.SemaphoreType.DMA, pltpu.SemaphoreType.DMA, pltpu.SemaphoreType.DMA],
    )(red)


def _reduce_scatter(g, cidx, sidx, tag):
    recv = _send_half_to_sibling(g, tag)
    chip_sum, chip_sum_bf16 = _add_half(g, recv, cidx, tag)
    got = _exchange_chips(chip_sum_bf16, tag)
    red = _add_chips(chip_sum, got, sidx, tag)
    return _join_halves(red, tag)


def _ada_forward(c_all, w_ada_s, b_s):
    nl = w_ada_s.shape[0]

    def body(c_ref, w_ref, b_ref, o_ref):
        cv = c_ref[...]
        ca = (cv * _sig(cv)).astype(BF16)
        for l in range(nl):
            o_ref[l] = _dot(ca, w_ref[l].astype(BF16)) + b_ref[l]

    return pl.pallas_call(
        body, name="ada_forward", out_shape=jax.ShapeDtypeStruct((nl, N_DEV, PIECE), F32),
        compiler_params=pltpu.CompilerParams(vmem_limit_bytes=VMEM_LIMIT),
    )(c_all, w_ada_s, b_s)


def _ada_backward(c_all, dmod_s):
    nl = dmod_s.shape[0]

    def body(c_ref, d_ref, o_ref):
        cv = c_ref[...]
        ca = (cv * _sig(cv)).astype(BF16)
        for l in range(nl):
            o_ref[l] = _dot_tn(ca, d_ref[l].astype(BF16))

    return pl.pallas_call(
        body, name="ada_backward", out_shape=jax.ShapeDtypeStruct((nl, D, PIECE), F32),
        compiler_params=pltpu.CompilerParams(vmem_limit_bytes=VMEM_LIMIT),
    )(c_all, dmod_s)


def _normed(xv, vecs):
    r = lax.rsqrt(jnp.mean(xv * xv, axis=-1, keepdims=True) + EPS)
    xhat = xv * r
    hn = xhat * vecs[3:4, :]
    return r, xhat, hn


def _fwd_in(x, vecs, w, l, tt):
    t = x.shape[0]

    def body(x_ref, v_ref, w_ref, z_ref):
        vecs_v = v_ref[...]
        _, _, hn = _normed(x_ref[...], vecs_v)
        hb = (hn * (1.0 + vecs_v[1:2, :]) + vecs_v[0:1, :]).astype(BF16)
        for j in range(N_CHIP):
            z_ref[:, PIECE * j:PIECE * (j + 1)] = _dot(hb, w_ref[j])

    return pl.pallas_call(
        body, name=f"fwd_in_{l}", grid=(t // tt,),
        in_specs=[pl.BlockSpec((tt, D), lambda i: (i, 0)), pl.BlockSpec((8, D), lambda i: (0, 0)),
                  pl.BlockSpec((None, N_CHIP, D, PIECE), lambda i: (l, 0, 0, 0))],
        out_specs=pl.BlockSpec((tt, DIN), lambda i: (i, 0)),
        out_shape=jax.ShapeDtypeStruct((t, DIN), F32),
        compiler_params=_params(("parallel",)),
    )(x, vecs, w)


def _bwd_in(dz, x, dxo, vecs, w, l, tt):
    t = x.shape[0]
    nt = t // tt

    def body(dz_ref, x_ref, dxo_ref, v_ref, w_any, dx_ref, gw_any, sums_ref, w_vmem, gw_acc, sem):
        i = pl.program_id(0)

        @pl.when(i == 0)
        def _():
            cp = pltpu.make_async_copy(w_any.at[l], w_vmem, sem)
            cp.start()
            cp.wait()
            gw_acc[...] = jnp.zeros_like(gw_acc)
            sums_ref[...] = jnp.zeros_like(sums_ref)

        vecs_v = v_ref[...]
        r, xhat, hn = _normed(x_ref[...], vecs_v)
        one_scale = 1.0 + vecs_v[1:2, :]
        hb = (hn * one_scale + vecs_v[0:1, :]).astype(BF16)
        dh = None
        for j in range(N_CHIP):
            dzj = dz_ref[:, PIECE * j:PIECE * (j + 1)]
            part = _dot_nt(dzj, w_vmem[j])
            dh = part if dh is None else dh + part
            gw_acc[j] += _dot_tn(hb, dzj)
        sums_ref[0:1, :] += _rowsum(dh)
        sums_ref[1:2, :] += _rowsum(dh * hn)
        sums_ref[2:3, :] += _rowsum(dh * one_scale * xhat)
        dxh = dh * (vecs_v[3:4, :] * one_scale)
        dx_ref[...] = dxo_ref[...] + r * (dxh - xhat * jnp.mean(dxh * xhat, axis=-1, keepdims=True))

        @pl.when(i == nt - 1)
        def _():
            cp = pltpu.make_async_copy(gw_acc, gw_any, sem)
            cp.start()
            cp.wait()

    return pl.pallas_call(
        body, name=f"bwd_in_{l}", grid=(nt,),
        in_specs=[pl.BlockSpec((tt, DIN), lambda i: (i, 0)), pl.BlockSpec((tt, D), lambda i: (i, 0)),
                  pl.BlockSpec((tt, D), lambda i: (i, 0)), pl.BlockSpec((8, D), lambda i: (0, 0)),
                  pl.BlockSpec(memory_space=pl.ANY)],
        out_specs=(pl.BlockSpec((tt, D), lambda i: (i, 0)), pl.BlockSpec(memory_space=pl.ANY),
                   pl.BlockSpec((8, D), lambda i: (0, 0))),
        out_shape=(jax.ShapeDtypeStruct((t, D), F32), jax.ShapeDtypeStruct((N_CHIP, D, PIECE), F32),
                   jax.ShapeDtypeStruct((8, D), F32)),
        scratch_shapes=[pltpu.VMEM((N_CHIP, D, PIECE), BF16), pltpu.VMEM((N_CHIP, D, PIECE), F32),
                        pltpu.SemaphoreType.DMA],
        compiler_params=_params(("arbitrary",)),
    )(dz, x, dxo, vecs, w)


def _col(ref, k):
    return ref[:, GW * k:GW * (k + 1)]


def _lane_group():
    return lax.broadcasted_iota(jnp.int32, (1, GW), 1) // (GW // 4)


def _pool_window(group):
    return jnp.where(group == 0, 2.0, jnp.where(group == 1, 4.0, jnp.where(group == 2, 8.0, 16.0)))


def _by_group(group, a, b, c, d):
    return jnp.where(group == 0, a, jnp.where(group == 1, b, jnp.where(group == 2, c, d)))


def _layer_norm(x, g, b):
    mu = jnp.mean(x, axis=-1, keepdims=True)
    xc = x - mu
    rstd = lax.rsqrt(jnp.mean(xc * xc, axis=-1, keepdims=True) + EPS)
    xh = xc * rstd
    return xh * g + b, xh, rstd


def _layer_norm_grad(d_out, xh, rstd, g):
    dxh = d_out * g
    return rstd * (dxh - jnp.mean(dxh, axis=-1, keepdims=True) - xh * jnp.mean(dxh * xh, axis=-1, keepdims=True))


def _shifted_copies(ext_ref, sh_ref, n):
    for b in range(1, 8):
        sh_ref[b, 0:n - 8, :] = ext_ref[b:b + n - 8, :]


def _rows_at(ext_ref, sh_ref, start, rows):
    b = start % 8
    if b == 0 or sh_ref is None:
        return ext_ref[start:start + rows, :]
    return sh_ref[b, start - b:start - b + rows, :]


def _conv_taps(ext_ref, w_ref, n_taps, first_row, tm, out_ref, flip=False, sh_ref=None):
    for rb in range(0, tm, CONV_ROWS):
        acc = None
        for k in range(n_taps):
            wk = n_taps - 1 - k if flip else k
            term = w_ref[wk:wk + 1, :] * _rows_at(ext_ref, sh_ref, first_row + rb + k, CONV_ROWS)
            acc = term if acc is None else acc + term
        out_ref[rb:rb + CONV_ROWS, :] = acc


def _mix_forward(zc, zp, first, row0, tm, p, s):
    f = {}
    keep = jnp.where(first, 0.0, 1.0)
    group = _lane_group()
    a_b, a_c, a_x, a_g = _col(zc, 0), _col(zc, 1), _col(zc, 2), _col(zc, 3)
    s['ua'][0:HALO, :] = _col(zp, 1) * _col(zp, 2) * keep
    s['ua'][HALO:HALO + tm, :] = a_c * a_x
    _conv_taps(s['ua'], p['wa'], CONV_A, HALO - (CONV_A - 1), tm, s['cv'])
    cv = s['cv'][...]
    sg_a = _sig(a_g)
    f.update(a_b=a_b, a_c=a_c, a_x=a_x, a_g=a_g, cv=cv, sg_a=sg_a)
    y_a = a_b * cv * (a_g * sg_a)
    b_p, b_g = _col(zc, 4), _col(zc, 5)
    s['p0'][0:HALO, :] = _col(zp, 4) * keep
    s['p0'][HALO:HALO + tm, :] = b_p
    n = HALO + tm
    s['p1'][8:n, :] = s['p0'][8:n, :] + s['p0'][7:n - 1, :]
    w2 = s['p1'][HALO:n, :]
    s['p0'][16:n, :] = s['p1'][16:n, :] + s['p1'][14:n - 2, :]
    w4 = s['p0'][HALO:n, :]
    s['p1'][24:n, :] = s['p0'][24:n, :] + s['p0'][20:n - 4, :]
    w8 = s['p1'][HALO:n, :]
    w16 = w8 + s['p1'][HALO - 8:n - 8, :]
    tpos = (row0 + lax.broadcasted_iota(jnp.int32, (tm, 1), 0) + 1).astype(F32)
    inv_cnt = 1.0 / jnp.minimum(tpos, _pool_window(group))
    pooled = (_by_group(group, w2, w4, w8, w16) * inv_cnt - b_p).astype(BF16)
    q = _dot(pooled, p['wbd'][...])
    sg_b = _sig(b_g)
    f.update(b_g=b_g, pooled=pooled, q=q, sg_b=sg_b, inv_cnt=inv_cnt)
    y_b = q * p['v256'][0:1, :] * (b_g * sg_b)
    c_a, c_gl, c_g = _col(zc, 6), _col(zc, 7), _col(zc, 8)
    sg_gl = _sig(c_gl)
    zp_gl = _col(zp, 7)
    s['hc'][0:HALO, :] = _col(zp, 6) * _sig(zp_gl) * keep
    s['hc'][HALO:HALO + tm, :] = c_a * sg_gl
    _shifted_copies(s['hc'], s['hcs'], HALO + tm)
    _conv_taps(s['hc'], p['wdw'], CONV_C, HALO - (CONV_C - 1), tm, s['co'], sh_ref=s['hcs'])
    co = s['co'][...] + p['v256'][1:2, :]
    ln_c, xh_c, rstd_c = _layer_norm(co, p['v256'][2:3, :], p['v256'][3:4, :])
    sg_ln = _sig(ln_c)
    sc = (ln_c * sg_ln).astype(BF16)
    pw = _dot(sc, p['wpw'][...]) + p['v256'][4:5, :]
    sg_c = _sig(c_g)
    f.update(c_a=c_a, c_g=c_g, sg_gl=sg_gl, ln_c=ln_c, xh_c=xh_c, rstd_c=rstd_c, sg_ln=sg_ln, sc=sc, pw=pw,
             sg_c=sg_c)
    y_c = pw * (c_g * sg_c)
    d_u, d_v, d_g = _col(zc, 9), _col(zc, 10), _col(zc, 11)
    gu, th_u = _gelu(d_u)
    gv, th_v = _gelu(d_v)
    v, xh_d, rstd_d = _layer_norm(gv, p['v256'][5:6, :], p['v256'][6:7, :])
    vb = v.astype(BF16)
    parts = []
    for ch in range(tm // CHUNK):
        vc = vb[ch * CHUNK:(ch + 1) * CHUNK, :]
        mixed = p['bsd'][...]
        for h in range(4):
            mixed = mixed + jnp.where(group == h, _dot(p['ws'][h], vc), 0.0)
        parts.append(mixed)
    mx = parts[0] if len(parts) == 1 else jnp.concatenate(parts, axis=0)
    sg_d = _sig(d_g)
    f.update(d_u=d_u, d_v=d_v, d_g=d_g, gu=gu, th_u=th_u, th_v=th_v, xh_d=xh_d, rstd_d=rstd_d, vb=vb, mx=mx,
             sg_d=sg_d)
    y_d = gu * mx * (d_g * sg_d)
    f['ys'] = [y.astype(BF16) for y in (y_a, y_b, y_c, y_d)]
    return f


def _mix_scratch(tm):
    ext = pltpu.VMEM((HALO + tm, GW), F32)
    tile = pltpu.VMEM((tm, GW), F32)
    return dict(ua=ext, cv=tile, p0=ext, p1=ext, hc=ext, co=tile, hcs=pltpu.VMEM((8, HALO + tm, GW), F32))


_MIX_PARAMS = ('wa', 'wdw', 'v256', 'wbd', 'wpw', 'ws', 'bsd', 'wout')


def _mix_param_specs(l):
    def whole(*shape):
        nd = len(shape)
        return pl.BlockSpec((None,) + shape, lambda i, _nd=nd: (l,) + (0,) * _nd)
    return [whole(8, GW), whole(32, GW), whole(16, GW), whole(GW, GW), whole(GW, GW), whole(4, CHUNK, CHUNK),
            whole(CHUNK, GW), whole(N_CHIP, GW, D)]


def _fwd_mix(z, x, vecs, mp, l, tm):
    t = x.shape[0]
    per_halo = tm // HALO
    names = list(_mix_scratch(tm))

    def body(zc, zp, x_ref, v_ref, *rest):
        p = dict(zip(_MIX_PARAMS, rest[:len(_MIX_PARAMS)]))
        xn_ref, y_ref = rest[len(_MIX_PARAMS):len(_MIX_PARAMS) + 2]
        s = dict(zip(names, rest[len(_MIX_PARAMS) + 2:]))
        i = pl.program_id(0)
        f = _mix_forward(zc, zp, i == 0, i * tm, tm, p, s)
        y = None
        for j in range(4):
            part = _dot(f['ys'][j], p['wout'][j])
            y = part if y is None else y + part
        y_ref[...] = y
        xn_ref[...] = x_ref[...] + v_ref[2:3, :] * y

    return pl.pallas_call(
        body, name=f"fwd_mix_{l}", grid=(t // tm,),
        in_specs=[pl.BlockSpec((tm, DIN), lambda i: (i, 0)),
                  pl.BlockSpec((HALO, DIN), lambda i: (jnp.maximum(i * per_halo - 1, 0), 0)),
                  pl.BlockSpec((tm, D), lambda i: (i, 0)), pl.BlockSpec((8, D), lambda i: (0, 0))]
        + _mix_param_specs(l),
        out_specs=(pl.BlockSpec((tm, D), lambda i: (i, 0)), pl.BlockSpec((tm, D), lambda i: (i, 0))),
        out_shape=(jax.ShapeDtypeStruct((t, D), F32), jax.ShapeDtypeStruct((t, D), F32)),
        scratch_shapes=list(_mix_scratch(tm).values()),
        compiler_params=_params(("parallel",)),
    )(z, z, x, vecs, *[mp[k] for k in _MIX_PARAMS])


def _bwd_mix(dxo, y, z, vecs, mp, wst, l, tm):
    t = dxo.shape[0]
    nt = t // tm
    per_halo = tm // HALO
    fwd_names = list(_mix_scratch(tm))
    ext = pltpu.VMEM((tm + HALO, GW), F32)
    carry = pltpu.VMEM((HALO, GW), F32)
    tile = pltpu.VMEM((tm, GW), F32)
    bwd_scratch = dict(ea=ext, eb=ext, eb2=ext, ec=ext, ca=carry, cb=carry, cc=carry, dua=tile, dhc=tile,
                       gbacc=pltpu.VMEM((CHUNK, GW), F32), ecs=pltpu.VMEM((8, tm + HALO, GW), F32))
    bwd_names = list(bwd_scratch)
    n_p = len(_MIX_PARAMS)

    def body(dxo_ref, y_ref, zc, zp, v_ref, *rest):
        p = dict(zip(_MIX_PARAMS, rest[:n_p]))
        wst_ref = rest[n_p]
        outs = rest[n_p + 1:n_p + 10]
        dz_ref, gwout, gwbd, gwpw, gws, gbs, gwdw, g256, g1024 = outs
        s = dict(zip(fwd_names + bwd_names, rest[n_p + 10:]))
        i = pl.program_id(0)
        ti = nt - 1 - i
        group = _lane_group()

        @pl.when(i == 0)
        def _():
            for ref in (gwout, gwbd, gwpw, gws, gbs, gwdw, g256, g1024, s['ca'], s['cb'], s['cc'], s['gbacc']):
                ref[...] = jnp.zeros_like(ref)

        f = _mix_forward(zc, zp, ti == 0, ti * tm, tm, p, s)
        dxo_v = dxo_ref[...]
        g1024[0:1, :] += _rowsum(dxo_v * y_ref[...])
        dy = (dxo_v * v_ref[2:3, :]).astype(BF16)
        dys = []
        for j in range(4):
            gwout[j] += _dot_tn(f['ys'][j], dy)
            dys.append(_dot_nt(dy, p['wout'][j]))
        dya, dyb, dyc, dyd = dys

        def put(k, val):
            dz_ref[:, GW * k:GW * (k + 1)] = val.astype(BF16)

        a_b, a_c, a_x, a_g, cv, sg_a = f['a_b'], f['a_c'], f['a_x'], f['a_g'], f['cv'], f['sg_a']
        silu_a = a_g * sg_a
        put(0, dya * cv * silu_a)
        put(3, dya * a_b * cv * (sg_a * (1.0 + a_g * (1.0 - sg_a))))
        d_cv = dya * a_b * silu_a
        for k in range(CONV_A):
            g256[8 + k:9 + k, :] += _rowsum(d_cv * s['ua'][HALO - (CONV_A - 1) + k:HALO - (CONV_A - 1) + k + tm, :])
        s['ea'][0:tm, :] = d_cv
        s['ea'][tm:tm + HALO, :] = s['ca'][...]
        s['ca'][...] = d_cv[0:HALO, :]
        _conv_taps(s['ea'], p['wa'], CONV_A, 0, tm, s['dua'], flip=True)
        d_u = s['dua'][...]
        put(1, d_u * a_x)
        put(2, d_u * a_c)
        b_g, q, sg_b, inv_cnt = f['b_g'], f['q'], f['sg_b'], f['inv_cnt']
        scale_b = p['v256'][0:1, :]
        silu_b = b_g * sg_b
        g256[0:1, :] += _rowsum(dyb * q * silu_b)
        put(5, dyb * q * scale_b * (sg_b * (1.0 + b_g * (1.0 - sg_b))))
        d_q = (dyb * scale_b * silu_b).astype(BF16)
        gwbd[...] += _dot_tn(f['pooled'], d_q)
        d_pooled = _dot_nt(d_q, p['wbd'][...])
        gp = d_pooled * inv_cnt
        n = tm + HALO
        s['eb'][0:tm, :] = gp
        s['eb'][tm:n, :] = s['cb'][...]
        s['cb'][...] = gp[0:HALO, :]
        s['eb2'][0:n - 8, :] = s['eb'][0:n - 8, :] + s['eb'][1:n - 7, :]
        r2 = s['eb2'][0:tm, :]
        s['eb'][0:n - 16, :] = s['eb2'][0:n - 16, :] + s['eb2'][2:n - 14, :]
        r4 = s['eb'][0:tm, :]
        s['eb2'][0:n - 24, :] = s['eb'][0:n - 24, :] + s['eb'][4:n - 20, :]
        r8 = s['eb2'][0:tm, :]
        r16 = r8 + s['eb2'][8:tm + 8, :]
        put(4, _by_group(group, r2, r4, r8, r16) - d_pooled)
        c_a, c_g, sg_gl, ln_c, xh_c, rstd_c, sg_ln, pw, sg_c = (
            f['c_a'], f['c_g'], f['sg_gl'], f['ln_c'], f['xh_c'], f['rstd_c'], f['sg_ln'], f['pw'], f['sg_c'])
        put(8, dyc * pw * (sg_c * (1.0 + c_g * (1.0 - sg_c))))
        d_pw = dyc * (c_g * sg_c)
        g256[4:5, :] += _rowsum(d_pw)
        d_pwb = d_pw.astype(BF16)
        gwpw[...] += _dot_tn(f['sc'], d_pwb)
        d_ln = _dot_nt(d_pwb, p['wpw'][...]) * (sg_ln * (1.0 + ln_c * (1.0 - sg_ln)))
        g256[2:3, :] += _rowsum(d_ln * xh_c)
        g256[3:4, :] += _rowsum(d_ln)
        d_co = _layer_norm_grad(d_ln, xh_c, rstd_c, p['v256'][2:3, :])
        g256[1:2, :] += _rowsum(d_co)
        for k in range(CONV_C):
            gwdw[k:k + 1, :] += _rowsum(d_co * _rows_at(s['hc'], s['hcs'], HALO - (CONV_C - 1) + k, tm))
        s['ec'][0:tm, :] = d_co
        s['ec'][tm:tm + HALO, :] = s['cc'][...]
        s['cc'][...] = d_co[0:HALO, :]
        _shifted_copies(s['ec'], s['ecs'], tm + HALO)
        _conv_taps(s['ec'], p['wdw'], CONV_C, 0, tm, s['dhc'], flip=True, sh_ref=s['ecs'])
        d_hc = s['dhc'][...]
        put(6, d_hc * sg_gl)
        put(7, d_hc * c_a * sg_gl * (1.0 - sg_gl))
        d_uu, d_vv, d_g, gu, th_u, th_v, xh_d, rstd_d, vb, mx, sg_d = (
            f['d_u'], f['d_v'], f['d_g'], f['gu'], f['th_u'], f['th_v'], f['xh_d'], f['rstd_d'], f['vb'], f['mx'],
            f['sg_d'])
        silu_d = d_g * sg_d
        put(9, dyd * mx * silu_d * _gelu_grad(d_uu, th_u))
        put(11, dyd * gu * mx * (sg_d * (1.0 + d_g * (1.0 - sg_d))))
        d_mx = dyd * gu * silu_d
        dv_parts = []
        gb = None
        for ch in range(tm // CHUNK):
            dmc = d_mx[ch * CHUNK:(ch + 1) * CHUNK, :]
            gb = dmc if gb is None else gb + dmc
            dmb = dmc.astype(BF16)
            vc = vb[ch * CHUNK:(ch + 1) * CHUNK, :]
            dvc = None
            for h in range(4):
                gws[h] += _dot_nt(jnp.where(group == h, dmb, jnp.zeros_like(dmb)), vc)
                part = jnp.where(group == h, _dot(wst_ref[h], dmb), 0.0)
                dvc = part if dvc is None else dvc + part
            dv_parts.append(dvc)
        s['gbacc'][...] += gb
        d_v = dv_parts[0] if len(dv_parts) == 1 else jnp.concatenate(dv_parts, axis=0)
        g256[5:6, :] += _rowsum(d_v * xh_d)
        g256[6:7, :] += _rowsum(d_v)
        d_gv = _layer_norm_grad(d_v, xh_d, rstd_d, p['v256'][5:6, :])
        put(10, d_gv * _gelu_grad(d_vv, th_v))

        @pl.when(i == nt - 1)
        def _():
            row = lax.broadcasted_iota(jnp.int32, (CHUNK, CHUNK), 0)
            col = lax.broadcasted_iota(jnp.int32, (CHUNK, CHUNK), 1)
            for h in range(4):
                gws[h] = jnp.where(col <= row, gws[h], 0.0)
            head_of_lane = lax.broadcasted_iota(jnp.int32, (GW, CHUNK), 0) // (GW // 4)
            sel = jnp.where(head_of_lane == lax.broadcasted_iota(jnp.int32, (GW, CHUNK), 1), 1.0, 0.0)
            gbs[...] = jnp.dot(s['gbacc'][...], sel, preferred_element_type=F32, precision=lax.Precision.HIGHEST)

    def const(*shape):
        nd = len(shape)
        return pl.BlockSpec(shape, lambda i, _nd=nd: (0,) * _nd)

    def rev(i):
        return nt - 1 - i

    out_shapes = (jax.ShapeDtypeStruct((t, DIN), BF16), jax.ShapeDtypeStruct((N_CHIP, GW, D), F32),
                  jax.ShapeDtypeStruct((GW, GW), F32), jax.ShapeDtypeStruct((GW, GW), F32),
                  jax.ShapeDtypeStruct((4, CHUNK, CHUNK), F32), jax.ShapeDtypeStruct((CHUNK, CHUNK), F32),
                  jax.ShapeDtypeStruct((32, GW), F32), jax.ShapeDtypeStruct((16, GW), F32),
                  jax.ShapeDtypeStruct((8, D), F32))
    out_specs = (pl.BlockSpec((tm, DIN), lambda i: (rev(i), 0)), const(N_CHIP, GW, D), const(GW, GW),
                 const(GW, GW), const(4, CHUNK, CHUNK), const(CHUNK, CHUNK), const(32, GW), const(16, GW),
                 const(8, D))
    scratch = list(_mix_scratch(tm).values()) + list(bwd_scratch.values())
    return pl.pallas_call(
        body, name=f"bwd_mix_{l}", grid=(nt,),
        in_specs=[pl.BlockSpec((tm, D), lambda i: (rev(i), 0)), pl.BlockSpec((tm, D), lambda i: (rev(i), 0)),
                  pl.BlockSpec((tm, DIN), lambda i: (rev(i), 0)),
                  pl.BlockSpec((HALO, DIN), lambda i: (jnp.maximum(rev(i) * per_halo - 1, 0), 0)),
                  pl.BlockSpec((8, D), lambda i: (0, 0))]
        + _mix_param_specs(l)
        + [pl.BlockSpec((None, 4, CHUNK, CHUNK), lambda i: (l, 0, 0, 0))],
        out_specs=out_specs, out_shape=out_shapes, scratch_shapes=scratch,
        compiler_params=_params(("arbitrary",)),
    )(dxo, y, z, z, vecs, *[mp[k] for k in _MIX_PARAMS], wst)


def _final(x, target, fg, tt):
    t = x.shape[0]
    nt = t // tt

    def body(x_ref, t_ref, g_ref, dx_ref, sums_ref):
        i = pl.program_id(0)

        @pl.when(i == 0)
        def _():
            sums_ref[...] = jnp.zeros_like(sums_ref)

        xv = x_ref[...]
        g = g_ref[0:1, :]
        r = lax.rsqrt(jnp.mean(xv * xv, axis=-1, keepdims=True) + EPS)
        xhat = xv * r
        err = xhat * g - t_ref[...]
        sums_ref[1:2, :] += _rowsum(err * err) * (0.5 / D)
        dyv = err * (1.0 / D)
        sums_ref[0:1, :] += _rowsum(dyv * xhat)
        dxh = dyv * g
        dx_ref[...] = r * (dxh - xhat * jnp.mean(dxh * xhat, axis=-1, keepdims=True))

        @pl.when(i == nt - 1)
        def _():
            sums_ref[2:3, :] = jnp.broadcast_to(jnp.sum(sums_ref[1:2, :], axis=-1, keepdims=True), (1, D))

    return pl.pallas_call(
        body, name="final_loss", grid=(nt,),
        in_specs=[pl.BlockSpec((tt, D), lambda i: (i, 0)), pl.BlockSpec((tt, D), lambda i: (i, 0)),
                  pl.BlockSpec((8, D), lambda i: (0, 0))],
        out_specs=(pl.BlockSpec((tt, D), lambda i: (i, 0)), pl.BlockSpec((8, D), lambda i: (0, 0))),
        out_shape=(jax.ShapeDtypeStruct((t, D), F32), jax.ShapeDtypeStruct((8, D), F32)),
        compiler_params=_params(("arbitrary",)),
    )(x, target, fg)


def _adamw(w, g, m, v, tag):
    rows, cols = w.shape
    rt = rows
    for cand in (512, 256, 128, 64, 32, 16, 8):
        if rows % cand == 0:
            rt = cand
            break

    def body(w_ref, g_ref, m_ref, v_ref, d_ref, nm_ref, nv_ref):
        gv = g_ref[...]
        nm = ADAM_B1 * m_ref[...] + (1.0 - ADAM_B1) * gv
        nv = ADAM_B2 * v_ref[...] + (1.0 - ADAM_B2) * (gv * gv)
        m_hat = nm / (1.0 - ADAM_B1 ** ADAM_STEP)
        v_hat = nv / (1.0 - ADAM_B2 ** ADAM_STEP)
        d_ref[...] = -ADAM_LR * (m_hat / (jnp.sqrt(v_hat) + ADAM_EPS) + ADAM_WD * w_ref[...])
        nm_ref[...] = nm
        nv_ref[...] = nv

    spec = pl.BlockSpec((rt, cols), lambda i: (i, 0))
    return pl.pallas_call(
        body, name=f"adamw_{tag}", grid=(rows // rt,), in_specs=[spec] * 4, out_specs=(spec,) * 3,
        out_shape=(jax.ShapeDtypeStruct(w.shape, F32),) * 3, compiler_params=_params(("parallel",)),
    )(w, g, m, v)


def _pack(arrays, row_multiple=8):
    parts, offsets, off = [], [], 0
    for a in arrays:
        n = int(np.prod(a.shape))
        padded = -(-n // 1024) * 1024
        flat = a.reshape(-1).astype(F32)
        if padded != n:
            flat = jnp.concatenate([flat, jnp.zeros((padded - n,), F32)])
        parts.append(flat)
        offsets.append((off, n, a.shape))
        off += padded
    tail = -off % (row_multiple * 128)
    if tail:
        parts.append(jnp.zeros((tail,), F32))
    return jnp.concatenate(parts).reshape(-1, 128), offsets


def _unpack(buf, offsets):
    flat = buf.reshape(-1)
    return [flat[off:off + n].reshape(shape) for off, n, shape in offsets]


def _pad_rows(a, rows):
    return jnp.concatenate([a, jnp.zeros((rows - a.shape[0],) + a.shape[1:], a.dtype)], axis=0)


def kernel(x, c, norm_g, w_ada, b_ada, w_in, w_conv_a, w_pool, pool_scale, w_dw_c, b_dw_c, ln_g_c, ln_b_c, w_pw2_c, b_pw2_c, ln_g_d, ln_b_d, w_s_d, b_s_d, w_out, final_g, loss_target, m_norm_g, m_w_ada, m_b_ada, m_w_in, m_w_conv_a, m_w_pool, m_pool_scale, m_w_dw_c, m_b_dw_c, m_ln_g_c, m_ln_b_c, m_w_pw2_c, m_b_pw2_c, m_ln_g_d, m_ln_b_d, m_w_s_d, m_b_s_d, m_w_out, m_final_g, v_norm_g, v_w_ada, v_b_ada, v_w_in, v_w_conv_a, v_w_pool, v_pool_scale, v_w_dw_c, v_b_dw_c, v_ln_g_c, v_ln_b_c, v_w_pw2_c, v_b_pw2_c, v_ln_g_d, v_ln_b_d, v_w_s_d, v_b_s_d, v_w_out, v_final_g):
    env = dict(locals())
    weights = {n: env[n] for n in WEIGHTS}
    mom_m = {n: env["m_" + n] for n in WEIGHTS}
    mom_v = {n: env["v_" + n] for n in WEIGHTS}
    nl = norm_g.shape[0]
    t = x.shape[1]
    tt, tm = _tiles(t)
    ax, ay, ac = lax.axis_index("x"), lax.axis_index("y"), lax.axis_index("c")
    chip = 2 * ax + ay
    dev = 2 * chip + ac
    cidx = jnp.reshape(ac, (1,)).astype(jnp.int32)
    sidx = jnp.reshape(chip, (1,)).astype(jnp.int32)
    xs, target = x[0], loss_target[0]

    shard_small, shard_off = _pack([c, w_conv_a, w_dw_c, w_pw2_c])
    gathered = _allgather8(shard_small, "gather_small").reshape(N_DEV, -1, 128)
    per_dev = [_unpack(gathered[d], shard_off) for d in range(0, N_DEV, 2)]
    c_all = jnp.concatenate([u[0] for d in range(N_DEV)
                             for u in [_unpack(gathered[d], shard_off[:1])]], axis=0)
    w_conv_full = jnp.concatenate([u[1] for u in per_dev], axis=-1)
    w_dw_full = jnp.concatenate([u[2] for u in per_dev], axis=-1)
    w_pw2_full = jnp.concatenate([u[3] for u in per_dev], axis=1)

    b_ada_s = lax.dynamic_slice_in_dim(b_ada, chip * PIECE, PIECE, axis=1)[:, None, :]
    mod_s = _ada_forward(c_all, w_ada, b_ada_s)
    mod_all = _allgather8(mod_s.reshape(nl * N_DEV, PIECE), "gather_mod").reshape(N_DEV, nl, N_DEV, PIECE)
    mod_rows = lax.dynamic_index_in_dim(mod_all, dev, axis=2, keepdims=False)
    mod = jnp.concatenate([mod_rows[2 * s] for s in range(N_CHIP)], axis=-1)

    w_in_full, w_out_full = _gather_weights(w_in, w_out)

    tril = jnp.tril(jnp.ones((CHUNK, CHUNK), bool))
    ws_masked = jnp.where(tril[None, None], w_s_d, 0.0)
    wbd = jnp.zeros((nl, GW, GW), F32)
    for g in range(4):
        wbd = wbd.at[:, 64 * g:64 * (g + 1), 64 * g:64 * (g + 1)].set(w_pool[:, g])
    v256 = jnp.stack([pool_scale, b_dw_c, ln_g_c, ln_b_c, b_pw2_c, ln_g_d, ln_b_d], axis=1)
    mp = dict(
        wa=_pad_rows(jnp.swapaxes(w_conv_full, 0, 1), 8).swapaxes(0, 1),
        wdw=_pad_rows(jnp.swapaxes(w_dw_full, 0, 1), 32).swapaxes(0, 1),
        v256=_pad_rows(jnp.swapaxes(v256, 0, 1), 16).swapaxes(0, 1),
        wbd=wbd.astype(BF16), wpw=w_pw2_full.astype(BF16), ws=ws_masked.astype(BF16),
        bsd=jnp.repeat(jnp.swapaxes(b_s_d, 1, 2), GW // 4, axis=2),
        wout=w_out_full)
    wst = jnp.swapaxes(ws_masked, 2, 3).astype(BF16)

    def vec_rows(l):
        rows = jnp.stack([mod[l, 0:D], mod[l, D:2 * D], mod[l, 2 * D:3 * D], norm_g[l]], axis=0)
        return _pad_rows(rows, 8)

    xin, zs, ys, vecs = [xs], [], [], []
    for l in range(nl):
        vecs.append(vec_rows(l))
        zs.append(_fwd_in(xin[l], vecs[l], w_in_full, l, tt))
        xn, yl = _fwd_mix(zs[l], xin[l], vecs[l], mp, l, tm)
        xin.append(xn)
        ys.append(yl)
    dx, fin_sums = _final(xin[nl], target, _pad_rows(final_g[None, :], 8), tt)

    g_in, g_out, small, dmod = [None] * nl, [None] * nl, [None] * nl, [None] * nl
    for l in reversed(range(nl)):
        dz, gwout, gwbd, gwpw, gws, gbs, gwdw, g256, g1024 = _bwd_mix(dx, ys[l], zs[l], vecs[l], mp, wst, l, tm)
        dx, g_in[l], sums = _bwd_in(dz, xin[l], dx, vecs[l], w_in_full, l, tt)
        g_out[l] = gwout
        dmod[l] = jnp.concatenate([sums[0], sums[1], g1024[0]])
        small[l] = dict(
            norm_g=sums[2], w_conv_a=g256[8:8 + CONV_A],
            w_pool=jnp.stack([gwbd[64 * g:64 * (g + 1), 64 * g:64 * (g + 1)] for g in range(4)]),
            pool_scale=g256[0], w_dw_c=gwdw[:CONV_C], b_dw_c=g256[1], ln_g_c=g256[2], ln_b_c=g256[3],
            w_pw2_c=gwpw, b_pw2_c=g256[4], ln_g_d=g256[5], ln_b_d=g256[6], w_s_d=gws, b_s_d=gbs[:, :4].T)
    grad_x = dx[None]

    dmod_all, dmod_sum = _allgather8(jnp.stack(dmod).reshape(nl * 3 * D // 128, 128), "gather_dmod", reduce=True)
    dmod_all = dmod_all.reshape(N_DEV, nl, 3 * D)
    grad_b_ada = dmod_sum.reshape(nl, 3 * D)
    dmod_s = jnp.swapaxes(lax.dynamic_slice_in_dim(dmod_all, chip * PIECE, PIECE, axis=2), 0, 1)
    grad_w_ada = _ada_backward(c_all, dmod_s)

    small_names = [n for n in WEIGHTS if n not in BIG and n not in ('b_ada', 'final_g')]
    stacked = [jnp.stack([small[l][n] for l in range(nl)]) for n in small_names]
    small_buf, small_off = _pack(stacked + [fin_sums[0], fin_sums[2, 0:1]])
    reduced = _unpack(_allgather8(small_buf, "allreduce_small", reduce=True)[1], small_off)
    grads = dict(zip(small_names, reduced[:len(small_names)]))
    grads['final_g'] = reduced[-2]
    loss = reduced[-1][0]
    grads['b_ada'] = grad_b_ada
    grads['w_ada'] = grad_w_ada
    grads['w_conv_a'] = lax.dynamic_slice_in_dim(grads['w_conv_a'], chip * 64, 64, axis=2)
    grads['w_dw_c'] = lax.dynamic_slice_in_dim(grads['w_dw_c'], chip * 64, 64, axis=2)
    grads['w_pw2_c'] = lax.dynamic_slice_in_dim(grads['w_pw2_c'], chip * 64, 64, axis=1)

    grads['w_in'] = jnp.concatenate(
        [_reduce_scatter(g_in[l][None], cidx, sidx, f"in{l}") for l in range(nl)], axis=0)
    grads['w_out'] = jnp.concatenate(
        [_reduce_scatter(g_out[l][None], cidx, sidx, f"out{l}") for l in range(nl)], axis=0)

    delta, new_m, new_v = {}, {}, {}
    for n in BIG:
        shape = weights[n].shape
        as2d = lambda a: a.reshape(-1, shape[-1])
        d2, m2, v2 = _adamw(as2d(weights[n]), as2d(grads[n]), as2d(mom_m[n]), as2d(mom_v[n]), n)
        delta[n], new_m[n], new_v[n] = d2.reshape(shape), m2.reshape(shape), v2.reshape(shape)
    rest = [n for n in WEIGHTS if n not in BIG]
    wb, off = _pack([weights[n] for n in rest], 512)
    gb, _ = _pack([grads[n] for n in rest], 512)
    mb, _ = _pack([mom_m[n] for n in rest], 512)
    vb, _ = _pack([mom_v[n] for n in rest], 512)
    d2, m2, v2 = _adamw(wb, gb, mb, vb, "small")
    for n, dn, mn, vn in zip(rest, _unpack(d2, off), _unpack(m2, off), _unpack(v2, off)):
        delta[n], new_m[n], new_v[n] = dn, mn, vn

    return (loss, grad_x, *[grads[n] for n in WEIGHTS], *[delta[n] for n in WEIGHTS],
            *[new_m[n] for n in WEIGHTS], *[new_v[n] for n in WEIGHTS])
```

```python
import functools
import math

import jax
import jax.numpy as jnp
import numpy as np
from jax import lax
from jax.experimental import pallas as pl
from jax.experimental.pallas import tpu as pltpu

F32 = jnp.float32
BF16 = jnp.bfloat16
MESH = pl.DeviceIdType.MESH

D = 1024
DIN = 3072
GW = 256
PIECE = 768
N_CHIP = 4
N_DEV = 8
HALO = 32
CONV_A = 3
CONV_C = 31
CHUNK = 128
CONV_ROWS = 64
MIX_FWD_POINTS = 18
EPS = 1e-6
VMEM_LIMIT = 56 * 1024 * 1024

ADAM_LR, ADAM_B1, ADAM_B2, ADAM_EPS, ADAM_WD, ADAM_STEP = 0.001, 0.9, 0.999, 1e-08, 0.01, 10
GELU_K = math.sqrt(2.0 / math.pi)
GELU_C = 0.044715

WEIGHTS = ['norm_g', 'w_ada', 'b_ada', 'w_in', 'w_conv_a', 'w_pool', 'pool_scale', 'w_dw_c', 'b_dw_c', 'ln_g_c',
           'ln_b_c', 'w_pw2_c', 'b_pw2_c', 'ln_g_d', 'ln_b_d', 'w_s_d', 'b_s_d', 'w_out', 'final_g']
BIG = ('w_ada', 'w_in', 'w_out')


def _tiles(t):
    if t % 512 == 0 and t >= 2048:
        return 512, 256
    return 128, 128


def _params(sem, limit=VMEM_LIMIT):
    return pltpu.CompilerParams(dimension_semantics=sem, vmem_limit_bytes=limit)


def _sig(x):
    return jax.nn.sigmoid(x)


def _gelu(x):
    th = jnp.tanh(GELU_K * (x + GELU_C * x * x * x))
    return 0.5 * x * (1.0 + th), th


def _gelu_grad(x, th):
    return 0.5 * (1.0 + th) + 0.5 * x * (1.0 - th * th) * GELU_K * (1.0 + 3.0 * GELU_C * x * x)


def _dot(a, b):
    return jnp.dot(a, b, preferred_element_type=F32)


def _dot_nt(a, b):
    return lax.dot_general(a, b, (((1,), (1,)), ((), ())), preferred_element_type=F32)


def _dot_tn(a, b):
    return lax.dot_general(a, b, (((0,), (0,)), ((), ())), preferred_element_type=F32)


def _rowsum(x):
    return jnp.sum(x, axis=0, keepdims=True)


def _allgather8(v, name, reduce=False):
    m_per, n = v.shape

    def body(x_ref, out_ref, *rest):
        if reduce:
            sum_ref, send_sems, recv_sems, local_sem = rest
        else:
            send_sems, recv_sems, local_sem = rest
        x, y, c = lax.axis_index("x"), lax.axis_index("y"), lax.axis_index("c")
        me, sibling = (x, y, c), (x, y, 1 - c)
        chips = [(1 - x, y), (x, 1 - y), (1 - x, 1 - y)]

        def rows(px, py, pc):
            return out_ref.at[pl.ds((4 * px + 2 * py + pc) * m_per, m_per), :]

        def copy(k, block, to, src=None):
            return pltpu.make_async_remote_copy(
                src_ref=rows(*block) if src is None else src, dst_ref=rows(*block),
                send_sem=send_sems.at[k], recv_sem=recv_sems.at[k], device_id=to, device_id_type=MESH)

        mine = pltpu.make_async_copy(x_ref, rows(*me), local_sem)
        mine.start()
        first = [copy(0, me, sibling, src=x_ref)]
        first += [copy(1 + j, me, (*chip, c), src=x_ref) for j, chip in enumerate(chips)]
        for cp in first:
            cp.start()
        passed = [copy(4 + j, (*chip, c), sibling) for j, chip in enumerate(chips)]
        for j, chip in enumerate(chips):
            copy(1 + j, (*chip, c), me).wait_recv()
            passed[j].start()
        copy(0, sibling, me).wait_recv()
        for j, chip in enumerate(chips):
            copy(4 + j, (*chip, 1 - c), me).wait_recv()
        for cp in first + passed:
            cp.wait_send()
        mine.wait()
        if reduce:
            acc = out_ref[0:m_per, :]
            for d in range(1, N_DEV):
                acc = acc + out_ref[d * m_per:(d + 1) * m_per, :]
            sum_ref[...] = acc

    out_shape = [jax.ShapeDtypeStruct((N_DEV * m_per, n), v.dtype)]
    out_specs = [pl.BlockSpec(memory_space=pltpu.VMEM)]
    if reduce:
        out_shape.append(jax.ShapeDtypeStruct((m_per, n), v.dtype))
        out_specs.append(pl.BlockSpec(memory_space=pltpu.VMEM))
    res = pl.pallas_call(
        body, name=name, out_shape=tuple(out_shape),
        in_specs=[pl.BlockSpec(memory_space=pltpu.VMEM)], out_specs=tuple(out_specs),
        scratch_shapes=[pltpu.SemaphoreType.DMA((7,)), pltpu.SemaphoreType.DMA((7,)), pltpu.SemaphoreType.DMA],
        compiler_params=pltpu.CompilerParams(vmem_limit_bytes=VMEM_LIMIT),
    )(v)
    return res if reduce else res[0]


def _allreduce8(v, name):
    rows, n = v.shape
    rc = rows // N_DEV

    def body(x_ref, sum_ref, stage, send1, recv1, send2, recv2):
        x, y, c = lax.axis_index("x"), lax.axis_index("y"), lax.axis_index("c")
        me = 4 * x + 2 * y + c
        peers = []
        for k in range(1, N_DEV):
            kx, ky, kc = (k >> 2) & 1, (k >> 1) & 1, k & 1
            peers.append((1 - x if kx else x, 1 - y if ky else y, 1 - c if kc else c))

        def chunk(ref, d):
            return ref.at[pl.ds(d * rc, rc), :]

        scatter, gather = [], []
        for k, (px, py, pc) in enumerate(peers):
            scatter.append(pltpu.make_async_remote_copy(
                src_ref=chunk(x_ref, 4 * px + 2 * py + pc), dst_ref=stage.at[me], send_sem=send1.at[k],
                recv_sem=recv1.at[k], device_id=(px, py, pc), device_id_type=MESH))
        for cp in scatter:
            cp.start()
        stage[me] = x_ref[pl.ds(me * rc, rc), :]
        for k, (px, py, pc) in enumerate(peers):
            pltpu.make_async_remote_copy(
                src_ref=chunk(x_ref, me), dst_ref=stage.at[4 * px + 2 * py + pc], send_sem=send1.at[k],
                recv_sem=recv1.at[k], device_id=(px, py, pc), device_id_type=MESH).wait_recv()
        total = stage[0]
        for d in range(1, N_DEV):
            total = total + stage[d]
        sum_ref[pl.ds(me * rc, rc), :] = total
        for k, (px, py, pc) in enumerate(peers):
            gather.append(pltpu.make_async_remote_copy(
                src_ref=chunk(sum_ref, me), dst_ref=chunk(sum_ref, me), send_sem=send2.at[k],
                recv_sem=recv2.at[k], device_id=(px, py, pc), device_id_type=MESH))
        for cp in gather:
            cp.start()
        for k, (px, py, pc) in enumerate(peers):
            theirs = 4 * px + 2 * py + pc
            pltpu.make_async_remote_copy(
                src_ref=chunk(sum_ref, theirs), dst_ref=chunk(sum_ref, theirs), send_sem=send2.at[k],
                recv_sem=recv2.at[k], device_id=(px, py, pc), device_id_type=MESH).wait_recv()
        for cp in scatter + gather:
            cp.wait_send()

    return pl.pallas_call(
        body, name=name, out_shape=jax.ShapeDtypeStruct((rows, n), v.dtype),
        in_specs=[pl.BlockSpec(memory_space=pltpu.VMEM)], out_specs=pl.BlockSpec(memory_space=pltpu.VMEM),
        scratch_shapes=[pltpu.VMEM((N_DEV, rc, n), v.dtype)] + [pltpu.SemaphoreType.DMA((N_DEV - 1,))] * 4,
        compiler_params=pltpu.CompilerParams(vmem_limit_bytes=VMEM_LIMIT),
    )(v)


def _gather_weights(w_in_s, w_out_s):
    nl = w_in_s.shape[0]
    shapes = ((nl, N_CHIP) + w_in_s.shape[1:], (nl, N_CHIP) + w_out_s.shape[1:])
    n_sem = 2 * 3 * 2

    def body(win_ref, wout_ref, win_o, wout_o, send_sems, recv_sems):
        x, y, c = lax.axis_index("x"), lax.axis_index("y"), lax.axis_index("c")
        s = 2 * x + y
        sibling = (x, y, 1 - c)
        chips = [(1 - x, y), (x, 1 - y), (1 - x, 1 - y)]
        outs = (win_o, wout_o)
        for src, dst in ((win_ref, win_o), (wout_ref, wout_o)):
            rows = src.shape[1]
            step = min(rows, 256)
            for l in range(nl):
                for r in range(0, rows, step):
                    dst[l, s, r:r + step, :] = src[l, r:r + step, :].astype(BF16)

        def copy(k, ref, piece, half, to):
            r2 = ref.shape[2] // 2
            rows = ref.at[:, piece, pl.ds(half * r2, r2), :]
            return pltpu.make_async_remote_copy(
                src_ref=rows, dst_ref=rows, send_sem=send_sems.at[k], recv_sem=recv_sems.at[k],
                device_id=to, device_id_type=MESH)

        sends = [copy(2 * j + a, ref, s, c, (px, py, c)) for j, (px, py) in enumerate(chips)
                 for a, ref in enumerate(outs)]
        for cp in sends:
            cp.start()
        passed = []
        for j, (px, py) in enumerate(chips):
            for a, ref in enumerate(outs):
                copy(2 * j + a, ref, 2 * px + py, c, (px, py, c)).wait_recv()
                passed.append(copy(6 + 2 * j + a, ref, 2 * px + py, c, sibling))
                passed[-1].start()
        for j, (px, py) in enumerate(chips):
            for a, ref in enumerate(outs):
                copy(6 + 2 * j + a, ref, 2 * px + py, 1 - c, sibling).wait_recv()
        for cp in sends + passed:
            cp.wait_send()

    return pl.pallas_call(
        body, name="gather_weights",
        out_shape=tuple(jax.ShapeDtypeStruct(s, BF16) for s in shapes),
        in_specs=[pl.BlockSpec(memory_space=pltpu.VMEM)] * 2,
        out_specs=tuple(pl.BlockSpec(memory_space=pltpu.VMEM) for _ in shapes),
        scratch_shapes=[pltpu.SemaphoreType.DMA((n_sem,)), pltpu.SemaphoreType.DMA((n_sem,))],
        compiler_params=pltpu.CompilerParams(vmem_limit_bytes=VMEM_LIMIT),
    )(w_in_s, w_out_s)


def _send_half_to_sibling(g, tag):
    nl, nc, r, cdim = g.shape
    half = r // 2

    def body(g_ref, recv_ref, send_sem, recv_sem):
        x, y, c = lax.axis_index("x"), lax.axis_index("y"), lax.axis_index("c")
        cp = pltpu.make_async_remote_copy(
            src_ref=g_ref.at[:, :, pl.ds((1 - c) * half, half), :], dst_ref=recv_ref,
            send_sem=send_sem, recv_sem=recv_sem, device_id=(x, y, 1 - c), device_id_type=MESH)
        cp.start()
        cp.wait()

    return pl.pallas_call(
        body, name=f"rs_sibling_{tag}",
        out_shape=jax.ShapeDtypeStruct((nl, nc, half, cdim), g.dtype),
        in_specs=[pl.BlockSpec(memory_space=pl.ANY)], out_specs=pl.BlockSpec(memory_space=pl.ANY),
        scratch_shapes=[pltpu.SemaphoreType.DMA, pltpu.SemaphoreType.DMA],
    )(g)


def _add_half(g, recv, cidx, tag):
    nl, nc, r, cdim = g.shape
    half = r // 2
    rt = min(half, 128)
    nrt = half // rt

    def body(c_ref, a_ref, b_ref, o_ref, ob_ref):
        c = c_ref[0]
        first = jnp.where(c == 0, a_ref[...], b_ref[...])
        second = jnp.where(c == 0, b_ref[...], a_ref[...])
        total = first + second
        o_ref[...] = total
        ob_ref[...] = total.astype(BF16)

    out_spec = pl.BlockSpec((None, None, rt, cdim), lambda l, j, i, c: (l, j, i, 0))
    grid_spec = pltpu.PrefetchScalarGridSpec(
        num_scalar_prefetch=1, grid=(nl, nc, nrt),
        in_specs=[pl.BlockSpec((None, None, rt, cdim), lambda l, j, i, c: (l, j, c[0] * nrt + i, 0)),
                  pl.BlockSpec((None, None, rt, cdim), lambda l, j, i, c: (l, j, i, 0))],
        out_specs=(out_spec, out_spec))
    return pl.pallas_call(
        body, name=f"rs_add_sibling_{tag}", grid_spec=grid_spec,
        out_shape=(jax.ShapeDtypeStruct((nl, nc, half, cdim), g.dtype),
                   jax.ShapeDtypeStruct((nl, nc, half, cdim), BF16)),
        compiler_params=_params(("parallel", "parallel", "parallel")),
    )(cidx, g, recv)


def _exchange_chips(h, tag):
    nl, nc, r2, cdim = h.shape

    def body(h_ref, recv_ref, send_sems, recv_sems):
        x, y, c = lax.axis_index("x"), lax.axis_index("y"), lax.axis_index("c")
        chips = [(1 - x, y), (x, 1 - y), (1 - x, 1 - y)]
        cps = []
        for k, (px, py) in enumerate(chips):
            cps.append(pltpu.make_async_remote_copy(
                src_ref=h_ref.at[:, 2 * px + py], dst_ref=recv_ref.at[k], send_sem=send_sems.at[k],
                recv_sem=recv_sems.at[k], device_id=(px, py, c), device_id_type=MESH))
        for cp in cps:
            cp.start()
        for cp in cps:
            cp.wait()

    return pl.pallas_call(
        body, name=f"rs_chips_{tag}",
        out_shape=jax.ShapeDtypeStruct((3, nl, r2, cdim), h.dtype),
        in_specs=[pl.BlockSpec(memory_space=pl.ANY)], out_specs=pl.BlockSpec(memory_space=pl.ANY),
        scratch_shapes=[pltpu.SemaphoreType.DMA((3,)), pltpu.SemaphoreType.DMA((3,))],
    )(h)


def _add_chips(h, recv, sidx, tag):
    nl, nc, r2, cdim = h.shape
    rt = min(r2, 128)

    def body(s_ref, own_ref, r_ref, o_ref):
        s = s_ref[0]
        got = [r_ref[k].astype(F32) for k in range(3)]
        total = None
        for chip in range(N_CHIP):
            term = jnp.where(s == chip, own_ref[...],
                             jnp.where((s ^ 2) == chip, got[0], jnp.where((s ^ 1) == chip, got[1], got[2])))
            total = term if total is None else total + term
        o_ref[...] = total

    grid_spec = pltpu.PrefetchScalarGridSpec(
        num_scalar_prefetch=1, grid=(nl, r2 // rt),
        in_specs=[pl.BlockSpec((None, None, rt, cdim), lambda l, i, s: (l, s[0], i, 0)),
                  pl.BlockSpec((3, None, rt, cdim), lambda l, i, s: (0, l, i, 0))],
        out_specs=pl.BlockSpec((None, rt, cdim), lambda l, i, s: (l, i, 0)))
    return pl.pallas_call(
        body, name=f"rs_add_chips_{tag}", grid_spec=grid_spec,
        out_shape=jax.ShapeDtypeStruct((nl, r2, cdim), h.dtype),
        compiler_params=_params(("parallel", "parallel")),
    )(sidx, h, recv)


def _swap_halves(red, tag):
    def body(red_ref, out_ref, send_sem, recv_sem):
        x, y, c = lax.axis_index("x"), lax.axis_index("y"), lax.axis_index("c")
        cp = pltpu.make_async_remote_copy(
            src_ref=red_ref, dst_ref=out_ref, send_sem=send_sem, recv_sem=recv_sem,
            device_id=(x, y, 1 - c), device_id_type=MESH)
        cp.start()
        cp.wait()

    return pl.pallas_call(
        body, name=f"rs_swap_{tag}", out_shape=jax.ShapeDtypeStruct(red.shape, red.dtype),
        in_specs=[pl.BlockSpec(memory_space=pl.ANY)], out_specs=pl.BlockSpec(memory_space=pl.ANY),
        scratch_shapes=[pltpu.SemaphoreType.DMA, pltpu.SemaphoreType.DMA],
    )(red)


def _reduce_scatter(g, cidx, sidx, tag):
    recv = _send_half_to_sibling(g, tag)
    chip_sum, chip_sum_bf16 = _add_half(g, recv, cidx, tag)
    got = _exchange_chips(chip_sum_bf16, tag)
    red = _add_chips(chip_sum, got, sidx, tag)
    return red, _swap_halves(red, tag)


def _ada_forward(c_all, w_ada_s, b_s):
    nl = w_ada_s.shape[0]

    def body(c_ref, w_ref, b_ref, o_ref):
        cv = c_ref[...]
        ca = (cv * _sig(cv)).astype(BF16)
        for l in range(nl):
            o_ref[l] = _dot(ca, w_ref[l].astype(BF16)) + b_ref[l]

    return pl.pallas_call(
        body, name="ada_forward", out_shape=jax.ShapeDtypeStruct((nl, N_DEV, PIECE), F32),
        compiler_params=pltpu.CompilerParams(vmem_limit_bytes=VMEM_LIMIT),
    )(c_all, w_ada_s, b_s)


def _ada_backward(c_all, dmod_s):
    nl = dmod_s.shape[0]

    def body(c_ref, d_ref, o_ref):
        cv = c_ref[...]
        ca = (cv * _sig(cv)).astype(BF16)
        for l in range(nl):
            o_ref[l] = _dot_tn(ca, d_ref[l].astype(BF16))

    return pl.pallas_call(
        body, name="ada_backward", out_shape=jax.ShapeDtypeStruct((nl, D, PIECE), F32),
        compiler_params=pltpu.CompilerParams(vmem_limit_bytes=VMEM_LIMIT),
    )(c_all, dmod_s)


def _normed(xv, vecs):
    r = lax.rsqrt(jnp.mean(xv * xv, axis=-1, keepdims=True) + EPS)
    xhat = xv * r
    hn = xhat * vecs[3:4, :]
    return r, xhat, hn


def _fwd_in(x, vecs, w, l, tt):
    t = x.shape[0]

    def body(x_ref, v_ref, w_ref, z_ref):
        vecs_v = v_ref[...]
        _, _, hn = _normed(x_ref[...], vecs_v)
        hb = (hn * (1.0 + vecs_v[1:2, :]) + vecs_v[0:1, :]).astype(BF16)
        for j in range(N_CHIP):
            z_ref[:, PIECE * j:PIECE * (j + 1)] = _dot(hb, w_ref[j])

    return pl.pallas_call(
        body, name=f"fwd_in_{l}", grid=(t // tt,),
        in_specs=[pl.BlockSpec((tt, D), lambda i: (i, 0)), pl.BlockSpec((8, D), lambda i: (0, 0)),
                  pl.BlockSpec((None, N_CHIP, D, PIECE), lambda i: (l, 0, 0, 0))],
        out_specs=pl.BlockSpec((tt, DIN), lambda i: (i, 0)),
        out_shape=jax.ShapeDtypeStruct((t, DIN), F32),
        compiler_params=_params(("parallel",)),
    )(x, vecs, w)


def _bwd_in(dz, x, dxo, vecs, w, l, tt, gw_all=None):
    t = x.shape[0]
    nt = t // tt
    nl = w.shape[0]

    def body(dz_ref, x_ref, dxo_ref, v_ref, w_any, *rest):
        dx_ref, gw_any, sums_ref, w_vmem, gw_acc, sem = rest[-6:]
        i = pl.program_id(0)

        @pl.when(i == 0)
        def _():
            cp = pltpu.make_async_copy(w_any.at[l], w_vmem, sem)
            cp.start()
            cp.wait()
            gw_acc[...] = jnp.zeros_like(gw_acc)
            sums_ref[...] = jnp.zeros_like(sums_ref)

        vecs_v = v_ref[...]
        r, xhat, hn = _normed(x_ref[...], vecs_v)
        one_scale = 1.0 + vecs_v[1:2, :]
        hb = (hn * one_scale + vecs_v[0:1, :]).astype(BF16)
        dh = None
        for j in range(N_CHIP):
            dzj = dz_ref[:, PIECE * j:PIECE * (j + 1)]
            part = _dot_nt(dzj, w_vmem[j])
            dh = part if dh is None else dh + part
            gw_acc[j] += _dot_tn(hb, dzj)
        sums_ref[0:1, :] += _rowsum(dh)
        sums_ref[1:2, :] += _rowsum(dh * hn)
        sums_ref[2:3, :] += _rowsum(dh * one_scale * xhat)
        dxh = dh * (vecs_v[3:4, :] * one_scale)
        dx_ref[...] = dxo_ref[...] + r * (dxh - xhat * jnp.mean(dxh * xhat, axis=-1, keepdims=True))

        @pl.when(i == nt - 1)
        def _():
            cp = pltpu.make_async_copy(gw_acc, gw_any.at[l], sem)
            cp.start()
            cp.wait()

    carried = [] if gw_all is None else [gw_all]
    return pl.pallas_call(
        body, name=f"bwd_in_{l}", grid=(nt,),
        in_specs=[pl.BlockSpec((tt, DIN), lambda i: (i, 0)), pl.BlockSpec((tt, D), lambda i: (i, 0)),
                  pl.BlockSpec((tt, D), lambda i: (i, 0)), pl.BlockSpec((8, D), lambda i: (0, 0)),
                  pl.BlockSpec(memory_space=pl.ANY)] + [pl.BlockSpec(memory_space=pl.ANY)] * len(carried),
        out_specs=(pl.BlockSpec((tt, D), lambda i: (i, 0)), pl.BlockSpec(memory_space=pl.ANY),
                   pl.BlockSpec((8, D), lambda i: (0, 0))),
        out_shape=(jax.ShapeDtypeStruct((t, D), F32), jax.ShapeDtypeStruct((nl, N_CHIP, D, PIECE), F32),
                   jax.ShapeDtypeStruct((8, D), F32)),
        scratch_shapes=[pltpu.VMEM((N_CHIP, D, PIECE), BF16), pltpu.VMEM((N_CHIP, D, PIECE), F32),
                        pltpu.SemaphoreType.DMA],
        input_output_aliases={5: 1} if carried else {},
        compiler_params=_params(("arbitrary",)),
    )(dz, x, dxo, vecs, w, *carried)


def _col(ref, k):
    return ref[:, GW * k:GW * (k + 1)]


def _lane_group():
    return lax.broadcasted_iota(jnp.int32, (1, GW), 1) // (GW // 4)


def _pool_window(group):
    return jnp.where(group == 0, 2.0, jnp.where(group == 1, 4.0, jnp.where(group == 2, 8.0, 16.0)))


def _by_group(group, a, b, c, d):
    return jnp.where(group == 0, a, jnp.where(group == 1, b, jnp.where(group == 2, c, d)))


def _layer_norm(x, g, b):
    mu = jnp.mean(x, axis=-1, keepdims=True)
    xc = x - mu
    rstd = lax.rsqrt(jnp.mean(xc * xc, axis=-1, keepdims=True) + EPS)
    xh = xc * rstd
    return xh * g + b, xh, rstd


def _layer_norm_grad(d_out, xh, rstd, g):
    dxh = d_out * g
    return rstd * (dxh - jnp.mean(dxh, axis=-1, keepdims=True) - xh * jnp.mean(dxh * xh, axis=-1, keepdims=True))


def _shifted_copies(ext_ref, sh_ref, n):
    for b in range(1, 8):
        sh_ref[b, 0:n - 8, :] = ext_ref[b:b + n - 8, :]


def _rows_at(ext_ref, sh_ref, start, rows):
    b = start % 8
    if b == 0 or sh_ref is None:
        return ext_ref[start:start + rows, :]
    return sh_ref[b, start - b:start - b + rows, :]


class _Beside:
    def __init__(self, thunks=(), points=1):
        self.thunks, self.points, self.seen, self.done = list(thunks), max(points, 1), 0, 0

    def _run_to(self, due):
        while self.done < min(due, len(self.thunks)):
            self.thunks[self.done]()
            self.done += 1

    def here(self):
        self.seen += 1
        self._run_to(-(-self.seen * len(self.thunks) // self.points))

    def rest(self):
        self._run_to(len(self.thunks))


def _conv_taps(ext_ref, w_ref, n_taps, first_row, tm, out_ref, flip=False, sh_ref=None, beside=None):
    for rb in range(0, tm, CONV_ROWS):
        acc = None
        for k in range(n_taps):
            wk = n_taps - 1 - k if flip else k
            term = w_ref[wk:wk + 1, :] * _rows_at(ext_ref, sh_ref, first_row + rb + k, CONV_ROWS)
            acc = term if acc is None else acc + term
        out_ref[rb:rb + CONV_ROWS, :] = acc
        if beside is not None:
            beside.here()


def _mix_forward(zc, zp, first, row0, tm, p, s, beside=None):
    f = {}
    beside = beside or _Beside()
    keep = jnp.where(first, 0.0, 1.0)
    group = _lane_group()
    beside.here()
    a_b, a_c, a_x, a_g = _col(zc, 0), _col(zc, 1), _col(zc, 2), _col(zc, 3)
    s['ua'][0:HALO, :] = _col(zp, 1) * _col(zp, 2) * keep
    s['ua'][HALO:HALO + tm, :] = a_c * a_x
    _conv_taps(s['ua'], p['wa'], CONV_A, HALO - (CONV_A - 1), tm, s['cv'], beside=beside)
    cv = s['cv'][...]
    sg_a = _sig(a_g)
    f.update(a_b=a_b, a_c=a_c, a_x=a_x, a_g=a_g, cv=cv, sg_a=sg_a)
    y_a = a_b * cv * (a_g * sg_a)
    beside.here()
    b_p, b_g = _col(zc, 4), _col(zc, 5)
    s['p0'][0:HALO, :] = _col(zp, 4) * keep
    s['p0'][HALO:HALO + tm, :] = b_p
    n = HALO + tm
    s['p1'][8:n, :] = s['p0'][8:n, :] + s['p0'][7:n - 1, :]
    w2 = s['p1'][HALO:n, :]
    beside.here()
    s['p0'][16:n, :] = s['p1'][16:n, :] + s['p1'][14:n - 2, :]
    w4 = s['p0'][HALO:n, :]
    beside.here()
    s['p1'][24:n, :] = s['p0'][24:n, :] + s['p0'][20:n - 4, :]
    w8 = s['p1'][HALO:n, :]
    w16 = w8 + s['p1'][HALO - 8:n - 8, :]
    tpos = (row0 + lax.broadcasted_iota(jnp.int32, (tm, 1), 0) + 1).astype(F32)
    inv_cnt = 1.0 / jnp.minimum(tpos, _pool_window(group))
    pooled = (_by_group(group, w2, w4, w8, w16) * inv_cnt - b_p).astype(BF16)
    beside.here()
    q = _dot(pooled, p['wbd'][...])
    sg_b = _sig(b_g)
    f.update(b_g=b_g, pooled=pooled, q=q, sg_b=sg_b, inv_cnt=inv_cnt)
    y_b = q * p['v256'][0:1, :] * (b_g * sg_b)
    beside.here()
    c_a, c_gl, c_g = _col(zc, 6), _col(zc, 7), _col(zc, 8)
    sg_gl = _sig(c_gl)
    zp_gl = _col(zp, 7)
    s['hc'][0:HALO, :] = _col(zp, 6) * _sig(zp_gl) * keep
    s['hc'][HALO:HALO + tm, :] = c_a * sg_gl
    _shifted_copies(s['hc'], s['hcs'], HALO + tm)
    beside.here()
    _conv_taps(s['hc'], p['wdw'], CONV_C, HALO - (CONV_C - 1), tm, s['co'], sh_ref=s['hcs'], beside=beside)
    co = s['co'][...] + p['v256'][1:2, :]
    ln_c, xh_c, rstd_c = _layer_norm(co, p['v256'][2:3, :], p['v256'][3:4, :])
    beside.here()
    sg_ln = _sig(ln_c)
    sc = (ln_c * sg_ln).astype(BF16)
    pw = _dot(sc, p['wpw'][...]) + p['v256'][4:5, :]
    sg_c = _sig(c_g)
    f.update(c_a=c_a, c_g=c_g, sg_gl=sg_gl, ln_c=ln_c, xh_c=xh_c, rstd_c=rstd_c, sg_ln=sg_ln, sc=sc, pw=pw,
             sg_c=sg_c)
    y_c = pw * (c_g * sg_c)
    beside.here()
    d_u, d_v, d_g = _col(zc, 9), _col(zc, 10), _col(zc, 11)
    gu, th_u = _gelu(d_u)
    gv, th_v = _gelu(d_v)
    beside.here()
    v, xh_d, rstd_d = _layer_norm(gv, p['v256'][5:6, :], p['v256'][6:7, :])
    vb = v.astype(BF16)
    parts = []
    for ch in range(tm // CHUNK):
        vc = vb[ch * CHUNK:(ch + 1) * CHUNK, :]
        mixed = p['bsd'][...]
        for h in range(4):
            mixed = mixed + jnp.where(group == h, _dot(p['ws'][h], vc), 0.0)
        parts.append(mixed)
        beside.here()
    mx = parts[0] if len(parts) == 1 else jnp.concatenate(parts, axis=0)
    sg_d = _sig(d_g)
    f.update(d_u=d_u, d_v=d_v, d_g=d_g, gu=gu, th_u=th_u, th_v=th_v, xh_d=xh_d, rstd_d=rstd_d, vb=vb, mx=mx,
             sg_d=sg_d)
    y_d = gu * mx * (d_g * sg_d)
    f['ys'] = [y.astype(BF16) for y in (y_a, y_b, y_c, y_d)]
    beside.rest()
    return f


def _mix_scratch(tm):
    ext = pltpu.VMEM((HALO + tm, GW), F32)
    tile = pltpu.VMEM((tm, GW), F32)
    return dict(ua=ext, cv=tile, p0=ext, p1=ext, hc=ext, co=tile, hcs=pltpu.VMEM((8, HALO + tm, GW), F32))


_MIX_PARAMS = ('wa', 'wdw', 'v256', 'wbd', 'wpw', 'ws', 'bsd', 'wout')


def _mix_param_specs(l):
    def whole(*shape):
        nd = len(shape)
        return pl.BlockSpec((None,) + shape, lambda i, _nd=nd: (l,) + (0,) * _nd)
    return [whole(8, GW), whole(32, GW), whole(16, GW), whole(GW, GW), whole(GW, GW), whole(4, CHUNK, CHUNK),
            whole(CHUNK, GW), whole(N_CHIP, GW, D)]


def _fwd_mix(z, x, vecs, mp, l, tm):
    t = x.shape[0]
    per_halo = tm // HALO
    names = list(_mix_scratch(tm))

    def body(zc, zp, x_ref, v_ref, *rest):
        p = dict(zip(_MIX_PARAMS, rest[:len(_MIX_PARAMS)]))
        xn_ref, y_ref = rest[len(_MIX_PARAMS):len(_MIX_PARAMS) + 2]
        s = dict(zip(names, rest[len(_MIX_PARAMS) + 2:]))
        i = pl.program_id(0)
        f = _mix_forward(zc, zp, i == 0, i * tm, tm, p, s)
        y = None
        for j in range(4):
            part = _dot(f['ys'][j], p['wout'][j])
            y = part if y is None else y + part
        y_ref[...] = y
        xn_ref[...] = x_ref[...] + v_ref[2:3, :] * y

    return pl.pallas_call(
        body, name=f"fwd_mix_{l}", grid=(t // tm,),
        in_specs=[pl.BlockSpec((tm, DIN), lambda i: (i, 0)),
                  pl.BlockSpec((HALO, DIN), lambda i: (jnp.maximum(i * per_halo - 1, 0), 0)),
                  pl.BlockSpec((tm, D), lambda i: (i, 0)), pl.BlockSpec((8, D), lambda i: (0, 0))]
        + _mix_param_specs(l),
        out_specs=(pl.BlockSpec((tm, D), lambda i: (i, 0)), pl.BlockSpec((tm, D), lambda i: (i, 0))),
        out_shape=(jax.ShapeDtypeStruct((t, D), F32), jax.ShapeDtypeStruct((t, D), F32)),
        scratch_shapes=list(_mix_scratch(tm).values()),
        compiler_params=_params(("parallel",)),
    )(z, z, x, vecs, *[mp[k] for k in _MIX_PARAMS])


def _fwd_layer(x, vecs, w, mp, l, tm):
    t = x.shape[0]
    nt = t // tm
    names = list(_mix_scratch(tm))
    n_p = len(_MIX_PARAMS)

    def body(xm_ref, xr_ref, v_ref, w_ref, *rest):
        p = dict(zip(_MIX_PARAMS, rest[:n_p]))
        z_ref, xn_ref, y_ref = rest[n_p:n_p + 3]
        s = dict(zip(names, rest[n_p + 3:n_p + 3 + len(names)]))
        z_next, z_cur, z_halo = rest[n_p + 3 + len(names):]
        j = pl.program_id(0)

        @pl.when(j == 0)
        def _():
            z_cur[...] = jnp.zeros_like(z_cur)
            z_halo[...] = jnp.zeros_like(z_halo)

        vecs_v = v_ref[...]
        _, _, hn = _normed(xm_ref[...], vecs_v)
        hb = (hn * (1.0 + vecs_v[1:2, :]) + vecs_v[0:1, :]).astype(BF16)
        def project(g):
            def emit():
                k, c0 = divmod(g * GW, PIECE)
                part = _dot(hb, w_ref[k, :, c0:c0 + GW])
                z_ref[:, GW * g:GW * (g + 1)] = part
                z_next[:, GW * g:GW * (g + 1)] = part
            return emit

        tile = jnp.maximum(j - 1, 0)
        beside = _Beside([project(g) for g in range(DIN // GW)], points=MIX_FWD_POINTS)
        f = _mix_forward(z_cur, z_halo, tile == 0, tile * tm, tm, p, s, beside=beside)
        y = None
        for k in range(4):
            part = _dot(f['ys'][k], p['wout'][k])
            y = part if y is None else y + part
        y_ref[...] = y
        xn_ref[...] = xr_ref[...] + vecs_v[2:3, :] * y

        z_halo[...] = z_cur[tm - HALO:tm, :]
        z_cur[...] = z_next[...]

    def ahead(j):
        return jnp.minimum(j, nt - 1)

    def behind(j):
        return jnp.maximum(j - 1, 0)

    return pl.pallas_call(
        body, name=f"fwd_layer_{l}", grid=(nt + 1,),
        in_specs=[pl.BlockSpec((tm, D), lambda j: (ahead(j), 0)), pl.BlockSpec((tm, D), lambda j: (behind(j), 0)),
                  pl.BlockSpec((8, D), lambda j: (0, 0)),
                  pl.BlockSpec((None, N_CHIP, D, PIECE), lambda j: (l, 0, 0, 0))] + _mix_param_specs(l),
        out_specs=(pl.BlockSpec((tm, DIN), lambda j: (ahead(j), 0)), pl.BlockSpec((tm, D), lambda j: (behind(j), 0)),
                   pl.BlockSpec((tm, D), lambda j: (behind(j), 0))),
        out_shape=(jax.ShapeDtypeStruct((t, DIN), F32), jax.ShapeDtypeStruct((t, D), F32),
                   jax.ShapeDtypeStruct((t, D), F32)),
        scratch_shapes=list(_mix_scratch(tm).values())
        + [pltpu.VMEM((tm, DIN), F32), pltpu.VMEM((tm, DIN), F32), pltpu.VMEM((HALO, DIN), F32)],
        compiler_params=_params(("arbitrary",)),
    )(x, x, vecs, w, *[mp[k] for k in _MIX_PARAMS])


def _bwd_mix(dxo, y, z, vecs, mp, wst, l, tm, gwout_all=None):
    t = dxo.shape[0]
    nt = t // tm
    nl = wst.shape[0]
    carried = [] if gwout_all is None else [gwout_all]
    per_halo = tm // HALO
    fwd_names = list(_mix_scratch(tm))
    ext = pltpu.VMEM((tm + HALO, GW), F32)
    carry = pltpu.VMEM((HALO, GW), F32)
    tile = pltpu.VMEM((tm, GW), F32)
    bwd_scratch = dict(ea=ext, eb=ext, eb2=ext, ec=ext, ca=carry, cb=carry, cc=carry, dua=tile, dhc=tile,
                       gbacc=pltpu.VMEM((CHUNK, GW), F32), ecs=pltpu.VMEM((8, tm + HALO, GW), F32))
    bwd_names = list(bwd_scratch)
    n_p = len(_MIX_PARAMS)

    def body(dxo_ref, y_ref, zc, zp, v_ref, *rest):
        p = dict(zip(_MIX_PARAMS, rest[:n_p]))
        wst_ref = rest[n_p]
        first_out = n_p + 1 + len(carried)
        dz_ref, gwout, gwbd, gwpw, gws, gbs, gwdw, g256, g1024 = rest[first_out:first_out + 9]
        s = dict(zip(fwd_names + bwd_names, rest[first_out + 9:]))
        i = pl.program_id(0)
        ti = nt - 1 - i
        group = _lane_group()

        @pl.when(i == 0)
        def _():
            for ref in (gwout, gwbd, gwpw, gws, gbs, gwdw, g256, g1024, s['ca'], s['cb'], s['cc'], s['gbacc']):
                ref[...] = jnp.zeros_like(ref)

        f = _mix_forward(zc, zp, ti == 0, ti * tm, tm, p, s)
        dxo_v = dxo_ref[...]
        g1024[0:1, :] += _rowsum(dxo_v * y_ref[...])
        dy = (dxo_v * v_ref[2:3, :]).astype(BF16)
        dys = []
        for j in range(4):
            gwout[j] += _dot_tn(f['ys'][j], dy)
            dys.append(_dot_nt(dy, p['wout'][j]))
        dya, dyb, dyc, dyd = dys

        def put(k, val):
            dz_ref[:, GW * k:GW * (k + 1)] = val.astype(BF16)

        a_b, a_c, a_x, a_g, cv, sg_a = f['a_b'], f['a_c'], f['a_x'], f['a_g'], f['cv'], f['sg_a']
        silu_a = a_g * sg_a
        put(0, dya * cv * silu_a)
        put(3, dya * a_b * cv * (sg_a * (1.0 + a_g * (1.0 - sg_a))))
        d_cv = dya * a_b * silu_a
        for k in range(CONV_A):
            g256[8 + k:9 + k, :] += _rowsum(d_cv * s['ua'][HALO - (CONV_A - 1) + k:HALO - (CONV_A - 1) + k + tm, :])
        s['ea'][0:tm, :] = d_cv
        s['ea'][tm:tm + HALO, :] = s['ca'][...]
        s['ca'][...] = d_cv[0:HALO, :]
        _conv_taps(s['ea'], p['wa'], CONV_A, 0, tm, s['dua'], flip=True)
        d_u = s['dua'][...]
        put(1, d_u * a_x)
        put(2, d_u * a_c)
        b_g, q, sg_b, inv_cnt = f['b_g'], f['q'], f['sg_b'], f['inv_cnt']
        scale_b = p['v256'][0:1, :]
        silu_b = b_g * sg_b
        g256[0:1, :] += _rowsum(dyb * q * silu_b)
        put(5, dyb * q * scale_b * (sg_b * (1.0 + b_g * (1.0 - sg_b))))
        d_q = (dyb * scale_b * silu_b).astype(BF16)
        gwbd[...] += _dot_tn(f['pooled'], d_q)
        d_pooled = _dot_nt(d_q, p['wbd'][...])
        gp = d_pooled * inv_cnt
        n = tm + HALO
        s['eb'][0:tm, :] = gp
        s['eb'][tm:n, :] = s['cb'][...]
        s['cb'][...] = gp[0:HALO, :]
        s['eb2'][0:n - 8, :] = s['eb'][0:n - 8, :] + s['eb'][1:n - 7, :]
        r2 = s['eb2'][0:tm, :]
        s['eb'][0:n - 16, :] = s['eb2'][0:n - 16, :] + s['eb2'][2:n - 14, :]
        r4 = s['eb'][0:tm, :]
        s['eb2'][0:n - 24, :] = s['eb'][0:n - 24, :] + s['eb'][4:n - 20, :]
        r8 = s['eb2'][0:tm, :]
        r16 = r8 + s['eb2'][8:tm + 8, :]
        put(4, _by_group(group, r2, r4, r8, r16) - d_pooled)
        c_a, c_g, sg_gl, ln_c, xh_c, rstd_c, sg_ln, pw, sg_c = (
            f['c_a'], f['c_g'], f['sg_gl'], f['ln_c'], f['xh_c'], f['rstd_c'], f['sg_ln'], f['pw'], f['sg_c'])
        put(8, dyc * pw * (sg_c * (1.0 + c_g * (1.0 - sg_c))))
        d_pw = dyc * (c_g * sg_c)
        g256[4:5, :] += _rowsum(d_pw)
        d_pwb = d_pw.astype(BF16)
        gwpw[...] += _dot_tn(f['sc'], d_pwb)
        d_ln = _dot_nt(d_pwb, p['wpw'][...]) * (sg_ln * (1.0 + ln_c * (1.0 - sg_ln)))
        g256[2:3, :] += _rowsum(d_ln * xh_c)
        g256[3:4, :] += _rowsum(d_ln)
        d_co = _layer_norm_grad(d_ln, xh_c, rstd_c, p['v256'][2:3, :])
        g256[1:2, :] += _rowsum(d_co)
        for k in range(CONV_C):
            gwdw[k:k + 1, :] += _rowsum(d_co * _rows_at(s['hc'], s['hcs'], HALO - (CONV_C - 1) + k, tm))
        s['ec'][0:tm, :] = d_co
        s['ec'][tm:tm + HALO, :] = s['cc'][...]
        s['cc'][...] = d_co[0:HALO, :]
        _shifted_copies(s['ec'], s['ecs'], tm + HALO)
        _conv_taps(s['ec'], p['wdw'], CONV_C, 0, tm, s['dhc'], flip=True, sh_ref=s['ecs'])
        d_hc = s['dhc'][...]
        put(6, d_hc * sg_gl)
        put(7, d_hc * c_a * sg_gl * (1.0 - sg_gl))
        d_uu, d_vv, d_g, gu, th_u, th_v, xh_d, rstd_d, vb, mx, sg_d = (
            f['d_u'], f['d_v'], f['d_g'], f['gu'], f['th_u'], f['th_v'], f['xh_d'], f['rstd_d'], f['vb'], f['mx'],
            f['sg_d'])
        silu_d = d_g * sg_d
        put(9, dyd * mx * silu_d * _gelu_grad(d_uu, th_u))
        put(11, dyd * gu * mx * (sg_d * (1.0 + d_g * (1.0 - sg_d))))
        d_mx = dyd * gu * silu_d
        dv_parts = []
        gb = None
        for ch in range(tm // CHUNK):
            dmc = d_mx[ch * CHUNK:(ch + 1) * CHUNK, :]
            gb = dmc if gb is None else gb + dmc
            dmb = dmc.astype(BF16)
            vc = vb[ch * CHUNK:(ch + 1) * CHUNK, :]
            dvc = None
            for h in range(4):
                gws[h] += _dot_nt(jnp.where(group == h, dmb, jnp.zeros_like(dmb)), vc)
                part = jnp.where(group == h, _dot(wst_ref[h], dmb), 0.0)
                dvc = part if dvc is None else dvc + part
            dv_parts.append(dvc)
        s['gbacc'][...] += gb
        d_v = dv_parts[0] if len(dv_parts) == 1 else jnp.concatenate(dv_parts, axis=0)
        g256[5:6, :] += _rowsum(d_v * xh_d)
        g256[6:7, :] += _rowsum(d_v)
        d_gv = _layer_norm_grad(d_v, xh_d, rstd_d, p['v256'][5:6, :])
        put(10, d_gv * _gelu_grad(d_vv, th_v))

        @pl.when(i == nt - 1)
        def _():
            row = lax.broadcasted_iota(jnp.int32, (CHUNK, CHUNK), 0)
            col = lax.broadcasted_iota(jnp.int32, (CHUNK, CHUNK), 1)
            for h in range(4):
                gws[h] = jnp.where(col <= row, gws[h], 0.0)
            head_of_lane = lax.broadcasted_iota(jnp.int32, (GW, CHUNK), 0) // (GW // 4)
            sel = jnp.where(head_of_lane == lax.broadcasted_iota(jnp.int32, (GW, CHUNK), 1), 1.0, 0.0)
            gbs[...] = jnp.dot(s['gbacc'][...], sel, preferred_element_type=F32, precision=lax.Precision.HIGHEST)

    def const(*shape):
        nd = len(shape)
        return pl.BlockSpec(shape, lambda i, _nd=nd: (0,) * _nd)

    def rev(i):
        return nt - 1 - i

    out_shapes = (jax.ShapeDtypeStruct((t, DIN), BF16), jax.ShapeDtypeStruct((nl, N_CHIP, GW, D), F32),
                  jax.ShapeDtypeStruct((GW, GW), F32), jax.ShapeDtypeStruct((GW, GW), F32),
                  jax.ShapeDtypeStruct((4, CHUNK, CHUNK), F32), jax.ShapeDtypeStruct((CHUNK, CHUNK), F32),
                  jax.ShapeDtypeStruct((32, GW), F32), jax.ShapeDtypeStruct((16, GW), F32),
                  jax.ShapeDtypeStruct((8, D), F32))
    out_specs = (pl.BlockSpec((tm, DIN), lambda i: (rev(i), 0)),
                 pl.BlockSpec((None, N_CHIP, GW, D), lambda i: (l, 0, 0, 0)), const(GW, GW),
                 const(GW, GW), const(4, CHUNK, CHUNK), const(CHUNK, CHUNK), const(32, GW), const(16, GW),
                 const(8, D))
    scratch = list(_mix_scratch(tm).values()) + list(bwd_scratch.values())
    n_in = 5 + n_p + 1
    return pl.pallas_call(
        body, name=f"bwd_mix_{l}", grid=(nt,),
        in_specs=[pl.BlockSpec((tm, D), lambda i: (rev(i), 0)), pl.BlockSpec((tm, D), lambda i: (rev(i), 0)),
                  pl.BlockSpec((tm, DIN), lambda i: (rev(i), 0)),
                  pl.BlockSpec((HALO, DIN), lambda i: (jnp.maximum(rev(i) * per_halo - 1, 0), 0)),
                  pl.BlockSpec((8, D), lambda i: (0, 0))]
        + _mix_param_specs(l)
        + [pl.BlockSpec((None, 4, CHUNK, CHUNK), lambda i: (l, 0, 0, 0))]
        + [pl.BlockSpec(memory_space=pl.ANY)] * len(carried),
        out_specs=out_specs, out_shape=out_shapes, scratch_shapes=scratch,
        input_output_aliases={n_in: 1} if carried else {},
        compiler_params=_params(("arbitrary",)),
    )(dxo, y, z, z, vecs, *[mp[k] for k in _MIX_PARAMS], wst, *carried)


def _final(x, target, fg, tt):
    t = x.shape[0]
    nt = t // tt

    def body(x_ref, t_ref, g_ref, dx_ref, sums_ref):
        i = pl.program_id(0)

        @pl.when(i == 0)
        def _():
            sums_ref[...] = jnp.zeros_like(sums_ref)

        xv = x_ref[...]
        g = g_ref[0:1, :]
        r = lax.rsqrt(jnp.mean(xv * xv, axis=-1, keepdims=True) + EPS)
        xhat = xv * r
        err = xhat * g - t_ref[...]
        sums_ref[1:2, :] += _rowsum(err * err) * (0.5 / D)
        dyv = err * (1.0 / D)
        sums_ref[0:1, :] += _rowsum(dyv * xhat)
        dxh = dyv * g
        dx_ref[...] = r * (dxh - xhat * jnp.mean(dxh * xhat, axis=-1, keepdims=True))

        @pl.when(i == nt - 1)
        def _():
            sums_ref[2:3, :] = jnp.broadcast_to(jnp.sum(sums_ref[1:2, :], axis=-1, keepdims=True), (1, D))

    return pl.pallas_call(
        body, name="final_loss", grid=(nt,),
        in_specs=[pl.BlockSpec((tt, D), lambda i: (i, 0)), pl.BlockSpec((tt, D), lambda i: (i, 0)),
                  pl.BlockSpec((8, D), lambda i: (0, 0))],
        out_specs=(pl.BlockSpec((tt, D), lambda i: (i, 0)), pl.BlockSpec((8, D), lambda i: (0, 0))),
        out_shape=(jax.ShapeDtypeStruct((t, D), F32), jax.ShapeDtypeStruct((8, D), F32)),
        compiler_params=_params(("arbitrary",)),
    )(x, target, fg)


def _adamw(w, g, m, v, tag):
    rows, cols = w.shape
    rt = rows
    for cand in (512, 256, 128, 64, 32, 16, 8):
        if rows % cand == 0:
            rt = cand
            break

    def body(w_ref, g_ref, m_ref, v_ref, d_ref, nm_ref, nv_ref):
        d_ref[...], nm_ref[...], nv_ref[...] = _adamw_update(g_ref[...], w_ref[...], m_ref[...], v_ref[...])

    spec = pl.BlockSpec((rt, cols), lambda i: (i, 0))
    return pl.pallas_call(
        body, name=f"adamw_{tag}", grid=(rows // rt,), in_specs=[spec] * 4, out_specs=(spec,) * 3,
        out_shape=(jax.ShapeDtypeStruct(w.shape, F32),) * 3, compiler_params=_params(("parallel",)),
    )(w, g, m, v)


def _adamw_update(gv, w, m, v):
    nm = ADAM_B1 * m + (1.0 - ADAM_B1) * gv
    nv = ADAM_B2 * v + (1.0 - ADAM_B2) * (gv * gv)
    m_hat = nm / (1.0 - ADAM_B1 ** ADAM_STEP)
    v_hat = nv / (1.0 - ADAM_B2 ** ADAM_STEP)
    return -ADAM_LR * (m_hat / (jnp.sqrt(v_hat) + ADAM_EPS) + ADAM_WD * w), nm, nv


def _adamw_many(ws, gs, ms, vs):
    n = len(ws)

    def body(*refs):
        w_refs, g_refs, m_refs, v_refs = (refs[k * n:(k + 1) * n] for k in range(4))
        d_refs, nm_refs, nv_refs = (refs[(4 + k) * n:(5 + k) * n] for k in range(3))
        for k in range(n):
            d_refs[k][...], nm_refs[k][...], nv_refs[k][...] = _adamw_update(
                g_refs[k][...], w_refs[k][...], m_refs[k][...], v_refs[k][...])

    shapes = tuple(jax.ShapeDtypeStruct(w.shape, F32) for w in ws)
    out = pl.pallas_call(body, name="adamw_small", out_shape=shapes * 3)(*ws, *gs, *ms, *vs)
    return out[:n], out[n:2 * n], out[2 * n:]


def _adamw_halves(w, mine, theirs, m, v, cidx, tag):
    nl, r, cdim = w.shape
    r2 = r // 2
    rt = min(r2, 256)
    nrt = r2 // rt

    def body(c_ref, a_ref, b_ref, w_ref, m_ref, v_ref, g_ref, d_ref, nm_ref, nv_ref):
        gv = jnp.where(pl.program_id(1) == c_ref[0], a_ref[...], b_ref[...])
        g_ref[...] = gv
        d_ref[...], nm_ref[...], nv_ref[...] = _adamw_update(gv, w_ref[...], m_ref[...], v_ref[...])

    half = pl.BlockSpec((None, rt, cdim), lambda l, h, i, c: (l, i, 0))
    whole = pl.BlockSpec((None, rt, cdim), lambda l, h, i, c: (l, h * nrt + i, 0))
    grid_spec = pltpu.PrefetchScalarGridSpec(
        num_scalar_prefetch=1, grid=(nl, 2, nrt), in_specs=[half, half, whole, whole, whole],
        out_specs=(whole,) * 4)
    return pl.pallas_call(
        body, name=f"adamw_{tag}", grid_spec=grid_spec,
        out_shape=(jax.ShapeDtypeStruct(w.shape, F32),) * 4,
        compiler_params=_params(("parallel", "parallel", "parallel")),
    )(cidx, mine, theirs, w, m, v)


def _pack(arrays, row_multiple=8):
    parts, offsets, off = [], [], 0
    for a in arrays:
        n = int(np.prod(a.shape))
        padded = -(-n // 1024) * 1024
        flat = a.reshape(-1).astype(F32)
        if padded != n:
            flat = jnp.concatenate([flat, jnp.zeros((padded - n,), F32)])
        parts.append(flat)
        offsets.append((off, n, a.shape))
        off += padded
    tail = -off % (row_multiple * 128)
    if tail:
        parts.append(jnp.zeros((tail,), F32))
    return jnp.concatenate(parts).reshape(-1, 128), offsets


def _unpack(buf, offsets):
    flat = buf.reshape(-1)
    return [flat[off:off + n].reshape(shape) for off, n, shape in offsets]


def _pad_rows(a, rows):
    return jnp.concatenate([a, jnp.zeros((rows - a.shape[0],) + a.shape[1:], a.dtype)], axis=0)


def kernel(x, c, norm_g, w_ada, b_ada, w_in, w_conv_a, w_pool, pool_scale, w_dw_c, b_dw_c, ln_g_c, ln_b_c, w_pw2_c, b_pw2_c, ln_g_d, ln_b_d, w_s_d, b_s_d, w_out, final_g, loss_target, m_norm_g, m_w_ada, m_b_ada, m_w_in, m_w_conv_a, m_w_pool, m_pool_scale, m_w_dw_c, m_b_dw_c, m_ln_g_c, m_ln_b_c, m_w_pw2_c, m_b_pw2_c, m_ln_g_d, m_ln_b_d, m_w_s_d, m_b_s_d, m_w_out, m_final_g, v_norm_g, v_w_ada, v_b_ada, v_w_in, v_w_conv_a, v_w_pool, v_pool_scale, v_w_dw_c, v_b_dw_c, v_ln_g_c, v_ln_b_c, v_w_pw2_c, v_b_pw2_c, v_ln_g_d, v_ln_b_d, v_w_s_d, v_b_s_d, v_w_out, v_final_g):
    env = dict(locals())
    weights = {n: env[n] for n in WEIGHTS}
    mom_m = {n: env["m_" + n] for n in WEIGHTS}
    mom_v = {n: env["v_" + n] for n in WEIGHTS}
    nl = norm_g.shape[0]
    t = x.shape[1]
    tt, tm = _tiles(t)
    ax, ay, ac = lax.axis_index("x"), lax.axis_index("y"), lax.axis_index("c")
    chip = 2 * ax + ay
    dev = 2 * chip + ac
    cidx = jnp.reshape(ac, (1,)).astype(jnp.int32)
    sidx = jnp.reshape(chip, (1,)).astype(jnp.int32)
    xs, target = x[0], loss_target[0]

    shard_small, shard_off = _pack([c, w_conv_a, w_dw_c, w_pw2_c])
    gathered = _allgather8(shard_small, "gather_small").reshape(N_DEV, -1, 128)
    per_dev = [_unpack(gathered[d], shard_off) for d in range(0, N_DEV, 2)]
    c_all = jnp.concatenate([u[0] for d in range(N_DEV)
                             for u in [_unpack(gathered[d], shard_off[:1])]], axis=0)
    w_conv_full = jnp.concatenate([u[1] for u in per_dev], axis=-1)
    w_dw_full = jnp.concatenate([u[2] for u in per_dev], axis=-1)
    w_pw2_full = jnp.concatenate([u[3] for u in per_dev], axis=1)

    b_ada_s = lax.dynamic_slice_in_dim(b_ada, chip * PIECE, PIECE, axis=1)[:, None, :]
    mod_s = _ada_forward(c_all, w_ada, b_ada_s)
    mod_all = _allgather8(mod_s.reshape(nl * N_DEV, PIECE), "gather_mod").reshape(N_DEV, nl, N_DEV, PIECE)
    mod_rows = lax.dynamic_index_in_dim(mod_all, dev, axis=2, keepdims=False)
    mod = jnp.concatenate([mod_rows[2 * s] for s in range(N_CHIP)], axis=-1)

    w_in_full, w_out_full = _gather_weights(w_in, w_out)

    tril = jnp.tril(jnp.ones((CHUNK, CHUNK), bool))
    ws_masked = jnp.where(tril[None, None], w_s_d, 0.0)
    wbd = jnp.zeros((nl, GW, GW), F32)
    for g in range(4):
        wbd = wbd.at[:, 64 * g:64 * (g + 1), 64 * g:64 * (g + 1)].set(w_pool[:, g])
    v256 = jnp.stack([pool_scale, b_dw_c, ln_g_c, ln_b_c, b_pw2_c, ln_g_d, ln_b_d], axis=1)
    mp = dict(
        wa=_pad_rows(jnp.swapaxes(w_conv_full, 0, 1), 8).swapaxes(0, 1),
        wdw=_pad_rows(jnp.swapaxes(w_dw_full, 0, 1), 32).swapaxes(0, 1),
        v256=_pad_rows(jnp.swapaxes(v256, 0, 1), 16).swapaxes(0, 1),
        wbd=wbd.astype(BF16), wpw=w_pw2_full.astype(BF16), ws=ws_masked.astype(BF16),
        bsd=jnp.repeat(jnp.swapaxes(b_s_d, 1, 2), GW // 4, axis=2),
        wout=w_out_full)
    wst = jnp.swapaxes(ws_masked, 2, 3).astype(BF16)

    def vec_rows(l):
        rows = jnp.stack([mod[l, 0:D], mod[l, D:2 * D], mod[l, 2 * D:3 * D], norm_g[l]], axis=0)
        return _pad_rows(rows, 8)

    xin, zs, ys, vecs = [xs], [], [], []
    for l in range(nl):
        vecs.append(vec_rows(l))
        zl, xn, yl = _fwd_layer(xin[l], vecs[l], w_in_full, mp, l, tm)
        zs.append(zl)
        xin.append(xn)
        ys.append(yl)
    dx, fin_sums = _final(xin[nl], target, _pad_rows(final_g[None, :], 8), tt)

    g_in, g_out, small, dmod = None, None, [None] * nl, [None] * nl
    for l in reversed(range(nl)):
        dz, g_out, gwbd, gwpw, gws, gbs, gwdw, g256, g1024 = _bwd_mix(
            dx, ys[l], zs[l], vecs[l], mp, wst, l, tm, gwout_all=g_out)
        dx, g_in, sums = _bwd_in(dz, xin[l], dx, vecs[l], w_in_full, l, tt, gw_all=g_in)
        dmod[l] = jnp.concatenate([sums[0], sums[1], g1024[0]])
        small[l] = dict(
            norm_g=sums[2], w_conv_a=g256[8:8 + CONV_A],
            w_pool=jnp.stack([gwbd[64 * g:64 * (g + 1), 64 * g:64 * (g + 1)] for g in range(4)]),
            pool_scale=g256[0], w_dw_c=gwdw[:CONV_C], b_dw_c=g256[1], ln_g_c=g256[2], ln_b_c=g256[3],
            w_pw2_c=gwpw, b_pw2_c=g256[4], ln_g_d=g256[5], ln_b_d=g256[6], w_s_d=gws, b_s_d=gbs[:, :4].T)
    grad_x = dx[None]

    dmod_all, dmod_sum = _allgather8(jnp.stack(dmod).reshape(nl * 3 * D // 128, 128), "gather_dmod", reduce=True)
    dmod_all = dmod_all.reshape(N_DEV, nl, 3 * D)
    grad_b_ada = dmod_sum.reshape(nl, 3 * D)
    dmod_s = jnp.swapaxes(lax.dynamic_slice_in_dim(dmod_all, chip * PIECE, PIECE, axis=2), 0, 1)
    grad_w_ada = _ada_backward(c_all, dmod_s)

    small_names = [n for n in WEIGHTS if n not in BIG and n not in ('b_ada', 'final_g')]
    stacked = [jnp.stack([small[l][n] for l in range(nl)]) for n in small_names]
    small_buf, small_off = _pack(stacked + [fin_sums[0], fin_sums[2, 0:1]], 8 * N_DEV)
    reduced = _unpack(_allreduce8(small_buf, "allreduce_small"), small_off)
    grads = dict(zip(small_names, reduced[:len(small_names)]))
    grads['final_g'] = reduced[-2]
    loss = reduced[-1][0]
    grads['b_ada'] = grad_b_ada
    grads['w_ada'] = grad_w_ada
    grads['w_conv_a'] = lax.dynamic_slice_in_dim(grads['w_conv_a'], chip * 64, 64, axis=2)
    grads['w_dw_c'] = lax.dynamic_slice_in_dim(grads['w_dw_c'], chip * 64, 64, axis=2)
    grads['w_pw2_c'] = lax.dynamic_slice_in_dim(grads['w_pw2_c'], chip * 64, 64, axis=1)

    halves = {'w_in': _reduce_scatter(g_in, cidx, sidx, "in"), 'w_out': _reduce_scatter(g_out, cidx, sidx, "out")}

    delta, new_m, new_v = {}, {}, {}
    for n in ('w_in', 'w_out'):
        grads[n], delta[n], new_m[n], new_v[n] = _adamw_halves(
            weights[n], *halves[n], mom_m[n], mom_v[n], cidx, n)
    shape = w_ada.shape
    as2d = lambda a: a.reshape(-1, shape[-1])
    d2, m2, v2 = _adamw(as2d(w_ada), as2d(grads['w_ada']), as2d(m_w_ada), as2d(v_w_ada), 'w_ada')
    delta['w_ada'], new_m['w_ada'], new_v['w_ada'] = d2.reshape(shape), m2.reshape(shape), v2.reshape(shape)
    rest = [n for n in WEIGHTS if n not in BIG]
    ds, nms, nvs = _adamw_many([weights[n] for n in rest], [grads[n] for n in rest],
                               [mom_m[n] for n in rest], [mom_v[n] for n in rest])
    for n, dn, mn, vn in zip(rest, ds, nms, nvs):
        delta[n], new_m[n], new_v[n] = dn, mn, vn

    return (loss, grad_x, *[grads[n] for n in WEIGHTS], *[delta[n] for n in WEIGHTS],
            *[new_m[n] for n in WEIGHTS], *[new_v[n] for n in WEIGHTS])
```

```python
import functools
import math

import jax
import jax.numpy as jnp
import numpy as np
from jax import lax
from jax.experimental import pallas as pl
from jax.experimental.pallas import tpu as pltpu

F32 = jnp.float32
BF16 = jnp.bfloat16
MESH = pl.DeviceIdType.MESH

D = 1024
DIN = 3072
GW = 256
PIECE = 768
N_CHIP = 4
N_DEV = 8
HALO = 32
CONV_A = 3
CONV_C = 31
CHUNK = 128
CONV_ROWS = 64
ROW_BLOCK = 32
MIX_FWD_POINTS = 18
EPS = 1e-6
VMEM_LIMIT = 56 * 1024 * 1024

ADAM_LR, ADAM_B1, ADAM_B2, ADAM_EPS, ADAM_WD, ADAM_STEP = 0.001, 0.9, 0.999, 1e-08, 0.01, 10
GELU_K = math.sqrt(2.0 / math.pi)
GELU_C = 0.044715

WEIGHTS = ['norm_g', 'w_ada', 'b_ada', 'w_in', 'w_conv_a', 'w_pool', 'pool_scale', 'w_dw_c', 'b_dw_c', 'ln_g_c',
           'ln_b_c', 'w_pw2_c', 'b_pw2_c', 'ln_g_d', 'ln_b_d', 'w_s_d', 'b_s_d', 'w_out', 'final_g']
BIG = ('w_ada', 'w_in', 'w_out')


def _tiles(t):
    if t % 512 == 0 and t >= 2048:
        return 512, 256
    return 128, 128


def _params(sem, limit=VMEM_LIMIT):
    return pltpu.CompilerParams(dimension_semantics=sem, vmem_limit_bytes=limit)


def _sig(x):
    return jax.nn.sigmoid(x)


def _gelu(x):
    th = jnp.tanh(GELU_K * (x + GELU_C * x * x * x))
    return 0.5 * x * (1.0 + th), th


def _gelu_grad(x, th):
    return 0.5 * (1.0 + th) + 0.5 * x * (1.0 - th * th) * GELU_K * (1.0 + 3.0 * GELU_C * x * x)


def _dot(a, b):
    return jnp.dot(a, b, preferred_element_type=F32)


def _dot_nt(a, b):
    return lax.dot_general(a, b, (((1,), (1,)), ((), ())), preferred_element_type=F32)


def _dot_tn(a, b):
    return lax.dot_general(a, b, (((0,), (0,)), ((), ())), preferred_element_type=F32)


def _rowsum(x):
    return jnp.sum(x, axis=0, keepdims=True)


def _allgather8(v, name, reduce=False):
    m_per, n = v.shape

    def body(x_ref, out_ref, *rest):
        if reduce:
            sum_ref, send_sems, recv_sems, local_sem = rest
        else:
            send_sems, recv_sems, local_sem = rest
        x, y, c = lax.axis_index("x"), lax.axis_index("y"), lax.axis_index("c")
        me, sibling = (x, y, c), (x, y, 1 - c)
        chips = [(1 - x, y), (x, 1 - y), (1 - x, 1 - y)]

        def rows(px, py, pc):
            return out_ref.at[pl.ds((4 * px + 2 * py + pc) * m_per, m_per), :]

        def copy(k, block, to, src=None):
            return pltpu.make_async_remote_copy(
                src_ref=rows(*block) if src is None else src, dst_ref=rows(*block),
                send_sem=send_sems.at[k], recv_sem=recv_sems.at[k], device_id=to, device_id_type=MESH)

        mine = pltpu.make_async_copy(x_ref, rows(*me), local_sem)
        mine.start()
        first = [copy(0, me, sibling, src=x_ref)]
        first += [copy(1 + j, me, (*chip, c), src=x_ref) for j, chip in enumerate(chips)]
        for cp in first:
            cp.start()
        passed = [copy(4 + j, (*chip, c), sibling) for j, chip in enumerate(chips)]
        for j, chip in enumerate(chips):
            copy(1 + j, (*chip, c), me).wait_recv()
            passed[j].start()
        copy(0, sibling, me).wait_recv()
        for j, chip in enumerate(chips):
            copy(4 + j, (*chip, 1 - c), me).wait_recv()
        for cp in first + passed:
            cp.wait_send()
        mine.wait()
        if reduce:
            acc = out_ref[0:m_per, :]
            for d in range(1, N_DEV):
                acc = acc + out_ref[d * m_per:(d + 1) * m_per, :]
            sum_ref[...] = acc

    out_shape = [jax.ShapeDtypeStruct((N_DEV * m_per, n), v.dtype)]
    out_specs = [pl.BlockSpec(memory_space=pltpu.VMEM)]
    if reduce:
        out_shape.append(jax.ShapeDtypeStruct((m_per, n), v.dtype))
        out_specs.append(pl.BlockSpec(memory_space=pltpu.VMEM))
    res = pl.pallas_call(
        body, name=name, out_shape=tuple(out_shape),
        in_specs=[pl.BlockSpec(memory_space=pltpu.VMEM)], out_specs=tuple(out_specs),
        scratch_shapes=[pltpu.SemaphoreType.DMA((7,)), pltpu.SemaphoreType.DMA((7,)), pltpu.SemaphoreType.DMA],
        compiler_params=pltpu.CompilerParams(vmem_limit_bytes=VMEM_LIMIT),
    )(v)
    return res if reduce else res[0]


def _allreduce8(v, name):
    rows, n = v.shape
    rc = rows // N_DEV

    def body(x_ref, sum_ref, stage, send1, recv1, send2, recv2):
        x, y, c = lax.axis_index("x"), lax.axis_index("y"), lax.axis_index("c")
        me = 4 * x + 2 * y + c
        peers = []
        for k in range(1, N_DEV):
            kx, ky, kc = (k >> 2) & 1, (k >> 1) & 1, k & 1
            peers.append((1 - x if kx else x, 1 - y if ky else y, 1 - c if kc else c))

        def chunk(ref, d):
            return ref.at[pl.ds(d * rc, rc), :]

        scatter, gather = [], []
        for k, (px, py, pc) in enumerate(peers):
            scatter.append(pltpu.make_async_remote_copy(
                src_ref=chunk(x_ref, 4 * px + 2 * py + pc), dst_ref=stage.at[me], send_sem=send1.at[k],
                recv_sem=recv1.at[k], device_id=(px, py, pc), device_id_type=MESH))
        for cp in scatter:
            cp.start()
        stage[me] = x_ref[pl.ds(me * rc, rc), :]
        for k, (px, py, pc) in enumerate(peers):
            pltpu.make_async_remote_copy(
                src_ref=chunk(x_ref, me), dst_ref=stage.at[4 * px + 2 * py + pc], send_sem=send1.at[k],
                recv_sem=recv1.at[k], device_id=(px, py, pc), device_id_type=MESH).wait_recv()
        total = stage[0]
        for d in range(1, N_DEV):
            total = total + stage[d]
        sum_ref[pl.ds(me * rc, rc), :] = total
        for k, (px, py, pc) in enumerate(peers):
            gather.append(pltpu.make_async_remote_copy(
                src_ref=chunk(sum_ref, me), dst_ref=chunk(sum_ref, me), send_sem=send2.at[k],
                recv_sem=recv2.at[k], device_id=(px, py, pc), device_id_type=MESH))
        for cp in gather:
            cp.start()
        for k, (px, py, pc) in enumerate(peers):
            theirs = 4 * px + 2 * py + pc
            pltpu.make_async_remote_copy(
                src_ref=chunk(sum_ref, theirs), dst_ref=chunk(sum_ref, theirs), send_sem=send2.at[k],
                recv_sem=recv2.at[k], device_id=(px, py, pc), device_id_type=MESH).wait_recv()
        for cp in scatter + gather:
            cp.wait_send()

    return pl.pallas_call(
        body, name=name, out_shape=jax.ShapeDtypeStruct((rows, n), v.dtype),
        in_specs=[pl.BlockSpec(memory_space=pltpu.VMEM)], out_specs=pl.BlockSpec(memory_space=pltpu.VMEM),
        scratch_shapes=[pltpu.VMEM((N_DEV, rc, n), v.dtype)] + [pltpu.SemaphoreType.DMA((N_DEV - 1,))] * 4,
        compiler_params=pltpu.CompilerParams(vmem_limit_bytes=VMEM_LIMIT),
    )(v)


def _gather_weights(w_in_s, w_out_s):
    nl = w_in_s.shape[0]
    shapes = ((nl, N_CHIP) + w_in_s.shape[1:], (nl, N_CHIP) + w_out_s.shape[1:])
    n_sem = 2 * 3 * 2

    def body(win_ref, wout_ref, win_o, wout_o, send_sems, recv_sems):
        x, y, c = lax.axis_index("x"), lax.axis_index("y"), lax.axis_index("c")
        s = 2 * x + y
        sibling = (x, y, 1 - c)
        chips = [(1 - x, y), (x, 1 - y), (1 - x, 1 - y)]
        outs = (win_o, wout_o)
        for src, dst in ((win_ref, win_o), (wout_ref, wout_o)):
            rows = src.shape[1]
            step = min(rows, 256)
            for l in range(nl):
                for r in range(0, rows, step):
                    dst[l, s, r:r + step, :] = src[l, r:r + step, :].astype(BF16)

        def copy(k, ref, piece, half, to):
            r2 = ref.shape[2] // 2
            rows = ref.at[:, piece, pl.ds(half * r2, r2), :]
            return pltpu.make_async_remote_copy(
                src_ref=rows, dst_ref=rows, send_sem=send_sems.at[k], recv_sem=recv_sems.at[k],
                device_id=to, device_id_type=MESH)

        sends = [copy(2 * j + a, ref, s, c, (px, py, c)) for j, (px, py) in enumerate(chips)
                 for a, ref in enumerate(outs)]
        for cp in sends:
            cp.start()
        passed = []
        for j, (px, py) in enumerate(chips):
            for a, ref in enumerate(outs):
                copy(2 * j + a, ref, 2 * px + py, c, (px, py, c)).wait_recv()
                passed.append(copy(6 + 2 * j + a, ref, 2 * px + py, c, sibling))
                passed[-1].start()
        for j, (px, py) in enumerate(chips):
            for a, ref in enumerate(outs):
                copy(6 + 2 * j + a, ref, 2 * px + py, 1 - c, sibling).wait_recv()
        for cp in sends + passed:
            cp.wait_send()

    return pl.pallas_call(
        body, name="gather_weights",
        out_shape=tuple(jax.ShapeDtypeStruct(s, BF16) for s in shapes),
        in_specs=[pl.BlockSpec(memory_space=pltpu.VMEM)] * 2,
        out_specs=tuple(pl.BlockSpec(memory_space=pltpu.VMEM) for _ in shapes),
        scratch_shapes=[pltpu.SemaphoreType.DMA((n_sem,)), pltpu.SemaphoreType.DMA((n_sem,))],
        compiler_params=pltpu.CompilerParams(vmem_limit_bytes=VMEM_LIMIT),
    )(w_in_s, w_out_s)


def _send_half_to_sibling(g, tag):
    nl, nc, r, cdim = g.shape
    half = r // 2

    def body(g_ref, recv_ref, send_sem, recv_sem):
        x, y, c = lax.axis_index("x"), lax.axis_index("y"), lax.axis_index("c")
        cp = pltpu.make_async_remote_copy(
            src_ref=g_ref.at[:, :, pl.ds((1 - c) * half, half), :], dst_ref=recv_ref,
            send_sem=send_sem, recv_sem=recv_sem, device_id=(x, y, 1 - c), device_id_type=MESH)
        cp.start()
        cp.wait()

    return pl.pallas_call(
        body, name=f"rs_sibling_{tag}",
        out_shape=jax.ShapeDtypeStruct((nl, nc, half, cdim), g.dtype),
        in_specs=[pl.BlockSpec(memory_space=pl.ANY)], out_specs=pl.BlockSpec(memory_space=pl.ANY),
        scratch_shapes=[pltpu.SemaphoreType.DMA, pltpu.SemaphoreType.DMA],
    )(g)


def _add_half(g, recv, cidx, tag):
    nl, nc, r, cdim = g.shape
    half = r // 2
    rt = min(half, 128)
    nrt = half // rt

    def body(c_ref, a_ref, b_ref, o_ref, ob_ref):
        c = c_ref[0]
        first = jnp.where(c == 0, a_ref[...], b_ref[...])
        second = jnp.where(c == 0, b_ref[...], a_ref[...])
        total = first + second
        o_ref[...] = total
        ob_ref[...] = total.astype(BF16)

    out_spec = pl.BlockSpec((None, None, rt, cdim), lambda l, j, i, c: (l, j, i, 0))
    grid_spec = pltpu.PrefetchScalarGridSpec(
        num_scalar_prefetch=1, grid=(nl, nc, nrt),
        in_specs=[pl.BlockSpec((None, None, rt, cdim), lambda l, j, i, c: (l, j, c[0] * nrt + i, 0)),
                  pl.BlockSpec((None, None, rt, cdim), lambda l, j, i, c: (l, j, i, 0))],
        out_specs=(out_spec, out_spec))
    return pl.pallas_call(
        body, name=f"rs_add_sibling_{tag}", grid_spec=grid_spec,
        out_shape=(jax.ShapeDtypeStruct((nl, nc, half, cdim), g.dtype),
                   jax.ShapeDtypeStruct((nl, nc, half, cdim), BF16)),
        compiler_params=_params(("parallel", "parallel", "parallel")),
    )(cidx, g, recv)


def _exchange_chips(h, tag):
    nl, nc, r2, cdim = h.shape

    def body(h_ref, recv_ref, send_sems, recv_sems):
        x, y, c = lax.axis_index("x"), lax.axis_index("y"), lax.axis_index("c")
        chips = [(1 - x, y), (x, 1 - y), (1 - x, 1 - y)]
        cps = []
        for k, (px, py) in enumerate(chips):
            cps.append(pltpu.make_async_remote_copy(
                src_ref=h_ref.at[:, 2 * px + py], dst_ref=recv_ref.at[k], send_sem=send_sems.at[k],
                recv_sem=recv_sems.at[k], device_id=(px, py, c), device_id_type=MESH))
        for cp in cps:
            cp.start()
        for cp in cps:
            cp.wait()

    return pl.pallas_call(
        body, name=f"rs_chips_{tag}",
        out_shape=jax.ShapeDtypeStruct((3, nl, r2, cdim), h.dtype),
        in_specs=[pl.BlockSpec(memory_space=pl.ANY)], out_specs=pl.BlockSpec(memory_space=pl.ANY),
        scratch_shapes=[pltpu.SemaphoreType.DMA((3,)), pltpu.SemaphoreType.DMA((3,))],
    )(h)


def _add_chips(h, recv, sidx, tag):
    nl, nc, r2, cdim = h.shape
    rt = min(r2, 128)

    def body(s_ref, own_ref, r_ref, o_ref):
        s = s_ref[0]
        got = [r_ref[k].astype(F32) for k in range(3)]
        total = None
        for chip in range(N_CHIP):
            term = jnp.where(s == chip, own_ref[...],
                             jnp.where((s ^ 2) == chip, got[0], jnp.where((s ^ 1) == chip, got[1], got[2])))
            total = term if total is None else total + term
        o_ref[...] = total

    grid_spec = pltpu.PrefetchScalarGridSpec(
        num_scalar_prefetch=1, grid=(nl, r2 // rt),
        in_specs=[pl.BlockSpec((None, None, rt, cdim), lambda l, i, s: (l, s[0], i, 0)),
                  pl.BlockSpec((3, None, rt, cdim), lambda l, i, s: (0, l, i, 0))],
        out_specs=pl.BlockSpec((None, rt, cdim), lambda l, i, s: (l, i, 0)))
    return pl.pallas_call(
        body, name=f"rs_add_chips_{tag}", grid_spec=grid_spec,
        out_shape=jax.ShapeDtypeStruct((nl, r2, cdim), h.dtype),
        compiler_params=_params(("parallel", "parallel")),
    )(sidx, h, recv)


def _swap_halves(red, tag):
    def body(red_ref, out_ref, send_sem, recv_sem):
        x, y, c = lax.axis_index("x"), lax.axis_index("y"), lax.axis_index("c")
        cp = pltpu.make_async_remote_copy(
            src_ref=red_ref, dst_ref=out_ref, send_sem=send_sem, recv_sem=recv_sem,
            device_id=(x, y, 1 - c), device_id_type=MESH)
        cp.start()
        cp.wait()

    return pl.pallas_call(
        body, name=f"rs_swap_{tag}", out_shape=jax.ShapeDtypeStruct(red.shape, red.dtype),
        in_specs=[pl.BlockSpec(memory_space=pl.ANY)], out_specs=pl.BlockSpec(memory_space=pl.ANY),
        scratch_shapes=[pltpu.SemaphoreType.DMA, pltpu.SemaphoreType.DMA],
    )(red)


def _reduce_scatter(g, cidx, sidx, tag):
    recv = _send_half_to_sibling(g, tag)
    chip_sum, chip_sum_bf16 = _add_half(g, recv, cidx, tag)
    got = _exchange_chips(chip_sum_bf16, tag)
    red = _add_chips(chip_sum, got, sidx, tag)
    return red, _swap_halves(red, tag)


def _ada_forward(c_all, w_ada_s, b_s):
    nl = w_ada_s.shape[0]

    def body(c_ref, w_ref, b_ref, o_ref):
        cv = c_ref[...]
        ca = (cv * _sig(cv)).astype(BF16)
        for l in range(nl):
            o_ref[l] = _dot(ca, w_ref[l].astype(BF16)) + b_ref[l]

    return pl.pallas_call(
        body, name="ada_forward", out_shape=jax.ShapeDtypeStruct((nl, N_DEV, PIECE), F32),
        compiler_params=pltpu.CompilerParams(vmem_limit_bytes=VMEM_LIMIT),
    )(c_all, w_ada_s, b_s)


def _ada_backward(c_all, dmod_s):
    nl = dmod_s.shape[0]

    def body(c_ref, d_ref, o_ref):
        cv = c_ref[...]
        ca = (cv * _sig(cv)).astype(BF16)
        for l in range(nl):
            o_ref[l] = _dot_tn(ca, d_ref[l].astype(BF16))

    return pl.pallas_call(
        body, name="ada_backward", out_shape=jax.ShapeDtypeStruct((nl, D, PIECE), F32),
        compiler_params=pltpu.CompilerParams(vmem_limit_bytes=VMEM_LIMIT),
    )(c_all, dmod_s)


def _normed(xv, vecs):
    r = lax.rsqrt(jnp.mean(xv * xv, axis=-1, keepdims=True) + EPS)
    xhat = xv * r
    hn = xhat * vecs[3:4, :]
    return r, xhat, hn


def _bwd_in(dz, x, dxo, vecs, w, l, tt, gw_all=None):
    t = x.shape[0]
    nt = t // tt
    nl = w.shape[0]

    def body(dz_ref, x_ref, dxo_ref, v_ref, w_any, *rest):
        dx_ref, gw_any, sums_ref, w_vmem, gw_acc, sem = rest[-6:]
        i = pl.program_id(0)

        @pl.when(i == 0)
        def _():
            cp = pltpu.make_async_copy(w_any.at[l], w_vmem, sem)
            cp.start()
            cp.wait()
            gw_acc[...] = jnp.zeros_like(gw_acc)
            sums_ref[...] = jnp.zeros_like(sums_ref)

        vecs_v = v_ref[...]
        r, xhat, hn = _normed(x_ref[...], vecs_v)
        one_scale = 1.0 + vecs_v[1:2, :]
        hb = (hn * one_scale + vecs_v[0:1, :]).astype(BF16)
        dh = None
        for j in range(N_CHIP):
            dzj = dz_ref[:, PIECE * j:PIECE * (j + 1)]
            part = _dot_nt(dzj, w_vmem[j])
            dh = part if dh is None else dh + part
            gw_acc[j] += _dot_tn(hb, dzj)
        sums_ref[0:1, :] += _rowsum(dh)
        sums_ref[1:2, :] += _rowsum(dh * hn)
        sums_ref[2:3, :] += _rowsum(dh * one_scale * xhat)
        dxh = dh * (vecs_v[3:4, :] * one_scale)
        dx_ref[...] = dxo_ref[...] + r * (dxh - xhat * jnp.mean(dxh * xhat, axis=-1, keepdims=True))

        @pl.when(i == nt - 1)
        def _():
            cp = pltpu.make_async_copy(gw_acc, gw_any.at[l], sem)
            cp.start()
            cp.wait()

    carried = [] if gw_all is None else [gw_all]
    return pl.pallas_call(
        body, name=f"bwd_in_{l}", grid=(nt,),
        in_specs=[pl.BlockSpec((tt, DIN), lambda i: (i, 0)), pl.BlockSpec((tt, D), lambda i: (i, 0)),
                  pl.BlockSpec((tt, D), lambda i: (i, 0)), pl.BlockSpec((8, D), lambda i: (0, 0)),
                  pl.BlockSpec(memory_space=pl.ANY)] + [pl.BlockSpec(memory_space=pl.ANY)] * len(carried),
        out_specs=(pl.BlockSpec((tt, D), lambda i: (i, 0)), pl.BlockSpec(memory_space=pl.ANY),
                   pl.BlockSpec((8, D), lambda i: (0, 0))),
        out_shape=(jax.ShapeDtypeStruct((t, D), F32), jax.ShapeDtypeStruct((nl, N_CHIP, D, PIECE), F32),
                   jax.ShapeDtypeStruct((8, D), F32)),
        scratch_shapes=[pltpu.VMEM((N_CHIP, D, PIECE), BF16), pltpu.VMEM((N_CHIP, D, PIECE), F32),
                        pltpu.SemaphoreType.DMA],
        input_output_aliases={5: 1} if carried else {},
        compiler_params=_params(("arbitrary",)),
    )(dz, x, dxo, vecs, w, *carried)


def _col(ref, k):
    return ref[:, GW * k:GW * (k + 1)]


def _lane_group():
    return lax.broadcasted_iota(jnp.int32, (1, GW), 1) // (GW // 4)


def _pool_window(group):
    return jnp.where(group == 0, 2.0, jnp.where(group == 1, 4.0, jnp.where(group == 2, 8.0, 16.0)))


def _by_group(group, a, b, c, d):
    return jnp.where(group == 0, a, jnp.where(group == 1, b, jnp.where(group == 2, c, d)))


def _layer_norm(x, g, b):
    mu = jnp.mean(x, axis=-1, keepdims=True)
    xc = x - mu
    rstd = lax.rsqrt(jnp.mean(xc * xc, axis=-1, keepdims=True) + EPS)
    xh = xc * rstd
    return xh * g + b, xh, rstd


def _layer_norm_grad(d_out, xh, rstd, g):
    dxh = d_out * g
    return rstd * (dxh - jnp.mean(dxh, axis=-1, keepdims=True) - xh * jnp.mean(dxh * xh, axis=-1, keepdims=True))


def _shifted_copies(ext_ref, sh_ref, n):
    for b in range(1, 8):
        sh_ref[b, 0:n - 8, :] = ext_ref[b:b + n - 8, :]


def _rows_at(ext_ref, sh_ref, start, rows):
    b = start % 8
    if b == 0 or sh_ref is None:
        return ext_ref[start:start + rows, :]
    return sh_ref[b, start - b:start - b + rows, :]


class _Beside:
    def __init__(self, thunks=(), points=1):
        self.thunks, self.points, self.seen, self.done = list(thunks), max(points, 1), 0, 0

    def _run_to(self, due):
        while self.done < min(due, len(self.thunks)):
            self.thunks[self.done]()
            self.done += 1

    def here(self):
        self.seen += 1
        self._run_to(-(-self.seen * len(self.thunks) // self.points))

    def rest(self):
        self._run_to(len(self.thunks))


def _conv_taps(ext_ref, w_ref, n_taps, first_row, tm, out_ref, flip=False, sh_ref=None, beside=None):
    for rb in range(0, tm, CONV_ROWS):
        acc = None
        for k in range(n_taps):
            wk = n_taps - 1 - k if flip else k
            term = w_ref[wk:wk + 1, :] * _rows_at(ext_ref, sh_ref, first_row + rb + k, CONV_ROWS)
            acc = term if acc is None else acc + term
        out_ref[rb:rb + CONV_ROWS, :] = acc
        if beside is not None:
            beside.here()


def _mix_forward(zc, zp, first, row0, tm, p, s, after, beside=None, conv_c=None):
    f = {}
    beside = beside or _Beside()
    keep = jnp.where(first, 0.0, 1.0)
    group = _lane_group()
    beside.here()
    a_b, a_c, a_x, a_g = _col(zc, 0), _col(zc, 1), _col(zc, 2), _col(zc, 3)
    s['ua'][0:HALO, :] = _col(zp, 1) * _col(zp, 2) * keep
    s['ua'][HALO:HALO + tm, :] = a_c * a_x
    _conv_taps(s['ua'], p['wa'], CONV_A, HALO - (CONV_A - 1), tm, s['cv'], beside=beside)
    cv = s['cv'][...]
    sg_a = _sig(a_g)
    f = dict(a_b=a_b, a_c=a_c, a_x=a_x, a_g=a_g, cv=cv, sg_a=sg_a)
    after(0, f, (a_b * cv * (a_g * sg_a)).astype(BF16))
    beside.here()
    b_p, b_g = _col(zc, 4), _col(zc, 5)
    s['p0'][0:HALO, :] = _col(zp, 4) * keep
    s['p0'][HALO:HALO + tm, :] = b_p
    n = HALO + tm
    s['p1'][8:n, :] = s['p0'][8:n, :] + s['p0'][7:n - 1, :]
    w2 = s['p1'][HALO:n, :]
    beside.here()
    s['p0'][16:n, :] = s['p1'][16:n, :] + s['p1'][14:n - 2, :]
    w4 = s['p0'][HALO:n, :]
    beside.here()
    s['p1'][24:n, :] = s['p0'][24:n, :] + s['p0'][20:n - 4, :]
    w8 = s['p1'][HALO:n, :]
    w16 = w8 + s['p1'][HALO - 8:n - 8, :]
    tpos = (row0 + lax.broadcasted_iota(jnp.int32, (tm, 1), 0) + 1).astype(F32)
    inv_cnt = 1.0 / jnp.minimum(tpos, _pool_window(group))
    pooled = (_by_group(group, w2, w4, w8, w16) * inv_cnt - b_p).astype(BF16)
    beside.here()
    q = _dot(pooled, p['wbd'][...])
    sg_b = _sig(b_g)
    f = dict(b_g=b_g, pooled=pooled, q=q, sg_b=sg_b, inv_cnt=inv_cnt)
    after(1, f, (q * p['v256'][0:1, :] * (b_g * sg_b)).astype(BF16))
    beside.here()
    c_a, c_gl, c_g = _col(zc, 6), _col(zc, 7), _col(zc, 8)
    sg_gl = _sig(c_gl)
    zp_gl = _col(zp, 7)
    s['hc'][0:HALO, :] = _col(zp, 6) * _sig(zp_gl) * keep
    s['hc'][HALO:HALO + tm, :] = c_a * sg_gl
    _shifted_copies(s['hc'], s['hcs'], HALO + tm)
    beside.here()
    if conv_c is None:
        _conv_taps(s['hc'], p['wdw'], CONV_C, HALO - (CONV_C - 1), tm, s['co'], sh_ref=s['hcs'], beside=beside)
        conv_c = s['co']
    conv_v = conv_c[...]
    co = conv_v + p['v256'][1:2, :]
    ln_c, xh_c, rstd_c = _layer_norm(co, p['v256'][2:3, :], p['v256'][3:4, :])
    beside.here()
    sg_ln = _sig(ln_c)
    sc = (ln_c * sg_ln).astype(BF16)
    pw = _dot(sc, p['wpw'][...]) + p['v256'][4:5, :]
    sg_c = _sig(c_g)
    f = dict(c_a=c_a, c_g=c_g, sg_gl=sg_gl, ln_c=ln_c, xh_c=xh_c, rstd_c=rstd_c, sg_ln=sg_ln, sc=sc, pw=pw,
             sg_c=sg_c, conv_c=conv_v)
    after(2, f, (pw * (c_g * sg_c)).astype(BF16))
    beside.here()
    d_u, d_v, d_g = _col(zc, 9), _col(zc, 10), _col(zc, 11)
    gu, th_u = _gelu(d_u)
    gv, th_v = _gelu(d_v)
    beside.here()
    v, xh_d, rstd_d = _layer_norm(gv, p['v256'][5:6, :], p['v256'][6:7, :])
    vb = v.astype(BF16)
    parts = []
    for ch in range(tm // CHUNK):
        vc = vb[ch * CHUNK:(ch + 1) * CHUNK, :]
        mixed = p['bsd'][...]
        for h in range(4):
            mixed = mixed + jnp.where(group == h, _dot(p['ws'][h], vc), 0.0)
        parts.append(mixed)
        beside.here()
    mx = parts[0] if len(parts) == 1 else jnp.concatenate(parts, axis=0)
    sg_d = _sig(d_g)
    f = dict(d_u=d_u, d_v=d_v, d_g=d_g, gu=gu, th_u=th_u, th_v=th_v, xh_d=xh_d, rstd_d=rstd_d, vb=vb, mx=mx,
             sg_d=sg_d)
    after(3, f, (gu * mx * (d_g * sg_d)).astype(BF16))
    beside.rest()


def _mix_scratch(tm):
    ext = pltpu.VMEM((HALO + tm, GW), F32)
    tile = pltpu.VMEM((tm, GW), F32)
    return dict(ua=ext, cv=tile, p0=ext, p1=ext, hc=ext, co=tile, hcs=pltpu.VMEM((8, HALO + tm, GW), F32))


_MIX_PARAMS = ('wa', 'wdw', 'v256', 'wbd', 'wpw', 'ws', 'bsd', 'wout')


def _mix_param_specs(l):
    def whole(*shape):
        nd = len(shape)
        return pl.BlockSpec((None,) + shape, lambda i, _nd=nd: (l,) + (0,) * _nd)
    return [whole(8, GW), whole(32, GW), whole(16, GW), whole(GW, GW), whole(GW, GW), whole(4, CHUNK, CHUNK),
            whole(CHUNK, GW), whole(N_CHIP, GW, D)]


def _fwd_layer(x, vecs, w, mp, l, tm):
    t = x.shape[0]
    nt = t // tm
    names = list(_mix_scratch(tm))
    n_p = len(_MIX_PARAMS)

    def body(xm_ref, xr_ref, v_ref, w_ref, *rest):
        p = dict(zip(_MIX_PARAMS, rest[:n_p]))
        z_ref, xn_ref, y_ref, conv_ref = rest[n_p:n_p + 4]
        s = dict(zip(names, rest[n_p + 4:n_p + 4 + len(names)]))
        z_next, z_cur, z_halo = rest[n_p + 4 + len(names):]
        j = pl.program_id(0)

        @pl.when(j == 0)
        def _():
            z_cur[...] = jnp.zeros_like(z_cur)
            z_halo[...] = jnp.zeros_like(z_halo)

        vecs_v = v_ref[...]
        _, _, hn = _normed(xm_ref[...], vecs_v)
        hb = (hn * (1.0 + vecs_v[1:2, :]) + vecs_v[0:1, :]).astype(BF16)
        def project(g):
            def emit():
                k, c0 = divmod(g * GW, PIECE)
                part = _dot(hb, w_ref[k, :, c0:c0 + GW])
                z_ref[:, GW * g:GW * (g + 1)] = part
                z_next[:, GW * g:GW * (g + 1)] = part
            return emit

        tile = jnp.maximum(j - 1, 0)
        beside = _Beside([project(g) for g in range(DIN // GW)], points=MIX_FWD_POINTS)
        mixed = []
        _mix_forward(z_cur, z_halo, tile == 0, tile * tm, tm, p, s, lambda k, f, yk: mixed.append((f, yk)),
                     beside=beside)
        y = None
        for k, (_, yk) in enumerate(mixed):
            part = _dot(yk, p['wout'][k])
            y = part if y is None else y + part
        y_ref[...] = y
        xn_ref[...] = xr_ref[...] + vecs_v[2:3, :] * y
        conv_ref[...] = mixed[2][0]['conv_c']

        z_halo[...] = z_cur[tm - HALO:tm, :]
        z_cur[...] = z_next[...]

    def ahead(j):
        return jnp.minimum(j, nt - 1)

    def behind(j):
        return jnp.maximum(j - 1, 0)

    return pl.pallas_call(
        body, name=f"fwd_layer_{l}", grid=(nt + 1,),
        in_specs=[pl.BlockSpec((tm, D), lambda j: (ahead(j), 0)), pl.BlockSpec((tm, D), lambda j: (behind(j), 0)),
                  pl.BlockSpec((8, D), lambda j: (0, 0)),
                  pl.BlockSpec((None, N_CHIP, D, PIECE), lambda j: (l, 0, 0, 0))] + _mix_param_specs(l),
        out_specs=(pl.BlockSpec((tm, DIN), lambda j: (ahead(j), 0)), pl.BlockSpec((tm, D), lambda j: (behind(j), 0)),
                   pl.BlockSpec((tm, D), lambda j: (behind(j), 0)), pl.BlockSpec((tm, GW), lambda j: (behind(j), 0))),
        out_shape=(jax.ShapeDtypeStruct((t, DIN), F32), jax.ShapeDtypeStruct((t, D), F32),
                   jax.ShapeDtypeStruct((t, D), F32), jax.ShapeDtypeStruct((t, GW), F32)),
        scratch_shapes=list(_mix_scratch(tm).values())
        + [pltpu.VMEM((tm, DIN), F32), pltpu.VMEM((tm, DIN), F32), pltpu.VMEM((HALO, DIN), F32)],
        compiler_params=_params(("arbitrary",)),
    )(x, x, vecs, w, *[mp[k] for k in _MIX_PARAMS])


def _bwd_mix(dxo, y, conv, z, vecs, mp, wst, l, tm, gwout_all=None):
    t = dxo.shape[0]
    nt = t // tm
    nl = wst.shape[0]
    carried = [] if gwout_all is None else [gwout_all]
    per_halo = tm // HALO
    fwd_names = list(_mix_scratch(tm))
    ext = pltpu.VMEM((tm + HALO, GW), F32)
    carry = pltpu.VMEM((HALO, GW), F32)
    tile = pltpu.VMEM((tm, GW), F32)
    bwd_scratch = dict(ea=ext, eb=ext, eb2=ext, ec=ext, ca=carry, cb=carry, cc=carry, dua=tile, dhc=tile,
                       gbacc=pltpu.VMEM((CHUNK, GW), F32), ecs=pltpu.VMEM((8, tm + HALO, GW), F32),
                       dys=pltpu.VMEM((1, tm, GW), F32))
    bwd_names = list(bwd_scratch)
    n_p = len(_MIX_PARAMS)

    def body(dxo_ref, y_ref, conv_ref, zc, zp, v_ref, *rest):
        p = dict(zip(_MIX_PARAMS, rest[:n_p]))
        wst_ref = rest[n_p]
        first_out = n_p + 1 + len(carried)
        dz_ref, gwout, gwbd, gwpw, gws, gbs, gwdw, g256, g1024 = rest[first_out:first_out + 9]
        s = dict(zip(fwd_names + bwd_names, rest[first_out + 9:]))
        i = pl.program_id(0)
        ti = nt - 1 - i
        group = _lane_group()

        @pl.when(i == 0)
        def _():
            for ref in (gwout, gwbd, gwpw, gws, gbs, gwdw, g256, g1024, s['ca'], s['cb'], s['cc'], s['gbacc']):
                ref[...] = jnp.zeros_like(ref)

        def put(k, val):
            dz_ref[:, GW * k:GW * (k + 1)] = val.astype(BF16)

        def bwd_a(f, dya):
            taps = [None] * CONV_A
            for r in range(0, tm, ROW_BLOCK):
                rs = slice(r, r + ROW_BLOCK)
                a_b, a_g = zc[rs, 0:GW], zc[rs, 3 * GW:4 * GW]
                dy_a, cv = s['dys'][0, rs, :], s['cv'][rs, :]
                sg = _sig(a_g)
                silu = a_g * sg
                t = dy_a * cv
                dz_ref[rs, 0:GW] = (t * silu).astype(BF16)
                dz_ref[rs, 3 * GW:4 * GW] = (t * a_b * (sg * (1.0 + a_g * (1.0 - sg)))).astype(BF16)
                d_cv = dy_a * a_b * silu
                s['ea'][rs, :] = d_cv
                for k in range(CONV_A):
                    first = HALO - (CONV_A - 1) + k + r
                    term = d_cv * s['ua'][first:first + ROW_BLOCK, :]
                    taps[k] = term if taps[k] is None else taps[k] + term
            for k in range(CONV_A):
                g256[8 + k:9 + k, :] += _rowsum(taps[k])
            s['ea'][tm:tm + HALO, :] = s['ca'][...]
            s['ca'][...] = s['ea'][0:HALO, :]
            _conv_taps(s['ea'], p['wa'], CONV_A, 0, tm, s['dua'], flip=True)
            for r in range(0, tm, ROW_BLOCK):
                rs = slice(r, r + ROW_BLOCK)
                d_u = s['dua'][rs, :]
                dz_ref[rs, GW:2 * GW] = (d_u * zc[rs, 2 * GW:3 * GW]).astype(BF16)
                dz_ref[rs, 2 * GW:3 * GW] = (d_u * zc[rs, GW:2 * GW]).astype(BF16)

        def bwd_b(f, dyb):
            b_g, q, sg_b, inv_cnt = f['b_g'], f['q'], f['sg_b'], f['inv_cnt']
            scale_b = p['v256'][0:1, :]
            silu_b = b_g * sg_b
            g256[0:1, :] += _rowsum(dyb * q * silu_b)
            put(5, dyb * q * scale_b * (sg_b * (1.0 + b_g * (1.0 - sg_b))))
            d_q = (dyb * scale_b * silu_b).astype(BF16)
            gwbd[...] += _dot_tn(f['pooled'], d_q)
            d_pooled = _dot_nt(d_q, p['wbd'][...])
            gp = d_pooled * inv_cnt
            n = tm + HALO
            s['eb'][0:tm, :] = gp
            s['eb'][tm:n, :] = s['cb'][...]
            s['cb'][...] = gp[0:HALO, :]
            s['eb2'][0:n - 8, :] = s['eb'][0:n - 8, :] + s['eb'][1:n - 7, :]
            r2 = s['eb2'][0:tm, :]
            s['eb'][0:n - 16, :] = s['eb2'][0:n - 16, :] + s['eb2'][2:n - 14, :]
            r4 = s['eb'][0:tm, :]
            s['eb2'][0:n - 24, :] = s['eb'][0:n - 24, :] + s['eb'][4:n - 20, :]
            r8 = s['eb2'][0:tm, :]
            r16 = r8 + s['eb2'][8:tm + 8, :]
            put(4, _by_group(group, r2, r4, r8, r16) - d_pooled)

        def bwd_c(f, dyc):
            c_a, c_g, sg_gl, ln_c, xh_c, rstd_c, sg_ln, pw, sg_c = (
                f['c_a'], f['c_g'], f['sg_gl'], f['ln_c'], f['xh_c'], f['rstd_c'], f['sg_ln'], f['pw'], f['sg_c'])
            put(8, dyc * pw * (sg_c * (1.0 + c_g * (1.0 - sg_c))))
            d_pw = dyc * (c_g * sg_c)
            g256[4:5, :] += _rowsum(d_pw)
            d_pwb = d_pw.astype(BF16)
            gwpw[...] += _dot_tn(f['sc'], d_pwb)
            d_ln = _dot_nt(d_pwb, p['wpw'][...]) * (sg_ln * (1.0 + ln_c * (1.0 - sg_ln)))
            g256[2:3, :] += _rowsum(d_ln * xh_c)
            g256[3:4, :] += _rowsum(d_ln)
            d_co = _layer_norm_grad(d_ln, xh_c, rstd_c, p['v256'][2:3, :])
            g256[1:2, :] += _rowsum(d_co)
            for k in range(CONV_C):
                gwdw[k:k + 1, :] += _rowsum(d_co * _rows_at(s['hc'], s['hcs'], HALO - (CONV_C - 1) + k, tm))
            s['ec'][0:tm, :] = d_co
            s['ec'][tm:tm + HALO, :] = s['cc'][...]
            s['cc'][...] = d_co[0:HALO, :]
            _shifted_copies(s['ec'], s['ecs'], tm + HALO)
            _conv_taps(s['ec'], p['wdw'], CONV_C, 0, tm, s['dhc'], flip=True, sh_ref=s['ecs'])
            d_hc = s['dhc'][...]
            put(6, d_hc * sg_gl)
            put(7, d_hc * c_a * sg_gl * (1.0 - sg_gl))

        def bwd_d(f, dyd):
            d_uu, d_vv, d_g, gu, th_u, th_v, xh_d, rstd_d, vb, mx, sg_d = (
                f['d_u'], f['d_v'], f['d_g'], f['gu'], f['th_u'], f['th_v'], f['xh_d'], f['rstd_d'], f['vb'],
                f['mx'], f['sg_d'])
            silu_d = d_g * sg_d
            put(9, dyd * mx * silu_d * _gelu_grad(d_uu, th_u))
            put(11, dyd * gu * mx * (sg_d * (1.0 + d_g * (1.0 - sg_d))))
            d_mx = dyd * gu * silu_d
            dv_parts = []
            gb = None
            for ch in range(tm // CHUNK):
                dmc = d_mx[ch * CHUNK:(ch + 1) * CHUNK, :]
                gb = dmc if gb is None else gb + dmc
                dmb = dmc.astype(BF16)
                vc = vb[ch * CHUNK:(ch + 1) * CHUNK, :]
                dvc = None
                for h in range(4):
                    gws[h] += _dot_nt(jnp.where(group == h, dmb, jnp.zeros_like(dmb)), vc)
                    part = jnp.where(group == h, _dot(wst_ref[h], dmb), 0.0)
                    dvc = part if dvc is None else dvc + part
                dv_parts.append(dvc)
            s['gbacc'][...] += gb
            d_v = dv_parts[0] if len(dv_parts) == 1 else jnp.concatenate(dv_parts, axis=0)
            g256[5:6, :] += _rowsum(d_v * xh_d)
            g256[6:7, :] += _rowsum(d_v)
            d_gv = _layer_norm_grad(d_v, xh_d, rstd_d, p['v256'][5:6, :])
            put(10, d_gv * _gelu_grad(d_vv, th_v))

        mixed = []
        _mix_forward(zc, zp, ti == 0, ti * tm, tm, p, s, lambda j, f, yj: mixed.append((f, yj)), conv_c=conv_ref)
        dxo_v = dxo_ref[...]
        g1024[0:1, :] += _rowsum(dxo_v * y_ref[...])
        dy = (dxo_v * v_ref[2:3, :]).astype(BF16)
        s['dys'][0] = _dot_nt(dy, p['wout'][0])
        dys = [None] + [_dot_nt(dy, p['wout'][j]) for j in range(1, 4)]
        for j, (f, yj) in enumerate(mixed):
            gwout[j] += _dot_tn(yj, dy)
        for (f, _), backward, dyj in zip(mixed, (bwd_a, bwd_b, bwd_c, bwd_d), dys):
            backward(f, dyj)

        @pl.when(i == nt - 1)
        def _():
            row = lax.broadcasted_iota(jnp.int32, (CHUNK, CHUNK), 0)
            col = lax.broadcasted_iota(jnp.int32, (CHUNK, CHUNK), 1)
            for h in range(4):
                gws[h] = jnp.where(col <= row, gws[h], 0.0)
            head_of_lane = lax.broadcasted_iota(jnp.int32, (GW, CHUNK), 0) // (GW // 4)
            sel = jnp.where(head_of_lane == lax.broadcasted_iota(jnp.int32, (GW, CHUNK), 1), 1.0, 0.0)
            gbs[...] = jnp.dot(s['gbacc'][...], sel, preferred_element_type=F32, precision=lax.Precision.HIGHEST)

    def const(*shape):
        nd = len(shape)
        return pl.BlockSpec(shape, lambda i, _nd=nd: (0,) * _nd)

    def rev(i):
        return nt - 1 - i

    out_shapes = (jax.ShapeDtypeStruct((t, DIN), BF16), jax.ShapeDtypeStruct((nl, N_CHIP, GW, D), F32),
                  jax.ShapeDtypeStruct((GW, GW), F32), jax.ShapeDtypeStruct((GW, GW), F32),
                  jax.ShapeDtypeStruct((4, CHUNK, CHUNK), F32), jax.ShapeDtypeStruct((CHUNK, CHUNK), F32),
                  jax.ShapeDtypeStruct((32, GW), F32), jax.ShapeDtypeStruct((16, GW), F32),
                  jax.ShapeDtypeStruct((8, D), F32))
    out_specs = (pl.BlockSpec((tm, DIN), lambda i: (rev(i), 0)),
                 pl.BlockSpec((None, N_CHIP, GW, D), lambda i: (l, 0, 0, 0)), const(GW, GW),
                 const(GW, GW), const(4, CHUNK, CHUNK), const(CHUNK, CHUNK), const(32, GW), const(16, GW),
                 const(8, D))
    scratch = list(_mix_scratch(tm).values()) + list(bwd_scratch.values())
    n_in = 6 + n_p + 1
    return pl.pallas_call(
        body, name=f"bwd_mix_{l}", grid=(nt,),
        in_specs=[pl.BlockSpec((tm, D), lambda i: (rev(i), 0)), pl.BlockSpec((tm, D), lambda i: (rev(i), 0)),
                  pl.BlockSpec((tm, GW), lambda i: (rev(i), 0)), pl.BlockSpec((tm, DIN), lambda i: (rev(i), 0)),
                  pl.BlockSpec((HALO, DIN), lambda i: (jnp.maximum(rev(i) * per_halo - 1, 0), 0)),
                  pl.BlockSpec((8, D), lambda i: (0, 0))]
        + _mix_param_specs(l)
        + [pl.BlockSpec((None, 4, CHUNK, CHUNK), lambda i: (l, 0, 0, 0))]
        + [pl.BlockSpec(memory_space=pl.ANY)] * len(carried),
        out_specs=out_specs, out_shape=out_shapes, scratch_shapes=scratch,
        input_output_aliases={n_in: 1} if carried else {},
        compiler_params=_params(("arbitrary",)),
    )(dxo, y, conv, z, z, vecs, *[mp[k] for k in _MIX_PARAMS], wst, *carried)


def _final(x, target, fg, tt):
    t = x.shape[0]
    nt = t // tt

    def body(x_ref, t_ref, g_ref, dx_ref, sums_ref):
        i = pl.program_id(0)

        @pl.when(i == 0)
        def _():
            sums_ref[...] = jnp.zeros_like(sums_ref)

        xv = x_ref[...]
        g = g_ref[0:1, :]
        r = lax.rsqrt(jnp.mean(xv * xv, axis=-1, keepdims=True) + EPS)
        xhat = xv * r
        err = xhat * g - t_ref[...]
        sums_ref[1:2, :] += _rowsum(err * err) * (0.5 / D)
        dyv = err * (1.0 / D)
        sums_ref[0:1, :] += _rowsum(dyv * xhat)
        dxh = dyv * g
        dx_ref[...] = r * (dxh - xhat * jnp.mean(dxh * xhat, axis=-1, keepdims=True))

        @pl.when(i == nt - 1)
        def _():
            sums_ref[2:3, :] = jnp.broadcast_to(jnp.sum(sums_ref[1:2, :], axis=-1, keepdims=True), (1, D))

    return pl.pallas_call(
        body, name="final_loss", grid=(nt,),
        in_specs=[pl.BlockSpec((tt, D), lambda i: (i, 0)), pl.BlockSpec((tt, D), lambda i: (i, 0)),
                  pl.BlockSpec((8, D), lambda i: (0, 0))],
        out_specs=(pl.BlockSpec((tt, D), lambda i: (i, 0)), pl.BlockSpec((8, D), lambda i: (0, 0))),
        out_shape=(jax.ShapeDtypeStruct((t, D), F32), jax.ShapeDtypeStruct((8, D), F32)),
        compiler_params=_params(("arbitrary",)),
    )(x, target, fg)


def _adamw(w, g, m, v, tag):
    rows, cols = w.shape
    rt = rows
    for cand in (512, 256, 128, 64, 32, 16, 8):
        if rows % cand == 0:
            rt = cand
            break

    def body(w_ref, g_ref, m_ref, v_ref, d_ref, nm_ref, nv_ref):
        d_ref[...], nm_ref[...], nv_ref[...] = _adamw_update(g_ref[...], w_ref[...], m_ref[...], v_ref[...])

    spec = pl.BlockSpec((rt, cols), lambda i: (i, 0))
    return pl.pallas_call(
        body, name=f"adamw_{tag}", grid=(rows // rt,), in_specs=[spec] * 4, out_specs=(spec,) * 3,
        out_shape=(jax.ShapeDtypeStruct(w.shape, F32),) * 3, compiler_params=_params(("parallel",)),
    )(w, g, m, v)


def _adamw_update(gv, w, m, v):
    nm = ADAM_B1 * m + (1.0 - ADAM_B1) * gv
    nv = ADAM_B2 * v + (1.0 - ADAM_B2) * (gv * gv)
    m_hat = nm / (1.0 - ADAM_B1 ** ADAM_STEP)
    v_hat = nv / (1.0 - ADAM_B2 ** ADAM_STEP)
    return -ADAM_LR * (m_hat / (jnp.sqrt(v_hat) + ADAM_EPS) + ADAM_WD * w), nm, nv


def _adamw_many(ws, gs, ms, vs):
    n = len(ws)

    def body(*refs):
        w_refs, g_refs, m_refs, v_refs = (refs[k * n:(k + 1) * n] for k in range(4))
        d_refs, nm_refs, nv_refs = (refs[(4 + k) * n:(5 + k) * n] for k in range(3))
        for k in range(n):
            d_refs[k][...], nm_refs[k][...], nv_refs[k][...] = _adamw_update(
                g_refs[k][...], w_refs[k][...], m_refs[k][...], v_refs[k][...])

    shapes = tuple(jax.ShapeDtypeStruct(w.shape, F32) for w in ws)
    out = pl.pallas_call(body, name="adamw_small", out_shape=shapes * 3)(*ws, *gs, *ms, *vs)
    return out[:n], out[n:2 * n], out[2 * n:]


def _adamw_halves(w, mine, theirs, m, v, cidx, tag):
    nl, r, cdim = w.shape
    r2 = r // 2
    rt = min(r2, 256)
    nrt = r2 // rt

    def body(c_ref, a_ref, b_ref, w_ref, m_ref, v_ref, g_ref, d_ref, nm_ref, nv_ref):
        gv = jnp.where(pl.program_id(1) == c_ref[0], a_ref[...], b_ref[...])
        g_ref[...] = gv
        d_ref[...], nm_ref[...], nv_ref[...] = _adamw_update(gv, w_ref[...], m_ref[...], v_ref[...])

    half = pl.BlockSpec((None, rt, cdim), lambda l, h, i, c: (l, i, 0))
    whole = pl.BlockSpec((None, rt, cdim), lambda l, h, i, c: (l, h * nrt + i, 0))
    grid_spec = pltpu.PrefetchScalarGridSpec(
        num_scalar_prefetch=1, grid=(nl, 2, nrt), in_specs=[half, half, whole, whole, whole],
        out_specs=(whole,) * 4)
    return pl.pallas_call(
        body, name=f"adamw_{tag}", grid_spec=grid_spec,
        out_shape=(jax.ShapeDtypeStruct(w.shape, F32),) * 4,
        compiler_params=_params(("parallel", "parallel", "parallel")),
    )(cidx, mine, theirs, w, m, v)


def _pack(arrays, row_multiple=8):
    parts, offsets, off = [], [], 0
    for a in arrays:
        n = int(np.prod(a.shape))
        padded = -(-n // 1024) * 1024
        flat = a.reshape(-1).astype(F32)
        if padded != n:
            flat = jnp.concatenate([flat, jnp.zeros((padded - n,), F32)])
        parts.append(flat)
        offsets.append((off, n, a.shape))
        off += padded
    tail = -off % (row_multiple * 128)
    if tail:
        parts.append(jnp.zeros((tail,), F32))
    return jnp.concatenate(parts).reshape(-1, 128), offsets


def _unpack(buf, offsets):
    flat = buf.reshape(-1)
    return [flat[off:off + n].reshape(shape) for off, n, shape in offsets]


def _pad_rows(a, rows):
    return jnp.concatenate([a, jnp.zeros((rows - a.shape[0],) + a.shape[1:], a.dtype)], axis=0)


def kernel(x, c, norm_g, w_ada, b_ada, w_in, w_conv_a, w_pool, pool_scale, w_dw_c, b_dw_c, ln_g_c, ln_b_c, w_pw2_c, b_pw2_c, ln_g_d, ln_b_d, w_s_d, b_s_d, w_out, final_g, loss_target, m_norm_g, m_w_ada, m_b_ada, m_w_in, m_w_conv_a, m_w_pool, m_pool_scale, m_w_dw_c, m_b_dw_c, m_ln_g_c, m_ln_b_c, m_w_pw2_c, m_b_pw2_c, m_ln_g_d, m_ln_b_d, m_w_s_d, m_b_s_d, m_w_out, m_final_g, v_norm_g, v_w_ada, v_b_ada, v_w_in, v_w_conv_a, v_w_pool, v_pool_scale, v_w_dw_c, v_b_dw_c, v_ln_g_c, v_ln_b_c, v_w_pw2_c, v_b_pw2_c, v_ln_g_d, v_ln_b_d, v_w_s_d, v_b_s_d, v_w_out, v_final_g):
    env = dict(locals())
    weights = {n: env[n] for n in WEIGHTS}
    mom_m = {n: env["m_" + n] for n in WEIGHTS}
    mom_v = {n: env["v_" + n] for n in WEIGHTS}
    nl = norm_g.shape[0]
    t = x.shape[1]
    tt, tm = _tiles(t)
    ax, ay, ac = lax.axis_index("x"), lax.axis_index("y"), lax.axis_index("c")
    chip = 2 * ax + ay
    dev = 2 * chip + ac
    cidx = jnp.reshape(ac, (1,)).astype(jnp.int32)
    sidx = jnp.reshape(chip, (1,)).astype(jnp.int32)
    xs, target = x[0], loss_target[0]

    shard_small, shard_off = _pack([c, w_conv_a, w_dw_c, w_pw2_c])
    gathered = _allgather8(shard_small, "gather_small").reshape(N_DEV, -1, 128)
    per_dev = [_unpack(gathered[d], shard_off) for d in range(0, N_DEV, 2)]
    c_all = jnp.concatenate([u[0] for d in range(N_DEV)
                             for u in [_unpack(gathered[d], shard_off[:1])]], axis=0)
    w_conv_full = jnp.concatenate([u[1] for u in per_dev], axis=-1)
    w_dw_full = jnp.concatenate([u[2] for u in per_dev], axis=-1)
    w_pw2_full = jnp.concatenate([u[3] for u in per_dev], axis=1)

    b_ada_s = lax.dynamic_slice_in_dim(b_ada, chip * PIECE, PIECE, axis=1)[:, None, :]
    mod_s = _ada_forward(c_all, w_ada, b_ada_s)
    mod_all = _allgather8(mod_s.reshape(nl * N_DEV, PIECE), "gather_mod").reshape(N_DEV, nl, N_DEV, PIECE)
    mod_rows = lax.dynamic_index_in_dim(mod_all, dev, axis=2, keepdims=False)
    mod = jnp.concatenate([mod_rows[2 * s] for s in range(N_CHIP)], axis=-1)

    w_in_full, w_out_full = _gather_weights(w_in, w_out)

    tril = jnp.tril(jnp.ones((CHUNK, CHUNK), bool))
    ws_masked = jnp.where(tril[None, None], w_s_d, 0.0)
    wbd = jnp.zeros((nl, GW, GW), F32)
    for g in range(4):
        wbd = wbd.at[:, 64 * g:64 * (g + 1), 64 * g:64 * (g + 1)].set(w_pool[:, g])
    v256 = jnp.stack([pool_scale, b_dw_c, ln_g_c, ln_b_c, b_pw2_c, ln_g_d, ln_b_d], axis=1)
    mp = dict(
        wa=_pad_rows(jnp.swapaxes(w_conv_full, 0, 1), 8).swapaxes(0, 1),
        wdw=_pad_rows(jnp.swapaxes(w_dw_full, 0, 1), 32).swapaxes(0, 1),
        v256=_pad_rows(jnp.swapaxes(v256, 0, 1), 16).swapaxes(0, 1),
        wbd=wbd.astype(BF16), wpw=w_pw2_full.astype(BF16), ws=ws_masked.astype(BF16),
        bsd=jnp.repeat(jnp.swapaxes(b_s_d, 1, 2), GW // 4, axis=2),
        wout=w_out_full)
    wst = jnp.swapaxes(ws_masked, 2, 3).astype(BF16)

    def vec_rows(l):
        rows = jnp.stack([mod[l, 0:D], mod[l, D:2 * D], mod[l, 2 * D:3 * D], norm_g[l]], axis=0)
        return _pad_rows(rows, 8)

    xin, zs, ys, convs, vecs = [xs], [], [], [], []
    for l in range(nl):
        vecs.append(vec_rows(l))
        zl, xn, yl, cl = _fwd_layer(xin[l], vecs[l], w_in_full, mp, l, tm)
        zs.append(zl)
        convs.append(cl)
        xin.append(xn)
        ys.append(yl)
    dx, fin_sums = _final(xin[nl], target, _pad_rows(final_g[None, :], 8), tt)

    g_in, g_out, small, dmod = None, None, [None] * nl, [None] * nl
    for l in reversed(range(nl)):
        dz, g_out, gwbd, gwpw, gws, gbs, gwdw, g256, g1024 = _bwd_mix(
            dx, ys[l], convs[l], zs[l], vecs[l], mp, wst, l, tm, gwout_all=g_out)
        dx, g_in, sums = _bwd_in(dz, xin[l], dx, vecs[l], w_in_full, l, tt, gw_all=g_in)
        dmod[l] = jnp.concatenate([sums[0], sums[1], g1024[0]])
        small[l] = dict(
            norm_g=sums[2], w_conv_a=g256[8:8 + CONV_A],
            w_pool=jnp.stack([gwbd[64 * g:64 * (g + 1), 64 * g:64 * (g + 1)] for g in range(4)]),
            pool_scale=g256[0], w_dw_c=gwdw[:CONV_C], b_dw_c=g256[1], ln_g_c=g256[2], ln_b_c=g256[3],
            w_pw2_c=gwpw, b_pw2_c=g256[4], ln_g_d=g256[5], ln_b_d=g256[6], w_s_d=gws, b_s_d=gbs[:, :4].T)
    grad_x = dx[None]

    dmod_all, dmod_sum = _allgather8(jnp.stack(dmod).reshape(nl * 3 * D // 128, 128), "gather_dmod", reduce=True)
    dmod_all = dmod_all.reshape(N_DEV, nl, 3 * D)
    grad_b_ada = dmod_sum.reshape(nl, 3 * D)
    dmod_s = jnp.swapaxes(lax.dynamic_slice_in_dim(dmod_all, chip * PIECE, PIECE, axis=2), 0, 1)
    grad_w_ada = _ada_backward(c_all, dmod_s)

    small_names = [n for n in WEIGHTS if n not in BIG and n not in ('b_ada', 'final_g')]
    stacked = [jnp.stack([small[l][n] for l in range(nl)]) for n in small_names]
    small_buf, small_off = _pack(stacked + [fin_sums[0], fin_sums[2, 0:1]], 8 * N_DEV)
    reduced = _unpack(_allreduce8(small_buf, "allreduce_small"), small_off)
    grads = dict(zip(small_names, reduced[:len(small_names)]))
    grads['final_g'] = reduced[-2]
    loss = reduced[-1][0]
    grads['b_ada'] = grad_b_ada
    grads['w_ada'] = grad_w_ada
    grads['w_conv_a'] = lax.dynamic_slice_in_dim(grads['w_conv_a'], chip * 64, 64, axis=2)
    grads['w_dw_c'] = lax.dynamic_slice_in_dim(grads['w_dw_c'], chip * 64, 64, axis=2)
    grads['w_pw2_c'] = lax.dynamic_slice_in_dim(grads['w_pw2_c'], chip * 64, 64, axis=1)

    halves = {'w_in': _reduce_scatter(g_in, cidx, sidx, "in"), 'w_out': _reduce_scatter(g_out, cidx, sidx, "out")}

    delta, new_m, new_v = {}, {}, {}
    for n in ('w_in', 'w_out'):
        grads[n], delta[n], new_m[n], new_v[n] = _adamw_halves(
            weights[n], *halves[n], mom_m[n], mom_v[n], cidx, n)
    shape = w_ada.shape
    as2d = lambda a: a.reshape(-1, shape[-1])
    d2, m2, v2 = _adamw(as2d(w_ada), as2d(grads['w_ada']), as2d(m_w_ada), as2d(v_w_ada), 'w_ada')
    delta['w_ada'], new_m['w_ada'], new_v['w_ada'] = d2.reshape(shape), m2.reshape(shape), v2.reshape(shape)
    rest = [n for n in WEIGHTS if n not in BIG]
    ds, nms, nvs = _adamw_many([weights[n] for n in rest], [grads[n] for n in rest],
                               [mom_m[n] for n in rest], [mom_v[n] for n in rest])
    for n, dn, mn, vn in zip(rest, ds, nms, nvs):
        delta[n], new_m[n], new_v[n] = dn, mn, vn

    return (loss, grad_x, *[grads[n] for n in WEIGHTS], *[delta[n] for n in WEIGHTS],
            *[new_m[n] for n in WEIGHTS], *[new_v[n] for n in WEIGHTS])
```

```python
import functools
import math

import jax
import jax.numpy as jnp
import numpy as np
from jax import lax
from jax.experimental import pallas as pl
from jax.experimental.pallas import tpu as pltpu

F32 = jnp.float32
BF16 = jnp.bfloat16
MESH = pl.DeviceIdType.MESH

D = 1024
DIN = 3072
GW = 256
PIECE = 768
N_CHIP = 4
N_DEV = 8
HALO = 32
CONV_A = 3
CONV_C = 31
CHUNK = 128
CONV_ROWS = 64
ROW_BLOCK = 32
MIX_FWD_POINTS = 18
EPS = 1e-6
VMEM_LIMIT = 56 * 1024 * 1024

ADAM_LR, ADAM_B1, ADAM_B2, ADAM_EPS, ADAM_WD, ADAM_STEP = 0.001, 0.9, 0.999, 1e-08, 0.01, 10
GELU_K = math.sqrt(2.0 / math.pi)
GELU_C = 0.044715

WEIGHTS = ['norm_g', 'w_ada', 'b_ada', 'w_in', 'w_conv_a', 'w_pool', 'pool_scale', 'w_dw_c', 'b_dw_c', 'ln_g_c',
           'ln_b_c', 'w_pw2_c', 'b_pw2_c', 'ln_g_d', 'ln_b_d', 'w_s_d', 'b_s_d', 'w_out', 'final_g']
BIG = ('w_ada', 'w_in', 'w_out')


def _tiles(t):
    if t % 512 == 0 and t >= 2048:
        return 512, 256
    return 128, 128


def _params(sem, limit=VMEM_LIMIT):
    return pltpu.CompilerParams(dimension_semantics=sem, vmem_limit_bytes=limit)


def _sig(x):
    return jax.nn.sigmoid(x)


def _gelu(x):
    th = jnp.tanh(GELU_K * (x + GELU_C * x * x * x))
    return 0.5 * x * (1.0 + th), th


def _gelu_grad(x, th):
    return 0.5 * (1.0 + th) + 0.5 * x * (1.0 - th * th) * GELU_K * (1.0 + 3.0 * GELU_C * x * x)


def _dot(a, b):
    return jnp.dot(a, b, preferred_element_type=F32)


def _dot_nt(a, b):
    return lax.dot_general(a, b, (((1,), (1,)), ((), ())), preferred_element_type=F32)


def _dot_tn(a, b):
    return lax.dot_general(a, b, (((0,), (0,)), ((), ())), preferred_element_type=F32)


def _rowsum(x):
    return jnp.sum(x, axis=0, keepdims=True)


def _allgather8(v, name, reduce=False):
    m_per, n = v.shape

    def body(x_ref, out_ref, *rest):
        if reduce:
            sum_ref, send_sems, recv_sems, local_sem = rest
        else:
            send_sems, recv_sems, local_sem = rest
        x, y, c = lax.axis_index("x"), lax.axis_index("y"), lax.axis_index("c")
        me, sibling = (x, y, c), (x, y, 1 - c)
        chips = [(1 - x, y), (x, 1 - y), (1 - x, 1 - y)]

        def rows(px, py, pc):
            return out_ref.at[pl.ds((4 * px + 2 * py + pc) * m_per, m_per), :]

        def copy(k, block, to, src=None):
            return pltpu.make_async_remote_copy(
                src_ref=rows(*block) if src is None else src, dst_ref=rows(*block),
                send_sem=send_sems.at[k], recv_sem=recv_sems.at[k], device_id=to, device_id_type=MESH)

        mine = pltpu.make_async_copy(x_ref, rows(*me), local_sem)
        mine.start()
        first = [copy(0, me, sibling, src=x_ref)]
        first += [copy(1 + j, me, (*chip, c), src=x_ref) for j, chip in enumerate(chips)]
        for cp in first:
            cp.start()
        passed = [copy(4 + j, (*chip, c), sibling) for j, chip in enumerate(chips)]
        for j, chip in enumerate(chips):
            copy(1 + j, (*chip, c), me).wait_recv()
            passed[j].start()
        copy(0, sibling, me).wait_recv()
        for j, chip in enumerate(chips):
            copy(4 + j, (*chip, 1 - c), me).wait_recv()
        for cp in first + passed:
            cp.wait_send()
        mine.wait()
        if reduce:
            acc = out_ref[0:m_per, :]
            for d in range(1, N_DEV):
                acc = acc + out_ref[d * m_per:(d + 1) * m_per, :]
            sum_ref[...] = acc

    out_shape = [jax.ShapeDtypeStruct((N_DEV * m_per, n), v.dtype)]
    out_specs = [pl.BlockSpec(memory_space=pltpu.VMEM)]
    if reduce:
        out_shape.append(jax.ShapeDtypeStruct((m_per, n), v.dtype))
        out_specs.append(pl.BlockSpec(memory_space=pltpu.VMEM))
    res = pl.pallas_call(
        body, name=name, out_shape=tuple(out_shape),
        in_specs=[pl.BlockSpec(memory_space=pltpu.VMEM)], out_specs=tuple(out_specs),
        scratch_shapes=[pltpu.SemaphoreType.DMA((7,)), pltpu.SemaphoreType.DMA((7,)), pltpu.SemaphoreType.DMA],
        compiler_params=pltpu.CompilerParams(vmem_limit_bytes=VMEM_LIMIT),
    )(v)
    return res if reduce else res[0]


def _allreduce8(v, name):
    rows, n = v.shape
    rc = rows // N_DEV

    def body(x_ref, sum_ref, stage, send1, recv1, send2, recv2):
        x, y, c = lax.axis_index("x"), lax.axis_index("y"), lax.axis_index("c")
        me = 4 * x + 2 * y + c
        peers = []
        for k in range(1, N_DEV):
            kx, ky, kc = (k >> 2) & 1, (k >> 1) & 1, k & 1
            peers.append((1 - x if kx else x, 1 - y if ky else y, 1 - c if kc else c))

        def chunk(ref, d):
            return ref.at[pl.ds(d * rc, rc), :]

        scatter, gather = [], []
        for k, (px, py, pc) in enumerate(peers):
            scatter.append(pltpu.make_async_remote_copy(
                src_ref=chunk(x_ref, 4 * px + 2 * py + pc), dst_ref=stage.at[me], send_sem=send1.at[k],
                recv_sem=recv1.at[k], device_id=(px, py, pc), device_id_type=MESH))
        for cp in scatter:
            cp.start()
        stage[me] = x_ref[pl.ds(me * rc, rc), :]
        for k, (px, py, pc) in enumerate(peers):
            pltpu.make_async_remote_copy(
                src_ref=chunk(x_ref, me), dst_ref=stage.at[4 * px + 2 * py + pc], send_sem=send1.at[k],
                recv_sem=recv1.at[k], device_id=(px, py, pc), device_id_type=MESH).wait_recv()
        total = stage[0]
        for d in range(1, N_DEV):
            total = total + stage[d]
        sum_ref[pl.ds(me * rc, rc), :] = total
        for k, (px, py, pc) in enumerate(peers):
            gather.append(pltpu.make_async_remote_copy(
                src_ref=chunk(sum_ref, me), dst_ref=chunk(sum_ref, me), send_sem=send2.at[k],
                recv_sem=recv2.at[k], device_id=(px, py, pc), device_id_type=MESH))
        for cp in gather:
            cp.start()
        for k, (px, py, pc) in enumerate(peers):
            theirs = 4 * px + 2 * py + pc
            pltpu.make_async_remote_copy(
                src_ref=chunk(sum_ref, theirs), dst_ref=chunk(sum_ref, theirs), send_sem=send2.at[k],
                recv_sem=recv2.at[k], device_id=(px, py, pc), device_id_type=MESH).wait_recv()
        for cp in scatter + gather:
            cp.wait_send()

    return pl.pallas_call(
        body, name=name, out_shape=jax.ShapeDtypeStruct((rows, n), v.dtype),
        in_specs=[pl.BlockSpec(memory_space=pltpu.VMEM)], out_specs=pl.BlockSpec(memory_space=pltpu.VMEM),
        scratch_shapes=[pltpu.VMEM((N_DEV, rc, n), v.dtype)] + [pltpu.SemaphoreType.DMA((N_DEV - 1,))] * 4,
        compiler_params=pltpu.CompilerParams(vmem_limit_bytes=VMEM_LIMIT),
    )(v)


def _gather_weights(w_in_s, w_out_s):
    nl = w_in_s.shape[0]
    shapes = ((nl, N_CHIP) + w_in_s.shape[1:], (nl, N_CHIP) + w_out_s.shape[1:])
    n_sem = 2 * 3 * 2

    def body(win_ref, wout_ref, win_o, wout_o, send_sems, recv_sems):
        x, y, c = lax.axis_index("x"), lax.axis_index("y"), lax.axis_index("c")
        s = 2 * x + y
        sibling = (x, y, 1 - c)
        chips = [(1 - x, y), (x, 1 - y), (1 - x, 1 - y)]
        outs = (win_o, wout_o)
        for src, dst in ((win_ref, win_o), (wout_ref, wout_o)):
            rows = src.shape[1]
            step = min(rows, 256)
            for l in range(nl):
                for r in range(0, rows, step):
                    dst[l, s, r:r + step, :] = src[l, r:r + step, :].astype(BF16)

        def copy(k, ref, piece, half, to):
            r2 = ref.shape[2] // 2
            rows = ref.at[:, piece, pl.ds(half * r2, r2), :]
            return pltpu.make_async_remote_copy(
                src_ref=rows, dst_ref=rows, send_sem=send_sems.at[k], recv_sem=recv_sems.at[k],
                device_id=to, device_id_type=MESH)

        sends = [copy(2 * j + a, ref, s, c, (px, py, c)) for j, (px, py) in enumerate(chips)
                 for a, ref in enumerate(outs)]
        for cp in sends:
            cp.start()
        passed = []
        for j, (px, py) in enumerate(chips):
            for a, ref in enumerate(outs):
                copy(2 * j + a, ref, 2 * px + py, c, (px, py, c)).wait_recv()
                passed.append(copy(6 + 2 * j + a, ref, 2 * px + py, c, sibling))
                passed[-1].start()
        for j, (px, py) in enumerate(chips):
            for a, ref in enumerate(outs):
                copy(6 + 2 * j + a, ref, 2 * px + py, 1 - c, sibling).wait_recv()
        for cp in sends + passed:
            cp.wait_send()

    return pl.pallas_call(
        body, name="gather_weights",
        out_shape=tuple(jax.ShapeDtypeStruct(s, BF16) for s in shapes),
        in_specs=[pl.BlockSpec(memory_space=pltpu.VMEM)] * 2,
        out_specs=tuple(pl.BlockSpec(memory_space=pltpu.VMEM) for _ in shapes),
        scratch_shapes=[pltpu.SemaphoreType.DMA((n_sem,)), pltpu.SemaphoreType.DMA((n_sem,))],
        compiler_params=pltpu.CompilerParams(vmem_limit_bytes=VMEM_LIMIT),
    )(w_in_s, w_out_s)


def _send_half_to_sibling(g, tag):
    nl, nc, r, cdim = g.shape
    half = r // 2

    def body(g_ref, recv_ref, send_sem, recv_sem):
        x, y, c = lax.axis_index("x"), lax.axis_index("y"), lax.axis_index("c")
        cp = pltpu.make_async_remote_copy(
            src_ref=g_ref.at[:, :, pl.ds((1 - c) * half, half), :], dst_ref=recv_ref,
            send_sem=send_sem, recv_sem=recv_sem, device_id=(x, y, 1 - c), device_id_type=MESH)
        cp.start()
        cp.wait()

    return pl.pallas_call(
        body, name=f"rs_sibling_{tag}",
        out_shape=jax.ShapeDtypeStruct((nl, nc, half, cdim), g.dtype),
        in_specs=[pl.BlockSpec(memory_space=pl.ANY)], out_specs=pl.BlockSpec(memory_space=pl.ANY),
        scratch_shapes=[pltpu.SemaphoreType.DMA, pltpu.SemaphoreType.DMA],
    )(g)


def _add_half(g, recv, cidx, tag):
    nl, nc, r, cdim = g.shape
    half = r // 2
    rt = min(half, 128)
    nrt = half // rt

    def body(c_ref, a_ref, b_ref, o_ref, ob_ref):
        c = c_ref[0]
        first = jnp.where(c == 0, a_ref[...], b_ref[...])
        second = jnp.where(c == 0, b_ref[...], a_ref[...])
        total = first + second
        o_ref[...] = total
        ob_ref[...] = total.astype(BF16)

    out_spec = pl.BlockSpec((None, None, rt, cdim), lambda l, j, i, c: (l, j, i, 0))
    grid_spec = pltpu.PrefetchScalarGridSpec(
        num_scalar_prefetch=1, grid=(nl, nc, nrt),
        in_specs=[pl.BlockSpec((None, None, rt, cdim), lambda l, j, i, c: (l, j, c[0] * nrt + i, 0)),
                  pl.BlockSpec((None, None, rt, cdim), lambda l, j, i, c: (l, j, i, 0))],
        out_specs=(out_spec, out_spec))
    return pl.pallas_call(
        body, name=f"rs_add_sibling_{tag}", grid_spec=grid_spec,
        out_shape=(jax.ShapeDtypeStruct((nl, nc, half, cdim), g.dtype),
                   jax.ShapeDtypeStruct((nl, nc, half, cdim), BF16)),
        compiler_params=_params(("parallel", "parallel", "parallel")),
    )(cidx, g, recv)


def _exchange_chips(h, tag):
    nl, nc, r2, cdim = h.shape

    def body(h_ref, recv_ref, send_sems, recv_sems):
        x, y, c = lax.axis_index("x"), lax.axis_index("y"), lax.axis_index("c")
        chips = [(1 - x, y), (x, 1 - y), (1 - x, 1 - y)]
        cps = []
        for k, (px, py) in enumerate(chips):
            cps.append(pltpu.make_async_remote_copy(
                src_ref=h_ref.at[:, 2 * px + py], dst_ref=recv_ref.at[k], send_sem=send_sems.at[k],
                recv_sem=recv_sems.at[k], device_id=(px, py, c), device_id_type=MESH))
        for cp in cps:
            cp.start()
        for cp in cps:
            cp.wait()

    return pl.pallas_call(
        body, name=f"rs_chips_{tag}",
        out_shape=jax.ShapeDtypeStruct((3, nl, r2, cdim), h.dtype),
        in_specs=[pl.BlockSpec(memory_space=pl.ANY)], out_specs=pl.BlockSpec(memory_space=pl.ANY),
        scratch_shapes=[pltpu.SemaphoreType.DMA((3,)), pltpu.SemaphoreType.DMA((3,))],
    )(h)


def _chip_copies(h_ref, land_ref, send_sems, recv_sems):
    x, y, c = lax.axis_index("x"), lax.axis_index("y"), lax.axis_index("c")
    chips = [(1 - x, y), (x, 1 - y), (1 - x, 1 - y)]
    return [pltpu.make_async_remote_copy(
        src_ref=h_ref.at[:, 2 * px + py], dst_ref=land_ref.at[k], send_sem=send_sems.at[k],
        recv_sem=recv_sems.at[k], device_id=(px, py, c), device_id_type=MESH) for k, (px, py) in enumerate(chips)]


def _exchange_chips_start(h, tag):
    nl, nc, r2, cdim = h.shape
    land_shape = (3, nl, r2, cdim)
    hbm = pl.BlockSpec(memory_space=pltpu.HBM)
    sem = pl.BlockSpec(memory_space=pltpu.SEMAPHORE)

    def body(h_ref, land_ref, send_sems, recv_sems, h_thru, land_thru, token):
        for cp in _chip_copies(h_ref, land_ref, send_sems, recv_sems):
            cp.start()
        token[...] = jnp.zeros_like(token)

    return pl.pallas_call(
        body, name=f"rs_chips_start_{tag}",
        out_shape=(pltpu.SemaphoreType.DMA((3,)), pltpu.SemaphoreType.DMA((3,)), pltpu.HBM(h.shape, h.dtype),
                   pltpu.HBM(land_shape, h.dtype), jax.ShapeDtypeStruct((8, 128), F32)),
        in_specs=(hbm, hbm), out_specs=(sem, sem, hbm, hbm, pl.BlockSpec(memory_space=pltpu.VMEM)),
        input_output_aliases={0: 2, 1: 3},
        compiler_params=pltpu.CompilerParams(has_side_effects=pltpu.SideEffectType.DATAFLOW_SIDE_EFFECTING),
    )(pltpu.with_memory_space_constraint(h, pltpu.HBM),
      pltpu.with_memory_space_constraint(lax.empty(land_shape, h.dtype), pltpu.HBM))


def _exchange_chips_wait(send_sems, recv_sems, h_thru, land_thru, after, tag):
    hbm = pl.BlockSpec(memory_space=pltpu.HBM)
    sem = pl.BlockSpec(memory_space=pltpu.SEMAPHORE)

    def body(h_ref, land_ref, send_sems, recv_sems, after_ref, h_dead, got_ref):
        for cp in _chip_copies(h_ref, land_ref, send_sems, recv_sems):
            cp.wait_send()
            cp.wait_recv()

    return pl.pallas_call(
        body, name=f"rs_chips_wait_{tag}",
        out_shape=(pltpu.HBM(h_thru.shape, h_thru.dtype), pltpu.HBM(land_thru.shape, land_thru.dtype)),
        in_specs=(hbm, hbm, sem, sem, pl.BlockSpec(memory_space=pl.ANY)), out_specs=(hbm, hbm),
        input_output_aliases={0: 0, 1: 1},
        compiler_params=pltpu.CompilerParams(has_side_effects=pltpu.SideEffectType.DATAFLOW_SIDE_EFFECTING),
    )(h_thru, land_thru, send_sems, recv_sems, after)[1]


def _add_chips(h, recv, sidx, tag):
    nl, nc, r2, cdim = h.shape
    rt = min(r2, 128)

    def body(s_ref, own_ref, r_ref, o_ref):
        s = s_ref[0]
        got = [r_ref[k].astype(F32) for k in range(3)]
        total = None
        for chip in range(N_CHIP):
            term = jnp.where(s == chip, own_ref[...],
                             jnp.where((s ^ 2) == chip, got[0], jnp.where((s ^ 1) == chip, got[1], got[2])))
            total = term if total is None else total + term
        o_ref[...] = total

    grid_spec = pltpu.PrefetchScalarGridSpec(
        num_scalar_prefetch=1, grid=(nl, r2 // rt),
        in_specs=[pl.BlockSpec((None, None, rt, cdim), lambda l, i, s: (l, s[0], i, 0)),
                  pl.BlockSpec((3, None, rt, cdim), lambda l, i, s: (0, l, i, 0))],
        out_specs=pl.BlockSpec((None, rt, cdim), lambda l, i, s: (l, i, 0)))
    return pl.pallas_call(
        body, name=f"rs_add_chips_{tag}", grid_spec=grid_spec,
        out_shape=jax.ShapeDtypeStruct((nl, r2, cdim), h.dtype),
        compiler_params=_params(("parallel", "parallel")),
    )(sidx, h, recv)


def _swap_halves(red, tag):
    def body(red_ref, out_ref, send_sem, recv_sem):
        x, y, c = lax.axis_index("x"), lax.axis_index("y"), lax.axis_index("c")
        cp = pltpu.make_async_remote_copy(
            src_ref=red_ref, dst_ref=out_ref, send_sem=send_sem, recv_sem=recv_sem,
            device_id=(x, y, 1 - c), device_id_type=MESH)
        cp.start()
        cp.wait()

    return pl.pallas_call(
        body, name=f"rs_swap_{tag}", out_shape=jax.ShapeDtypeStruct(red.shape, red.dtype),
        in_specs=[pl.BlockSpec(memory_space=pl.ANY)], out_specs=pl.BlockSpec(memory_space=pl.ANY),
        scratch_shapes=[pltpu.SemaphoreType.DMA, pltpu.SemaphoreType.DMA],
    )(red)


def _reduce_scatter(g, cidx, sidx, tag):
    started = _reduce_scatter_start(g, cidx, tag)
    return _reduce_scatter_finish(started, started[-1], sidx, tag)


def _reduce_scatter_start(g, cidx, tag):
    recv = _send_half_to_sibling(g, tag)
    chip_sum, chip_sum_bf16 = _add_half(g, recv, cidx, tag)
    return (chip_sum,) + tuple(_exchange_chips_start(chip_sum_bf16, tag))


def _reduce_scatter_finish(started, after, sidx, tag):
    chip_sum, send_sems, recv_sems, h_thru, land_thru, _ = started
    got = _exchange_chips_wait(send_sems, recv_sems, h_thru, land_thru, after, tag)
    red = _add_chips(chip_sum, got, sidx, tag)
    return red, _swap_halves(red, tag)


def _ada_forward(c_all, w_ada_s, b_s):
    nl = w_ada_s.shape[0]

    def body(c_ref, w_ref, b_ref, o_ref):
        cv = c_ref[...]
        ca = (cv * _sig(cv)).astype(BF16)
        for l in range(nl):
            o_ref[l] = _dot(ca, w_ref[l].astype(BF16)) + b_ref[l]

    return pl.pallas_call(
        body, name="ada_forward", out_shape=jax.ShapeDtypeStruct((nl, N_DEV, PIECE), F32),
        compiler_params=pltpu.CompilerParams(vmem_limit_bytes=VMEM_LIMIT),
    )(c_all, w_ada_s, b_s)


def _ada_backward(c_all, dmod_s):
    nl = dmod_s.shape[0]

    def body(c_ref, d_ref, o_ref):
        cv = c_ref[...]
        ca = (cv * _sig(cv)).astype(BF16)
        for l in range(nl):
            o_ref[l] = _dot_tn(ca, d_ref[l].astype(BF16))

    return pl.pallas_call(
        body, name="ada_backward", out_shape=jax.ShapeDtypeStruct((nl, D, PIECE), F32),
        compiler_params=pltpu.CompilerParams(vmem_limit_bytes=VMEM_LIMIT),
    )(c_all, dmod_s)


def _normed(xv, vecs):
    r = lax.rsqrt(jnp.mean(xv * xv, axis=-1, keepdims=True) + EPS)
    xhat = xv * r
    hn = xhat * vecs[3:4, :]
    return r, xhat, hn


def _bwd_in(dz, x, dxo, vecs, w, l, tt, gw_all=None):
    t = x.shape[0]
    nt = t // tt
    nl = w.shape[0]

    def body(dz_ref, x_ref, dxo_ref, v_ref, w_any, *rest):
        dx_ref, gw_any, sums_ref, w_vmem, gw_acc, sem = rest[-6:]
        i = pl.program_id(0)

        @pl.when(i == 0)
        def _():
            cp = pltpu.make_async_copy(w_any.at[l], w_vmem, sem)
            cp.start()
            cp.wait()
            gw_acc[...] = jnp.zeros_like(gw_acc)
            sums_ref[...] = jnp.zeros_like(sums_ref)

        vecs_v = v_ref[...]
        r, xhat, hn = _normed(x_ref[...], vecs_v)
        one_scale = 1.0 + vecs_v[1:2, :]
        hb = (hn * one_scale + vecs_v[0:1, :]).astype(BF16)
        dh = None
        for j in range(N_CHIP):
            dzj = dz_ref[:, PIECE * j:PIECE * (j + 1)]
            part = _dot_nt(dzj, w_vmem[j])
            dh = part if dh is None else dh + part
            gw_acc[j] += _dot_tn(hb, dzj)
        sums_ref[0:1, :] += _rowsum(dh)
        sums_ref[1:2, :] += _rowsum(dh * hn)
        sums_ref[2:3, :] += _rowsum(dh * one_scale * xhat)
        dxh = dh * (vecs_v[3:4, :] * one_scale)
        dx_ref[...] = dxo_ref[...] + r * (dxh - xhat * jnp.mean(dxh * xhat, axis=-1, keepdims=True))

        @pl.when(i == nt - 1)
        def _():
            cp = pltpu.make_async_copy(gw_acc, gw_any.at[l], sem)
            cp.start()
            cp.wait()

    carried = [] if gw_all is None else [gw_all]
    return pl.pallas_call(
        body, name=f"bwd_in_{l}", grid=(nt,),
        in_specs=[pl.BlockSpec((tt, DIN), lambda i: (i, 0)), pl.BlockSpec((tt, D), lambda i: (i, 0)),
                  pl.BlockSpec((tt, D), lambda i: (i, 0)), pl.BlockSpec((8, D), lambda i: (0, 0)),
                  pl.BlockSpec(memory_space=pl.ANY)] + [pl.BlockSpec(memory_space=pl.ANY)] * len(carried),
        out_specs=(pl.BlockSpec((tt, D), lambda i: (i, 0)), pl.BlockSpec(memory_space=pl.ANY),
                   pl.BlockSpec((8, D), lambda i: (0, 0))),
        out_shape=(jax.ShapeDtypeStruct((t, D), F32), jax.ShapeDtypeStruct((nl, N_CHIP, D, PIECE), F32),
                   jax.ShapeDtypeStruct((8, D), F32)),
        scratch_shapes=[pltpu.VMEM((N_CHIP, D, PIECE), BF16), pltpu.VMEM((N_CHIP, D, PIECE), F32),
                        pltpu.SemaphoreType.DMA],
        input_output_aliases={5: 1} if carried else {},
        compiler_params=_params(("arbitrary",)),
    )(dz, x, dxo, vecs, w, *carried)


def _col(ref, k):
    return ref[:, GW * k:GW * (k + 1)]


def _lane_group():
    return lax.broadcasted_iota(jnp.int32, (1, GW), 1) // (GW // 4)


def _pool_window(group):
    return jnp.where(group == 0, 2.0, jnp.where(group == 1, 4.0, jnp.where(group == 2, 8.0, 16.0)))


def _by_group(group, a, b, c, d):
    return jnp.where(group == 0, a, jnp.where(group == 1, b, jnp.where(group == 2, c, d)))


def _layer_norm(x, g, b):
    mu = jnp.mean(x, axis=-1, keepdims=True)
    xc = x - mu
    rstd = lax.rsqrt(jnp.mean(xc * xc, axis=-1, keepdims=True) + EPS)
    xh = xc * rstd
    return xh * g + b, xh, rstd


def _layer_norm_grad(d_out, xh, rstd, g):
    dxh = d_out * g
    return rstd * (dxh - jnp.mean(dxh, axis=-1, keepdims=True) - xh * jnp.mean(dxh * xh, axis=-1, keepdims=True))


def _shifted_copies(ext_ref, sh_ref, n):
    for b in range(1, 8):
        sh_ref[b, 0:n - 8, :] = ext_ref[b:b + n - 8, :]


def _rows_at(ext_ref, sh_ref, start, rows):
    b = start % 8
    if b == 0 or sh_ref is None:
        return ext_ref[start:start + rows, :]
    return sh_ref[b, start - b:start - b + rows, :]


class _Beside:
    def __init__(self, thunks=(), points=1):
        self.thunks, self.points, self.seen, self.done = list(thunks), max(points, 1), 0, 0

    def _run_to(self, due):
        while self.done < min(due, len(self.thunks)):
            self.thunks[self.done]()
            self.done += 1

    def here(self):
        self.seen += 1
        self._run_to(-(-self.seen * len(self.thunks) // self.points))

    def rest(self):
        self._run_to(len(self.thunks))


def _conv_taps(ext_ref, w_ref, n_taps, first_row, tm, out_ref, flip=False, sh_ref=None, beside=None):
    for rb in range(0, tm, CONV_ROWS):
        acc = None
        for k in range(n_taps):
            wk = n_taps - 1 - k if flip else k
            term = w_ref[wk:wk + 1, :] * _rows_at(ext_ref, sh_ref, first_row + rb + k, CONV_ROWS)
            acc = term if acc is None else acc + term
        out_ref[rb:rb + CONV_ROWS, :] = acc
        if beside is not None:
            beside.here()


def _mix_forward(zc, zp, first, row0, tm, p, s, after, beside=None, conv_c=None):
    f = {}
    beside = beside or _Beside()
    keep = jnp.where(first, 0.0, 1.0)
    group = _lane_group()
    beside.here()
    a_b, a_c, a_x, a_g = _col(zc, 0), _col(zc, 1), _col(zc, 2), _col(zc, 3)
    s['ua'][0:HALO, :] = _col(zp, 1) * _col(zp, 2) * keep
    s['ua'][HALO:HALO + tm, :] = a_c * a_x
    _conv_taps(s['ua'], p['wa'], CONV_A, HALO - (CONV_A - 1), tm, s['cv'], beside=beside)
    cv = s['cv'][...]
    sg_a = _sig(a_g)
    f = dict(a_b=a_b, a_c=a_c, a_x=a_x, a_g=a_g, cv=cv, sg_a=sg_a)
    after(0, f, (a_b * cv * (a_g * sg_a)).astype(BF16))
    beside.here()
    b_p, b_g = _col(zc, 4), _col(zc, 5)
    s['p0'][0:HALO, :] = _col(zp, 4) * keep
    s['p0'][HALO:HALO + tm, :] = b_p
    n = HALO + tm
    s['p1'][8:n, :] = s['p0'][8:n, :] + s['p0'][7:n - 1, :]
    w2 = s['p1'][HALO:n, :]
    beside.here()
    s['p0'][16:n, :] = s['p1'][16:n, :] + s['p1'][14:n - 2, :]
    w4 = s['p0'][HALO:n, :]
    beside.here()
    s['p1'][24:n, :] = s['p0'][24:n, :] + s['p0'][20:n - 4, :]
    w8 = s['p1'][HALO:n, :]
    w16 = w8 + s['p1'][HALO - 8:n - 8, :]
    tpos = (row0 + lax.broadcasted_iota(jnp.int32, (tm, 1), 0) + 1).astype(F32)
    inv_cnt = 1.0 / jnp.minimum(tpos, _pool_window(group))
    pooled = (_by_group(group, w2, w4, w8, w16) * inv_cnt - b_p).astype(BF16)
    beside.here()
    q = _dot(pooled, p['wbd'][...])
    sg_b = _sig(b_g)
    f = dict(b_g=b_g, pooled=pooled, q=q, sg_b=sg_b, inv_cnt=inv_cnt)
    after(1, f, (q * p['v256'][0:1, :] * (b_g * sg_b)).astype(BF16))
    beside.here()
    c_a, c_gl, c_g = _col(zc, 6), _col(zc, 7), _col(zc, 8)
    sg_gl = _sig(c_gl)
    zp_gl = _col(zp, 7)
    s['hc'][0:HALO, :] = _col(zp, 6) * _sig(zp_gl) * keep
    s['hc'][HALO:HALO + tm, :] = c_a * sg_gl
    _shifted_copies(s['hc'], s['hcs'], HALO + tm)
    beside.here()
    if conv_c is None:
        _conv_taps(s['hc'], p['wdw'], CONV_C, HALO - (CONV_C - 1), tm, s['co'], sh_ref=s['hcs'], beside=beside)
        conv_c = s['co']
    conv_v = conv_c[...]
    co = conv_v + p['v256'][1:2, :]
    ln_c, xh_c, rstd_c = _layer_norm(co, p['v256'][2:3, :], p['v256'][3:4, :])
    beside.here()
    sg_ln = _sig(ln_c)
    sc = (ln_c * sg_ln).astype(BF16)
    pw = _dot(sc, p['wpw'][...]) + p['v256'][4:5, :]
    sg_c = _sig(c_g)
    f = dict(c_a=c_a, c_g=c_g, sg_gl=sg_gl, ln_c=ln_c, xh_c=xh_c, rstd_c=rstd_c, sg_ln=sg_ln, sc=sc, pw=pw,
             sg_c=sg_c, conv_c=conv_v)
    after(2, f, (pw * (c_g * sg_c)).astype(BF16))
    beside.here()
    d_u, d_v, d_g = _col(zc, 9), _col(zc, 10), _col(zc, 11)
    gu, th_u = _gelu(d_u)
    gv, th_v = _gelu(d_v)
    beside.here()
    v, xh_d, rstd_d = _layer_norm(gv, p['v256'][5:6, :], p['v256'][6:7, :])
    vb = v.astype(BF16)
    parts = []
    for ch in range(tm // CHUNK):
        vc = vb[ch * CHUNK:(ch + 1) * CHUNK, :]
        mixed = p['bsd'][...]
        for h in range(4):
            mixed = mixed + jnp.where(group == h, _dot(p['ws'][h], vc), 0.0)
        parts.append(mixed)
        beside.here()
    mx = parts[0] if len(parts) == 1 else jnp.concatenate(parts, axis=0)
    sg_d = _sig(d_g)
    f = dict(d_u=d_u, d_v=d_v, d_g=d_g, gu=gu, th_u=th_u, th_v=th_v, xh_d=xh_d, rstd_d=rstd_d, vb=vb, mx=mx,
             sg_d=sg_d)
    after(3, f, (gu * mx * (d_g * sg_d)).astype(BF16))
    beside.rest()


def _mix_scratch(tm):
    ext = pltpu.VMEM((HALO + tm, GW), F32)
    tile = pltpu.VMEM((tm, GW), F32)
    return dict(ua=ext, cv=tile, p0=ext, p1=ext, hc=ext, co=tile, hcs=pltpu.VMEM((8, HALO + tm, GW), F32))


_MIX_PARAMS = ('wa', 'wdw', 'v256', 'wbd', 'wpw', 'ws', 'bsd', 'wout')


def _mix_param_specs(l):
    def whole(*shape):
        nd = len(shape)
        return pl.BlockSpec((None,) + shape, lambda i, _nd=nd: (l,) + (0,) * _nd)
    return [whole(8, GW), whole(32, GW), whole(16, GW), whole(GW, GW), whole(GW, GW), whole(4, CHUNK, CHUNK),
            whole(CHUNK, GW), whole(N_CHIP, GW, D)]


def _fwd_layer(x, vecs, w, mp, l, tm):
    t = x.shape[0]
    nt = t // tm
    names = list(_mix_scratch(tm))
    n_p = len(_MIX_PARAMS)

    def body(xm_ref, xr_ref, v_ref, w_ref, *rest):
        p = dict(zip(_MIX_PARAMS, rest[:n_p]))
        z_ref, xn_ref, y_ref, conv_ref = rest[n_p:n_p + 4]
        s = dict(zip(names, rest[n_p + 4:n_p + 4 + len(names)]))
        z_next, z_cur, z_halo = rest[n_p + 4 + len(names):]
        j = pl.program_id(0)

        @pl.when(j == 0)
        def _():
            z_cur[...] = jnp.zeros_like(z_cur)
            z_halo[...] = jnp.zeros_like(z_halo)

        vecs_v = v_ref[...]
        _, _, hn = _normed(xm_ref[...], vecs_v)
        hb = (hn * (1.0 + vecs_v[1:2, :]) + vecs_v[0:1, :]).astype(BF16)
        def project(g):
            def emit():
                k, c0 = divmod(g * GW, PIECE)
                part = _dot(hb, w_ref[k, :, c0:c0 + GW])
                z_ref[:, GW * g:GW * (g + 1)] = part
                z_next[:, GW * g:GW * (g + 1)] = part
            return emit

        tile = jnp.maximum(j - 1, 0)
        beside = _Beside([project(g) for g in range(DIN // GW)], points=MIX_FWD_POINTS)
        mixed = []
        _mix_forward(z_cur, z_halo, tile == 0, tile * tm, tm, p, s, lambda k, f, yk: mixed.append((f, yk)),
                     beside=beside)
        y = None
        for k, (_, yk) in enumerate(mixed):
            part = _dot(yk, p['wout'][k])
            y = part if y is None else y + part
        y_ref[...] = y
        xn_ref[...] = xr_ref[...] + vecs_v[2:3, :] * y
        conv_ref[...] = mixed[2][0]['conv_c']

        z_halo[...] = z_cur[tm - HALO:tm, :]
        z_cur[...] = z_next[...]

    def ahead(j):
        return jnp.minimum(j, nt - 1)

    def behind(j):
        return jnp.maximum(j - 1, 0)

    return pl.pallas_call(
        body, name=f"fwd_layer_{l}", grid=(nt + 1,),
        in_specs=[pl.BlockSpec((tm, D), lambda j: (ahead(j), 0)), pl.BlockSpec((tm, D), lambda j: (behind(j), 0)),
                  pl.BlockSpec((8, D), lambda j: (0, 0)),
                  pl.BlockSpec((None, N_CHIP, D, PIECE), lambda j: (l, 0, 0, 0))] + _mix_param_specs(l),
        out_specs=(pl.BlockSpec((tm, DIN), lambda j: (ahead(j), 0)), pl.BlockSpec((tm, D), lambda j: (behind(j), 0)),
                   pl.BlockSpec((tm, D), lambda j: (behind(j), 0)), pl.BlockSpec((tm, GW), lambda j: (behind(j), 0))),
        out_shape=(jax.ShapeDtypeStruct((t, DIN), F32), jax.ShapeDtypeStruct((t, D), F32),
                   jax.ShapeDtypeStruct((t, D), F32), jax.ShapeDtypeStruct((t, GW), F32)),
        scratch_shapes=list(_mix_scratch(tm).values())
        + [pltpu.VMEM((tm, DIN), F32), pltpu.VMEM((tm, DIN), F32), pltpu.VMEM((HALO, DIN), F32)],
        compiler_params=_params(("arbitrary",)),
    )(x, x, vecs, w, *[mp[k] for k in _MIX_PARAMS])


def _bwd_mix(dxo, y, conv, z, vecs, mp, wst, l, tm, gwout_all=None):
    t = dxo.shape[0]
    nt = t // tm
    nl = wst.shape[0]
    carried = [] if gwout_all is None else [gwout_all]
    per_halo = tm // HALO
    fwd_names = list(_mix_scratch(tm))
    ext = pltpu.VMEM((tm + HALO, GW), F32)
    carry = pltpu.VMEM((HALO, GW), F32)
    tile = pltpu.VMEM((tm, GW), F32)
    bwd_scratch = dict(ea=ext, eb=ext, eb2=ext, ec=ext, ca=carry, cb=carry, cc=carry, dua=tile, dhc=tile,
                       gbacc=pltpu.VMEM((CHUNK, GW), F32), ecs=pltpu.VMEM((8, tm + HALO, GW), F32),
                       dys=pltpu.VMEM((1, tm, GW), F32))
    bwd_names = list(bwd_scratch)
    n_p = len(_MIX_PARAMS)

    def body(dxo_ref, y_ref, conv_ref, zc, zp, v_ref, *rest):
        p = dict(zip(_MIX_PARAMS, rest[:n_p]))
        wst_ref = rest[n_p]
        first_out = n_p + 1 + len(carried)
        dz_ref, gwout, gwbd, gwpw, gws, gbs, gwdw, g256, g1024 = rest[first_out:first_out + 9]
        s = dict(zip(fwd_names + bwd_names, rest[first_out + 9:]))
        i = pl.program_id(0)
        ti = nt - 1 - i
        group = _lane_group()

        @pl.when(i == 0)
        def _():
            for ref in (gwout, gwbd, gwpw, gws, gbs, gwdw, g256, g1024, s['ca'], s['cb'], s['cc'], s['gbacc']):
                ref[...] = jnp.zeros_like(ref)

        def put(k, val):
            dz_ref[:, GW * k:GW * (k + 1)] = val.astype(BF16)

        def bwd_a(f, dya):
            taps = [None] * CONV_A
            for r in range(0, tm, ROW_BLOCK):
                rs = slice(r, r + ROW_BLOCK)
                a_b, a_g = zc[rs, 0:GW], zc[rs, 3 * GW:4 * GW]
                dy_a, cv = s['dys'][0, rs, :], s['cv'][rs, :]
                sg = _sig(a_g)
                silu = a_g * sg
                t = dy_a * cv
                dz_ref[rs, 0:GW] = (t * silu).astype(BF16)
                dz_ref[rs, 3 * GW:4 * GW] = (t * a_b * (sg * (1.0 + a_g * (1.0 - sg)))).astype(BF16)
                d_cv = dy_a * a_b * silu
                s['ea'][rs, :] = d_cv
                for k in range(CONV_A):
                    first = HALO - (CONV_A - 1) + k + r
                    term = d_cv * s['ua'][first:first + ROW_BLOCK, :]
                    taps[k] = term if taps[k] is None else taps[k] + term
            for k in range(CONV_A):
                g256[8 + k:9 + k, :] += _rowsum(taps[k])
            s['ea'][tm:tm + HALO, :] = s['ca'][...]
            s['ca'][...] = s['ea'][0:HALO, :]
            _conv_taps(s['ea'], p['wa'], CONV_A, 0, tm, s['dua'], flip=True)
            for r in range(0, tm, ROW_BLOCK):
                rs = slice(r, r + ROW_BLOCK)
                d_u = s['dua'][rs, :]
                dz_ref[rs, GW:2 * GW] = (d_u * zc[rs, 2 * GW:3 * GW]).astype(BF16)
                dz_ref[rs, 2 * GW:3 * GW] = (d_u * zc[rs, GW:2 * GW]).astype(BF16)

        def bwd_b(f, dyb):
            b_g, q, sg_b, inv_cnt = f['b_g'], f['q'], f['sg_b'], f['inv_cnt']
            scale_b = p['v256'][0:1, :]
            silu_b = b_g * sg_b
            g256[0:1, :] += _rowsum(dyb * q * silu_b)
            put(5, dyb * q * scale_b * (sg_b * (1.0 + b_g * (1.0 - sg_b))))
            d_q = (dyb * scale_b * silu_b).astype(BF16)
            gwbd[...] += _dot_tn(f['pooled'], d_q)
            d_pooled = _dot_nt(d_q, p['wbd'][...])
            gp = d_pooled * inv_cnt
            n = tm + HALO
            s['eb'][0:tm, :] = gp
            s['eb'][tm:n, :] = s['cb'][...]
            s['cb'][...] = gp[0:HALO, :]
            s['eb2'][0:n - 8, :] = s['eb'][0:n - 8, :] + s['eb'][1:n - 7, :]
            r2 = s['eb2'][0:tm, :]
            s['eb'][0:n - 16, :] = s['eb2'][0:n - 16, :] + s['eb2'][2:n - 14, :]
            r4 = s['eb'][0:tm, :]
            s['eb2'][0:n - 24, :] = s['eb'][0:n - 24, :] + s['eb'][4:n - 20, :]
            r8 = s['eb2'][0:tm, :]
            r16 = r8 + s['eb2'][8:tm + 8, :]
            put(4, _by_group(group, r2, r4, r8, r16) - d_pooled)

        def bwd_c(f, dyc):
            c_a, c_g, sg_gl, ln_c, xh_c, rstd_c, sg_ln, pw, sg_c = (
                f['c_a'], f['c_g'], f['sg_gl'], f['ln_c'], f['xh_c'], f['rstd_c'], f['sg_ln'], f['pw'], f['sg_c'])
            put(8, dyc * pw * (sg_c * (1.0 + c_g * (1.0 - sg_c))))
            d_pw = dyc * (c_g * sg_c)
            g256[4:5, :] += _rowsum(d_pw)
            d_pwb = d_pw.astype(BF16)
            gwpw[...] += _dot_tn(f['sc'], d_pwb)
            d_ln = _dot_nt(d_pwb, p['wpw'][...]) * (sg_ln * (1.0 + ln_c * (1.0 - sg_ln)))
            g256[2:3, :] += _rowsum(d_ln * xh_c)
            g256[3:4, :] += _rowsum(d_ln)
            d_co = _layer_norm_grad(d_ln, xh_c, rstd_c, p['v256'][2:3, :])
            g256[1:2, :] += _rowsum(d_co)
            for k in range(CONV_C):
                gwdw[k:k + 1, :] += _rowsum(d_co * _rows_at(s['hc'], s['hcs'], HALO - (CONV_C - 1) + k, tm))
            s['ec'][0:tm, :] = d_co
            s['ec'][tm:tm + HALO, :] = s['cc'][...]
            s['cc'][...] = d_co[0:HALO, :]
            _shifted_copies(s['ec'], s['ecs'], tm + HALO)
            _conv_taps(s['ec'], p['wdw'], CONV_C, 0, tm, s['dhc'], flip=True, sh_ref=s['ecs'])
            d_hc = s['dhc'][...]
            put(6, d_hc * sg_gl)
            put(7, d_hc * c_a * sg_gl * (1.0 - sg_gl))

        def bwd_d(f, dyd):
            d_uu, d_vv, d_g, gu, th_u, th_v, xh_d, rstd_d, vb, mx, sg_d = (
                f['d_u'], f['d_v'], f['d_g'], f['gu'], f['th_u'], f['th_v'], f['xh_d'], f['rstd_d'], f['vb'],
                f['mx'], f['sg_d'])
            silu_d = d_g * sg_d
            put(9, dyd * mx * silu_d * _gelu_grad(d_uu, th_u))
            put(11, dyd * gu * mx * (sg_d * (1.0 + d_g * (1.0 - sg_d))))
            d_mx = dyd * gu * silu_d
            dv_parts = []
            gb = None
            for ch in range(tm // CHUNK):
                dmc = d_mx[ch * CHUNK:(ch + 1) * CHUNK, :]
                gb = dmc if gb is None else gb + dmc
                dmb = dmc.astype(BF16)
                vc = vb[ch * CHUNK:(ch + 1) * CHUNK, :]
                dvc = None
                for h in range(4):
                    gws[h] += _dot_nt(jnp.where(group == h, dmb, jnp.zeros_like(dmb)), vc)
                    part = jnp.where(group == h, _dot(wst_ref[h], dmb), 0.0)
                    dvc = part if dvc is None else dvc + part
                dv_parts.append(dvc)
            s['gbacc'][...] += gb
            d_v = dv_parts[0] if len(dv_parts) == 1 else jnp.concatenate(dv_parts, axis=0)
            g256[5:6, :] += _rowsum(d_v * xh_d)
            g256[6:7, :] += _rowsum(d_v)
            d_gv = _layer_norm_grad(d_v, xh_d, rstd_d, p['v256'][5:6, :])
            put(10, d_gv * _gelu_grad(d_vv, th_v))

        mixed = []
        _mix_forward(zc, zp, ti == 0, ti * tm, tm, p, s, lambda j, f, yj: mixed.append((f, yj)), conv_c=conv_ref)
        dxo_v = dxo_ref[...]
        g1024[0:1, :] += _rowsum(dxo_v * y_ref[...])
        dy = (dxo_v * v_ref[2:3, :]).astype(BF16)
        s['dys'][0] = _dot_nt(dy, p['wout'][0])
        dys = [None] + [_dot_nt(dy, p['wout'][j]) for j in range(1, 4)]
        for j, (f, yj) in enumerate(mixed):
            gwout[j] += _dot_tn(yj, dy)
        for (f, _), backward, dyj in zip(mixed, (bwd_a, bwd_b, bwd_c, bwd_d), dys):
            backward(f, dyj)

        @pl.when(i == nt - 1)
        def _():
            row = lax.broadcasted_iota(jnp.int32, (CHUNK, CHUNK), 0)
            col = lax.broadcasted_iota(jnp.int32, (CHUNK, CHUNK), 1)
            for h in range(4):
                gws[h] = jnp.where(col <= row, gws[h], 0.0)
            head_of_lane = lax.broadcasted_iota(jnp.int32, (GW, CHUNK), 0) // (GW // 4)
            sel = jnp.where(head_of_lane == lax.broadcasted_iota(jnp.int32, (GW, CHUNK), 1), 1.0, 0.0)
            gbs[...] = jnp.dot(s['gbacc'][...], sel, preferred_element_type=F32, precision=lax.Precision.HIGHEST)

    def const(*shape):
        nd = len(shape)
        return pl.BlockSpec(shape, lambda i, _nd=nd: (0,) * _nd)

    def rev(i):
        return nt - 1 - i

    out_shapes = (jax.ShapeDtypeStruct((t, DIN), BF16), jax.ShapeDtypeStruct((nl, N_CHIP, GW, D), F32),
                  jax.ShapeDtypeStruct((GW, GW), F32), jax.ShapeDtypeStruct((GW, GW), F32),
                  jax.ShapeDtypeStruct((4, CHUNK, CHUNK), F32), jax.ShapeDtypeStruct((CHUNK, CHUNK), F32),
                  jax.ShapeDtypeStruct((32, GW), F32), jax.ShapeDtypeStruct((16, GW), F32),
                  jax.ShapeDtypeStruct((8, D), F32))
    out_specs = (pl.BlockSpec((tm, DIN), lambda i: (rev(i), 0)),
                 pl.BlockSpec((None, N_CHIP, GW, D), lambda i: (l, 0, 0, 0)), const(GW, GW),
                 const(GW, GW), const(4, CHUNK, CHUNK), const(CHUNK, CHUNK), const(32, GW), const(16, GW),
                 const(8, D))
    scratch = list(_mix_scratch(tm).values()) + list(bwd_scratch.values())
    n_in = 6 + n_p + 1
    return pl.pallas_call(
        body, name=f"bwd_mix_{l}", grid=(nt,),
        in_specs=[pl.BlockSpec((tm, D), lambda i: (rev(i), 0)), pl.BlockSpec((tm, D), lambda i: (rev(i), 0)),
                  pl.BlockSpec((tm, GW), lambda i: (rev(i), 0)), pl.BlockSpec((tm, DIN), lambda i: (rev(i), 0)),
                  pl.BlockSpec((HALO, DIN), lambda i: (jnp.maximum(rev(i) * per_halo - 1, 0), 0)),
                  pl.BlockSpec((8, D), lambda i: (0, 0))]
        + _mix_param_specs(l)
        + [pl.BlockSpec((None, 4, CHUNK, CHUNK), lambda i: (l, 0, 0, 0))]
        + [pl.BlockSpec(memory_space=pl.ANY)] * len(carried),
        out_specs=out_specs, out_shape=out_shapes, scratch_shapes=scratch,
        input_output_aliases={n_in: 1} if carried else {},
        compiler_params=_params(("arbitrary",)),
    )(dxo, y, conv, z, z, vecs, *[mp[k] for k in _MIX_PARAMS], wst, *carried)


def _final(x, target, fg, tt):
    t = x.shape[0]
    nt = t // tt

    def body(x_ref, t_ref, g_ref, dx_ref, sums_ref):
        i = pl.program_id(0)

        @pl.when(i == 0)
        def _():
            sums_ref[...] = jnp.zeros_like(sums_ref)

        xv = x_ref[...]
        g = g_ref[0:1, :]
        r = lax.rsqrt(jnp.mean(xv * xv, axis=-1, keepdims=True) + EPS)
        xhat = xv * r
        err = xhat * g - t_ref[...]
        sums_ref[1:2, :] += _rowsum(err * err) * (0.5 / D)
        dyv = err * (1.0 / D)
        sums_ref[0:1, :] += _rowsum(dyv * xhat)
        dxh = dyv * g
        dx_ref[...] = r * (dxh - xhat * jnp.mean(dxh * xhat, axis=-1, keepdims=True))

        @pl.when(i == nt - 1)
        def _():
            sums_ref[2:3, :] = jnp.broadcast_to(jnp.sum(sums_ref[1:2, :], axis=-1, keepdims=True), (1, D))

    return pl.pallas_call(
        body, name="final_loss", grid=(nt,),
        in_specs=[pl.BlockSpec((tt, D), lambda i: (i, 0)), pl.BlockSpec((tt, D), lambda i: (i, 0)),
                  pl.BlockSpec((8, D), lambda i: (0, 0))],
        out_specs=(pl.BlockSpec((tt, D), lambda i: (i, 0)), pl.BlockSpec((8, D), lambda i: (0, 0))),
        out_shape=(jax.ShapeDtypeStruct((t, D), F32), jax.ShapeDtypeStruct((8, D), F32)),
        compiler_params=_params(("arbitrary",)),
    )(x, target, fg)


def _adamw(w, g, m, v, tag):
    rows, cols = w.shape
    rt = rows
    for cand in (512, 256, 128, 64, 32, 16, 8):
        if rows % cand == 0:
            rt = cand
            break

    def body(w_ref, g_ref, m_ref, v_ref, d_ref, nm_ref, nv_ref):
        d_ref[...], nm_ref[...], nv_ref[...] = _adamw_update(g_ref[...], w_ref[...], m_ref[...], v_ref[...])

    spec = pl.BlockSpec((rt, cols), lambda i: (i, 0))
    return pl.pallas_call(
        body, name=f"adamw_{tag}", grid=(rows // rt,), in_specs=[spec] * 4, out_specs=(spec,) * 3,
        out_shape=(jax.ShapeDtypeStruct(w.shape, F32),) * 3, compiler_params=_params(("parallel",)),
    )(w, g, m, v)


def _adamw_update(gv, w, m, v):
    nm = ADAM_B1 * m + (1.0 - ADAM_B1) * gv
    nv = ADAM_B2 * v + (1.0 - ADAM_B2) * (gv * gv)
    m_hat = nm / (1.0 - ADAM_B1 ** ADAM_STEP)
    v_hat = nv / (1.0 - ADAM_B2 ** ADAM_STEP)
    return -ADAM_LR * (m_hat / (jnp.sqrt(v_hat) + ADAM_EPS) + ADAM_WD * w), nm, nv


def _adamw_many(ws, gs, ms, vs):
    n = len(ws)

    def body(*refs):
        w_refs, g_refs, m_refs, v_refs = (refs[k * n:(k + 1) * n] for k in range(4))
        d_refs, nm_refs, nv_refs = (refs[(4 + k) * n:(5 + k) * n] for k in range(3))
        for k in range(n):
            d_refs[k][...], nm_refs[k][...], nv_refs[k][...] = _adamw_update(
                g_refs[k][...], w_refs[k][...], m_refs[k][...], v_refs[k][...])

    shapes = tuple(jax.ShapeDtypeStruct(w.shape, F32) for w in ws)
    out = pl.pallas_call(body, name="adamw_small", out_shape=shapes * 3)(*ws, *gs, *ms, *vs)
    return out[:n], out[n:2 * n], out[2 * n:]


def _adamw_halves(w, mine, theirs, m, v, cidx, tag):
    nl, r, cdim = w.shape
    r2 = r // 2
    rt = min(r2, 256)
    nrt = r2 // rt

    def body(c_ref, a_ref, b_ref, w_ref, m_ref, v_ref, g_ref, d_ref, nm_ref, nv_ref):
        gv = jnp.where(pl.program_id(1) == c_ref[0], a_ref[...], b_ref[...])
        g_ref[...] = gv
        d_ref[...], nm_ref[...], nv_ref[...] = _adamw_update(gv, w_ref[...], m_ref[...], v_ref[...])

    half = pl.BlockSpec((None, rt, cdim), lambda l, h, i, c: (l, i, 0))
    whole = pl.BlockSpec((None, rt, cdim), lambda l, h, i, c: (l, h * nrt + i, 0))
    grid_spec = pltpu.PrefetchScalarGridSpec(
        num_scalar_prefetch=1, grid=(nl, 2, nrt), in_specs=[half, half, whole, whole, whole],
        out_specs=(whole,) * 4)
    return pl.pallas_call(
        body, name=f"adamw_{tag}", grid_spec=grid_spec,
        out_shape=(jax.ShapeDtypeStruct(w.shape, F32),) * 4,
        compiler_params=_params(("parallel", "parallel", "parallel")),
    )(cidx, mine, theirs, w, m, v)


def _pack(arrays, row_multiple=8):
    parts, offsets, off = [], [], 0
    for a in arrays:
        n = int(np.prod(a.shape))
        padded = -(-n // 1024) * 1024
        flat = a.reshape(-1).astype(F32)
        if padded != n:
            flat = jnp.concatenate([flat, jnp.zeros((padded - n,), F32)])
        parts.append(flat.reshape(-1, 128))
        offsets.append((off // 128, n, a.shape))
        off += padded
    tail = -off % (row_multiple * 128)
    if tail:
        parts.append(jnp.zeros((tail // 128, 128), F32))
    return jnp.concatenate(parts, axis=0), offsets


def _unpack(buf, offsets):
    return [buf[row:row + -(-n // 128)].reshape(-1)[:n].reshape(shape) for row, n, shape in offsets]


def _pad_rows(a, rows):
    return jnp.concatenate([a, jnp.zeros((rows - a.shape[0],) + a.shape[1:], a.dtype)], axis=0)


def kernel(x, c, norm_g, w_ada, b_ada, w_in, w_conv_a, w_pool, pool_scale, w_dw_c, b_dw_c, ln_g_c, ln_b_c, w_pw2_c, b_pw2_c, ln_g_d, ln_b_d, w_s_d, b_s_d, w_out, final_g, loss_target, m_norm_g, m_w_ada, m_b_ada, m_w_in, m_w_conv_a, m_w_pool, m_pool_scale, m_w_dw_c, m_b_dw_c, m_ln_g_c, m_ln_b_c, m_w_pw2_c, m_b_pw2_c, m_ln_g_d, m_ln_b_d, m_w_s_d, m_b_s_d, m_w_out, m_final_g, v_norm_g, v_w_ada, v_b_ada, v_w_in, v_w_conv_a, v_w_pool, v_pool_scale, v_w_dw_c, v_b_dw_c, v_ln_g_c, v_ln_b_c, v_w_pw2_c, v_b_pw2_c, v_ln_g_d, v_ln_b_d, v_w_s_d, v_b_s_d, v_w_out, v_final_g):
    env = dict(locals())
    weights = {n: env[n] for n in WEIGHTS}
    mom_m = {n: env["m_" + n] for n in WEIGHTS}
    mom_v = {n: env["v_" + n] for n in WEIGHTS}
    nl = norm_g.shape[0]
    t = x.shape[1]
    tt, tm = _tiles(t)
    ax, ay, ac = lax.axis_index("x"), lax.axis_index("y"), lax.axis_index("c")
    chip = 2 * ax + ay
    dev = 2 * chip + ac
    cidx = jnp.reshape(ac, (1,)).astype(jnp.int32)
    sidx = jnp.reshape(chip, (1,)).astype(jnp.int32)
    xs, target = x[0], loss_target[0]

    shard_small, shard_off = _pack([c, w_conv_a, w_dw_c, w_pw2_c])
    gathered = _allgather8(shard_small, "gather_small").reshape(N_DEV, -1, 128)
    per_dev = [_unpack(gathered[d], shard_off) for d in range(0, N_DEV, 2)]
    c_all = jnp.concatenate([u[0] for d in range(N_DEV)
                             for u in [_unpack(gathered[d], shard_off[:1])]], axis=0)
    w_conv_full = jnp.concatenate([u[1] for u in per_dev], axis=-1)
    w_dw_full = jnp.concatenate([u[2] for u in per_dev], axis=-1)
    w_pw2_full = jnp.concatenate([u[3] for u in per_dev], axis=1)

    b_ada_s = lax.dynamic_slice_in_dim(b_ada, chip * PIECE, PIECE, axis=1)[:, None, :]
    mod_s = _ada_forward(c_all, w_ada, b_ada_s)
    mod_all = _allgather8(mod_s.reshape(nl * N_DEV, PIECE), "gather_mod").reshape(N_DEV, nl, N_DEV, PIECE)
    mod_rows = lax.dynamic_index_in_dim(mod_all, dev, axis=2, keepdims=False)
    mod = jnp.concatenate([mod_rows[2 * s] for s in range(N_CHIP)], axis=-1)

    w_in_full, w_out_full = _gather_weights(w_in, w_out)

    tril = jnp.tril(jnp.ones((CHUNK, CHUNK), bool))
    ws_masked = jnp.where(tril[None, None], w_s_d, 0.0)
    wbd = jnp.zeros((nl, GW, GW), F32)
    for g in range(4):
        wbd = wbd.at[:, 64 * g:64 * (g + 1), 64 * g:64 * (g + 1)].set(w_pool[:, g])
    v256 = jnp.stack([pool_scale, b_dw_c, ln_g_c, ln_b_c, b_pw2_c, ln_g_d, ln_b_d], axis=1)
    mp = dict(
        wa=_pad_rows(jnp.swapaxes(w_conv_full, 0, 1), 8).swapaxes(0, 1),
        wdw=_pad_rows(jnp.swapaxes(w_dw_full, 0, 1), 32).swapaxes(0, 1),
        v256=_pad_rows(jnp.swapaxes(v256, 0, 1), 16).swapaxes(0, 1),
        wbd=wbd.astype(BF16), wpw=w_pw2_full.astype(BF16), ws=ws_masked.astype(BF16),
        bsd=jnp.repeat(jnp.swapaxes(b_s_d, 1, 2), GW // 4, axis=2),
        wout=w_out_full)
    wst = jnp.swapaxes(ws_masked, 2, 3).astype(BF16)

    def vec_rows(l):
        rows = jnp.stack([mod[l, 0:D], mod[l, D:2 * D], mod[l, 2 * D:3 * D], norm_g[l]], axis=0)
        return _pad_rows(rows, 8)

    xin, zs, ys, convs, vecs = [xs], [], [], [], []
    for l in range(nl):
        vecs.append(vec_rows(l))
        zl, xn, yl, cl = _fwd_layer(xin[l], vecs[l], w_in_full, mp, l, tm)
        zs.append(zl)
        convs.append(cl)
        xin.append(xn)
        ys.append(yl)
    dx, fin_sums = _final(xin[nl], target, _pad_rows(final_g[None, :], 8), tt)

    g_in, g_out, small, dmod = None, None, [None] * nl, [None] * nl
    for l in reversed(range(nl)):
        dz, g_out, gwbd, gwpw, gws, gbs, gwdw, g256, g1024 = _bwd_mix(
            dx, ys[l], convs[l], zs[l], vecs[l], mp, wst, l, tm, gwout_all=g_out)
        dx, g_in, sums = _bwd_in(dz, xin[l], dx, vecs[l], w_in_full, l, tt, gw_all=g_in)
        dmod[l] = jnp.concatenate([sums[0], sums[1], g1024[0]])
        small[l] = dict(
            norm_g=sums[2], w_conv_a=g256[8:8 + CONV_A],
            w_pool=jnp.stack([gwbd[64 * g:64 * (g + 1), 64 * g:64 * (g + 1)] for g in range(4)]),
            pool_scale=g256[0], w_dw_c=gwdw[:CONV_C], b_dw_c=g256[1], ln_g_c=g256[2], ln_b_c=g256[3],
            w_pw2_c=gwpw, b_pw2_c=g256[4], ln_g_d=g256[5], ln_b_d=g256[6], w_s_d=gws, b_s_d=gbs[:, :4].T)
    grad_x = dx[None]

    rs_in = _reduce_scatter_start(g_in, cidx, "in")
    rs_out = _reduce_scatter_start(g_out, cidx, "out")
    token = rs_in[-1][0, 0] + rs_out[-1][0, 0]

    dmod_all, dmod_sum = _allgather8(
        (jnp.stack(dmod) + token).reshape(nl * 3 * D // 128, 128), "gather_dmod", reduce=True)
    dmod_all = dmod_all.reshape(N_DEV, nl, 3 * D)
    grad_b_ada = dmod_sum.reshape(nl, 3 * D)
    dmod_s = jnp.swapaxes(lax.dynamic_slice_in_dim(dmod_all, chip * PIECE, PIECE, axis=2), 0, 1)
    grad_w_ada = _ada_backward(c_all, dmod_s)

    small_names = [n for n in WEIGHTS if n not in BIG and n not in ('b_ada', 'final_g')]
    stacked = [jnp.stack([small[l][n] for l in range(nl)]) for n in small_names]
    small_buf, small_off = _pack(stacked + [fin_sums[0], fin_sums[2, 0:1]], 8 * N_DEV)
    reduced = _unpack(_allreduce8(small_buf, "allreduce_small"), small_off)
    grads = dict(zip(small_names, reduced[:len(small_names)]))
    grads['final_g'] = reduced[-2]
    loss = reduced[-1][0]
    grads['b_ada'] = grad_b_ada
    grads['w_ada'] = grad_w_ada
    grads['w_conv_a'] = lax.dynamic_slice_in_dim(grads['w_conv_a'], chip * 64, 64, axis=2)
    grads['w_dw_c'] = lax.dynamic_slice_in_dim(grads['w_dw_c'], chip * 64, 64, axis=2)
    grads['w_pw2_c'] = lax.dynamic_slice_in_dim(grads['w_pw2_c'], chip * 64, 64, axis=1)

    delta, new_m, new_v = {}, {}, {}
    shape = w_ada.shape
    as2d = lambda a: a.reshape(-1, shape[-1])
    d2, m2, v2 = _adamw(as2d(w_ada), as2d(grads['w_ada']), as2d(m_w_ada), as2d(v_w_ada), 'w_ada')
    delta['w_ada'], new_m['w_ada'], new_v['w_ada'] = d2.reshape(shape), m2.reshape(shape), v2.reshape(shape)
    rest = [n for n in WEIGHTS if n not in BIG]
    ds, nms, nvs = _adamw_many([weights[n] for n in rest], [grads[n] for n in rest],
                               [mom_m[n] for n in rest], [mom_v[n] for n in rest])
    for n, dn, mn, vn in zip(rest, ds, nms, nvs):
        delta[n], new_m[n], new_v[n] = dn, mn, vn
    halves = {'w_in': _reduce_scatter_finish(rs_in, nvs[-1], sidx, "in"),
              'w_out': _reduce_scatter_finish(rs_out, nvs[-1], sidx, "out")}
    for n in ('w_in', 'w_out'):
        grads[n], delta[n], new_m[n], new_v[n] = _adamw_halves(
            weights[n], *halves[n], mom_m[n], mom_v[n], cidx, n)

    return (loss, grad_x, *[grads[n] for n in WEIGHTS], *[delta[n] for n in WEIGHTS],
            *[new_m[n] for n in WEIGHTS], *[new_v[n] for n in WEIGHTS])
```

```python
import functools
import math

import jax
import jax.numpy as jnp
import numpy as np
from jax import lax
from jax.experimental import pallas as pl
from jax.experimental.pallas import tpu as pltpu

F32 = jnp.float32
BF16 = jnp.bfloat16
MESH = pl.DeviceIdType.MESH

D = 1024
DIN = 3072
GW = 256
PIECE = 768
N_CHIP = 4
N_DEV = 8
HALO = 32
CONV_A = 3
CONV_C = 31
CHUNK = 128
CONV_ROWS = 64
ROW_BLOCK = 32
MIX_FWD_POINTS = 18
EPS = 1e-6
VMEM_LIMIT = 56 * 1024 * 1024

ADAM_LR, ADAM_B1, ADAM_B2, ADAM_EPS, ADAM_WD, ADAM_STEP = 0.001, 0.9, 0.999, 1e-08, 0.01, 10
GELU_K = math.sqrt(2.0 / math.pi)
GELU_C = 0.044715

WEIGHTS = ['norm_g', 'w_ada', 'b_ada', 'w_in', 'w_conv_a', 'w_pool', 'pool_scale', 'w_dw_c', 'b_dw_c', 'ln_g_c',
           'ln_b_c', 'w_pw2_c', 'b_pw2_c', 'ln_g_d', 'ln_b_d', 'w_s_d', 'b_s_d', 'w_out', 'final_g']
BIG = ('w_ada', 'w_in', 'w_out')


def _tiles(t):
    if t % 512 == 0 and t >= 2048:
        return 512, 256
    return 128, 128


def _params(sem, limit=VMEM_LIMIT):
    return pltpu.CompilerParams(dimension_semantics=sem, vmem_limit_bytes=limit)


def _sig(x):
    return jax.nn.sigmoid(x)


def _gelu(x):
    th = jnp.tanh(GELU_K * (x + GELU_C * x * x * x))
    return 0.5 * x * (1.0 + th), th


def _gelu_grad(x, th):
    return 0.5 * (1.0 + th) + 0.5 * x * (1.0 - th * th) * GELU_K * (1.0 + 3.0 * GELU_C * x * x)


def _dot(a, b):
    return jnp.dot(a, b, preferred_element_type=F32)


def _dot_nt(a, b):
    return lax.dot_general(a, b, (((1,), (1,)), ((), ())), preferred_element_type=F32)


def _dot_tn(a, b):
    return lax.dot_general(a, b, (((0,), (0,)), ((), ())), preferred_element_type=F32)


def _rowsum(x):
    return jnp.sum(x, axis=0, keepdims=True)


def _allgather8(v, name, reduce=False):
    m_per, n = v.shape

    def body(x_ref, out_ref, *rest):
        if reduce:
            sum_ref, send_sems, recv_sems, local_sem = rest
        else:
            send_sems, recv_sems, local_sem = rest
        x, y, c = lax.axis_index("x"), lax.axis_index("y"), lax.axis_index("c")
        me, sibling = (x, y, c), (x, y, 1 - c)
        chips = [(1 - x, y), (x, 1 - y), (1 - x, 1 - y)]

        def rows(px, py, pc):
            return out_ref.at[pl.ds((4 * px + 2 * py + pc) * m_per, m_per), :]

        def copy(k, block, to, src=None):
            return pltpu.make_async_remote_copy(
                src_ref=rows(*block) if src is None else src, dst_ref=rows(*block),
                send_sem=send_sems.at[k], recv_sem=recv_sems.at[k], device_id=to, device_id_type=MESH)

        mine = pltpu.make_async_copy(x_ref, rows(*me), local_sem)
        mine.start()
        first = [copy(0, me, sibling, src=x_ref)]
        first += [copy(1 + j, me, (*chip, c), src=x_ref) for j, chip in enumerate(chips)]
        for cp in first:
            cp.start()
        passed = [copy(4 + j, (*chip, c), sibling) for j, chip in enumerate(chips)]
        for j, chip in enumerate(chips):
            copy(1 + j, (*chip, c), me).wait_recv()
            passed[j].start()
        copy(0, sibling, me).wait_recv()
        for j, chip in enumerate(chips):
            copy(4 + j, (*chip, 1 - c), me).wait_recv()
        for cp in first + passed:
            cp.wait_send()
        mine.wait()
        if reduce:
            acc = out_ref[0:m_per, :]
            for d in range(1, N_DEV):
                acc = acc + out_ref[d * m_per:(d + 1) * m_per, :]
            sum_ref[...] = acc

    out_shape = [jax.ShapeDtypeStruct((N_DEV * m_per, n), v.dtype)]
    out_specs = [pl.BlockSpec(memory_space=pltpu.VMEM)]
    if reduce:
        out_shape.append(jax.ShapeDtypeStruct((m_per, n), v.dtype))
        out_specs.append(pl.BlockSpec(memory_space=pltpu.VMEM))
    res = pl.pallas_call(
        body, name=name, out_shape=tuple(out_shape),
        in_specs=[pl.BlockSpec(memory_space=pltpu.VMEM)], out_specs=tuple(out_specs),
        scratch_shapes=[pltpu.SemaphoreType.DMA((7,)), pltpu.SemaphoreType.DMA((7,)), pltpu.SemaphoreType.DMA],
        compiler_params=pltpu.CompilerParams(vmem_limit_bytes=VMEM_LIMIT),
    )(v)
    return res if reduce else res[0]


def _allreduce8(v, name):
    rows, n = v.shape
    rc = rows // N_DEV

    def body(x_ref, sum_ref, stage, send1, recv1, send2, recv2):
        x, y, c = lax.axis_index("x"), lax.axis_index("y"), lax.axis_index("c")
        me = 4 * x + 2 * y + c
        peers = []
        for k in range(1, N_DEV):
            kx, ky, kc = (k >> 2) & 1, (k >> 1) & 1, k & 1
            peers.append((1 - x if kx else x, 1 - y if ky else y, 1 - c if kc else c))

        def chunk(ref, d):
            return ref.at[pl.ds(d * rc, rc), :]

        scatter, gather = [], []
        for k, (px, py, pc) in enumerate(peers):
            scatter.append(pltpu.make_async_remote_copy(
                src_ref=chunk(x_ref, 4 * px + 2 * py + pc), dst_ref=stage.at[me], send_sem=send1.at[k],
                recv_sem=recv1.at[k], device_id=(px, py, pc), device_id_type=MESH))
        for cp in scatter:
            cp.start()
        stage[me] = x_ref[pl.ds(me * rc, rc), :]
        for k, (px, py, pc) in enumerate(peers):
            pltpu.make_async_remote_copy(
                src_ref=chunk(x_ref, me), dst_ref=stage.at[4 * px + 2 * py + pc], send_sem=send1.at[k],
                recv_sem=recv1.at[k], device_id=(px, py, pc), device_id_type=MESH).wait_recv()
        total = stage[0]
        for d in range(1, N_DEV):
            total = total + stage[d]
        sum_ref[pl.ds(me * rc, rc), :] = total
        for k, (px, py, pc) in enumerate(peers):
            gather.append(pltpu.make_async_remote_copy(
                src_ref=chunk(sum_ref, me), dst_ref=chunk(sum_ref, me), send_sem=send2.at[k],
                recv_sem=recv2.at[k], device_id=(px, py, pc), device_id_type=MESH))
        for cp in gather:
            cp.start()
        for k, (px, py, pc) in enumerate(peers):
            theirs = 4 * px + 2 * py + pc
            pltpu.make_async_remote_copy(
                src_ref=chunk(sum_ref, theirs), dst_ref=chunk(sum_ref, theirs), send_sem=send2.at[k],
                recv_sem=recv2.at[k], device_id=(px, py, pc), device_id_type=MESH).wait_recv()
        for cp in scatter + gather:
            cp.wait_send()

    return pl.pallas_call(
        body, name=name, out_shape=jax.ShapeDtypeStruct((rows, n), v.dtype),
        in_specs=[pl.BlockSpec(memory_space=pltpu.VMEM)], out_specs=pl.BlockSpec(memory_space=pltpu.VMEM),
        scratch_shapes=[pltpu.VMEM((N_DEV, rc, n), v.dtype)] + [pltpu.SemaphoreType.DMA((N_DEV - 1,))] * 4,
        compiler_params=pltpu.CompilerParams(vmem_limit_bytes=VMEM_LIMIT),
    )(v)


def _gather_weights(w_in_s, w_out_s):
    nl = w_in_s.shape[0]
    shapes = ((nl, N_CHIP) + w_in_s.shape[1:], (nl, N_CHIP) + w_out_s.shape[1:])
    n_sem = 2 * 3 * 2

    def body(win_ref, wout_ref, win_o, wout_o, send_sems, recv_sems):
        x, y, c = lax.axis_index("x"), lax.axis_index("y"), lax.axis_index("c")
        s = 2 * x + y
        sibling = (x, y, 1 - c)
        chips = [(1 - x, y), (x, 1 - y), (1 - x, 1 - y)]
        outs = (win_o, wout_o)
        for src, dst in ((win_ref, win_o), (wout_ref, wout_o)):
            rows = src.shape[1]
            step = min(rows, 256)
            for l in range(nl):
                for r in range(0, rows, step):
                    dst[l, s, r:r + step, :] = src[l, r:r + step, :].astype(BF16)

        def copy(k, ref, piece, half, to):
            r2 = ref.shape[2] // 2
            rows = ref.at[:, piece, pl.ds(half * r2, r2), :]
            return pltpu.make_async_remote_copy(
                src_ref=rows, dst_ref=rows, send_sem=send_sems.at[k], recv_sem=recv_sems.at[k],
                device_id=to, device_id_type=MESH)

        sends = [copy(2 * j + a, ref, s, c, (px, py, c)) for j, (px, py) in enumerate(chips)
                 for a, ref in enumerate(outs)]
        for cp in sends:
            cp.start()
        passed = []
        for j, (px, py) in enumerate(chips):
            for a, ref in enumerate(outs):
                copy(2 * j + a, ref, 2 * px + py, c, (px, py, c)).wait_recv()
                passed.append(copy(6 + 2 * j + a, ref, 2 * px + py, c, sibling))
                passed[-1].start()
        for j, (px, py) in enumerate(chips):
            for a, ref in enumerate(outs):
                copy(6 + 2 * j + a, ref, 2 * px + py, 1 - c, sibling).wait_recv()
        for cp in sends + passed:
            cp.wait_send()

    return pl.pallas_call(
        body, name="gather_weights",
        out_shape=tuple(jax.ShapeDtypeStruct(s, BF16) for s in shapes),
        in_specs=[pl.BlockSpec(memory_space=pltpu.VMEM)] * 2,
        out_specs=tuple(pl.BlockSpec(memory_space=pltpu.VMEM) for _ in shapes),
        scratch_shapes=[pltpu.SemaphoreType.DMA((n_sem,)), pltpu.SemaphoreType.DMA((n_sem,))],
        compiler_params=pltpu.CompilerParams(vmem_limit_bytes=VMEM_LIMIT),
    )(w_in_s, w_out_s)


def _cast_bf16(arrays):
    n = len(arrays)

    def body(*refs):
        for src, dst in zip(refs[:n], refs[n:]):
            rows = src.shape[-2]
            step = min(rows, 256)
            for l in range(src.shape[0]):
                for r in range(0, rows, step):
                    dst[l, r:r + step, :] = src[l, r:r + step, :].astype(BF16)

    return pl.pallas_call(
        body, name="cast_weights", out_shape=tuple(jax.ShapeDtypeStruct(a.shape, BF16) for a in arrays),
        compiler_params=pltpu.CompilerParams(vmem_limit_bytes=VMEM_LIMIT),
    )(*arrays)


def _shard_copies(a_ref, b_ref, la_ref, lb_ref, send_sems, recv_sems):
    x, y, c = lax.axis_index("x"), lax.axis_index("y"), lax.axis_index("c")
    s = 2 * x + y
    sends, recvs = [], []
    for j, (px, py) in enumerate([(1 - x, y), (x, 1 - y), (1 - x, 1 - y)]):
        for a, (src, land) in enumerate(((a_ref, la_ref), (b_ref, lb_ref))):
            k = 2 * j + a
            for slot, into in ((s, sends), (2 * px + py, recvs)):
                into.append(pltpu.make_async_remote_copy(
                    src_ref=src.at[0], dst_ref=land.at[0, slot], send_sem=send_sems.at[k], recv_sem=recv_sems.at[k],
                    device_id=(px, py, c), device_id_type=MESH))
    return sends, recvs


def _gather_shards_start(a, b):
    lands = ((1, N_CHIP) + a.shape[1:], (1, N_CHIP) + b.shape[1:])
    hbm = pl.BlockSpec(memory_space=pltpu.HBM)
    sem = pl.BlockSpec(memory_space=pltpu.SEMAPHORE)

    def body(a_ref, b_ref, la_ref, lb_ref, send_sems, recv_sems, a_thru, b_thru, la_thru, lb_thru, token):
        for cp in _shard_copies(a_ref, b_ref, la_ref, lb_ref, send_sems, recv_sems)[0]:
            cp.start()
        token[...] = jnp.zeros_like(token)

    return pl.pallas_call(
        body, name="gather_weights_start",
        out_shape=(pltpu.SemaphoreType.DMA((6,)), pltpu.SemaphoreType.DMA((6,)), pltpu.HBM(a.shape, BF16),
                   pltpu.HBM(b.shape, BF16), pltpu.HBM(lands[0], BF16), pltpu.HBM(lands[1], BF16),
                   jax.ShapeDtypeStruct((8, 128), F32)),
        in_specs=(hbm,) * 4, out_specs=(sem, sem, hbm, hbm, hbm, hbm, pl.BlockSpec(memory_space=pltpu.VMEM)),
        input_output_aliases={0: 2, 1: 3, 2: 4, 3: 5},
        compiler_params=pltpu.CompilerParams(has_side_effects=pltpu.SideEffectType.DATAFLOW_SIDE_EFFECTING),
    )(pltpu.with_memory_space_constraint(a, pltpu.HBM), pltpu.with_memory_space_constraint(b, pltpu.HBM),
      pltpu.with_memory_space_constraint(lax.empty(lands[0], BF16), pltpu.HBM),
      pltpu.with_memory_space_constraint(lax.empty(lands[1], BF16), pltpu.HBM))


def _gather_shards_wait(send_sems, recv_sems, a, b, la, lb, after):
    hbm = pl.BlockSpec(memory_space=pltpu.HBM)
    sem = pl.BlockSpec(memory_space=pltpu.SEMAPHORE)

    def body(a_ref, b_ref, la_ref, lb_ref, send_sems, recv_sems, after_ref, a_dead, b_dead, la_out, lb_out):
        sends, recvs = _shard_copies(a_ref, b_ref, la_ref, lb_ref, send_sems, recv_sems)
        for cp in sends:
            cp.wait_send()
        for cp in recvs:
            cp.wait_recv()

    out = pl.pallas_call(
        body, name="gather_weights_wait",
        out_shape=tuple(pltpu.HBM(v.shape, v.dtype) for v in (a, b, la, lb)),
        in_specs=(hbm, hbm, hbm, hbm, sem, sem, pl.BlockSpec(memory_space=pl.ANY)), out_specs=(hbm,) * 4,
        input_output_aliases={0: 0, 1: 1, 2: 2, 3: 3},
        compiler_params=pltpu.CompilerParams(has_side_effects=pltpu.SideEffectType.DATAFLOW_SIDE_EFFECTING),
    )(a, b, la, lb, send_sems, recv_sems, after)
    return out


def _send_half_to_sibling(g, tag):
    nl, nc, r, cdim = g.shape
    half = r // 2

    def body(g_ref, recv_ref, send_sem, recv_sem):
        x, y, c = lax.axis_index("x"), lax.axis_index("y"), lax.axis_index("c")
        cp = pltpu.make_async_remote_copy(
            src_ref=g_ref.at[:, :, pl.ds((1 - c) * half, half), :], dst_ref=recv_ref,
            send_sem=send_sem, recv_sem=recv_sem, device_id=(x, y, 1 - c), device_id_type=MESH)
        cp.start()
        cp.wait()

    return pl.pallas_call(
        body, name=f"rs_sibling_{tag}",
        out_shape=jax.ShapeDtypeStruct((nl, nc, half, cdim), g.dtype),
        in_specs=[pl.BlockSpec(memory_space=pl.ANY)], out_specs=pl.BlockSpec(memory_space=pl.ANY),
        scratch_shapes=[pltpu.SemaphoreType.DMA, pltpu.SemaphoreType.DMA],
    )(g)


def _add_half(g, recv, cidx, tag):
    nl, nc, r, cdim = g.shape
    half = r // 2
    rt = min(half, 128)
    nrt = half // rt

    def body(c_ref, a_ref, b_ref, o_ref, ob_ref):
        c = c_ref[0]
        first = jnp.where(c == 0, a_ref[...], b_ref[...])
        second = jnp.where(c == 0, b_ref[...], a_ref[...])
        total = first + second
        o_ref[...] = total
        ob_ref[...] = total.astype(BF16)

    out_spec = pl.BlockSpec((None, None, rt, cdim), lambda l, j, i, c: (l, j, i, 0))
    grid_spec = pltpu.PrefetchScalarGridSpec(
        num_scalar_prefetch=1, grid=(nl, nc, nrt),
        in_specs=[pl.BlockSpec((None, None, rt, cdim), lambda l, j, i, c: (l, j, c[0] * nrt + i, 0)),
                  pl.BlockSpec((None, None, rt, cdim), lambda l, j, i, c: (l, j, i, 0))],
        out_specs=(out_spec, out_spec))
    return pl.pallas_call(
        body, name=f"rs_add_sibling_{tag}", grid_spec=grid_spec,
        out_shape=(jax.ShapeDtypeStruct((nl, nc, half, cdim), g.dtype),
                   jax.ShapeDtypeStruct((nl, nc, half, cdim), BF16)),
        compiler_params=_params(("parallel", "parallel", "parallel")),
    )(cidx, g, recv)


def _exchange_chips(h, tag):
    nl, nc, r2, cdim = h.shape

    def body(h_ref, recv_ref, send_sems, recv_sems):
        x, y, c = lax.axis_index("x"), lax.axis_index("y"), lax.axis_index("c")
        chips = [(1 - x, y), (x, 1 - y), (1 - x, 1 - y)]
        cps = []
        for k, (px, py) in enumerate(chips):
            cps.append(pltpu.make_async_remote_copy(
                src_ref=h_ref.at[:, 2 * px + py], dst_ref=recv_ref.at[k], send_sem=send_sems.at[k],
                recv_sem=recv_sems.at[k], device_id=(px, py, c), device_id_type=MESH))
        for cp in cps:
            cp.start()
        for cp in cps:
            cp.wait()

    return pl.pallas_call(
        body, name=f"rs_chips_{tag}",
        out_shape=jax.ShapeDtypeStruct((3, nl, r2, cdim), h.dtype),
        in_specs=[pl.BlockSpec(memory_space=pl.ANY)], out_specs=pl.BlockSpec(memory_space=pl.ANY),
        scratch_shapes=[pltpu.SemaphoreType.DMA((3,)), pltpu.SemaphoreType.DMA((3,))],
    )(h)


def _chip_copies(h_ref, land_ref, send_sems, recv_sems):
    x, y, c = lax.axis_index("x"), lax.axis_index("y"), lax.axis_index("c")
    chips = [(1 - x, y), (x, 1 - y), (1 - x, 1 - y)]
    return [pltpu.make_async_remote_copy(
        src_ref=h_ref.at[:, 2 * px + py], dst_ref=land_ref.at[k], send_sem=send_sems.at[k],
        recv_sem=recv_sems.at[k], device_id=(px, py, c), device_id_type=MESH) for k, (px, py) in enumerate(chips)]


def _exchange_chips_start(h, tag):
    nl, nc, r2, cdim = h.shape
    land_shape = (3, nl, r2, cdim)
    hbm = pl.BlockSpec(memory_space=pltpu.HBM)
    sem = pl.BlockSpec(memory_space=pltpu.SEMAPHORE)

    def body(h_ref, land_ref, send_sems, recv_sems, h_thru, land_thru, token):
        for cp in _chip_copies(h_ref, land_ref, send_sems, recv_sems):
            cp.start()
        token[...] = jnp.zeros_like(token)

    return pl.pallas_call(
        body, name=f"rs_chips_start_{tag}",
        out_shape=(pltpu.SemaphoreType.DMA((3,)), pltpu.SemaphoreType.DMA((3,)), pltpu.HBM(h.shape, h.dtype),
                   pltpu.HBM(land_shape, h.dtype), jax.ShapeDtypeStruct((8, 128), F32)),
        in_specs=(hbm, hbm), out_specs=(sem, sem, hbm, hbm, pl.BlockSpec(memory_space=pltpu.VMEM)),
        input_output_aliases={0: 2, 1: 3},
        compiler_params=pltpu.CompilerParams(has_side_effects=pltpu.SideEffectType.DATAFLOW_SIDE_EFFECTING),
    )(pltpu.with_memory_space_constraint(h, pltpu.HBM),
      pltpu.with_memory_space_constraint(lax.empty(land_shape, h.dtype), pltpu.HBM))


def _exchange_chips_wait(send_sems, recv_sems, h_thru, land_thru, after, tag):
    hbm = pl.BlockSpec(memory_space=pltpu.HBM)
    sem = pl.BlockSpec(memory_space=pltpu.SEMAPHORE)

    def body(h_ref, land_ref, send_sems, recv_sems, after_ref, h_dead, got_ref):
        for cp in _chip_copies(h_ref, land_ref, send_sems, recv_sems):
            cp.wait_send()
            cp.wait_recv()

    return pl.pallas_call(
        body, name=f"rs_chips_wait_{tag}",
        out_shape=(pltpu.HBM(h_thru.shape, h_thru.dtype), pltpu.HBM(land_thru.shape, land_thru.dtype)),
        in_specs=(hbm, hbm, sem, sem, pl.BlockSpec(memory_space=pl.ANY)), out_specs=(hbm, hbm),
        input_output_aliases={0: 0, 1: 1},
        compiler_params=pltpu.CompilerParams(has_side_effects=pltpu.SideEffectType.DATAFLOW_SIDE_EFFECTING),
    )(h_thru, land_thru, send_sems, recv_sems, after)[1]


def _add_chips(h, recv, sidx, tag):
    nl, nc, r2, cdim = h.shape
    rt = min(r2, 128)

    def body(s_ref, own_ref, r_ref, o_ref):
        s = s_ref[0]
        got = [r_ref[k].astype(F32) for k in range(3)]
        total = None
        for chip in range(N_CHIP):
            term = jnp.where(s == chip, own_ref[...],
                             jnp.where((s ^ 2) == chip, got[0], jnp.where((s ^ 1) == chip, got[1], got[2])))
            total = term if total is None else total + term
        o_ref[...] = total

    grid_spec = pltpu.PrefetchScalarGridSpec(
        num_scalar_prefetch=1, grid=(nl, r2 // rt),
        in_specs=[pl.BlockSpec((None, None, rt, cdim), lambda l, i, s: (l, s[0], i, 0)),
                  pl.BlockSpec((3, None, rt, cdim), lambda l, i, s: (0, l, i, 0))],
        out_specs=pl.BlockSpec((None, rt, cdim), lambda l, i, s: (l, i, 0)))
    return pl.pallas_call(
        body, name=f"rs_add_chips_{tag}", grid_spec=grid_spec,
        out_shape=jax.ShapeDtypeStruct((nl, r2, cdim), h.dtype),
        compiler_params=_params(("parallel", "parallel")),
    )(sidx, h, recv)


def _swap_halves(red, tag):
    def body(red_ref, out_ref, send_sem, recv_sem):
        x, y, c = lax.axis_index("x"), lax.axis_index("y"), lax.axis_index("c")
        cp = pltpu.make_async_remote_copy(
            src_ref=red_ref, dst_ref=out_ref, send_sem=send_sem, recv_sem=recv_sem,
            device_id=(x, y, 1 - c), device_id_type=MESH)
        cp.start()
        cp.wait()

    return pl.pallas_call(
        body, name=f"rs_swap_{tag}", out_shape=jax.ShapeDtypeStruct(red.shape, red.dtype),
        in_specs=[pl.BlockSpec(memory_space=pl.ANY)], out_specs=pl.BlockSpec(memory_space=pl.ANY),
        scratch_shapes=[pltpu.SemaphoreType.DMA, pltpu.SemaphoreType.DMA],
    )(red)


def _reduce_scatter(g, cidx, sidx, tag):
    started = _reduce_scatter_start(g, cidx, tag)
    return _reduce_scatter_finish(started, started[-1], sidx, tag)


def _reduce_scatter_start(g, cidx, tag):
    recv = _send_half_to_sibling(g, tag)
    chip_sum, chip_sum_bf16 = _add_half(g, recv, cidx, tag)
    return (chip_sum,) + tuple(_exchange_chips_start(chip_sum_bf16, tag))


def _reduce_scatter_finish(started, after, sidx, tag):
    chip_sum, send_sems, recv_sems, h_thru, land_thru, _ = started
    got = _exchange_chips_wait(send_sems, recv_sems, h_thru, land_thru, after, tag)
    red = _add_chips(chip_sum, got, sidx, tag)
    return red, _swap_halves(red, tag)


def _ada_forward(c_all, w_ada_s, b_s):
    nl = w_ada_s.shape[0]

    def body(c_ref, w_ref, b_ref, o_ref):
        cv = c_ref[...]
        ca = (cv * _sig(cv)).astype(BF16)
        for l in range(nl):
            o_ref[l] = _dot(ca, w_ref[l].astype(BF16)) + b_ref[l]

    return pl.pallas_call(
        body, name="ada_forward", out_shape=jax.ShapeDtypeStruct((nl, N_DEV, PIECE), F32),
        compiler_params=pltpu.CompilerParams(vmem_limit_bytes=VMEM_LIMIT),
    )(c_all, w_ada_s, b_s)


def _ada_backward(c_all, dmod_s):
    nl = dmod_s.shape[0]

    def body(c_ref, d_ref, o_ref):
        cv = c_ref[...]
        ca = (cv * _sig(cv)).astype(BF16)
        for l in range(nl):
            o_ref[l] = _dot_tn(ca, d_ref[l].astype(BF16))

    return pl.pallas_call(
        body, name="ada_backward", out_shape=jax.ShapeDtypeStruct((nl, D, PIECE), F32),
        compiler_params=pltpu.CompilerParams(vmem_limit_bytes=VMEM_LIMIT),
    )(c_all, dmod_s)


def _normed(xv, vecs):
    r = lax.rsqrt(jnp.mean(xv * xv, axis=-1, keepdims=True) + EPS)
    xhat = xv * r
    hn = xhat * vecs[3:4, :]
    return r, xhat, hn


def _bwd_in(dz, x, dxo, vecs, w, l, nl, tt, gw_all=None):
    t = x.shape[0]
    nt = t // tt

    def body(dz_ref, x_ref, dxo_ref, v_ref, w_any, *rest):
        dx_ref, gw_any, sums_ref, w_vmem, gw_acc, sem = rest[-6:]
        i = pl.program_id(0)

        @pl.when(i == 0)
        def _():
            cp = pltpu.make_async_copy(w_any.at[0], w_vmem, sem)
            cp.start()
            cp.wait()
            gw_acc[...] = jnp.zeros_like(gw_acc)
            sums_ref[...] = jnp.zeros_like(sums_ref)

        vecs_v = v_ref[...]
        r, xhat, hn = _normed(x_ref[...], vecs_v)
        one_scale = 1.0 + vecs_v[1:2, :]
        hb = (hn * one_scale + vecs_v[0:1, :]).astype(BF16)
        dh = None
        for j in range(N_CHIP):
            dzj = dz_ref[:, PIECE * j:PIECE * (j + 1)]
            part = _dot_nt(dzj, w_vmem[j])
            dh = part if dh is None else dh + part
            gw_acc[j] += _dot_tn(hb, dzj)
        sums_ref[0:1, :] += _rowsum(dh)
        sums_ref[1:2, :] += _rowsum(dh * hn)
        sums_ref[2:3, :] += _rowsum(dh * one_scale * xhat)
        dxh = dh * (vecs_v[3:4, :] * one_scale)
        dx_ref[...] = dxo_ref[...] + r * (dxh - xhat * jnp.mean(dxh * xhat, axis=-1, keepdims=True))

        @pl.when(i == nt - 1)
        def _():
            cp = pltpu.make_async_copy(gw_acc, gw_any.at[l], sem)
            cp.start()
            cp.wait()

    carried = [] if gw_all is None else [gw_all]
    return pl.pallas_call(
        body, name=f"bwd_in_{l}", grid=(nt,),
        in_specs=[pl.BlockSpec((tt, DIN), lambda i: (i, 0)), pl.BlockSpec((tt, D), lambda i: (i, 0)),
                  pl.BlockSpec((tt, D), lambda i: (i, 0)), pl.BlockSpec((8, D), lambda i: (0, 0)),
                  pl.BlockSpec(memory_space=pl.ANY)] + [pl.BlockSpec(memory_space=pl.ANY)] * len(carried),
        out_specs=(pl.BlockSpec((tt, D), lambda i: (i, 0)), pl.BlockSpec(memory_space=pl.ANY),
                   pl.BlockSpec((8, D), lambda i: (0, 0))),
        out_shape=(jax.ShapeDtypeStruct((t, D), F32), jax.ShapeDtypeStruct((nl, N_CHIP, D, PIECE), F32),
                   jax.ShapeDtypeStruct((8, D), F32)),
        scratch_shapes=[pltpu.VMEM((N_CHIP, D, PIECE), BF16), pltpu.VMEM((N_CHIP, D, PIECE), F32),
                        pltpu.SemaphoreType.DMA],
        input_output_aliases={5: 1} if carried else {},
        compiler_params=_params(("arbitrary",)),
    )(dz, x, dxo, vecs, w, *carried)


def _col(ref, k):
    return ref[:, GW * k:GW * (k + 1)]


def _lane_group():
    return lax.broadcasted_iota(jnp.int32, (1, GW), 1) // (GW // 4)


def _pool_window(group):
    return jnp.where(group == 0, 2.0, jnp.where(group == 1, 4.0, jnp.where(group == 2, 8.0, 16.0)))


def _by_group(group, a, b, c, d):
    return jnp.where(group == 0, a, jnp.where(group == 1, b, jnp.where(group == 2, c, d)))


def _layer_norm(x, g, b):
    mu = jnp.mean(x, axis=-1, keepdims=True)
    xc = x - mu
    rstd = lax.rsqrt(jnp.mean(xc * xc, axis=-1, keepdims=True) + EPS)
    xh = xc * rstd
    return xh * g + b, xh, rstd


def _layer_norm_grad(d_out, xh, rstd, g):
    dxh = d_out * g
    return rstd * (dxh - jnp.mean(dxh, axis=-1, keepdims=True) - xh * jnp.mean(dxh * xh, axis=-1, keepdims=True))


def _shifted_copies(ext_ref, sh_ref, n):
    for b in range(1, 8):
        sh_ref[b, 0:n - 8, :] = ext_ref[b:b + n - 8, :]


def _rows_at(ext_ref, sh_ref, start, rows):
    b = start % 8
    if b == 0 or sh_ref is None:
        return ext_ref[start:start + rows, :]
    return sh_ref[b, start - b:start - b + rows, :]


class _Beside:
    def __init__(self, thunks=(), points=1):
        self.thunks, self.points, self.seen, self.done = list(thunks), max(points, 1), 0, 0

    def _run_to(self, due):
        while self.done < min(due, len(self.thunks)):
            self.thunks[self.done]()
            self.done += 1

    def here(self):
        self.seen += 1
        self._run_to(-(-self.seen * len(self.thunks) // self.points))

    def rest(self):
        self._run_to(len(self.thunks))


def _conv_taps(ext_ref, w_ref, n_taps, first_row, tm, out_ref, flip=False, sh_ref=None, beside=None):
    for rb in range(0, tm, CONV_ROWS):
        acc = None
        for k in range(n_taps):
            wk = n_taps - 1 - k if flip else k
            term = w_ref[wk:wk + 1, :] * _rows_at(ext_ref, sh_ref, first_row + rb + k, CONV_ROWS)
            acc = term if acc is None else acc + term
        out_ref[rb:rb + CONV_ROWS, :] = acc
        if beside is not None:
            beside.here()


def _mix_forward(zc, zp, first, row0, tm, p, s, after, beside=None, conv_c=None):
    f = {}
    beside = beside or _Beside()
    keep = jnp.where(first, 0.0, 1.0)
    group = _lane_group()
    beside.here()
    a_b, a_c, a_x, a_g = _col(zc, 0), _col(zc, 1), _col(zc, 2), _col(zc, 3)
    s['ua'][0:HALO, :] = _col(zp, 1) * _col(zp, 2) * keep
    s['ua'][HALO:HALO + tm, :] = a_c * a_x
    _conv_taps(s['ua'], p['wa'], CONV_A, HALO - (CONV_A - 1), tm, s['cv'], beside=beside)
    cv = s['cv'][...]
    sg_a = _sig(a_g)
    f = dict(a_b=a_b, a_c=a_c, a_x=a_x, a_g=a_g, cv=cv, sg_a=sg_a)
    after(0, f, (a_b * cv * (a_g * sg_a)).astype(BF16))
    beside.here()
    b_p, b_g = _col(zc, 4), _col(zc, 5)
    s['p0'][0:HALO, :] = _col(zp, 4) * keep
    s['p0'][HALO:HALO + tm, :] = b_p
    n = HALO + tm
    s['p1'][8:n, :] = s['p0'][8:n, :] + s['p0'][7:n - 1, :]
    w2 = s['p1'][HALO:n, :]
    beside.here()
    s['p0'][16:n, :] = s['p1'][16:n, :] + s['p1'][14:n - 2, :]
    w4 = s['p0'][HALO:n, :]
    beside.here()
    s['p1'][24:n, :] = s['p0'][24:n, :] + s['p0'][20:n - 4, :]
    w8 = s['p1'][HALO:n, :]
    w16 = w8 + s['p1'][HALO - 8:n - 8, :]
    tpos = (row0 + lax.broadcasted_iota(jnp.int32, (tm, 1), 0) + 1).astype(F32)
    inv_cnt = 1.0 / jnp.minimum(tpos, _pool_window(group))
    pooled = (_by_group(group, w2, w4, w8, w16) * inv_cnt - b_p).astype(BF16)
    beside.here()
    q = _dot(pooled, p['wbd'][...])
    sg_b = _sig(b_g)
    f = dict(b_g=b_g, pooled=pooled, q=q, sg_b=sg_b, inv_cnt=inv_cnt)
    after(1, f, (q * p['v256'][0:1, :] * (b_g * sg_b)).astype(BF16))
    beside.here()
    c_a, c_gl, c_g = _col(zc, 6), _col(zc, 7), _col(zc, 8)
    sg_gl = _sig(c_gl)
    zp_gl = _col(zp, 7)
    s['hc'][0:HALO, :] = _col(zp, 6) * _sig(zp_gl) * keep
    s['hc'][HALO:HALO + tm, :] = c_a * sg_gl
    _shifted_copies(s['hc'], s['hcs'], HALO + tm)
    beside.here()
    if conv_c is None:
        _conv_taps(s['hc'], p['wdw'], CONV_C, HALO - (CONV_C - 1), tm, s['co'], sh_ref=s['hcs'], beside=beside)
        conv_c = s['co']
    conv_v = conv_c[...]
    co = conv_v + p['v256'][1:2, :]
    ln_c, xh_c, rstd_c = _layer_norm(co, p['v256'][2:3, :], p['v256'][3:4, :])
    beside.here()
    sg_ln = _sig(ln_c)
    sc = (ln_c * sg_ln).astype(BF16)
    pw = _dot(sc, p['wpw'][...]) + p['v256'][4:5, :]
    sg_c = _sig(c_g)
    f = dict(c_a=c_a, c_g=c_g, sg_gl=sg_gl, ln_c=ln_c, xh_c=xh_c, rstd_c=rstd_c, sg_ln=sg_ln, sc=sc, pw=pw,
             sg_c=sg_c, conv_c=conv_v)
    after(2, f, (pw * (c_g * sg_c)).astype(BF16))
    beside.here()
    d_u, d_v, d_g = _col(zc, 9), _col(zc, 10), _col(zc, 11)
    gu, th_u = _gelu(d_u)
    gv, th_v = _gelu(d_v)
    beside.here()
    v, xh_d, rstd_d = _layer_norm(gv, p['v256'][5:6, :], p['v256'][6:7, :])
    vb = v.astype(BF16)
    parts = []
    for ch in range(tm // CHUNK):
        vc = vb[ch * CHUNK:(ch + 1) * CHUNK, :]
        mixed = p['bsd'][...]
        for h in range(4):
            mixed = mixed + jnp.where(group == h, _dot(p['ws'][h], vc), 0.0)
        parts.append(mixed)
        beside.here()
    mx = parts[0] if len(parts) == 1 else jnp.concatenate(parts, axis=0)
    sg_d = _sig(d_g)
    f = dict(d_u=d_u, d_v=d_v, d_g=d_g, gu=gu, th_u=th_u, th_v=th_v, xh_d=xh_d, rstd_d=rstd_d, vb=vb, mx=mx,
             sg_d=sg_d)
    after(3, f, (gu * mx * (d_g * sg_d)).astype(BF16))
    beside.rest()


def _mix_scratch(tm):
    ext = pltpu.VMEM((HALO + tm, GW), F32)
    tile = pltpu.VMEM((tm, GW), F32)
    return dict(ua=ext, cv=tile, p0=ext, p1=ext, hc=ext, co=tile, hcs=pltpu.VMEM((8, HALO + tm, GW), F32))


_MIX_PARAMS = ('wa', 'wdw', 'v256', 'wbd', 'wpw', 'ws', 'bsd', 'wout')


def _mix_param_specs(l):
    def whole(*shape, at=l):
        nd = len(shape)
        return pl.BlockSpec((None,) + shape, lambda i, _nd=nd: (at,) + (0,) * _nd)
    return [whole(8, GW), whole(32, GW), whole(16, GW), whole(GW, GW), whole(GW, GW), whole(4, CHUNK, CHUNK),
            whole(CHUNK, GW), whole(N_CHIP, GW, D, at=0)]


def _mix_param_arrays(mp, w_out_l):
    return [mp[k] for k in _MIX_PARAMS[:-1]] + [w_out_l]


def _fwd_layer(x, vecs, w, w_out_l, mp, l, tm):
    t = x.shape[0]
    nt = t // tm
    names = list(_mix_scratch(tm))
    n_p = len(_MIX_PARAMS)

    def body(xm_ref, xr_ref, v_ref, w_ref, *rest):
        p = dict(zip(_MIX_PARAMS, rest[:n_p]))
        z_ref, xn_ref, y_ref, conv_ref = rest[n_p:n_p + 4]
        s = dict(zip(names, rest[n_p + 4:n_p + 4 + len(names)]))
        z_next, z_cur, z_halo = rest[n_p + 4 + len(names):]
        j = pl.program_id(0)

        @pl.when(j == 0)
        def _():
            z_cur[...] = jnp.zeros_like(z_cur)
            z_halo[...] = jnp.zeros_like(z_halo)

        vecs_v = v_ref[...]
        _, _, hn = _normed(xm_ref[...], vecs_v)
        hb = (hn * (1.0 + vecs_v[1:2, :]) + vecs_v[0:1, :]).astype(BF16)
        def project(g):
            def emit():
                k, c0 = divmod(g * GW, PIECE)
                part = _dot(hb, w_ref[k, :, c0:c0 + GW])
                z_ref[:, GW * g:GW * (g + 1)] = part
                z_next[:, GW * g:GW * (g + 1)] = part
            return emit

        tile = jnp.maximum(j - 1, 0)
        beside = _Beside([project(g) for g in range(DIN // GW)], points=MIX_FWD_POINTS)
        mixed = []
        _mix_forward(z_cur, z_halo, tile == 0, tile * tm, tm, p, s, lambda k, f, yk: mixed.append((f, yk)),
                     beside=beside)
        y = None
        for k, (_, yk) in enumerate(mixed):
            part = _dot(yk, p['wout'][k])
            y = part if y is None else y + part
        y_ref[...] = y
        xn_ref[...] = xr_ref[...] + vecs_v[2:3, :] * y
        conv_ref[...] = mixed[2][0]['conv_c']

        z_halo[...] = z_cur[tm - HALO:tm, :]
        z_cur[...] = z_next[...]

    def ahead(j):
        return jnp.minimum(j, nt - 1)

    def behind(j):
        return jnp.maximum(j - 1, 0)

    return pl.pallas_call(
        body, name=f"fwd_layer_{l}", grid=(nt + 1,),
        in_specs=[pl.BlockSpec((tm, D), lambda j: (ahead(j), 0)), pl.BlockSpec((tm, D), lambda j: (behind(j), 0)),
                  pl.BlockSpec((8, D), lambda j: (0, 0)),
                  pl.BlockSpec((None, N_CHIP, D, PIECE), lambda j: (0, 0, 0, 0))] + _mix_param_specs(l),
        out_specs=(pl.BlockSpec((tm, DIN), lambda j: (ahead(j), 0)), pl.BlockSpec((tm, D), lambda j: (behind(j), 0)),
                   pl.BlockSpec((tm, D), lambda j: (behind(j), 0)), pl.BlockSpec((tm, GW), lambda j: (behind(j), 0))),
        out_shape=(jax.ShapeDtypeStruct((t, DIN), F32), jax.ShapeDtypeStruct((t, D), F32),
                   jax.ShapeDtypeStruct((t, D), F32), jax.ShapeDtypeStruct((t, GW), F32)),
        scratch_shapes=list(_mix_scratch(tm).values())
        + [pltpu.VMEM((tm, DIN), F32), pltpu.VMEM((tm, DIN), F32), pltpu.VMEM((HALO, DIN), F32)],
        compiler_params=_params(("arbitrary",)),
    )(x, x, vecs, w, *_mix_param_arrays(mp, w_out_l))


def _bwd_mix(dxo, y, conv, z, vecs, mp, w_out_l, wst, l, tm, gwout_all=None):
    t = dxo.shape[0]
    nt = t // tm
    nl = wst.shape[0]
    carried = [] if gwout_all is None else [gwout_all]
    per_halo = tm // HALO
    fwd_names = list(_mix_scratch(tm))
    ext = pltpu.VMEM((tm + HALO, GW), F32)
    carry = pltpu.VMEM((HALO, GW), F32)
    tile = pltpu.VMEM((tm, GW), F32)
    bwd_scratch = dict(ea=ext, eb=ext, eb2=ext, ec=ext, ca=carry, cb=carry, cc=carry, dua=tile, dhc=tile,
                       gbacc=pltpu.VMEM((CHUNK, GW), F32), ecs=pltpu.VMEM((8, tm + HALO, GW), F32),
                       dys=pltpu.VMEM((1, tm, GW), F32))
    bwd_names = list(bwd_scratch)
    n_p = len(_MIX_PARAMS)

    def body(dxo_ref, y_ref, conv_ref, zc, zp, v_ref, *rest):
        p = dict(zip(_MIX_PARAMS, rest[:n_p]))
        wst_ref = rest[n_p]
        first_out = n_p + 1 + len(carried)
        dz_ref, gwout, gwbd, gwpw, gws, gbs, gwdw, g256, g1024 = rest[first_out:first_out + 9]
        s = dict(zip(fwd_names + bwd_names, rest[first_out + 9:]))
        i = pl.program_id(0)
        ti = nt - 1 - i
        group = _lane_group()

        @pl.when(i == 0)
        def _():
            for ref in (gwout, gwbd, gwpw, gws, gbs, gwdw, g256, g1024, s['ca'], s['cb'], s['cc'], s['gbacc']):
                ref[...] = jnp.zeros_like(ref)

        def put(k, val):
            dz_ref[:, GW * k:GW * (k + 1)] = val.astype(BF16)

        def bwd_a(f, dya):
            taps = [None] * CONV_A
            for r in range(0, tm, ROW_BLOCK):
                rs = slice(r, r + ROW_BLOCK)
                a_b, a_g = zc[rs, 0:GW], zc[rs, 3 * GW:4 * GW]
                dy_a, cv = s['dys'][0, rs, :], s['cv'][rs, :]
                sg = _sig(a_g)
                silu = a_g * sg
                t = dy_a * cv
                dz_ref[rs, 0:GW] = (t * silu).astype(BF16)
                dz_ref[rs, 3 * GW:4 * GW] = (t * a_b * (sg * (1.0 + a_g * (1.0 - sg)))).astype(BF16)
                d_cv = dy_a * a_b * silu
                s['ea'][rs, :] = d_cv
                for k in range(CONV_A):
                    first = HALO - (CONV_A - 1) + k + r
                    term = d_cv * s['ua'][first:first + ROW_BLOCK, :]
                    taps[k] = term if taps[k] is None else taps[k] + term
            for k in range(CONV_A):
                g256[8 + k:9 + k, :] += _rowsum(taps[k])
            s['ea'][tm:tm + HALO, :] = s['ca'][...]
            s['ca'][...] = s['ea'][0:HALO, :]
            _conv_taps(s['ea'], p['wa'], CONV_A, 0, tm, s['dua'], flip=True)
            for r in range(0, tm, ROW_BLOCK):
                rs = slice(r, r + ROW_BLOCK)
                d_u = s['dua'][rs, :]
                dz_ref[rs, GW:2 * GW] = (d_u * zc[rs, 2 * GW:3 * GW]).astype(BF16)
                dz_ref[rs, 2 * GW:3 * GW] = (d_u * zc[rs, GW:2 * GW]).astype(BF16)

        def bwd_b(f, dyb):
            b_g, q, sg_b, inv_cnt = f['b_g'], f['q'], f['sg_b'], f['inv_cnt']
            scale_b = p['v256'][0:1, :]
            silu_b = b_g * sg_b
            g256[0:1, :] += _rowsum(dyb * q * silu_b)
            put(5, dyb * q * scale_b * (sg_b * (1.0 + b_g * (1.0 - sg_b))))
            d_q = (dyb * scale_b * silu_b).astype(BF16)
            gwbd[...] += _dot_tn(f['pooled'], d_q)
            d_pooled = _dot_nt(d_q, p['wbd'][...])
            gp = d_pooled * inv_cnt
            n = tm + HALO
            s['eb'][0:tm, :] = gp
            s['eb'][tm:n, :] = s['cb'][...]
            s['cb'][...] = gp[0:HALO, :]
            s['eb2'][0:n - 8, :] = s['eb'][0:n - 8, :] + s['eb'][1:n - 7, :]
            r2 = s['eb2'][0:tm, :]
            s['eb'][0:n - 16, :] = s['eb2'][0:n - 16, :] + s['eb2'][2:n - 14, :]
            r4 = s['eb'][0:tm, :]
            s['eb2'][0:n - 24, :] = s['eb'][0:n - 24, :] + s['eb'][4:n - 20, :]
            r8 = s['eb2'][0:tm, :]
            r16 = r8 + s['eb2'][8:tm + 8, :]
            put(4, _by_group(group, r2, r4, r8, r16) - d_pooled)

        def bwd_c(f, dyc):
            c_a, c_g, sg_gl, ln_c, xh_c, rstd_c, sg_ln, pw, sg_c = (
                f['c_a'], f['c_g'], f['sg_gl'], f['ln_c'], f['xh_c'], f['rstd_c'], f['sg_ln'], f['pw'], f['sg_c'])
            put(8, dyc * pw * (sg_c * (1.0 + c_g * (1.0 - sg_c))))
            d_pw = dyc * (c_g * sg_c)
            g256[4:5, :] += _rowsum(d_pw)
            d_pwb = d_pw.astype(BF16)
            gwpw[...] += _dot_tn(f['sc'], d_pwb)
            d_ln = _dot_nt(d_pwb, p['wpw'][...]) * (sg_ln * (1.0 + ln_c * (1.0 - sg_ln)))
            g256[2:3, :] += _rowsum(d_ln * xh_c)
            g256[3:4, :] += _rowsum(d_ln)
            d_co = _layer_norm_grad(d_ln, xh_c, rstd_c, p['v256'][2:3, :])
            g256[1:2, :] += _rowsum(d_co)
            for k in range(CONV_C):
                gwdw[k:k + 1, :] += _rowsum(d_co * _rows_at(s['hc'], s['hcs'], HALO - (CONV_C - 1) + k, tm))
            s['ec'][0:tm, :] = d_co
            s['ec'][tm:tm + HALO, :] = s['cc'][...]
            s['cc'][...] = d_co[0:HALO, :]
            _shifted_copies(s['ec'], s['ecs'], tm + HALO)
            _conv_taps(s['ec'], p['wdw'], CONV_C, 0, tm, s['dhc'], flip=True, sh_ref=s['ecs'])
            d_hc = s['dhc'][...]
            put(6, d_hc * sg_gl)
            put(7, d_hc * c_a * sg_gl * (1.0 - sg_gl))

        def bwd_d(f, dyd):
            d_uu, d_vv, d_g, gu, th_u, th_v, xh_d, rstd_d, vb, mx, sg_d = (
                f['d_u'], f['d_v'], f['d_g'], f['gu'], f['th_u'], f['th_v'], f['xh_d'], f['rstd_d'], f['vb'],
                f['mx'], f['sg_d'])
            silu_d = d_g * sg_d
            put(9, dyd * mx * silu_d * _gelu_grad(d_uu, th_u))
            put(11, dyd * gu * mx * (sg_d * (1.0 + d_g * (1.0 - sg_d))))
            d_mx = dyd * gu * silu_d
            dv_parts = []
            gb = None
            for ch in range(tm // CHUNK):
                dmc = d_mx[ch * CHUNK:(ch + 1) * CHUNK, :]
                gb = dmc if gb is None else gb + dmc
                dmb = dmc.astype(BF16)
                vc = vb[ch * CHUNK:(ch + 1) * CHUNK, :]
                dvc = None
                for h in range(4):
                    gws[h] += _dot_nt(jnp.where(group == h, dmb, jnp.zeros_like(dmb)), vc)
                    part = jnp.where(group == h, _dot(wst_ref[h], dmb), 0.0)
                    dvc = part if dvc is None else dvc + part
                dv_parts.append(dvc)
            s['gbacc'][...] += gb
            d_v = dv_parts[0] if len(dv_parts) == 1 else jnp.concatenate(dv_parts, axis=0)
            g256[5:6, :] += _rowsum(d_v * xh_d)
            g256[6:7, :] += _rowsum(d_v)
            d_gv = _layer_norm_grad(d_v, xh_d, rstd_d, p['v256'][5:6, :])
            put(10, d_gv * _gelu_grad(d_vv, th_v))

        mixed = []
        _mix_forward(zc, zp, ti == 0, ti * tm, tm, p, s, lambda j, f, yj: mixed.append((f, yj)), conv_c=conv_ref)
        dxo_v = dxo_ref[...]
        g1024[0:1, :] += _rowsum(dxo_v * y_ref[...])
        dy = (dxo_v * v_ref[2:3, :]).astype(BF16)
        s['dys'][0] = _dot_nt(dy, p['wout'][0])
        dys = [None] + [_dot_nt(dy, p['wout'][j]) for j in range(1, 4)]
        for j, (f, yj) in enumerate(mixed):
            gwout[j] += _dot_tn(yj, dy)
        for (f, _), backward, dyj in zip(mixed, (bwd_a, bwd_b, bwd_c, bwd_d), dys):
            backward(f, dyj)

        @pl.when(i == nt - 1)
        def _():
            row = lax.broadcasted_iota(jnp.int32, (CHUNK, CHUNK), 0)
            col = lax.broadcasted_iota(jnp.int32, (CHUNK, CHUNK), 1)
            for h in range(4):
                gws[h] = jnp.where(col <= row, gws[h], 0.0)
            head_of_lane = lax.broadcasted_iota(jnp.int32, (GW, CHUNK), 0) // (GW // 4)
            sel = jnp.where(head_of_lane == lax.broadcasted_iota(jnp.int32, (GW, CHUNK), 1), 1.0, 0.0)
            gbs[...] = jnp.dot(s['gbacc'][...], sel, preferred_element_type=F32, precision=lax.Precision.HIGHEST)

    def const(*shape):
        nd = len(shape)
        return pl.BlockSpec(shape, lambda i, _nd=nd: (0,) * _nd)

    def rev(i):
        return nt - 1 - i

    out_shapes = (jax.ShapeDtypeStruct((t, DIN), BF16), jax.ShapeDtypeStruct((nl, N_CHIP, GW, D), F32),
                  jax.ShapeDtypeStruct((GW, GW), F32), jax.ShapeDtypeStruct((GW, GW), F32),
                  jax.ShapeDtypeStruct((4, CHUNK, CHUNK), F32), jax.ShapeDtypeStruct((CHUNK, CHUNK), F32),
                  jax.ShapeDtypeStruct((32, GW), F32), jax.ShapeDtypeStruct((16, GW), F32),
                  jax.ShapeDtypeStruct((8, D), F32))
    out_specs = (pl.BlockSpec((tm, DIN), lambda i: (rev(i), 0)),
                 pl.BlockSpec((None, N_CHIP, GW, D), lambda i: (l, 0, 0, 0)), const(GW, GW),
                 const(GW, GW), const(4, CHUNK, CHUNK), const(CHUNK, CHUNK), const(32, GW), const(16, GW),
                 const(8, D))
    scratch = list(_mix_scratch(tm).values()) + list(bwd_scratch.values())
    n_in = 6 + n_p + 1
    return pl.pallas_call(
        body, name=f"bwd_mix_{l}", grid=(nt,),
        in_specs=[pl.BlockSpec((tm, D), lambda i: (rev(i), 0)), pl.BlockSpec((tm, D), lambda i: (rev(i), 0)),
                  pl.BlockSpec((tm, GW), lambda i: (rev(i), 0)), pl.BlockSpec((tm, DIN), lambda i: (rev(i), 0)),
                  pl.BlockSpec((HALO, DIN), lambda i: (jnp.maximum(rev(i) * per_halo - 1, 0), 0)),
                  pl.BlockSpec((8, D), lambda i: (0, 0))]
        + _mix_param_specs(l)
        + [pl.BlockSpec((None, 4, CHUNK, CHUNK), lambda i: (l, 0, 0, 0))]
        + [pl.BlockSpec(memory_space=pl.ANY)] * len(carried),
        out_specs=out_specs, out_shape=out_shapes, scratch_shapes=scratch,
        input_output_aliases={n_in: 1} if carried else {},
        compiler_params=_params(("arbitrary",)),
    )(dxo, y, conv, z, z, vecs, *_mix_param_arrays(mp, w_out_l), wst, *carried)


def _final(x, target, fg, tt):
    t = x.shape[0]
    nt = t // tt

    def body(x_ref, t_ref, g_ref, dx_ref, sums_ref):
        i = pl.program_id(0)

        @pl.when(i == 0)
        def _():
            sums_ref[...] = jnp.zeros_like(sums_ref)

        xv = x_ref[...]
        g = g_ref[0:1, :]
        r = lax.rsqrt(jnp.mean(xv * xv, axis=-1, keepdims=True) + EPS)
        xhat = xv * r
        err = xhat * g - t_ref[...]
        sums_ref[1:2, :] += _rowsum(err * err) * (0.5 / D)
        dyv = err * (1.0 / D)
        sums_ref[0:1, :] += _rowsum(dyv * xhat)
        dxh = dyv * g
        dx_ref[...] = r * (dxh - xhat * jnp.mean(dxh * xhat, axis=-1, keepdims=True))

        @pl.when(i == nt - 1)
        def _():
            sums_ref[2:3, :] = jnp.broadcast_to(jnp.sum(sums_ref[1:2, :], axis=-1, keepdims=True), (1, D))

    return pl.pallas_call(
        body, name="final_loss", grid=(nt,),
        in_specs=[pl.BlockSpec((tt, D), lambda i: (i, 0)), pl.BlockSpec((tt, D), lambda i: (i, 0)),
                  pl.BlockSpec((8, D), lambda i: (0, 0))],
        out_specs=(pl.BlockSpec((tt, D), lambda i: (i, 0)), pl.BlockSpec((8, D), lambda i: (0, 0))),
        out_shape=(jax.ShapeDtypeStruct((t, D), F32), jax.ShapeDtypeStruct((8, D), F32)),
        compiler_params=_params(("arbitrary",)),
    )(x, target, fg)


def _adamw(w, g, m, v, tag):
    rows, cols = w.shape
    rt = rows
    for cand in (512, 256, 128, 64, 32, 16, 8):
        if rows % cand == 0:
            rt = cand
            break

    def body(w_ref, g_ref, m_ref, v_ref, d_ref, nm_ref, nv_ref):
        d_ref[...], nm_ref[...], nv_ref[...] = _adamw_update(g_ref[...], w_ref[...], m_ref[...], v_ref[...])

    spec = pl.BlockSpec((rt, cols), lambda i: (i, 0))
    return pl.pallas_call(
        body, name=f"adamw_{tag}", grid=(rows // rt,), in_specs=[spec] * 4, out_specs=(spec,) * 3,
        out_shape=(jax.ShapeDtypeStruct(w.shape, F32),) * 3, compiler_params=_params(("parallel",)),
    )(w, g, m, v)


def _adamw_update(gv, w, m, v):
    nm = ADAM_B1 * m + (1.0 - ADAM_B1) * gv
    nv = ADAM_B2 * v + (1.0 - ADAM_B2) * (gv * gv)
    m_hat = nm / (1.0 - ADAM_B1 ** ADAM_STEP)
    v_hat = nv / (1.0 - ADAM_B2 ** ADAM_STEP)
    return -ADAM_LR * (m_hat / (jnp.sqrt(v_hat) + ADAM_EPS) + ADAM_WD * w), nm, nv


def _adamw_many(ws, gs, ms, vs):
    n = len(ws)

    def body(*refs):
        w_refs, g_refs, m_refs, v_refs = (refs[k * n:(k + 1) * n] for k in range(4))
        d_refs, nm_refs, nv_refs = (refs[(4 + k) * n:(5 + k) * n] for k in range(3))
        for k in range(n):
            d_refs[k][...], nm_refs[k][...], nv_refs[k][...] = _adamw_update(
                g_refs[k][...], w_refs[k][...], m_refs[k][...], v_refs[k][...])

    shapes = tuple(jax.ShapeDtypeStruct(w.shape, F32) for w in ws)
    out = pl.pallas_call(body, name="adamw_small", out_shape=shapes * 3)(*ws, *gs, *ms, *vs)
    return out[:n], out[n:2 * n], out[2 * n:]


def _adamw_halves(w, mine, theirs, m, v, cidx, tag):
    nl, r, cdim = w.shape
    r2 = r // 2
    rt = min(r2, 256)
    nrt = r2 // rt

    def body(c_ref, a_ref, b_ref, w_ref, m_ref, v_ref, g_ref, d_ref, nm_ref, nv_ref):
        gv = jnp.where(pl.program_id(1) == c_ref[0], a_ref[...], b_ref[...])
        g_ref[...] = gv
        d_ref[...], nm_ref[...], nv_ref[...] = _adamw_update(gv, w_ref[...], m_ref[...], v_ref[...])

    half = pl.BlockSpec((None, rt, cdim), lambda l, h, i, c: (l, i, 0))
    whole = pl.BlockSpec((None, rt, cdim), lambda l, h, i, c: (l, h * nrt + i, 0))
    grid_spec = pltpu.PrefetchScalarGridSpec(
        num_scalar_prefetch=1, grid=(nl, 2, nrt), in_specs=[half, half, whole, whole, whole],
        out_specs=(whole,) * 4)
    return pl.pallas_call(
        body, name=f"adamw_{tag}", grid_spec=grid_spec,
        out_shape=(jax.ShapeDtypeStruct(w.shape, F32),) * 4,
        compiler_params=_params(("parallel", "parallel", "parallel")),
    )(cidx, mine, theirs, w, m, v)


def _pack(arrays, row_multiple=8):
    parts, offsets, off = [], [], 0
    for a in arrays:
        n = int(np.prod(a.shape))
        padded = -(-n // 1024) * 1024
        flat = a.reshape(-1).astype(F32)
        if padded != n:
            flat = jnp.concatenate([flat, jnp.zeros((padded - n,), F32)])
        parts.append(flat.reshape(-1, 128))
        offsets.append((off // 128, n, a.shape))
        off += padded
    tail = -off % (row_multiple * 128)
    if tail:
        parts.append(jnp.zeros((tail // 128, 128), F32))
    return jnp.concatenate(parts, axis=0), offsets


def _unpack(buf, offsets):
    return [buf[row:row + -(-n // 128)].reshape(-1)[:n].reshape(shape) for row, n, shape in offsets]


def _pad_rows(a, rows):
    return jnp.concatenate([a, jnp.zeros((rows - a.shape[0],) + a.shape[1:], a.dtype)], axis=0)


def kernel(x, c, norm_g, w_ada, b_ada, w_in, w_conv_a, w_pool, pool_scale, w_dw_c, b_dw_c, ln_g_c, ln_b_c, w_pw2_c, b_pw2_c, ln_g_d, ln_b_d, w_s_d, b_s_d, w_out, final_g, loss_target, m_norm_g, m_w_ada, m_b_ada, m_w_in, m_w_conv_a, m_w_pool, m_pool_scale, m_w_dw_c, m_b_dw_c, m_ln_g_c, m_ln_b_c, m_w_pw2_c, m_b_pw2_c, m_ln_g_d, m_ln_b_d, m_w_s_d, m_b_s_d, m_w_out, m_final_g, v_norm_g, v_w_ada, v_b_ada, v_w_in, v_w_conv_a, v_w_pool, v_pool_scale, v_w_dw_c, v_b_dw_c, v_ln_g_c, v_ln_b_c, v_w_pw2_c, v_b_pw2_c, v_ln_g_d, v_ln_b_d, v_w_s_d, v_b_s_d, v_w_out, v_final_g):
    env = dict(locals())
    weights = {n: env[n] for n in WEIGHTS}
    mom_m = {n: env["m_" + n] for n in WEIGHTS}
    mom_v = {n: env["v_" + n] for n in WEIGHTS}
    nl = norm_g.shape[0]
    t = x.shape[1]
    tt, tm = _tiles(t)
    ax, ay, ac = lax.axis_index("x"), lax.axis_index("y"), lax.axis_index("c")
    chip = 2 * ax + ay
    dev = 2 * chip + ac
    cidx = jnp.reshape(ac, (1,)).astype(jnp.int32)
    sidx = jnp.reshape(chip, (1,)).astype(jnp.int32)
    xs, target = x[0], loss_target[0]

    shard_small, shard_off = _pack([c, w_conv_a, w_dw_c, w_pw2_c])
    gathered = _allgather8(shard_small, "gather_small").reshape(N_DEV, -1, 128)
    per_dev = [_unpack(gathered[d], shard_off) for d in range(0, N_DEV, 2)]
    c_all = jnp.concatenate([u[0] for d in range(N_DEV)
                             for u in [_unpack(gathered[d], shard_off[:1])]], axis=0)
    w_conv_full = jnp.concatenate([u[1] for u in per_dev], axis=-1)
    w_dw_full = jnp.concatenate([u[2] for u in per_dev], axis=-1)
    w_pw2_full = jnp.concatenate([u[3] for u in per_dev], axis=1)

    b_ada_s = lax.dynamic_slice_in_dim(b_ada, chip * PIECE, PIECE, axis=1)[:, None, :]
    mod_s = _ada_forward(c_all, w_ada, b_ada_s)
    mod_all = _allgather8(mod_s.reshape(nl * N_DEV, PIECE), "gather_mod").reshape(N_DEV, nl, N_DEV, PIECE)
    mod_rows = lax.dynamic_index_in_dim(mod_all, dev, axis=2, keepdims=False)
    mod = jnp.concatenate([mod_rows[2 * s] for s in range(N_CHIP)], axis=-1)

    w_in_layers, w_out_layers = (list(v) for v in zip(_gather_weights(w_in[:1], w_out[:1])))
    if nl > 1:
        w_in_b, w_out_b = _cast_bf16([w_in[1:], w_out[1:]])

    tril = jnp.tril(jnp.ones((CHUNK, CHUNK), bool))
    ws_masked = jnp.where(tril[None, None], w_s_d, 0.0)
    wbd = jnp.zeros((nl, GW, GW), F32)
    for g in range(4):
        wbd = wbd.at[:, 64 * g:64 * (g + 1), 64 * g:64 * (g + 1)].set(w_pool[:, g])
    v256 = jnp.stack([pool_scale, b_dw_c, ln_g_c, ln_b_c, b_pw2_c, ln_g_d, ln_b_d], axis=1)
    mp = dict(
        wa=_pad_rows(jnp.swapaxes(w_conv_full, 0, 1), 8).swapaxes(0, 1),
        wdw=_pad_rows(jnp.swapaxes(w_dw_full, 0, 1), 32).swapaxes(0, 1),
        v256=_pad_rows(jnp.swapaxes(v256, 0, 1), 16).swapaxes(0, 1),
        wbd=wbd.astype(BF16), wpw=w_pw2_full.astype(BF16), ws=ws_masked.astype(BF16),
        bsd=jnp.repeat(jnp.swapaxes(b_s_d, 1, 2), GW // 4, axis=2))
    wst = jnp.swapaxes(ws_masked, 2, 3).astype(BF16)

    def vec_rows(l):
        rows = jnp.stack([mod[l, 0:D], mod[l, D:2 * D], mod[l, 2 * D:3 * D], norm_g[l]], axis=0)
        return _pad_rows(rows, 8)

    xin, zs, ys, convs, vecs = [xs], [], [], [], []
    zero = jnp.zeros((), jnp.int32)
    for l in range(nl):
        vecs.append(vec_rows(l))
        started, token = None, 0.0
        if l + 1 < nl:
            started = _gather_shards_start(w_in_b[l:l + 1], w_out_b[l:l + 1])
            token = started[-1][0, 0]
        zl, xn, yl, cl = _fwd_layer(xin[l], vecs[l] + token, w_in_layers[l], w_out_layers[l], mp, l, tm)
        zs.append(zl)
        convs.append(cl)
        xin.append(xn)
        ys.append(yl)
        if started is not None:
            own_in, own_out, land_in, land_out = _gather_shards_wait(*started[:6], cl)
            w_in_layers.append(lax.dynamic_update_slice(land_in, own_in[:, None], (zero, chip, zero, zero)))
            w_out_layers.append(lax.dynamic_update_slice(land_out, own_out[:, None], (zero, chip, zero, zero)))
    dx, fin_sums = _final(xin[nl], target, _pad_rows(final_g[None, :], 8), tt)

    g_in, g_out, small, dmod = None, None, [None] * nl, [None] * nl
    for l in reversed(range(nl)):
        dz, g_out, gwbd, gwpw, gws, gbs, gwdw, g256, g1024 = _bwd_mix(
            dx, ys[l], convs[l], zs[l], vecs[l], mp, w_out_layers[l], wst, l, tm, gwout_all=g_out)
        dx, g_in, sums = _bwd_in(dz, xin[l], dx, vecs[l], w_in_layers[l], l, nl, tt, gw_all=g_in)
        dmod[l] = jnp.concatenate([sums[0], sums[1], g1024[0]])
        small[l] = dict(
            norm_g=sums[2], w_conv_a=g256[8:8 + CONV_A],
            w_pool=jnp.stack([gwbd[64 * g:64 * (g + 1), 64 * g:64 * (g + 1)] for g in range(4)]),
            pool_scale=g256[0], w_dw_c=gwdw[:CONV_C], b_dw_c=g256[1], ln_g_c=g256[2], ln_b_c=g256[3],
            w_pw2_c=gwpw, b_pw2_c=g256[4], ln_g_d=g256[5], ln_b_d=g256[6], w_s_d=gws, b_s_d=gbs[:, :4].T)
    grad_x = dx[None]

    rs_in = _reduce_scatter_start(g_in, cidx, "in")
    rs_out = _reduce_scatter_start(g_out, cidx, "out")
    token = rs_in[-1][0, 0] + rs_out[-1][0, 0]

    dmod_all, dmod_sum = _allgather8(
        (jnp.stack(dmod) + token).reshape(nl * 3 * D // 128, 128), "gather_dmod", reduce=True)
    dmod_all = dmod_all.reshape(N_DEV, nl, 3 * D)
    grad_b_ada = dmod_sum.reshape(nl, 3 * D)
    dmod_s = jnp.swapaxes(lax.dynamic_slice_in_dim(dmod_all, chip * PIECE, PIECE, axis=2), 0, 1)
    grad_w_ada = _ada_backward(c_all, dmod_s)

    small_names = [n for n in WEIGHTS if n not in BIG and n not in ('b_ada', 'final_g')]
    stacked = [jnp.stack([small[l][n] for l in range(nl)]) for n in small_names]
    small_buf, small_off = _pack(stacked + [fin_sums[0], fin_sums[2, 0:1]], 8 * N_DEV)
    reduced = _unpack(_allreduce8(small_buf, "allreduce_small"), small_off)
    grads = dict(zip(small_names, reduced[:len(small_names)]))
    grads['final_g'] = reduced[-2]
    loss = reduced[-1][0]
    grads['b_ada'] = grad_b_ada
    grads['w_ada'] = grad_w_ada
    grads['w_conv_a'] = lax.dynamic_slice_in_dim(grads['w_conv_a'], chip * 64, 64, axis=2)
    grads['w_dw_c'] = lax.dynamic_slice_in_dim(grads['w_dw_c'], chip * 64, 64, axis=2)
    grads['w_pw2_c'] = lax.dynamic_slice_in_dim(grads['w_pw2_c'], chip * 64, 64, axis=1)

    delta, new_m, new_v = {}, {}, {}
    shape = w_ada.shape
    as2d = lambda a: a.reshape(-1, shape[-1])
    d2, m2, v2 = _adamw(as2d(w_ada), as2d(grads['w_ada']), as2d(m_w_ada), as2d(v_w_ada), 'w_ada')
    delta['w_ada'], new_m['w_ada'], new_v['w_ada'] = d2.reshape(shape), m2.reshape(shape), v2.reshape(shape)
    rest = [n for n in WEIGHTS if n not in BIG]
    ds, nms, nvs = _adamw_many([weights[n] for n in rest], [grads[n] for n in rest],
                               [mom_m[n] for n in rest], [mom_v[n] for n in rest])
    for n, dn, mn, vn in zip(rest, ds, nms, nvs):
        delta[n], new_m[n], new_v[n] = dn, mn, vn
    halves = {'w_in': _reduce_scatter_finish(rs_in, nvs[-1], sidx, "in"),
              'w_out': _reduce_scatter_finish(rs_out, nvs[-1], sidx, "out")}
    for n in ('w_in', 'w_out'):
        grads[n], delta[n], new_m[n], new_v[n] = _adamw_halves(
            weights[n], *halves[n], mom_m[n], mom_v[n], cidx, n)

    return (loss, grad_x, *[grads[n] for n in WEIGHTS], *[delta[n] for n in WEIGHTS],
            *[new_m[n] for n in WEIGHTS], *[new_v[n] for n in WEIGHTS])
```

```python
import functools
import math

import jax
import jax.numpy as jnp
import numpy as np
from jax import lax
from jax.experimental import pallas as pl
from jax.experimental.pallas import tpu as pltpu

F32 = jnp.float32
BF16 = jnp.bfloat16
MESH = pl.DeviceIdType.MESH

D = 1024
DIN = 3072
GW = 256
PIECE = 768
N_CHIP = 4
N_DEV = 8
HALO = 32
CONV_A = 3
CONV_C = 31
CHUNK = 128
CONV_ROWS = 64
ROW_BLOCK = 32
MIX_FWD_POINTS = 18
EPS = 1e-6
VMEM_LIMIT = 56 * 1024 * 1024

ADAM_LR, ADAM_B1, ADAM_B2, ADAM_EPS, ADAM_WD, ADAM_STEP = 0.001, 0.9, 0.999, 1e-08, 0.01, 10
GELU_K = math.sqrt(2.0 / math.pi)
GELU_C = 0.044715

WEIGHTS = ['norm_g', 'w_ada', 'b_ada', 'w_in', 'w_conv_a', 'w_pool', 'pool_scale', 'w_dw_c', 'b_dw_c', 'ln_g_c',
           'ln_b_c', 'w_pw2_c', 'b_pw2_c', 'ln_g_d', 'ln_b_d', 'w_s_d', 'b_s_d', 'w_out', 'final_g']
BIG = ('w_ada', 'w_in', 'w_out')


def _tiles(t):
    if t % 512 == 0 and t >= 2048:
        return 512, 256
    return 128, 128


def _params(sem, limit=VMEM_LIMIT):
    return pltpu.CompilerParams(dimension_semantics=sem, vmem_limit_bytes=limit)


def _sig(x):
    return jax.nn.sigmoid(x)


def _gelu(x):
    th = jnp.tanh(GELU_K * (x + GELU_C * x * x * x))
    return 0.5 * x * (1.0 + th), th


def _gelu_grad(x, th):
    return 0.5 * (1.0 + th) + 0.5 * x * (1.0 - th * th) * GELU_K * (1.0 + 3.0 * GELU_C * x * x)


def _dot(a, b):
    return jnp.dot(a, b, preferred_element_type=F32)


def _dot_nt(a, b):
    return lax.dot_general(a, b, (((1,), (1,)), ((), ())), preferred_element_type=F32)


def _dot_tn(a, b):
    return lax.dot_general(a, b, (((0,), (0,)), ((), ())), preferred_element_type=F32)


def _rowsum(x):
    return jnp.sum(x, axis=0, keepdims=True)


def _allgather8(v, name, reduce=False):
    m_per, n = v.shape

    def body(x_ref, out_ref, *rest):
        if reduce:
            sum_ref, send_sems, recv_sems, local_sem = rest
        else:
            send_sems, recv_sems, local_sem = rest
        x, y, c = lax.axis_index("x"), lax.axis_index("y"), lax.axis_index("c")
        me, sibling = (x, y, c), (x, y, 1 - c)
        chips = [(1 - x, y), (x, 1 - y), (1 - x, 1 - y)]

        def rows(px, py, pc):
            return out_ref.at[pl.ds((4 * px + 2 * py + pc) * m_per, m_per), :]

        def copy(k, block, to, src=None):
            return pltpu.make_async_remote_copy(
                src_ref=rows(*block) if src is None else src, dst_ref=rows(*block),
                send_sem=send_sems.at[k], recv_sem=recv_sems.at[k], device_id=to, device_id_type=MESH)

        mine = pltpu.make_async_copy(x_ref, rows(*me), local_sem)
        mine.start()
        first = [copy(0, me, sibling, src=x_ref)]
        first += [copy(1 + j, me, (*chip, c), src=x_ref) for j, chip in enumerate(chips)]
        for cp in first:
            cp.start()
        passed = [copy(4 + j, (*chip, c), sibling) for j, chip in enumerate(chips)]
        for j, chip in enumerate(chips):
            copy(1 + j, (*chip, c), me).wait_recv()
            passed[j].start()
        copy(0, sibling, me).wait_recv()
        for j, chip in enumerate(chips):
            copy(4 + j, (*chip, 1 - c), me).wait_recv()
        for cp in first + passed:
            cp.wait_send()
        mine.wait()
        if reduce:
            acc = out_ref[0:m_per, :]
            for d in range(1, N_DEV):
                acc = acc + out_ref[d * m_per:(d + 1) * m_per, :]
            sum_ref[...] = acc

    out_shape = [jax.ShapeDtypeStruct((N_DEV * m_per, n), v.dtype)]
    out_specs = [pl.BlockSpec(memory_space=pltpu.VMEM)]
    if reduce:
        out_shape.append(jax.ShapeDtypeStruct((m_per, n), v.dtype))
        out_specs.append(pl.BlockSpec(memory_space=pltpu.VMEM))
    res = pl.pallas_call(
        body, name=name, out_shape=tuple(out_shape),
        in_specs=[pl.BlockSpec(memory_space=pltpu.VMEM)], out_specs=tuple(out_specs),
        scratch_shapes=[pltpu.SemaphoreType.DMA((7,)), pltpu.SemaphoreType.DMA((7,)), pltpu.SemaphoreType.DMA],
        compiler_params=pltpu.CompilerParams(vmem_limit_bytes=VMEM_LIMIT),
    )(v)
    return res if reduce else res[0]


def _allreduce8(v, name):
    rows, n = v.shape
    rc = rows // N_DEV

    def body(x_ref, sum_ref, stage, send1, recv1, send2, recv2):
        x, y, c = lax.axis_index("x"), lax.axis_index("y"), lax.axis_index("c")
        me = 4 * x + 2 * y + c
        peers = []
        for k in range(1, N_DEV):
            kx, ky, kc = (k >> 2) & 1, (k >> 1) & 1, k & 1
            peers.append((1 - x if kx else x, 1 - y if ky else y, 1 - c if kc else c))

        def chunk(ref, d):
            return ref.at[pl.ds(d * rc, rc), :]

        scatter, gather = [], []
        for k, (px, py, pc) in enumerate(peers):
            scatter.append(pltpu.make_async_remote_copy(
                src_ref=chunk(x_ref, 4 * px + 2 * py + pc), dst_ref=stage.at[me], send_sem=send1.at[k],
                recv_sem=recv1.at[k], device_id=(px, py, pc), device_id_type=MESH))
        for cp in scatter:
            cp.start()
        stage[me] = x_ref[pl.ds(me * rc, rc), :]
        for k, (px, py, pc) in enumerate(peers):
            pltpu.make_async_remote_copy(
                src_ref=chunk(x_ref, me), dst_ref=stage.at[4 * px + 2 * py + pc], send_sem=send1.at[k],
                recv_sem=recv1.at[k], device_id=(px, py, pc), device_id_type=MESH).wait_recv()
        total = stage[0]
        for d in range(1, N_DEV):
            total = total + stage[d]
        sum_ref[pl.ds(me * rc, rc), :] = total
        for k, (px, py, pc) in enumerate(peers):
            gather.append(pltpu.make_async_remote_copy(
                src_ref=chunk(sum_ref, me), dst_ref=chunk(sum_ref, me), send_sem=send2.at[k],
                recv_sem=recv2.at[k], device_id=(px, py, pc), device_id_type=MESH))
        for cp in gather:
            cp.start()
        for k, (px, py, pc) in enumerate(peers):
            theirs = 4 * px + 2 * py + pc
            pltpu.make_async_remote_copy(
                src_ref=chunk(sum_ref, theirs), dst_ref=chunk(sum_ref, theirs), send_sem=send2.at[k],
                recv_sem=recv2.at[k], device_id=(px, py, pc), device_id_type=MESH).wait_recv()
        for cp in scatter + gather:
            cp.wait_send()

    return pl.pallas_call(
        body, name=name, out_shape=jax.ShapeDtypeStruct((rows, n), v.dtype),
        in_specs=[pl.BlockSpec(memory_space=pltpu.VMEM)], out_specs=pl.BlockSpec(memory_space=pltpu.VMEM),
        scratch_shapes=[pltpu.VMEM((N_DEV, rc, n), v.dtype)] + [pltpu.SemaphoreType.DMA((N_DEV - 1,))] * 4,
        compiler_params=pltpu.CompilerParams(vmem_limit_bytes=VMEM_LIMIT),
    )(v)


def _gather_weights(w_in_s, w_out_s):
    nl = w_in_s.shape[0]
    shapes = ((nl, N_CHIP) + w_in_s.shape[1:], (nl, N_CHIP) + w_out_s.shape[1:])
    n_sem = 2 * 3 * 2

    def body(win_ref, wout_ref, win_o, wout_o, send_sems, recv_sems):
        x, y, c = lax.axis_index("x"), lax.axis_index("y"), lax.axis_index("c")
        s = 2 * x + y
        sibling = (x, y, 1 - c)
        chips = [(1 - x, y), (x, 1 - y), (1 - x, 1 - y)]
        outs = (win_o, wout_o)
        for src, dst in ((win_ref, win_o), (wout_ref, wout_o)):
            rows = src.shape[1]
            step = min(rows, 256)
            for l in range(nl):
                for r in range(0, rows, step):
                    dst[l, s, r:r + step, :] = src[l, r:r + step, :].astype(BF16)

        def copy(k, ref, piece, half, to):
            r2 = ref.shape[2] // 2
            rows = ref.at[:, piece, pl.ds(half * r2, r2), :]
            return pltpu.make_async_remote_copy(
                src_ref=rows, dst_ref=rows, send_sem=send_sems.at[k], recv_sem=recv_sems.at[k],
                device_id=to, device_id_type=MESH)

        sends = [copy(2 * j + a, ref, s, c, (px, py, c)) for j, (px, py) in enumerate(chips)
                 for a, ref in enumerate(outs)]
        for cp in sends:
            cp.start()
        passed = []
        for j, (px, py) in enumerate(chips):
            for a, ref in enumerate(outs):
                copy(2 * j + a, ref, 2 * px + py, c, (px, py, c)).wait_recv()
                passed.append(copy(6 + 2 * j + a, ref, 2 * px + py, c, sibling))
                passed[-1].start()
        for j, (px, py) in enumerate(chips):
            for a, ref in enumerate(outs):
                copy(6 + 2 * j + a, ref, 2 * px + py, 1 - c, sibling).wait_recv()
        for cp in sends + passed:
            cp.wait_send()

    return pl.pallas_call(
        body, name="gather_weights",
        out_shape=tuple(jax.ShapeDtypeStruct(s, BF16) for s in shapes),
        in_specs=[pl.BlockSpec(memory_space=pltpu.VMEM)] * 2,
        out_specs=tuple(pl.BlockSpec(memory_space=pltpu.VMEM) for _ in shapes),
        scratch_shapes=[pltpu.SemaphoreType.DMA((n_sem,)), pltpu.SemaphoreType.DMA((n_sem,))],
        compiler_params=pltpu.CompilerParams(vmem_limit_bytes=VMEM_LIMIT),
    )(w_in_s, w_out_s)


def _cast_bf16(arrays, after):
    n = len(arrays)

    def body(*refs):
        for src, dst in zip(refs[:n], refs[n + 1:]):
            rows = src.shape[-2]
            step = min(rows, 256)
            for l in range(src.shape[0]):
                for r in range(0, rows, step):
                    dst[l, r:r + step, :] = src[l, r:r + step, :].astype(BF16)

    vmem = pl.BlockSpec(memory_space=pltpu.VMEM)
    return pl.pallas_call(
        body, name="cast_weights", out_shape=tuple(jax.ShapeDtypeStruct(a.shape, BF16) for a in arrays),
        in_specs=[vmem] * n + [pl.BlockSpec(memory_space=pl.ANY)], out_specs=(vmem,) * n,
        compiler_params=pltpu.CompilerParams(vmem_limit_bytes=VMEM_LIMIT),
    )(*arrays, after)


def _shard_copies(a_ref, b_ref, la_ref, lb_ref, send_sems, recv_sems):
    x, y, c = lax.axis_index("x"), lax.axis_index("y"), lax.axis_index("c")
    s = 2 * x + y
    sends, recvs = [], []
    for j, (px, py) in enumerate([(1 - x, y), (x, 1 - y), (1 - x, 1 - y)]):
        for a, (src, land) in enumerate(((a_ref, la_ref), (b_ref, lb_ref))):
            k = 2 * j + a
            for slot, into in ((s, sends), (2 * px + py, recvs)):
                into.append(pltpu.make_async_remote_copy(
                    src_ref=src.at[0], dst_ref=land.at[0, slot], send_sem=send_sems.at[k], recv_sem=recv_sems.at[k],
                    device_id=(px, py, c), device_id_type=MESH))
    return sends, recvs


def _gather_shards_start(a, b):
    lands = ((1, N_CHIP) + a.shape[1:], (1, N_CHIP) + b.shape[1:])
    hbm = pl.BlockSpec(memory_space=pltpu.HBM)
    sem = pl.BlockSpec(memory_space=pltpu.SEMAPHORE)

    def body(a_ref, b_ref, la_ref, lb_ref, send_sems, recv_sems, a_thru, b_thru, la_thru, lb_thru, token):
        for cp in _shard_copies(a_ref, b_ref, la_ref, lb_ref, send_sems, recv_sems)[0]:
            cp.start()
        token[...] = jnp.zeros_like(token)

    return pl.pallas_call(
        body, name="gather_weights_start",
        out_shape=(pltpu.SemaphoreType.DMA((6,)), pltpu.SemaphoreType.DMA((6,)), pltpu.HBM(a.shape, BF16),
                   pltpu.HBM(b.shape, BF16), pltpu.HBM(lands[0], BF16), pltpu.HBM(lands[1], BF16),
                   jax.ShapeDtypeStruct((8, 128), F32)),
        in_specs=(hbm,) * 4, out_specs=(sem, sem, hbm, hbm, hbm, hbm, pl.BlockSpec(memory_space=pltpu.VMEM)),
        input_output_aliases={0: 2, 1: 3, 2: 4, 3: 5},
        compiler_params=pltpu.CompilerParams(has_side_effects=pltpu.SideEffectType.DATAFLOW_SIDE_EFFECTING),
    )(pltpu.with_memory_space_constraint(a, pltpu.HBM), pltpu.with_memory_space_constraint(b, pltpu.HBM),
      pltpu.with_memory_space_constraint(lax.empty(lands[0], BF16), pltpu.HBM),
      pltpu.with_memory_space_constraint(lax.empty(lands[1], BF16), pltpu.HBM))


def _gather_shards_wait(send_sems, recv_sems, a, b, la, lb, after):
    hbm = pl.BlockSpec(memory_space=pltpu.HBM)
    sem = pl.BlockSpec(memory_space=pltpu.SEMAPHORE)

    def body(a_ref, b_ref, la_ref, lb_ref, send_sems, recv_sems, after_ref, a_dead, b_dead, la_out, lb_out):
        sends, recvs = _shard_copies(a_ref, b_ref, la_ref, lb_ref, send_sems, recv_sems)
        for cp in sends:
            cp.wait_send()
        for cp in recvs:
            cp.wait_recv()

    out = pl.pallas_call(
        body, name="gather_weights_wait",
        out_shape=tuple(pltpu.HBM(v.shape, v.dtype) for v in (a, b, la, lb)),
        in_specs=(hbm, hbm, hbm, hbm, sem, sem, pl.BlockSpec(memory_space=pl.ANY)), out_specs=(hbm,) * 4,
        input_output_aliases={0: 0, 1: 1, 2: 2, 3: 3},
        compiler_params=pltpu.CompilerParams(has_side_effects=pltpu.SideEffectType.DATAFLOW_SIDE_EFFECTING),
    )(a, b, la, lb, send_sems, recv_sems, after)
    return out


def _send_half_to_sibling(g, tag):
    nl, nc, r, cdim = g.shape
    half = r // 2

    def body(g_ref, recv_ref, send_sem, recv_sem):
        x, y, c = lax.axis_index("x"), lax.axis_index("y"), lax.axis_index("c")
        cp = pltpu.make_async_remote_copy(
            src_ref=g_ref.at[:, :, pl.ds((1 - c) * half, half), :], dst_ref=recv_ref,
            send_sem=send_sem, recv_sem=recv_sem, device_id=(x, y, 1 - c), device_id_type=MESH)
        cp.start()
        cp.wait()

    return pl.pallas_call(
        body, name=f"rs_sibling_{tag}",
        out_shape=jax.ShapeDtypeStruct((nl, nc, half, cdim), g.dtype),
        in_specs=[pl.BlockSpec(memory_space=pl.ANY)], out_specs=pl.BlockSpec(memory_space=pl.ANY),
        scratch_shapes=[pltpu.SemaphoreType.DMA, pltpu.SemaphoreType.DMA],
    )(g)


def _add_half(g, recv, cidx, tag):
    nl, nc, r, cdim = g.shape
    half = r // 2
    rt = min(half, 128)
    nrt = half // rt

    def body(c_ref, a_ref, b_ref, o_ref, ob_ref):
        c = c_ref[0]
        first = jnp.where(c == 0, a_ref[...], b_ref[...])
        second = jnp.where(c == 0, b_ref[...], a_ref[...])
        total = first + second
        o_ref[...] = total
        ob_ref[...] = total.astype(BF16)

    out_spec = pl.BlockSpec((None, None, rt, cdim), lambda l, j, i, c: (l, j, i, 0))
    grid_spec = pltpu.PrefetchScalarGridSpec(
        num_scalar_prefetch=1, grid=(nl, nc, nrt),
        in_specs=[pl.BlockSpec((None, None, rt, cdim), lambda l, j, i, c: (l, j, c[0] * nrt + i, 0)),
                  pl.BlockSpec((None, None, rt, cdim), lambda l, j, i, c: (l, j, i, 0))],
        out_specs=(out_spec, out_spec))
    return pl.pallas_call(
        body, name=f"rs_add_sibling_{tag}", grid_spec=grid_spec,
        out_shape=(jax.ShapeDtypeStruct((nl, nc, half, cdim), g.dtype),
                   jax.ShapeDtypeStruct((nl, nc, half, cdim), BF16)),
        compiler_params=_params(("parallel", "parallel", "parallel")),
    )(cidx, g, recv)


def _exchange_chips(h, tag):
    nl, nc, r2, cdim = h.shape

    def body(h_ref, recv_ref, send_sems, recv_sems):
        x, y, c = lax.axis_index("x"), lax.axis_index("y"), lax.axis_index("c")
        chips = [(1 - x, y), (x, 1 - y), (1 - x, 1 - y)]
        cps = []
        for k, (px, py) in enumerate(chips):
            cps.append(pltpu.make_async_remote_copy(
                src_ref=h_ref.at[:, 2 * px + py], dst_ref=recv_ref.at[k], send_sem=send_sems.at[k],
                recv_sem=recv_sems.at[k], device_id=(px, py, c), device_id_type=MESH))
        for cp in cps:
            cp.start()
        for cp in cps:
            cp.wait()

    return pl.pallas_call(
        body, name=f"rs_chips_{tag}",
        out_shape=jax.ShapeDtypeStruct((3, nl, r2, cdim), h.dtype),
        in_specs=[pl.BlockSpec(memory_space=pl.ANY)], out_specs=pl.BlockSpec(memory_space=pl.ANY),
        scratch_shapes=[pltpu.SemaphoreType.DMA((3,)), pltpu.SemaphoreType.DMA((3,))],
    )(h)


def _chip_copies(h_ref, land_ref, send_sems, recv_sems):
    x, y, c = lax.axis_index("x"), lax.axis_index("y"), lax.axis_index("c")
    chips = [(1 - x, y), (x, 1 - y), (1 - x, 1 - y)]
    return [pltpu.make_async_remote_copy(
        src_ref=h_ref.at[:, 2 * px + py], dst_ref=land_ref.at[k], send_sem=send_sems.at[k],
        recv_sem=recv_sems.at[k], device_id=(px, py, c), device_id_type=MESH) for k, (px, py) in enumerate(chips)]


def _exchange_chips_start(h, tag):
    nl, nc, r2, cdim = h.shape
    land_shape = (3, nl, r2, cdim)
    hbm = pl.BlockSpec(memory_space=pltpu.HBM)
    sem = pl.BlockSpec(memory_space=pltpu.SEMAPHORE)

    def body(h_ref, land_ref, send_sems, recv_sems, h_thru, land_thru, token):
        for cp in _chip_copies(h_ref, land_ref, send_sems, recv_sems):
            cp.start()
        token[...] = jnp.zeros_like(token)

    return pl.pallas_call(
        body, name=f"rs_chips_start_{tag}",
        out_shape=(pltpu.SemaphoreType.DMA((3,)), pltpu.SemaphoreType.DMA((3,)), pltpu.HBM(h.shape, h.dtype),
                   pltpu.HBM(land_shape, h.dtype), jax.ShapeDtypeStruct((8, 128), F32)),
        in_specs=(hbm, hbm), out_specs=(sem, sem, hbm, hbm, pl.BlockSpec(memory_space=pltpu.VMEM)),
        input_output_aliases={0: 2, 1: 3},
        compiler_params=pltpu.CompilerParams(has_side_effects=pltpu.SideEffectType.DATAFLOW_SIDE_EFFECTING),
    )(pltpu.with_memory_space_constraint(h, pltpu.HBM),
      pltpu.with_memory_space_constraint(lax.empty(land_shape, h.dtype), pltpu.HBM))


def _exchange_chips_wait(send_sems, recv_sems, h_thru, land_thru, after, tag):
    hbm = pl.BlockSpec(memory_space=pltpu.HBM)
    sem = pl.BlockSpec(memory_space=pltpu.SEMAPHORE)

    def body(h_ref, land_ref, send_sems, recv_sems, after_ref, h_dead, got_ref):
        for cp in _chip_copies(h_ref, land_ref, send_sems, recv_sems):
            cp.wait_send()
            cp.wait_recv()

    return pl.pallas_call(
        body, name=f"rs_chips_wait_{tag}",
        out_shape=(pltpu.HBM(h_thru.shape, h_thru.dtype), pltpu.HBM(land_thru.shape, land_thru.dtype)),
        in_specs=(hbm, hbm, sem, sem, pl.BlockSpec(memory_space=pl.ANY)), out_specs=(hbm, hbm),
        input_output_aliases={0: 0, 1: 1},
        compiler_params=pltpu.CompilerParams(has_side_effects=pltpu.SideEffectType.DATAFLOW_SIDE_EFFECTING),
    )(h_thru, land_thru, send_sems, recv_sems, after)[1]


def _add_chips(h, recv, sidx, tag):
    nl, nc, r2, cdim = h.shape
    rt = min(r2, 128)

    def body(s_ref, own_ref, r_ref, o_ref):
        s = s_ref[0]
        got = [r_ref[k].astype(F32) for k in range(3)]
        total = None
        for chip in range(N_CHIP):
            term = jnp.where(s == chip, own_ref[...],
                             jnp.where((s ^ 2) == chip, got[0], jnp.where((s ^ 1) == chip, got[1], got[2])))
            total = term if total is None else total + term
        o_ref[...] = total

    grid_spec = pltpu.PrefetchScalarGridSpec(
        num_scalar_prefetch=1, grid=(nl, r2 // rt),
        in_specs=[pl.BlockSpec((None, None, rt, cdim), lambda l, i, s: (l, s[0], i, 0)),
                  pl.BlockSpec((3, None, rt, cdim), lambda l, i, s: (0, l, i, 0))],
        out_specs=pl.BlockSpec((None, rt, cdim), lambda l, i, s: (l, i, 0)))
    return pl.pallas_call(
        body, name=f"rs_add_chips_{tag}", grid_spec=grid_spec,
        out_shape=jax.ShapeDtypeStruct((nl, r2, cdim), h.dtype),
        compiler_params=_params(("parallel", "parallel")),
    )(sidx, h, recv)


def _swap_halves(red, tag):
    def body(red_ref, out_ref, send_sem, recv_sem):
        x, y, c = lax.axis_index("x"), lax.axis_index("y"), lax.axis_index("c")
        cp = pltpu.make_async_remote_copy(
            src_ref=red_ref, dst_ref=out_ref, send_sem=send_sem, recv_sem=recv_sem,
            device_id=(x, y, 1 - c), device_id_type=MESH)
        cp.start()
        cp.wait()

    return pl.pallas_call(
        body, name=f"rs_swap_{tag}", out_shape=jax.ShapeDtypeStruct(red.shape, red.dtype),
        in_specs=[pl.BlockSpec(memory_space=pl.ANY)], out_specs=pl.BlockSpec(memory_space=pl.ANY),
        scratch_shapes=[pltpu.SemaphoreType.DMA, pltpu.SemaphoreType.DMA],
    )(red)


def _reduce_scatter(g, cidx, sidx, tag):
    started = _reduce_scatter_start(g, cidx, tag)
    return _reduce_scatter_finish(started, started[-1], sidx, tag)


def _reduce_scatter_start(g, cidx, tag):
    recv = _send_half_to_sibling(g, tag)
    chip_sum, chip_sum_bf16 = _add_half(g, recv, cidx, tag)
    return (chip_sum,) + tuple(_exchange_chips_start(chip_sum_bf16, tag))


def _reduce_scatter_finish(started, after, sidx, tag):
    chip_sum, send_sems, recv_sems, h_thru, land_thru, _ = started
    got = _exchange_chips_wait(send_sems, recv_sems, h_thru, land_thru, after, tag)
    red = _add_chips(chip_sum, got, sidx, tag)
    return red, _swap_halves(red, tag)


def _ada_forward(c_all, w_ada_s, b_s):
    nl = w_ada_s.shape[0]

    def body(c_ref, w_ref, b_ref, o_ref):
        cv = c_ref[...]
        ca = (cv * _sig(cv)).astype(BF16)
        for l in range(nl):
            o_ref[l] = _dot(ca, w_ref[l].astype(BF16)) + b_ref[l]

    return pl.pallas_call(
        body, name="ada_forward", out_shape=jax.ShapeDtypeStruct((nl, N_DEV, PIECE), F32),
        compiler_params=pltpu.CompilerParams(vmem_limit_bytes=VMEM_LIMIT),
    )(c_all, w_ada_s, b_s)


def _ada_backward(c_all, dmod_s):
    nl = dmod_s.shape[0]

    def body(c_ref, d_ref, o_ref):
        cv = c_ref[...]
        ca = (cv * _sig(cv)).astype(BF16)
        for l in range(nl):
            o_ref[l] = _dot_tn(ca, d_ref[l].astype(BF16))

    return pl.pallas_call(
        body, name="ada_backward", out_shape=jax.ShapeDtypeStruct((nl, D, PIECE), F32),
        compiler_params=pltpu.CompilerParams(vmem_limit_bytes=VMEM_LIMIT),
    )(c_all, dmod_s)


def _normed(xv, vecs):
    r = lax.rsqrt(jnp.mean(xv * xv, axis=-1, keepdims=True) + EPS)
    xhat = xv * r
    hn = xhat * vecs[3:4, :]
    return r, xhat, hn


def _bwd_in(dz, x, dxo, vecs, w, l, nl, tt, gw_all=None):
    t = x.shape[0]
    nt = t // tt

    def body(dz_ref, x_ref, dxo_ref, v_ref, w_any, *rest):
        dx_ref, gw_any, sums_ref, w_vmem, gw_acc, sem = rest[-6:]
        i = pl.program_id(0)

        @pl.when(i == 0)
        def _():
            cp = pltpu.make_async_copy(w_any.at[0], w_vmem, sem)
            cp.start()
            cp.wait()
            gw_acc[...] = jnp.zeros_like(gw_acc)
            sums_ref[...] = jnp.zeros_like(sums_ref)

        vecs_v = v_ref[...]
        r, xhat, hn = _normed(x_ref[...], vecs_v)
        one_scale = 1.0 + vecs_v[1:2, :]
        hb = (hn * one_scale + vecs_v[0:1, :]).astype(BF16)
        dh = None
        for j in range(N_CHIP):
            dzj = dz_ref[:, PIECE * j:PIECE * (j + 1)]
            part = _dot_nt(dzj, w_vmem[j])
            dh = part if dh is None else dh + part
            gw_acc[j] += _dot_tn(hb, dzj)
        sums_ref[0:1, :] += _rowsum(dh)
        sums_ref[1:2, :] += _rowsum(dh * hn)
        sums_ref[2:3, :] += _rowsum(dh * one_scale * xhat)
        dxh = dh * (vecs_v[3:4, :] * one_scale)
        dx_ref[...] = dxo_ref[...] + r * (dxh - xhat * jnp.mean(dxh * xhat, axis=-1, keepdims=True))

        @pl.when(i == nt - 1)
        def _():
            cp = pltpu.make_async_copy(gw_acc, gw_any.at[l], sem)
            cp.start()
            cp.wait()

    carried = [] if gw_all is None else [gw_all]
    return pl.pallas_call(
        body, name=f"bwd_in_{l}", grid=(nt,),
        in_specs=[pl.BlockSpec((tt, DIN), lambda i: (i, 0)), pl.BlockSpec((tt, D), lambda i: (i, 0)),
                  pl.BlockSpec((tt, D), lambda i: (i, 0)), pl.BlockSpec((8, D), lambda i: (0, 0)),
                  pl.BlockSpec(memory_space=pl.ANY)] + [pl.BlockSpec(memory_space=pl.ANY)] * len(carried),
        out_specs=(pl.BlockSpec((tt, D), lambda i: (i, 0)), pl.BlockSpec(memory_space=pl.ANY),
                   pl.BlockSpec((8, D), lambda i: (0, 0))),
        out_shape=(jax.ShapeDtypeStruct((t, D), F32), jax.ShapeDtypeStruct((nl, N_CHIP, D, PIECE), F32),
                   jax.ShapeDtypeStruct((8, D), F32)),
        scratch_shapes=[pltpu.VMEM((N_CHIP, D, PIECE), BF16), pltpu.VMEM((N_CHIP, D, PIECE), F32),
                        pltpu.SemaphoreType.DMA],
        input_output_aliases={5: 1} if carried else {},
        compiler_params=_params(("arbitrary",)),
    )(dz, x, dxo, vecs, w, *carried)


def _col(ref, k):
    return ref[:, GW * k:GW * (k + 1)]


def _lane_group():
    return lax.broadcasted_iota(jnp.int32, (1, GW), 1) // (GW // 4)


def _pool_window(group):
    return jnp.where(group == 0, 2.0, jnp.where(group == 1, 4.0, jnp.where(group == 2, 8.0, 16.0)))


def _by_group(group, a, b, c, d):
    return jnp.where(group == 0, a, jnp.where(group == 1, b, jnp.where(group == 2, c, d)))


def _layer_norm(x, g, b):
    mu = jnp.mean(x, axis=-1, keepdims=True)
    xc = x - mu
    rstd = lax.rsqrt(jnp.mean(xc * xc, axis=-1, keepdims=True) + EPS)
    xh = xc * rstd
    return xh * g + b, xh, rstd


def _layer_norm_grad(d_out, xh, rstd, g):
    dxh = d_out * g
    return rstd * (dxh - jnp.mean(dxh, axis=-1, keepdims=True) - xh * jnp.mean(dxh * xh, axis=-1, keepdims=True))


def _shifted_copies(ext_ref, sh_ref, n):
    for b in range(1, 8):
        sh_ref[b, 0:n - 8, :] = ext_ref[b:b + n - 8, :]


def _rows_at(ext_ref, sh_ref, start, rows):
    b = start % 8
    if b == 0 or sh_ref is None:
        return ext_ref[start:start + rows, :]
    return sh_ref[b, start - b:start - b + rows, :]


class _Beside:
    def __init__(self, thunks=(), points=1):
        self.thunks, self.points, self.seen, self.done = list(thunks), max(points, 1), 0, 0

    def _run_to(self, due):
        while self.done < min(due, len(self.thunks)):
            self.thunks[self.done]()
            self.done += 1

    def here(self):
        self.seen += 1
        self._run_to(-(-self.seen * len(self.thunks) // self.points))

    def rest(self):
        self._run_to(len(self.thunks))


def _conv_taps(ext_ref, w_ref, n_taps, first_row, tm, out_ref, flip=False, sh_ref=None, beside=None):
    for rb in range(0, tm, CONV_ROWS):
        acc = None
        for k in range(n_taps):
            wk = n_taps - 1 - k if flip else k
            term = w_ref[wk:wk + 1, :] * _rows_at(ext_ref, sh_ref, first_row + rb + k, CONV_ROWS)
            acc = term if acc is None else acc + term
        out_ref[rb:rb + CONV_ROWS, :] = acc
        if beside is not None:
            beside.here()


def _mix_forward(zc, zp, first, row0, tm, p, s, after, beside=None, conv_c=None):
    f = {}
    beside = beside or _Beside()
    keep = jnp.where(first, 0.0, 1.0)
    group = _lane_group()
    beside.here()
    a_b, a_c, a_x, a_g = _col(zc, 0), _col(zc, 1), _col(zc, 2), _col(zc, 3)
    s['ua'][0:HALO, :] = _col(zp, 1) * _col(zp, 2) * keep
    s['ua'][HALO:HALO + tm, :] = a_c * a_x
    _conv_taps(s['ua'], p['wa'], CONV_A, HALO - (CONV_A - 1), tm, s['cv'], beside=beside)
    cv = s['cv'][...]
    sg_a = _sig(a_g)
    f = dict(a_b=a_b, a_c=a_c, a_x=a_x, a_g=a_g, cv=cv, sg_a=sg_a)
    after(0, f, (a_b * cv * (a_g * sg_a)).astype(BF16))
    beside.here()
    b_p, b_g = _col(zc, 4), _col(zc, 5)
    s['p0'][0:HALO, :] = _col(zp, 4) * keep
    s['p0'][HALO:HALO + tm, :] = b_p
    n = HALO + tm
    s['p1'][8:n, :] = s['p0'][8:n, :] + s['p0'][7:n - 1, :]
    w2 = s['p1'][HALO:n, :]
    beside.here()
    s['p0'][16:n, :] = s['p1'][16:n, :] + s['p1'][14:n - 2, :]
    w4 = s['p0'][HALO:n, :]
    beside.here()
    s['p1'][24:n, :] = s['p0'][24:n, :] + s['p0'][20:n - 4, :]
    w8 = s['p1'][HALO:n, :]
    w16 = w8 + s['p1'][HALO - 8:n - 8, :]
    tpos = (row0 + lax.broadcasted_iota(jnp.int32, (tm, 1), 0) + 1).astype(F32)
    inv_cnt = 1.0 / jnp.minimum(tpos, _pool_window(group))
    pooled = (_by_group(group, w2, w4, w8, w16) * inv_cnt - b_p).astype(BF16)
    beside.here()
    q = _dot(pooled, p['wbd'][...])
    sg_b = _sig(b_g)
    f = dict(b_g=b_g, pooled=pooled, q=q, sg_b=sg_b, inv_cnt=inv_cnt)
    after(1, f, (q * p['v256'][0:1, :] * (b_g * sg_b)).astype(BF16))
    beside.here()
    c_a, c_gl, c_g = _col(zc, 6), _col(zc, 7), _col(zc, 8)
    sg_gl = _sig(c_gl)
    zp_gl = _col(zp, 7)
    s['hc'][0:HALO, :] = _col(zp, 6) * _sig(zp_gl) * keep
    s['hc'][HALO:HALO + tm, :] = c_a * sg_gl
    _shifted_copies(s['hc'], s['hcs'], HALO + tm)
    beside.here()
    if conv_c is None:
        _conv_taps(s['hc'], p['wdw'], CONV_C, HALO - (CONV_C - 1), tm, s['co'], sh_ref=s['hcs'], beside=beside)
        conv_c = s['co']
    conv_v = conv_c[...]
    co = conv_v + p['v256'][1:2, :]
    ln_c, xh_c, rstd_c = _layer_norm(co, p['v256'][2:3, :], p['v256'][3:4, :])
    beside.here()
    sg_ln = _sig(ln_c)
    sc = (ln_c * sg_ln).astype(BF16)
    pw = _dot(sc, p['wpw'][...]) + p['v256'][4:5, :]
    sg_c = _sig(c_g)
    f = dict(c_a=c_a, c_g=c_g, sg_gl=sg_gl, ln_c=ln_c, xh_c=xh_c, rstd_c=rstd_c, sg_ln=sg_ln, sc=sc, pw=pw,
             sg_c=sg_c, conv_c=conv_v)
    after(2, f, (pw * (c_g * sg_c)).astype(BF16))
    beside.here()
    d_u, d_v, d_g = _col(zc, 9), _col(zc, 10), _col(zc, 11)
    gu, th_u = _gelu(d_u)
    gv, th_v = _gelu(d_v)
    beside.here()
    v, xh_d, rstd_d = _layer_norm(gv, p['v256'][5:6, :], p['v256'][6:7, :])
    vb = v.astype(BF16)
    parts = []
    for ch in range(tm // CHUNK):
        vc = vb[ch * CHUNK:(ch + 1) * CHUNK, :]
        mixed = p['bsd'][...]
        for h in range(4):
            mixed = mixed + jnp.where(group == h, _dot(p['ws'][h], vc), 0.0)
        parts.append(mixed)
        beside.here()
    mx = parts[0] if len(parts) == 1 else jnp.concatenate(parts, axis=0)
    sg_d = _sig(d_g)
    f = dict(d_u=d_u, d_v=d_v, d_g=d_g, gu=gu, th_u=th_u, th_v=th_v, xh_d=xh_d, rstd_d=rstd_d, vb=vb, mx=mx,
             sg_d=sg_d)
    after(3, f, (gu * mx * (d_g * sg_d)).astype(BF16))
    beside.rest()


def _mix_scratch(tm):
    ext = pltpu.VMEM((HALO + tm, GW), F32)
    tile = pltpu.VMEM((tm, GW), F32)
    return dict(ua=ext, cv=tile, p0=ext, p1=ext, hc=ext, co=tile, hcs=pltpu.VMEM((8, HALO + tm, GW), F32))


_MIX_PARAMS = ('wa', 'wdw', 'v256', 'wbd', 'wpw', 'ws', 'bsd', 'wout')


def _mix_param_specs(l):
    def whole(*shape, at=l):
        nd = len(shape)
        return pl.BlockSpec((None,) + shape, lambda i, _nd=nd: (at,) + (0,) * _nd)
    return [whole(8, GW), whole(32, GW), whole(16, GW), whole(GW, GW), whole(GW, GW), whole(4, CHUNK, CHUNK),
            whole(CHUNK, GW), whole(N_CHIP, GW, D, at=0)]


def _mix_param_arrays(mp, w_out_l):
    return [mp[k] for k in _MIX_PARAMS[:-1]] + [w_out_l]


def _fwd_layer(x, vecs, w, w_out_l, mp, l, tm):
    t = x.shape[0]
    nt = t // tm
    names = list(_mix_scratch(tm))
    n_p = len(_MIX_PARAMS)

    def body(xm_ref, xr_ref, v_ref, w_ref, *rest):
        p = dict(zip(_MIX_PARAMS, rest[:n_p]))
        z_ref, xn_ref, y_ref, conv_ref = rest[n_p:n_p + 4]
        s = dict(zip(names, rest[n_p + 4:n_p + 4 + len(names)]))
        z_next, z_cur, z_halo = rest[n_p + 4 + len(names):]
        j = pl.program_id(0)

        @pl.when(j == 0)
        def _():
            z_cur[...] = jnp.zeros_like(z_cur)
            z_halo[...] = jnp.zeros_like(z_halo)

        vecs_v = v_ref[...]
        _, _, hn = _normed(xm_ref[...], vecs_v)
        hb = (hn * (1.0 + vecs_v[1:2, :]) + vecs_v[0:1, :]).astype(BF16)
        def project(g):
            def emit():
                k, c0 = divmod(g * GW, PIECE)
                part = _dot(hb, w_ref[k, :, c0:c0 + GW])
                z_ref[:, GW * g:GW * (g + 1)] = part
                z_next[:, GW * g:GW * (g + 1)] = part
            return emit

        tile = jnp.maximum(j - 1, 0)
        beside = _Beside([project(g) for g in range(DIN // GW)], points=MIX_FWD_POINTS)
        mixed = []
        _mix_forward(z_cur, z_halo, tile == 0, tile * tm, tm, p, s, lambda k, f, yk: mixed.append((f, yk)),
                     beside=beside)
        y = None
        for k, (_, yk) in enumerate(mixed):
            part = _dot(yk, p['wout'][k])
            y = part if y is None else y + part
        y_ref[...] = y
        xn_ref[...] = xr_ref[...] + vecs_v[2:3, :] * y
        conv_ref[...] = mixed[2][0]['conv_c']

        z_halo[...] = z_cur[tm - HALO:tm, :]
        z_cur[...] = z_next[...]

    def ahead(j):
        return jnp.minimum(j, nt - 1)

    def behind(j):
        return jnp.maximum(j - 1, 0)

    return pl.pallas_call(
        body, name=f"fwd_layer_{l}", grid=(nt + 1,),
        in_specs=[pl.BlockSpec((tm, D), lambda j: (ahead(j), 0)), pl.BlockSpec((tm, D), lambda j: (behind(j), 0)),
                  pl.BlockSpec((8, D), lambda j: (0, 0)),
                  pl.BlockSpec((None, N_CHIP, D, PIECE), lambda j: (0, 0, 0, 0))] + _mix_param_specs(l),
        out_specs=(pl.BlockSpec((tm, DIN), lambda j: (ahead(j), 0)), pl.BlockSpec((tm, D), lambda j: (behind(j), 0)),
                   pl.BlockSpec((tm, D), lambda j: (behind(j), 0)), pl.BlockSpec((tm, GW), lambda j: (behind(j), 0))),
        out_shape=(jax.ShapeDtypeStruct((t, DIN), F32), jax.ShapeDtypeStruct((t, D), F32),
                   jax.ShapeDtypeStruct((t, D), F32), jax.ShapeDtypeStruct((t, GW), F32)),
        scratch_shapes=list(_mix_scratch(tm).values())
        + [pltpu.VMEM((tm, DIN), F32), pltpu.VMEM((tm, DIN), F32), pltpu.VMEM((HALO, DIN), F32)],
        compiler_params=_params(("arbitrary",)),
    )(x, x, vecs, w, *_mix_param_arrays(mp, w_out_l))


def _bwd_mix(dxo, y, conv, z, vecs, mp, w_out_l, wst, l, tm, gwout_all=None):
    t = dxo.shape[0]
    nt = t // tm
    nl = wst.shape[0]
    carried = [] if gwout_all is None else [gwout_all]
    per_halo = tm // HALO
    fwd_names = list(_mix_scratch(tm))
    ext = pltpu.VMEM((tm + HALO, GW), F32)
    carry = pltpu.VMEM((HALO, GW), F32)
    tile = pltpu.VMEM((tm, GW), F32)
    bwd_scratch = dict(ea=ext, eb=ext, eb2=ext, ec=ext, ca=carry, cb=carry, cc=carry, dua=tile, dhc=tile,
                       gbacc=pltpu.VMEM((CHUNK, GW), F32), ecs=pltpu.VMEM((8, tm + HALO, GW), F32),
                       dys=pltpu.VMEM((1, tm, GW), F32))
    bwd_names = list(bwd_scratch)
    n_p = len(_MIX_PARAMS)

    def body(dxo_ref, y_ref, conv_ref, zc, zp, v_ref, *rest):
        p = dict(zip(_MIX_PARAMS, rest[:n_p]))
        wst_ref = rest[n_p]
        first_out = n_p + 1 + len(carried)
        dz_ref, gwout, gwbd, gwpw, gws, gbs, gwdw, g256, g1024 = rest[first_out:first_out + 9]
        s = dict(zip(fwd_names + bwd_names, rest[first_out + 9:]))
        i = pl.program_id(0)
        ti = nt - 1 - i
        group = _lane_group()

        @pl.when(i == 0)
        def _():
            for ref in (gwout, gwbd, gwpw, gws, gbs, gwdw, g256, g1024, s['ca'], s['cb'], s['cc'], s['gbacc']):
                ref[...] = jnp.zeros_like(ref)

        def put(k, val):
            dz_ref[:, GW * k:GW * (k + 1)] = val.astype(BF16)

        def bwd_a(f, dya):
            taps = [None] * CONV_A
            for r in range(0, tm, ROW_BLOCK):
                rs = slice(r, r + ROW_BLOCK)
                a_b, a_g = zc[rs, 0:GW], zc[rs, 3 * GW:4 * GW]
                dy_a, cv = s['dys'][0, rs, :], s['cv'][rs, :]
                sg = _sig(a_g)
                silu = a_g * sg
                t = dy_a * cv
                dz_ref[rs, 0:GW] = (t * silu).astype(BF16)
                dz_ref[rs, 3 * GW:4 * GW] = (t * a_b * (sg * (1.0 + a_g * (1.0 - sg)))).astype(BF16)
                d_cv = dy_a * a_b * silu
                s['ea'][rs, :] = d_cv
                for k in range(CONV_A):
                    first = HALO - (CONV_A - 1) + k + r
                    term = d_cv * s['ua'][first:first + ROW_BLOCK, :]
                    taps[k] = term if taps[k] is None else taps[k] + term
            for k in range(CONV_A):
                g256[8 + k:9 + k, :] += _rowsum(taps[k])
            s['ea'][tm:tm + HALO, :] = s['ca'][...]
            s['ca'][...] = s['ea'][0:HALO, :]
            _conv_taps(s['ea'], p['wa'], CONV_A, 0, tm, s['dua'], flip=True)
            for r in range(0, tm, ROW_BLOCK):
                rs = slice(r, r + ROW_BLOCK)
                d_u = s['dua'][rs, :]
                dz_ref[rs, GW:2 * GW] = (d_u * zc[rs, 2 * GW:3 * GW]).astype(BF16)
                dz_ref[rs, 2 * GW:3 * GW] = (d_u * zc[rs, GW:2 * GW]).astype(BF16)

        def bwd_b(f, dyb):
            b_g, q, sg_b, inv_cnt = f['b_g'], f['q'], f['sg_b'], f['inv_cnt']
            scale_b = p['v256'][0:1, :]
            silu_b = b_g * sg_b
            g256[0:1, :] += _rowsum(dyb * q * silu_b)
            put(5, dyb * q * scale_b * (sg_b * (1.0 + b_g * (1.0 - sg_b))))
            d_q = (dyb * scale_b * silu_b).astype(BF16)
            gwbd[...] += _dot_tn(f['pooled'], d_q)
            d_pooled = _dot_nt(d_q, p['wbd'][...])
            gp = d_pooled * inv_cnt
            n = tm + HALO
            s['eb'][0:tm, :] = gp
            s['eb'][tm:n, :] = s['cb'][...]
            s['cb'][...] = gp[0:HALO, :]
            s['eb2'][0:n - 8, :] = s['eb'][0:n - 8, :] + s['eb'][1:n - 7, :]
            r2 = s['eb2'][0:tm, :]
            s['eb'][0:n - 16, :] = s['eb2'][0:n - 16, :] + s['eb2'][2:n - 14, :]
            r4 = s['eb'][0:tm, :]
            s['eb2'][0:n - 24, :] = s['eb'][0:n - 24, :] + s['eb'][4:n - 20, :]
            r8 = s['eb2'][0:tm, :]
            r16 = r8 + s['eb2'][8:tm + 8, :]
            put(4, _by_group(group, r2, r4, r8, r16) - d_pooled)

        def bwd_c(f, dyc):
            c_a, c_g, sg_gl, ln_c, xh_c, rstd_c, sg_ln, pw, sg_c = (
                f['c_a'], f['c_g'], f['sg_gl'], f['ln_c'], f['xh_c'], f['rstd_c'], f['sg_ln'], f['pw'], f['sg_c'])
            put(8, dyc * pw * (sg_c * (1.0 + c_g * (1.0 - sg_c))))
            d_pw = dyc * (c_g * sg_c)
            g256[4:5, :] += _rowsum(d_pw)
            d_pwb = d_pw.astype(BF16)
            gwpw[...] += _dot_tn(f['sc'], d_pwb)
            d_ln = _dot_nt(d_pwb, p['wpw'][...]) * (sg_ln * (1.0 + ln_c * (1.0 - sg_ln)))
            g256[2:3, :] += _rowsum(d_ln * xh_c)
            g256[3:4, :] += _rowsum(d_ln)
            d_co = _layer_norm_grad(d_ln, xh_c, rstd_c, p['v256'][2:3, :])
            g256[1:2, :] += _rowsum(d_co)
            for k in range(CONV_C):
                gwdw[k:k + 1, :] += _rowsum(d_co * _rows_at(s['hc'], s['hcs'], HALO - (CONV_C - 1) + k, tm))
            s['ec'][0:tm, :] = d_co
            s['ec'][tm:tm + HALO, :] = s['cc'][...]
            s['cc'][...] = d_co[0:HALO, :]
            _shifted_copies(s['ec'], s['ecs'], tm + HALO)
            _conv_taps(s['ec'], p['wdw'], CONV_C, 0, tm, s['dhc'], flip=True, sh_ref=s['ecs'])
            d_hc = s['dhc'][...]
            put(6, d_hc * sg_gl)
            put(7, d_hc * c_a * sg_gl * (1.0 - sg_gl))

        def bwd_d(f, dyd):
            d_uu, d_vv, d_g, gu, th_u, th_v, xh_d, rstd_d, vb, mx, sg_d = (
                f['d_u'], f['d_v'], f['d_g'], f['gu'], f['th_u'], f['th_v'], f['xh_d'], f['rstd_d'], f['vb'],
                f['mx'], f['sg_d'])
            silu_d = d_g * sg_d
            put(9, dyd * mx * silu_d * _gelu_grad(d_uu, th_u))
            put(11, dyd * gu * mx * (sg_d * (1.0 + d_g * (1.0 - sg_d))))
            d_mx = dyd * gu * silu_d
            dv_parts = []
            gb = None
            for ch in range(tm // CHUNK):
                dmc = d_mx[ch * CHUNK:(ch + 1) * CHUNK, :]
                gb = dmc if gb is None else gb + dmc
                dmb = dmc.astype(BF16)
                vc = vb[ch * CHUNK:(ch + 1) * CHUNK, :]
                dvc = None
                for h in range(4):
                    gws[h] += _dot_nt(jnp.where(group == h, dmb, jnp.zeros_like(dmb)), vc)
                    part = jnp.where(group == h, _dot(wst_ref[h], dmb), 0.0)
                    dvc = part if dvc is None else dvc + part
                dv_parts.append(dvc)
            s['gbacc'][...] += gb
            d_v = dv_parts[0] if len(dv_parts) == 1 else jnp.concatenate(dv_parts, axis=0)
            g256[5:6, :] += _rowsum(d_v * xh_d)
            g256[6:7, :] += _rowsum(d_v)
            d_gv = _layer_norm_grad(d_v, xh_d, rstd_d, p['v256'][5:6, :])
            put(10, d_gv * _gelu_grad(d_vv, th_v))

        mixed = []
        _mix_forward(zc, zp, ti == 0, ti * tm, tm, p, s, lambda j, f, yj: mixed.append((f, yj)), conv_c=conv_ref)
        dxo_v = dxo_ref[...]
        g1024[0:1, :] += _rowsum(dxo_v * y_ref[...])
        dy = (dxo_v * v_ref[2:3, :]).astype(BF16)
        s['dys'][0] = _dot_nt(dy, p['wout'][0])
        dys = [None] + [_dot_nt(dy, p['wout'][j]) for j in range(1, 4)]
        for j, (f, yj) in enumerate(mixed):
            gwout[j] += _dot_tn(yj, dy)
        for (f, _), backward, dyj in zip(mixed, (bwd_a, bwd_b, bwd_c, bwd_d), dys):
            backward(f, dyj)

        @pl.when(i == nt - 1)
        def _():
            row = lax.broadcasted_iota(jnp.int32, (CHUNK, CHUNK), 0)
            col = lax.broadcasted_iota(jnp.int32, (CHUNK, CHUNK), 1)
            for h in range(4):
                gws[h] = jnp.where(col <= row, gws[h], 0.0)
            head_of_lane = lax.broadcasted_iota(jnp.int32, (GW, CHUNK), 0) // (GW // 4)
            sel = jnp.where(head_of_lane == lax.broadcasted_iota(jnp.int32, (GW, CHUNK), 1), 1.0, 0.0)
            gbs[...] = jnp.dot(s['gbacc'][...], sel, preferred_element_type=F32, precision=lax.Precision.HIGHEST)

    def const(*shape):
        nd = len(shape)
        return pl.BlockSpec(shape, lambda i, _nd=nd: (0,) * _nd)

    def rev(i):
        return nt - 1 - i

    out_shapes = (jax.ShapeDtypeStruct((t, DIN), BF16), jax.ShapeDtypeStruct((nl, N_CHIP, GW, D), F32),
                  jax.ShapeDtypeStruct((GW, GW), F32), jax.ShapeDtypeStruct((GW, GW), F32),
                  jax.ShapeDtypeStruct((4, CHUNK, CHUNK), F32), jax.ShapeDtypeStruct((CHUNK, CHUNK), F32),
                  jax.ShapeDtypeStruct((32, GW), F32), jax.ShapeDtypeStruct((16, GW), F32),
                  jax.ShapeDtypeStruct((8, D), F32))
    out_specs = (pl.BlockSpec((tm, DIN), lambda i: (rev(i), 0)),
                 pl.BlockSpec((None, N_CHIP, GW, D), lambda i: (l, 0, 0, 0)), const(GW, GW),
                 const(GW, GW), const(4, CHUNK, CHUNK), const(CHUNK, CHUNK), const(32, GW), const(16, GW),
                 const(8, D))
    scratch = list(_mix_scratch(tm).values()) + list(bwd_scratch.values())
    n_in = 6 + n_p + 1
    return pl.pallas_call(
        body, name=f"bwd_mix_{l}", grid=(nt,),
        in_specs=[pl.BlockSpec((tm, D), lambda i: (rev(i), 0)), pl.BlockSpec((tm, D), lambda i: (rev(i), 0)),
                  pl.BlockSpec((tm, GW), lambda i: (rev(i), 0)), pl.BlockSpec((tm, DIN), lambda i: (rev(i), 0)),
                  pl.BlockSpec((HALO, DIN), lambda i: (jnp.maximum(rev(i) * per_halo - 1, 0), 0)),
                  pl.BlockSpec((8, D), lambda i: (0, 0))]
        + _mix_param_specs(l)
        + [pl.BlockSpec((None, 4, CHUNK, CHUNK), lambda i: (l, 0, 0, 0))]
        + [pl.BlockSpec(memory_space=pl.ANY)] * len(carried),
        out_specs=out_specs, out_shape=out_shapes, scratch_shapes=scratch,
        input_output_aliases={n_in: 1} if carried else {},
        compiler_params=_params(("arbitrary",)),
    )(dxo, y, conv, z, z, vecs, *_mix_param_arrays(mp, w_out_l), wst, *carried)


def _final(x, target, fg, tt):
    t = x.shape[0]
    nt = t // tt

    def body(x_ref, t_ref, g_ref, dx_ref, sums_ref):
        i = pl.program_id(0)

        @pl.when(i == 0)
        def _():
            sums_ref[...] = jnp.zeros_like(sums_ref)

        xv = x_ref[...]
        g = g_ref[0:1, :]
        r = lax.rsqrt(jnp.mean(xv * xv, axis=-1, keepdims=True) + EPS)
        xhat = xv * r
        err = xhat * g - t_ref[...]
        sums_ref[1:2, :] += _rowsum(err * err) * (0.5 / D)
        dyv = err * (1.0 / D)
        sums_ref[0:1, :] += _rowsum(dyv * xhat)
        dxh = dyv * g
        dx_ref[...] = r * (dxh - xhat * jnp.mean(dxh * xhat, axis=-1, keepdims=True))

        @pl.when(i == nt - 1)
        def _():
            sums_ref[2:3, :] = jnp.broadcast_to(jnp.sum(sums_ref[1:2, :], axis=-1, keepdims=True), (1, D))

    return pl.pallas_call(
        body, name="final_loss", grid=(nt,),
        in_specs=[pl.BlockSpec((tt, D), lambda i: (i, 0)), pl.BlockSpec((tt, D), lambda i: (i, 0)),
                  pl.BlockSpec((8, D), lambda i: (0, 0))],
        out_specs=(pl.BlockSpec((tt, D), lambda i: (i, 0)), pl.BlockSpec((8, D), lambda i: (0, 0))),
        out_shape=(jax.ShapeDtypeStruct((t, D), F32), jax.ShapeDtypeStruct((8, D), F32)),
        compiler_params=_params(("arbitrary",)),
    )(x, target, fg)


def _adamw(w, g, m, v, tag):
    rows, cols = w.shape
    rt = rows
    for cand in (512, 256, 128, 64, 32, 16, 8):
        if rows % cand == 0:
            rt = cand
            break

    def body(w_ref, g_ref, m_ref, v_ref, d_ref, nm_ref, nv_ref):
        d_ref[...], nm_ref[...], nv_ref[...] = _adamw_update(g_ref[...], w_ref[...], m_ref[...], v_ref[...])

    spec = pl.BlockSpec((rt, cols), lambda i: (i, 0))
    return pl.pallas_call(
        body, name=f"adamw_{tag}", grid=(rows // rt,), in_specs=[spec] * 4, out_specs=(spec,) * 3,
        out_shape=(jax.ShapeDtypeStruct(w.shape, F32),) * 3, compiler_params=_params(("parallel",)),
    )(w, g, m, v)


def _adamw_update(gv, w, m, v):
    nm = ADAM_B1 * m + (1.0 - ADAM_B1) * gv
    nv = ADAM_B2 * v + (1.0 - ADAM_B2) * (gv * gv)
    m_hat = nm / (1.0 - ADAM_B1 ** ADAM_STEP)
    v_hat = nv / (1.0 - ADAM_B2 ** ADAM_STEP)
    return -ADAM_LR * (m_hat / (jnp.sqrt(v_hat) + ADAM_EPS) + ADAM_WD * w), nm, nv


def _adamw_many(ws, gs, ms, vs):
    n = len(ws)

    def body(*refs):
        w_refs, g_refs, m_refs, v_refs = (refs[k * n:(k + 1) * n] for k in range(4))
        d_refs, nm_refs, nv_refs = (refs[(4 + k) * n:(5 + k) * n] for k in range(3))
        for k in range(n):
            d_refs[k][...], nm_refs[k][...], nv_refs[k][...] = _adamw_update(
                g_refs[k][...], w_refs[k][...], m_refs[k][...], v_refs[k][...])

    shapes = tuple(jax.ShapeDtypeStruct(w.shape, F32) for w in ws)
    out = pl.pallas_call(body, name="adamw_small", out_shape=shapes * 3)(*ws, *gs, *ms, *vs)
    return out[:n], out[n:2 * n], out[2 * n:]


def _adamw_halves(w, mine, theirs, m, v, cidx, tag):
    nl, r, cdim = w.shape
    r2 = r // 2
    rt = min(r2, 256)
    nrt = r2 // rt

    def body(c_ref, a_ref, b_ref, w_ref, m_ref, v_ref, g_ref, d_ref, nm_ref, nv_ref):
        gv = jnp.where(pl.program_id(1) == c_ref[0], a_ref[...], b_ref[...])
        g_ref[...] = gv
        d_ref[...], nm_ref[...], nv_ref[...] = _adamw_update(gv, w_ref[...], m_ref[...], v_ref[...])

    half = pl.BlockSpec((None, rt, cdim), lambda l, h, i, c: (l, i, 0))
    whole = pl.BlockSpec((None, rt, cdim), lambda l, h, i, c: (l, h * nrt + i, 0))
    grid_spec = pltpu.PrefetchScalarGridSpec(
        num_scalar_prefetch=1, grid=(nl, 2, nrt), in_specs=[half, half, whole, whole, whole],
        out_specs=(whole,) * 4)
    return pl.pallas_call(
        body, name=f"adamw_{tag}", grid_spec=grid_spec,
        out_shape=(jax.ShapeDtypeStruct(w.shape, F32),) * 4,
        compiler_params=_params(("parallel", "parallel", "parallel")),
    )(cidx, mine, theirs, w, m, v)


def _pack(arrays, row_multiple=8):
    parts, offsets, off = [], [], 0
    for a in arrays:
        n = int(np.prod(a.shape))
        padded = -(-n // 1024) * 1024
        flat = a.reshape(-1).astype(F32)
        if padded != n:
            flat = jnp.concatenate([flat, jnp.zeros((padded - n,), F32)])
        parts.append(flat.reshape(-1, 128))
        offsets.append((off // 128, n, a.shape))
        off += padded
    tail = -off % (row_multiple * 128)
    if tail:
        parts.append(jnp.zeros((tail // 128, 128), F32))
    return jnp.concatenate(parts, axis=0), offsets


def _unpack(buf, offsets):
    return [buf[row:row + -(-n // 128)].reshape(-1)[:n].reshape(shape) for row, n, shape in offsets]


def _pad_rows(a, rows):
    return jnp.concatenate([a, jnp.zeros((rows - a.shape[0],) + a.shape[1:], a.dtype)], axis=0)


def kernel(x, c, norm_g, w_ada, b_ada, w_in, w_conv_a, w_pool, pool_scale, w_dw_c, b_dw_c, ln_g_c, ln_b_c, w_pw2_c, b_pw2_c, ln_g_d, ln_b_d, w_s_d, b_s_d, w_out, final_g, loss_target, m_norm_g, m_w_ada, m_b_ada, m_w_in, m_w_conv_a, m_w_pool, m_pool_scale, m_w_dw_c, m_b_dw_c, m_ln_g_c, m_ln_b_c, m_w_pw2_c, m_b_pw2_c, m_ln_g_d, m_ln_b_d, m_w_s_d, m_b_s_d, m_w_out, m_final_g, v_norm_g, v_w_ada, v_b_ada, v_w_in, v_w_conv_a, v_w_pool, v_pool_scale, v_w_dw_c, v_b_dw_c, v_ln_g_c, v_ln_b_c, v_w_pw2_c, v_b_pw2_c, v_ln_g_d, v_ln_b_d, v_w_s_d, v_b_s_d, v_w_out, v_final_g):
    env = dict(locals())
    weights = {n: env[n] for n in WEIGHTS}
    mom_m = {n: env["m_" + n] for n in WEIGHTS}
    mom_v = {n: env["v_" + n] for n in WEIGHTS}
    nl = norm_g.shape[0]
    t = x.shape[1]
    tt, tm = _tiles(t)
    ax, ay, ac = lax.axis_index("x"), lax.axis_index("y"), lax.axis_index("c")
    chip = 2 * ax + ay
    dev = 2 * chip + ac
    cidx = jnp.reshape(ac, (1,)).astype(jnp.int32)
    sidx = jnp.reshape(chip, (1,)).astype(jnp.int32)
    xs, target = x[0], loss_target[0]

    shard_small, shard_off = _pack([c, w_conv_a, w_dw_c, w_pw2_c])
    gathered = _allgather8(shard_small, "gather_small").reshape(N_DEV, -1, 128)
    per_dev = [_unpack(gathered[d], shard_off) for d in range(0, N_DEV, 2)]
    c_all = jnp.concatenate([u[0] for d in range(N_DEV)
                             for u in [_unpack(gathered[d], shard_off[:1])]], axis=0)
    w_conv_full = jnp.concatenate([u[1] for u in per_dev], axis=-1)
    w_dw_full = jnp.concatenate([u[2] for u in per_dev], axis=-1)
    w_pw2_full = jnp.concatenate([u[3] for u in per_dev], axis=1)

    b_ada_s = lax.dynamic_slice_in_dim(b_ada, chip * PIECE, PIECE, axis=1)[:, None, :]
    mod_s = _ada_forward(c_all, w_ada, b_ada_s)
    mod_all = _allgather8(mod_s.reshape(nl * N_DEV, PIECE), "gather_mod").reshape(N_DEV, nl, N_DEV, PIECE)
    mod_rows = lax.dynamic_index_in_dim(mod_all, dev, axis=2, keepdims=False)
    mod = jnp.concatenate([mod_rows[2 * s] for s in range(N_CHIP)], axis=-1)

    w_in_layers, w_out_layers = (list(v) for v in zip(_gather_weights(w_in[:1], w_out[:1])))
    if nl > 1:
        w_in_b, w_out_b = _cast_bf16([w_in[1:], w_out[1:]], after=w_out_layers[0])

    tril = jnp.tril(jnp.ones((CHUNK, CHUNK), bool))
    ws_masked = jnp.where(tril[None, None], w_s_d, 0.0)
    wbd = jnp.zeros((nl, GW, GW), F32)
    for g in range(4):
        wbd = wbd.at[:, 64 * g:64 * (g + 1), 64 * g:64 * (g + 1)].set(w_pool[:, g])
    v256 = jnp.stack([pool_scale, b_dw_c, ln_g_c, ln_b_c, b_pw2_c, ln_g_d, ln_b_d], axis=1)
    mp = dict(
        wa=_pad_rows(jnp.swapaxes(w_conv_full, 0, 1), 8).swapaxes(0, 1),
        wdw=_pad_rows(jnp.swapaxes(w_dw_full, 0, 1), 32).swapaxes(0, 1),
        v256=_pad_rows(jnp.swapaxes(v256, 0, 1), 16).swapaxes(0, 1),
        wbd=wbd.astype(BF16), wpw=w_pw2_full.astype(BF16), ws=ws_masked.astype(BF16),
        bsd=jnp.repeat(jnp.swapaxes(b_s_d, 1, 2), GW // 4, axis=2))
    wst = jnp.swapaxes(ws_masked, 2, 3).astype(BF16)

    def vec_rows(l):
        rows = jnp.stack([mod[l, 0:D], mod[l, D:2 * D], mod[l, 2 * D:3 * D], norm_g[l]], axis=0)
        return _pad_rows(rows, 8)

    xin, zs, ys, convs, vecs = [xs], [], [], [], []
    zero = jnp.zeros((), jnp.int32)
    for l in range(nl):
        vecs.append(vec_rows(l))
        started, token = None, 0.0
        if l + 1 < nl:
            started = _gather_shards_start(w_in_b[l:l + 1], w_out_b[l:l + 1])
            token = started[-1][0, 0]
        zl, xn, yl, cl = _fwd_layer(xin[l], vecs[l] + token, w_in_layers[l], w_out_layers[l], mp, l, tm)
        zs.append(zl)
        convs.append(cl)
        xin.append(xn)
        ys.append(yl)
        if started is not None:
            own_in, own_out, land_in, land_out = _gather_shards_wait(*started[:6], cl)
            w_in_layers.append(lax.dynamic_update_slice(land_in, own_in[:, None], (zero, chip, zero, zero)))
            w_out_layers.append(lax.dynamic_update_slice(land_out, own_out[:, None], (zero, chip, zero, zero)))
    dx, fin_sums = _final(xin[nl], target, _pad_rows(final_g[None, :], 8), tt)

    g_in, g_out, small, dmod = None, None, [None] * nl, [None] * nl
    for l in reversed(range(nl)):
        dz, g_out, gwbd, gwpw, gws, gbs, gwdw, g256, g1024 = _bwd_mix(
            dx, ys[l], convs[l], zs[l], vecs[l], mp, w_out_layers[l], wst, l, tm, gwout_all=g_out)
        dx, g_in, sums = _bwd_in(dz, xin[l], dx, vecs[l], w_in_layers[l], l, nl, tt, gw_all=g_in)
        dmod[l] = jnp.concatenate([sums[0], sums[1], g1024[0]])
        small[l] = dict(
            norm_g=sums[2], w_conv_a=g256[8:8 + CONV_A],
            w_pool=jnp.stack([gwbd[64 * g:64 * (g + 1), 64 * g:64 * (g + 1)] for g in range(4)]),
            pool_scale=g256[0], w_dw_c=gwdw[:CONV_C], b_dw_c=g256[1], ln_g_c=g256[2], ln_b_c=g256[3],
            w_pw2_c=gwpw, b_pw2_c=g256[4], ln_g_d=g256[5], ln_b_d=g256[6], w_s_d=gws, b_s_d=gbs[:, :4].T)
    grad_x = dx[None]

    rs_in = _reduce_scatter_start(g_in, cidx, "in")
    rs_out = _reduce_scatter_start(g_out, cidx, "out")
    token = rs_in[-1][0, 0] + rs_out[-1][0, 0]

    dmod_all, dmod_sum = _allgather8(
        (jnp.stack(dmod) + token).reshape(nl * 3 * D // 128, 128), "gather_dmod", reduce=True)
    dmod_all = dmod_all.reshape(N_DEV, nl, 3 * D)
    grad_b_ada = dmod_sum.reshape(nl, 3 * D)
    dmod_s = jnp.swapaxes(lax.dynamic_slice_in_dim(dmod_all, chip * PIECE, PIECE, axis=2), 0, 1)
    grad_w_ada = _ada_backward(c_all, dmod_s)

    small_names = [n for n in WEIGHTS if n not in BIG and n not in ('b_ada', 'final_g')]
    stacked = [jnp.stack([small[l][n] for l in range(nl)]) for n in small_names]
    small_buf, small_off = _pack(stacked + [fin_sums[0], fin_sums[2, 0:1]], 8 * N_DEV)
    reduced = _unpack(_allreduce8(small_buf, "allreduce_small"), small_off)
    grads = dict(zip(small_names, reduced[:len(small_names)]))
    grads['final_g'] = reduced[-2]
    loss = reduced[-1][0]
    grads['b_ada'] = grad_b_ada
    grads['w_ada'] = grad_w_ada
    grads['w_conv_a'] = lax.dynamic_slice_in_dim(grads['w_conv_a'], chip * 64, 64, axis=2)
    grads['w_dw_c'] = lax.dynamic_slice_in_dim(grads['w_dw_c'], chip * 64, 64, axis=2)
    grads['w_pw2_c'] = lax.dynamic_slice_in_dim(grads['w_pw2_c'], chip * 64, 64, axis=1)

    delta, new_m, new_v = {}, {}, {}
    shape = w_ada.shape
    as2d = lambda a: a.reshape(-1, shape[-1])
    d2, m2, v2 = _adamw(as2d(w_ada), as2d(grads['w_ada']), as2d(m_w_ada), as2d(v_w_ada), 'w_ada')
    delta['w_ada'], new_m['w_ada'], new_v['w_ada'] = d2.reshape(shape), m2.reshape(shape), v2.reshape(shape)
    rest = [n for n in WEIGHTS if n not in BIG]
    ds, nms, nvs = _adamw_many([weights[n] for n in rest], [grads[n] for n in rest],
                               [mom_m[n] for n in rest], [mom_v[n] for n in rest])
    for n, dn, mn, vn in zip(rest, ds, nms, nvs):
        delta[n], new_m[n], new_v[n] = dn, mn, vn
    halves = {'w_in': _reduce_scatter_finish(rs_in, nvs[-1], sidx, "in"),
              'w_out': _reduce_scatter_finish(rs_out, nvs[-1], sidx, "out")}
    for n in ('w_in', 'w_out'):
        grads[n], delta[n], new_m[n], new_v[n] = _adamw_halves(
            weights[n], *halves[n], mom_m[n], mom_v[n], cidx, n)

    return (loss, grad_x, *[grads[n] for n in WEIGHTS], *[delta[n] for n in WEIGHTS],
            *[new_m[n] for n in WEIGHTS], *[new_v[n] for n in WEIGHTS])
```

```python
import math

import jax
import jax.numpy as jnp
import numpy as np
from jax import lax
from jax.experimental import pallas as pl
from jax.experimental.pallas import tpu as pltpu

F32 = jnp.float32
BF16 = jnp.bfloat16
MESH = pl.DeviceIdType.MESH

D = 1024
DIN = 3072
GW = 256
PIECE = 768
N_CHIP = 4
N_DEV = 8
HALO = 32
CONV_A = 3
CONV_C = 31
CHUNK = 128
CONV_ROWS = 64
ROW_BLOCK = 32
EPS = 1e-6
VMEM_LIMIT = 56 * 1024 * 1024

ADAM_LR, ADAM_B1, ADAM_B2, ADAM_EPS, ADAM_WD, ADAM_STEP = 0.001, 0.9, 0.999, 1e-08, 0.01, 10
GELU_K = math.sqrt(2.0 / math.pi)
GELU_C = 0.044715

WEIGHTS = ['norm_g', 'w_ada', 'b_ada', 'w_in', 'w_conv_a', 'w_pool', 'pool_scale', 'w_dw_c', 'b_dw_c', 'ln_g_c',
           'ln_b_c', 'w_pw2_c', 'b_pw2_c', 'ln_g_d', 'ln_b_d', 'w_s_d', 'b_s_d', 'w_out', 'final_g']
BIG = ('w_ada', 'w_in', 'w_out')


def _tiles(t):
    if t % 512 == 0 and t >= 2048:
        return 512, 256
    return 128, 128


def _params(sem, limit=VMEM_LIMIT):
    return pltpu.CompilerParams(dimension_semantics=sem, vmem_limit_bytes=limit)


def _sig(x):
    return jax.nn.sigmoid(x)


def _gelu(x):
    th = jnp.tanh(GELU_K * (x + GELU_C * x * x * x))
    return 0.5 * x * (1.0 + th), th


def _gelu_grad(x, th):
    return 0.5 * (1.0 + th) + 0.5 * x * (1.0 - th * th) * GELU_K * (1.0 + 3.0 * GELU_C * x * x)


def _dot(a, b):
    return jnp.dot(a, b, preferred_element_type=F32)


def _dot_nt(a, b):
    return lax.dot_general(a, b, (((1,), (1,)), ((), ())), preferred_element_type=F32)


def _dot_tn(a, b):
    return lax.dot_general(a, b, (((0,), (0,)), ((), ())), preferred_element_type=F32)


def _rowsum(x):
    return jnp.sum(x, axis=0, keepdims=True)


def _allgather8(v, name, reduce=False):
    m_per, n = v.shape

    def body(x_ref, out_ref, *rest):
        if reduce:
            sum_ref, send_sems, recv_sems, local_sem = rest
        else:
            send_sems, recv_sems, local_sem = rest
        x, y, c = lax.axis_index("x"), lax.axis_index("y"), lax.axis_index("c")
        me, sibling = (x, y, c), (x, y, 1 - c)
        chips = [(1 - x, y), (x, 1 - y), (1 - x, 1 - y)]

        def rows(px, py, pc):
            return out_ref.at[pl.ds((4 * px + 2 * py + pc) * m_per, m_per), :]

        def copy(k, block, to, src=None):
            return pltpu.make_async_remote_copy(
                src_ref=rows(*block) if src is None else src, dst_ref=rows(*block),
                send_sem=send_sems.at[k], recv_sem=recv_sems.at[k], device_id=to, device_id_type=MESH)

        mine = pltpu.make_async_copy(x_ref, rows(*me), local_sem)
        mine.start()
        first = [copy(0, me, sibling, src=x_ref)]
        first += [copy(1 + j, me, (*chip, c), src=x_ref) for j, chip in enumerate(chips)]
        for cp in first:
            cp.start()
        passed = [copy(4 + j, (*chip, c), sibling) for j, chip in enumerate(chips)]
        for j, chip in enumerate(chips):
            copy(1 + j, (*chip, c), me).wait_recv()
            passed[j].start()
        copy(0, sibling, me).wait_recv()
        for j, chip in enumerate(chips):
            copy(4 + j, (*chip, 1 - c), me).wait_recv()
        for cp in first + passed:
            cp.wait_send()
        mine.wait()
        if reduce:
            acc = out_ref[0:m_per, :]
            for d in range(1, N_DEV):
                acc = acc + out_ref[d * m_per:(d + 1) * m_per, :]
            sum_ref[...] = acc

    out_shape = [jax.ShapeDtypeStruct((N_DEV * m_per, n), v.dtype)]
    out_specs = [pl.BlockSpec(memory_space=pltpu.VMEM)]
    if reduce:
        out_shape.append(jax.ShapeDtypeStruct((m_per, n), v.dtype))
        out_specs.append(pl.BlockSpec(memory_space=pltpu.VMEM))
    res = pl.pallas_call(
        body, name=name, out_shape=tuple(out_shape),
        in_specs=[pl.BlockSpec(memory_space=pltpu.VMEM)], out_specs=tuple(out_specs),
        scratch_shapes=[pltpu.SemaphoreType.DMA((7,)), pltpu.SemaphoreType.DMA((7,)), pltpu.SemaphoreType.DMA],
        compiler_params=pltpu.CompilerParams(vmem_limit_bytes=VMEM_LIMIT),
    )(v)
    return res if reduce else res[0]


def _allreduce8(v, name):
    rows, n = v.shape
    rc = rows // N_DEV

    def body(x_ref, sum_ref, stage, send1, recv1, send2, recv2):
        x, y, c = lax.axis_index("x"), lax.axis_index("y"), lax.axis_index("c")
        me = 4 * x + 2 * y + c
        peers = []
        for k in range(1, N_DEV):
            kx, ky, kc = (k >> 2) & 1, (k >> 1) & 1, k & 1
            peers.append((1 - x if kx else x, 1 - y if ky else y, 1 - c if kc else c))

        def chunk(ref, d):
            return ref.at[pl.ds(d * rc, rc), :]

        scatter, gather = [], []
        for k, (px, py, pc) in enumerate(peers):
            scatter.append(pltpu.make_async_remote_copy(
                src_ref=chunk(x_ref, 4 * px + 2 * py + pc), dst_ref=stage.at[me], send_sem=send1.at[k],
                recv_sem=recv1.at[k], device_id=(px, py, pc), device_id_type=MESH))
        for cp in scatter:
            cp.start()
        stage[me] = x_ref[pl.ds(me * rc, rc), :]
        for k, (px, py, pc) in enumerate(peers):
            pltpu.make_async_remote_copy(
                src_ref=chunk(x_ref, me), dst_ref=stage.at[4 * px + 2 * py + pc], send_sem=send1.at[k],
                recv_sem=recv1.at[k], device_id=(px, py, pc), device_id_type=MESH).wait_recv()
        total = stage[0]
        for d in range(1, N_DEV):
            total = total + stage[d]
        sum_ref[pl.ds(me * rc, rc), :] = total
        for k, (px, py, pc) in enumerate(peers):
            gather.append(pltpu.make_async_remote_copy(
                src_ref=chunk(sum_ref, me), dst_ref=chunk(sum_ref, me), send_sem=send2.at[k],
                recv_sem=recv2.at[k], device_id=(px, py, pc), device_id_type=MESH))
        for cp in gather:
            cp.start()
        for k, (px, py, pc) in enumerate(peers):
            theirs = 4 * px + 2 * py + pc
            pltpu.make_async_remote_copy(
                src_ref=chunk(sum_ref, theirs), dst_ref=chunk(sum_ref, theirs), send_sem=send2.at[k],
                recv_sem=recv2.at[k], device_id=(px, py, pc), device_id_type=MESH).wait_recv()
        for cp in scatter + gather:
            cp.wait_send()

    return pl.pallas_call(
        body, name=name, out_shape=jax.ShapeDtypeStruct((rows, n), v.dtype),
        in_specs=[pl.BlockSpec(memory_space=pltpu.VMEM)], out_specs=pl.BlockSpec(memory_space=pltpu.VMEM),
        scratch_shapes=[pltpu.VMEM((N_DEV, rc, n), v.dtype)] + [pltpu.SemaphoreType.DMA((N_DEV - 1,))] * 4,
        compiler_params=pltpu.CompilerParams(vmem_limit_bytes=VMEM_LIMIT),
    )(v)


def _gather_weights(w_in_s, w_out_s):
    nl = w_in_s.shape[0]
    shapes = ((nl, N_CHIP) + w_in_s.shape[1:], (nl, N_CHIP) + w_out_s.shape[1:])
    n_sem = 2 * 3 * 2

    def body(win_ref, wout_ref, win_o, wout_o, send_sems, recv_sems):
        x, y, c = lax.axis_index("x"), lax.axis_index("y"), lax.axis_index("c")
        s = 2 * x + y
        sibling = (x, y, 1 - c)
        chips = [(1 - x, y), (x, 1 - y), (1 - x, 1 - y)]
        outs = (win_o, wout_o)
        for src, dst in ((win_ref, win_o), (wout_ref, wout_o)):
            rows = src.shape[1]
            step = min(rows, 256)
            for l in range(nl):
                for r in range(0, rows, step):
                    dst[l, s, r:r + step, :] = src[l, r:r + step, :].astype(BF16)

        def copy(k, ref, piece, half, to):
            r2 = ref.shape[2] // 2
            rows = ref.at[:, piece, pl.ds(half * r2, r2), :]
            return pltpu.make_async_remote_copy(
                src_ref=rows, dst_ref=rows, send_sem=send_sems.at[k], recv_sem=recv_sems.at[k],
                device_id=to, device_id_type=MESH)

        sends = [copy(2 * j + a, ref, s, c, (px, py, c)) for j, (px, py) in enumerate(chips)
                 for a, ref in enumerate(outs)]
        for cp in sends:
            cp.start()
        passed = []
        for j, (px, py) in enumerate(chips):
            for a, ref in enumerate(outs):
                copy(2 * j + a, ref, 2 * px + py, c, (px, py, c)).wait_recv()
                passed.append(copy(6 + 2 * j + a, ref, 2 * px + py, c, sibling))
                passed[-1].start()
        for j, (px, py) in enumerate(chips):
            for a, ref in enumerate(outs):
                copy(6 + 2 * j + a, ref, 2 * px + py, 1 - c, sibling).wait_recv()
        for cp in sends + passed:
            cp.wait_send()

    return pl.pallas_call(
        body, name="gather_weights",
        out_shape=tuple(jax.ShapeDtypeStruct(s, BF16) for s in shapes),
        in_specs=[pl.BlockSpec(memory_space=pltpu.VMEM)] * 2,
        out_specs=tuple(pl.BlockSpec(memory_space=pltpu.VMEM) for _ in shapes),
        scratch_shapes=[pltpu.SemaphoreType.DMA((n_sem,)), pltpu.SemaphoreType.DMA((n_sem,))],
        compiler_params=pltpu.CompilerParams(vmem_limit_bytes=VMEM_LIMIT),
    )(w_in_s, w_out_s)


def _cast_bf16(arrays, after):
    n = len(arrays)

    def body(*refs):
        for src, dst in zip(refs[:n], refs[n + 1:]):
            rows = src.shape[-2]
            step = min(rows, 256)
            for l in range(src.shape[0]):
                for r in range(0, rows, step):
                    dst[l, r:r + step, :] = src[l, r:r + step, :].astype(BF16)

    vmem = pl.BlockSpec(memory_space=pltpu.VMEM)
    return pl.pallas_call(
        body, name="cast_weights", out_shape=tuple(jax.ShapeDtypeStruct(a.shape, BF16) for a in arrays),
        in_specs=[vmem] * n + [pl.BlockSpec(memory_space=pl.ANY)], out_specs=(vmem,) * n,
        compiler_params=pltpu.CompilerParams(vmem_limit_bytes=VMEM_LIMIT),
    )(*arrays, after)


def _shard_copies(a_ref, b_ref, la_ref, lb_ref, send_sems, recv_sems):
    x, y, c = lax.axis_index("x"), lax.axis_index("y"), lax.axis_index("c")
    s = 2 * x + y
    sends, recvs = [], []
    for j, (px, py) in enumerate([(1 - x, y), (x, 1 - y), (1 - x, 1 - y)]):
        for a, (src, land) in enumerate(((a_ref, la_ref), (b_ref, lb_ref))):
            k = 2 * j + a
            for slot, into in ((s, sends), (2 * px + py, recvs)):
                into.append(pltpu.make_async_remote_copy(
                    src_ref=src.at[0], dst_ref=land.at[0, slot], send_sem=send_sems.at[k], recv_sem=recv_sems.at[k],
                    device_id=(px, py, c), device_id_type=MESH))
    return sends, recvs


def _gather_shards_start(a, b):
    lands = ((1, N_CHIP) + a.shape[1:], (1, N_CHIP) + b.shape[1:])
    hbm = pl.BlockSpec(memory_space=pltpu.HBM)
    sem = pl.BlockSpec(memory_space=pltpu.SEMAPHORE)

    def body(a_ref, b_ref, la_ref, lb_ref, send_sems, recv_sems, a_thru, b_thru, la_thru, lb_thru, token):
        for cp in _shard_copies(a_ref, b_ref, la_ref, lb_ref, send_sems, recv_sems)[0]:
            cp.start()
        token[...] = jnp.zeros_like(token)

    return pl.pallas_call(
        body, name="gather_weights_start",
        out_shape=(pltpu.SemaphoreType.DMA((6,)), pltpu.SemaphoreType.DMA((6,)), pltpu.HBM(a.shape, BF16),
                   pltpu.HBM(b.shape, BF16), pltpu.HBM(lands[0], BF16), pltpu.HBM(lands[1], BF16),
                   jax.ShapeDtypeStruct((8, 128), F32)),
        in_specs=(hbm,) * 4, out_specs=(sem, sem, hbm, hbm, hbm, hbm, pl.BlockSpec(memory_space=pltpu.VMEM)),
        input_output_aliases={0: 2, 1: 3, 2: 4, 3: 5},
        compiler_params=pltpu.CompilerParams(has_side_effects=pltpu.SideEffectType.DATAFLOW_SIDE_EFFECTING),
    )(pltpu.with_memory_space_constraint(a, pltpu.HBM), pltpu.with_memory_space_constraint(b, pltpu.HBM),
      pltpu.with_memory_space_constraint(lax.empty(lands[0], BF16), pltpu.HBM),
      pltpu.with_memory_space_constraint(lax.empty(lands[1], BF16), pltpu.HBM))


def _gather_shards_wait(send_sems, recv_sems, a, b, la, lb, after):
    hbm = pl.BlockSpec(memory_space=pltpu.HBM)
    sem = pl.BlockSpec(memory_space=pltpu.SEMAPHORE)

    def body(a_ref, b_ref, la_ref, lb_ref, send_sems, recv_sems, after_ref, a_dead, b_dead, la_out, lb_out):
        sends, recvs = _shard_copies(a_ref, b_ref, la_ref, lb_ref, send_sems, recv_sems)
        for cp in sends:
            cp.wait_send()
        for cp in recvs:
            cp.wait_recv()

    out = pl.pallas_call(
        body, name="gather_weights_wait",
        out_shape=tuple(pltpu.HBM(v.shape, v.dtype) for v in (a, b, la, lb)),
        in_specs=(hbm, hbm, hbm, hbm, sem, sem, pl.BlockSpec(memory_space=pl.ANY)), out_specs=(hbm,) * 4,
        input_output_aliases={0: 0, 1: 1, 2: 2, 3: 3},
        compiler_params=pltpu.CompilerParams(has_side_effects=pltpu.SideEffectType.DATAFLOW_SIDE_EFFECTING),
    )(a, b, la, lb, send_sems, recv_sems, after)
    return out


def _send_half_to_sibling(g, tag):
    nl, nc, r, cdim = g.shape
    half = r // 2

    def body(g_ref, recv_ref, send_sem, recv_sem):
        x, y, c = lax.axis_index("x"), lax.axis_index("y"), lax.axis_index("c")
        cp = pltpu.make_async_remote_copy(
            src_ref=g_ref.at[:, :, pl.ds((1 - c) * half, half), :], dst_ref=recv_ref,
            send_sem=send_sem, recv_sem=recv_sem, device_id=(x, y, 1 - c), device_id_type=MESH)
        cp.start()
        cp.wait()

    return pl.pallas_call(
        body, name=f"rs_sibling_{tag}",
        out_shape=jax.ShapeDtypeStruct((nl, nc, half, cdim), g.dtype),
        in_specs=[pl.BlockSpec(memory_space=pl.ANY)], out_specs=pl.BlockSpec(memory_space=pl.ANY),
        scratch_shapes=[pltpu.SemaphoreType.DMA, pltpu.SemaphoreType.DMA],
    )(g)


def _add_half(g, recv, cidx, tag):
    nl, nc, r, cdim = g.shape
    half = r // 2
    rt = min(half, 512)
    nrt = half // rt

    def body(c_ref, a_ref, b_ref, o_ref, ob_ref):
        c = c_ref[0]
        first = jnp.where(c == 0, a_ref[...], b_ref[...])
        second = jnp.where(c == 0, b_ref[...], a_ref[...])
        total = first + second
        o_ref[...] = total
        ob_ref[...] = total.astype(BF16)

    out_spec = pl.BlockSpec((None, None, rt, cdim), lambda l, j, i, c: (l, j, i, 0))
    grid_spec = pltpu.PrefetchScalarGridSpec(
        num_scalar_prefetch=1, grid=(nl, nc, nrt),
        in_specs=[pl.BlockSpec((None, None, rt, cdim), lambda l, j, i, c: (l, j, c[0] * nrt + i, 0)),
                  pl.BlockSpec((None, None, rt, cdim), lambda l, j, i, c: (l, j, i, 0))],
        out_specs=(out_spec, out_spec))
    return pl.pallas_call(
        body, name=f"rs_add_sibling_{tag}", grid_spec=grid_spec,
        out_shape=(jax.ShapeDtypeStruct((nl, nc, half, cdim), g.dtype),
                   jax.ShapeDtypeStruct((nl, nc, half, cdim), BF16)),
        compiler_params=_params(("parallel", "parallel", "parallel")),
    )(cidx, g, recv)


def _chip_copies(h_ref, land_ref, send_sems, recv_sems):
    x, y, c = lax.axis_index("x"), lax.axis_index("y"), lax.axis_index("c")
    chips = [(1 - x, y), (x, 1 - y), (1 - x, 1 - y)]
    return [pltpu.make_async_remote_copy(
        src_ref=h_ref.at[:, 2 * px + py], dst_ref=land_ref.at[k], send_sem=send_sems.at[k],
        recv_sem=recv_sems.at[k], device_id=(px, py, c), device_id_type=MESH) for k, (px, py) in enumerate(chips)]


def _exchange_chips_start(h, tag):
    nl, nc, r2, cdim = h.shape
    land_shape = (3, nl, r2, cdim)
    hbm = pl.BlockSpec(memory_space=pltpu.HBM)
    sem = pl.BlockSpec(memory_space=pltpu.SEMAPHORE)

    def body(h_ref, land_ref, send_sems, recv_sems, h_thru, land_thru, token):
        for cp in _chip_copies(h_ref, land_ref, send_sems, recv_sems):
            cp.start()
        token[...] = jnp.zeros_like(token)

    return pl.pallas_call(
        body, name=f"rs_chips_start_{tag}",
        out_shape=(pltpu.SemaphoreType.DMA((3,)), pltpu.SemaphoreType.DMA((3,)), pltpu.HBM(h.shape, h.dtype),
                   pltpu.HBM(land_shape, h.dtype), jax.ShapeDtypeStruct((8, 128), F32)),
        in_specs=(hbm, hbm), out_specs=(sem, sem, hbm, hbm, pl.BlockSpec(memory_space=pltpu.VMEM)),
        input_output_aliases={0: 2, 1: 3},
        compiler_params=pltpu.CompilerParams(has_side_effects=pltpu.SideEffectType.DATAFLOW_SIDE_EFFECTING),
    )(pltpu.with_memory_space_constraint(h, pltpu.HBM),
      pltpu.with_memory_space_constraint(lax.empty(land_shape, h.dtype), pltpu.HBM))


def _exchange_chips_wait(send_sems, recv_sems, h_thru, land_thru, after, tag):
    hbm = pl.BlockSpec(memory_space=pltpu.HBM)
    sem = pl.BlockSpec(memory_space=pltpu.SEMAPHORE)

    def body(h_ref, land_ref, send_sems, recv_sems, after_ref, h_dead, got_ref):
        for cp in _chip_copies(h_ref, land_ref, send_sems, recv_sems):
            cp.wait_send()
            cp.wait_recv()

    return pl.pallas_call(
        body, name=f"rs_chips_wait_{tag}",
        out_shape=(pltpu.HBM(h_thru.shape, h_thru.dtype), pltpu.HBM(land_thru.shape, land_thru.dtype)),
        in_specs=(hbm, hbm, sem, sem, pl.BlockSpec(memory_space=pl.ANY)), out_specs=(hbm, hbm),
        input_output_aliases={0: 0, 1: 1},
        compiler_params=pltpu.CompilerParams(has_side_effects=pltpu.SideEffectType.DATAFLOW_SIDE_EFFECTING),
    )(h_thru, land_thru, send_sems, recv_sems, after)[1]


def _add_chips(h, recv, sidx, tag):
    nl, nc, r2, cdim = h.shape
    rt = min(r2, 512)

    def body(s_ref, own_ref, r_ref, o_ref):
        s = s_ref[0]
        got = [r_ref[k].astype(F32) for k in range(3)]
        total = None
        for chip in range(N_CHIP):
            term = jnp.where(s == chip, own_ref[...],
                             jnp.where((s ^ 2) == chip, got[0], jnp.where((s ^ 1) == chip, got[1], got[2])))
            total = term if total is None else total + term
        o_ref[...] = total

    grid_spec = pltpu.PrefetchScalarGridSpec(
        num_scalar_prefetch=1, grid=(nl, r2 // rt),
        in_specs=[pl.BlockSpec((None, None, rt, cdim), lambda l, i, s: (l, s[0], i, 0)),
                  pl.BlockSpec((3, None, rt, cdim), lambda l, i, s: (0, l, i, 0))],
        out_specs=pl.BlockSpec((None, rt, cdim), lambda l, i, s: (l, i, 0)))
    return pl.pallas_call(
        body, name=f"rs_add_chips_{tag}", grid_spec=grid_spec,
        out_shape=jax.ShapeDtypeStruct((nl, r2, cdim), h.dtype),
        compiler_params=_params(("parallel", "parallel")),
    )(sidx, h, recv)


def _swap_halves(red, tag):
    def body(red_ref, out_ref, send_sem, recv_sem):
        x, y, c = lax.axis_index("x"), lax.axis_index("y"), lax.axis_index("c")
        cp = pltpu.make_async_remote_copy(
            src_ref=red_ref, dst_ref=out_ref, send_sem=send_sem, recv_sem=recv_sem,
            device_id=(x, y, 1 - c), device_id_type=MESH)
        cp.start()
        cp.wait()

    return pl.pallas_call(
        body, name=f"rs_swap_{tag}", out_shape=jax.ShapeDtypeStruct(red.shape, red.dtype),
        in_specs=[pl.BlockSpec(memory_space=pl.ANY)], out_specs=pl.BlockSpec(memory_space=pl.ANY),
        scratch_shapes=[pltpu.SemaphoreType.DMA, pltpu.SemaphoreType.DMA],
    )(red)


def _reduce_scatter_start(g, cidx, tag):
    recv = _send_half_to_sibling(g, tag)
    chip_sum, chip_sum_bf16 = _add_half(g, recv, cidx, tag)
    return (chip_sum,) + tuple(_exchange_chips_start(chip_sum_bf16, tag))


def _reduce_scatter_finish(started, after, sidx, tag):
    chip_sum, send_sems, recv_sems, h_thru, land_thru, _ = started
    got = _exchange_chips_wait(send_sems, recv_sems, h_thru, land_thru, after, tag)
    red = _add_chips(chip_sum, got, sidx, tag)
    return red, _swap_halves(red, tag)


def _ada_forward(c_all, w_ada_s, b_s):
    nl = w_ada_s.shape[0]

    def body(c_ref, w_ref, b_ref, o_ref):
        cv = c_ref[...]
        ca = (cv * _sig(cv)).astype(BF16)
        for l in range(nl):
            o_ref[l] = _dot(ca, w_ref[l].astype(BF16)) + b_ref[l]

    return pl.pallas_call(
        body, name="ada_forward", out_shape=jax.ShapeDtypeStruct((nl, N_DEV, PIECE), F32),
        compiler_params=pltpu.CompilerParams(vmem_limit_bytes=VMEM_LIMIT),
    )(c_all, w_ada_s, b_s)


def _ada_backward(c_all, dmod_s):
    nl = dmod_s.shape[0]

    def body(c_ref, d_ref, o_ref):
        cv = c_ref[...]
        ca = (cv * _sig(cv)).astype(BF16)
        for l in range(nl):
            o_ref[l] = _dot_tn(ca, d_ref[l].astype(BF16))

    return pl.pallas_call(
        body, name="ada_backward", out_shape=jax.ShapeDtypeStruct((nl, D, PIECE), F32),
        compiler_params=pltpu.CompilerParams(vmem_limit_bytes=VMEM_LIMIT),
    )(c_all, dmod_s)


def _normed(xv, vecs):
    r = lax.rsqrt(jnp.mean(xv * xv, axis=-1, keepdims=True) + EPS)
    xhat = xv * r
    hn = xhat * vecs[3:4, :]
    return r, xhat, hn


def _bwd_in(dz, x, dxo, vecs, w, l, nl, tt, gw_all=None):
    t = x.shape[0]
    nt = t // tt

    def body(dz_ref, x_ref, dxo_ref, v_ref, w_any, *rest):
        dx_ref, gw_any, sums_ref, w_vmem, gw_acc, sem = rest[-6:]
        i = pl.program_id(0)

        @pl.when(i == 0)
        def _():
            cp = pltpu.make_async_copy(w_any.at[0], w_vmem, sem)
            cp.start()
            cp.wait()
            gw_acc[...] = jnp.zeros_like(gw_acc)
            sums_ref[...] = jnp.zeros_like(sums_ref)

        vecs_v = v_ref[...]
        r, xhat, hn = _normed(x_ref[...], vecs_v)
        one_scale = 1.0 + vecs_v[1:2, :]
        hb = (hn * one_scale + vecs_v[0:1, :]).astype(BF16)
        dh = None
        for j in range(N_CHIP):
            dzj = dz_ref[:, PIECE * j:PIECE * (j + 1)]
            part = _dot_nt(dzj, w_vmem[j])
            dh = part if dh is None else dh + part
            gw_acc[j] += _dot_tn(hb, dzj)
        sums_ref[0:1, :] += _rowsum(dh)
        sums_ref[1:2, :] += _rowsum(dh * hn)
        sums_ref[2:3, :] += _rowsum(dh * one_scale * xhat)
        dxh = dh * (vecs_v[3:4, :] * one_scale)
        dx_ref[...] = dxo_ref[...] + r * (dxh - xhat * jnp.mean(dxh * xhat, axis=-1, keepdims=True))

        @pl.when(i == nt - 1)
        def _():
            cp = pltpu.make_async_copy(gw_acc, gw_any.at[l], sem)
            cp.start()
            cp.wait()

    carried = [] if gw_all is None else [gw_all]
    return pl.pallas_call(
        body, name=f"bwd_in_{l}", grid=(nt,),
        in_specs=[pl.BlockSpec((tt, DIN), lambda i: (i, 0)), pl.BlockSpec((tt, D), lambda i: (i, 0)),
                  pl.BlockSpec((tt, D), lambda i: (i, 0)), pl.BlockSpec((8, D), lambda i: (0, 0)),
                  pl.BlockSpec(memory_space=pl.ANY)] + [pl.BlockSpec(memory_space=pl.ANY)] * len(carried),
        out_specs=(pl.BlockSpec((tt, D), lambda i: (i, 0)), pl.BlockSpec(memory_space=pl.ANY),
                   pl.BlockSpec((8, D), lambda i: (0, 0))),
        out_shape=(jax.ShapeDtypeStruct((t, D), F32), jax.ShapeDtypeStruct((nl, N_CHIP, D, PIECE), F32),
                   jax.ShapeDtypeStruct((8, D), F32)),
        scratch_shapes=[pltpu.VMEM((N_CHIP, D, PIECE), BF16), pltpu.VMEM((N_CHIP, D, PIECE), F32),
                        pltpu.SemaphoreType.DMA],
        input_output_aliases={5: 1} if carried else {},
        compiler_params=_params(("arbitrary",)),
    )(dz, x, dxo, vecs, w, *carried)


def _col(ref, k):
    return ref[:, GW * k:GW * (k + 1)]


def _lane_group():
    return lax.broadcasted_iota(jnp.int32, (1, GW), 1) // (GW // 4)


def _pool_window(group):
    return jnp.where(group == 0, 2.0, jnp.where(group == 1, 4.0, jnp.where(group == 2, 8.0, 16.0)))


def _by_group(group, a, b, c, d):
    return jnp.where(group == 0, a, jnp.where(group == 1, b, jnp.where(group == 2, c, d)))


def _layer_norm(x, g, b):
    mu = jnp.mean(x, axis=-1, keepdims=True)
    xc = x - mu
    rstd = lax.rsqrt(jnp.mean(xc * xc, axis=-1, keepdims=True) + EPS)
    xh = xc * rstd
    return xh * g + b, xh, rstd


def _layer_norm_grad(d_out, xh, rstd, g):
    dxh = d_out * g
    return rstd * (dxh - jnp.mean(dxh, axis=-1, keepdims=True) - xh * jnp.mean(dxh * xh, axis=-1, keepdims=True))


def _shifted_copies(ext_ref, sh_ref, n):
    for b in range(1, 8):
        sh_ref[b, 0:n - 8, :] = ext_ref[b:b + n - 8, :]


def _rows_at(ext_ref, sh_ref, start, rows):
    b = start % 8
    if b == 0 or sh_ref is None:
        return ext_ref[start:start + rows, :]
    return sh_ref[b, start - b:start - b + rows, :]


def _conv_taps(ext_ref, w_ref, n_taps, first_row, tm, out_ref, flip=False, sh_ref=None):
    for rb in range(0, tm, CONV_ROWS):
        acc = None
        for k in range(n_taps):
            wk = n_taps - 1 - k if flip else k
            term = w_ref[wk:wk + 1, :] * _rows_at(ext_ref, sh_ref, first_row + rb + k, CONV_ROWS)
            acc = term if acc is None else acc + term
        out_ref[rb:rb + CONV_ROWS, :] = acc


def _mix_forward(zc, zp, first, row0, tm, p, s, after, conv_c=None):
    keep = jnp.where(first, 0.0, 1.0)
    group = _lane_group()
    a_b, a_c, a_x, a_g = _col(zc, 0), _col(zc, 1), _col(zc, 2), _col(zc, 3)
    s['ua'][0:HALO, :] = _col(zp, 1) * _col(zp, 2) * keep
    s['ua'][HALO:HALO + tm, :] = a_c * a_x
    _conv_taps(s['ua'], p['wa'], CONV_A, HALO - (CONV_A - 1), tm, s['cv'])
    cv = s['cv'][...]
    sg_a = _sig(a_g)
    f = dict(a_b=a_b, a_c=a_c, a_x=a_x, a_g=a_g, cv=cv, sg_a=sg_a)
    after(0, f, (a_b * cv * (a_g * sg_a)).astype(BF16))
    b_p, b_g = _col(zc, 4), _col(zc, 5)
    s['p0'][0:HALO, :] = _col(zp, 4) * keep
    s['p0'][HALO:HALO + tm, :] = b_p
    n = HALO + tm
    s['p1'][8:n, :] = s['p0'][8:n, :] + s['p0'][7:n - 1, :]
    w2 = s['p1'][HALO:n, :]
    s['p0'][16:n, :] = s['p1'][16:n, :] + s['p1'][14:n - 2, :]
    w4 = s['p0'][HALO:n, :]
    s['p1'][24:n, :] = s['p0'][24:n, :] + s['p0'][20:n - 4, :]
    w8 = s['p1'][HALO:n, :]
    w16 = w8 + s['p1'][HALO - 8:n - 8, :]
    tpos = (row0 + lax.broadcasted_iota(jnp.int32, (tm, 1), 0) + 1).astype(F32)
    inv_cnt = 1.0 / jnp.minimum(tpos, _pool_window(group))
    pooled = (_by_group(group, w2, w4, w8, w16) * inv_cnt - b_p).astype(BF16)
    q = _dot(pooled, p['wbd'][...])
    sg_b = _sig(b_g)
    f = dict(b_g=b_g, pooled=pooled, q=q, sg_b=sg_b, inv_cnt=inv_cnt)
    after(1, f, (q * p['v256'][0:1, :] * (b_g * sg_b)).astype(BF16))
    c_a, c_gl, c_g = _col(zc, 6), _col(zc, 7), _col(zc, 8)
    sg_gl = _sig(c_gl)
    zp_gl = _col(zp, 7)
    s['hc'][0:HALO, :] = _col(zp, 6) * _sig(zp_gl) * keep
    s['hc'][HALO:HALO + tm, :] = c_a * sg_gl
    _shifted_copies(s['hc'], s['hcs'], HALO + tm)
    if conv_c is None:
        _conv_taps(s['hc'], p['wdw'], CONV_C, HALO - (CONV_C - 1), tm, s['co'], sh_ref=s['hcs'])
        conv_c = s['co']
    conv_v = conv_c[...]
    co = conv_v + p['v256'][1:2, :]
    ln_c, xh_c, rstd_c = _layer_norm(co, p['v256'][2:3, :], p['v256'][3:4, :])
    sg_ln = _sig(ln_c)
    sc = (ln_c * sg_ln).astype(BF16)
    pw = _dot(sc, p['wpw'][...]) + p['v256'][4:5, :]
    sg_c = _sig(c_g)
    f = dict(c_a=c_a, c_g=c_g, sg_gl=sg_gl, ln_c=ln_c, xh_c=xh_c, rstd_c=rstd_c, sg_ln=sg_ln, sc=sc, pw=pw,
             sg_c=sg_c, conv_c=conv_v)
    after(2, f, (pw * (c_g * sg_c)).astype(BF16))
    d_u, d_v, d_g = _col(zc, 9), _col(zc, 10), _col(zc, 11)
    gu, th_u = _gelu(d_u)
    gv, th_v = _gelu(d_v)
    v, xh_d, rstd_d = _layer_norm(gv, p['v256'][5:6, :], p['v256'][6:7, :])
    vb = v.astype(BF16)
    parts = []
    for ch in range(tm // CHUNK):
        vc = vb[ch * CHUNK:(ch + 1) * CHUNK, :]
        mixed = p['bsd'][...]
        for h in range(4):
            mixed = mixed + jnp.where(group == h, _dot(p['ws'][h], vc), 0.0)
        parts.append(mixed)
    mx = parts[0] if len(parts) == 1 else jnp.concatenate(parts, axis=0)
    sg_d = _sig(d_g)
    f = dict(d_u=d_u, d_v=d_v, d_g=d_g, gu=gu, th_u=th_u, th_v=th_v, xh_d=xh_d, rstd_d=rstd_d, vb=vb, mx=mx,
             sg_d=sg_d)
    after(3, f, (gu * mx * (d_g * sg_d)).astype(BF16))


def _mix_scratch(tm):
    ext = pltpu.VMEM((HALO + tm, GW), F32)
    tile = pltpu.VMEM((tm, GW), F32)
    return dict(ua=ext, cv=tile, p0=ext, p1=ext, hc=ext, co=tile, hcs=pltpu.VMEM((8, HALO + tm, GW), F32))


_MIX_PARAMS = ('wa', 'wdw', 'v256', 'wbd', 'wpw', 'ws', 'bsd', 'wout')


def _mix_param_specs(l):
    def whole(*shape, at=l):
        nd = len(shape)
        return pl.BlockSpec((None,) + shape, lambda i, _nd=nd: (at,) + (0,) * _nd)
    return [whole(8, GW), whole(32, GW), whole(16, GW), whole(GW, GW), whole(GW, GW), whole(4, CHUNK, CHUNK),
            whole(CHUNK, GW), whole(N_CHIP, GW, D, at=0)]


def _mix_param_arrays(mp, w_out_l):
    return [mp[k] for k in _MIX_PARAMS[:-1]] + [w_out_l]


def _fwd_layer(x, vecs, w, w_out_l, mp, l, tm):
    t = x.shape[0]
    nt = t // tm
    names = list(_mix_scratch(tm))
    n_p = len(_MIX_PARAMS)

    def body(xm_ref, xr_ref, v_ref, w_ref, *rest):
        p = dict(zip(_MIX_PARAMS, rest[:n_p]))
        z_ref, xn_ref, y_ref, conv_ref = rest[n_p:n_p + 4]
        s = dict(zip(names, rest[n_p + 4:n_p + 4 + len(names)]))
        z_next, z_cur, z_halo = rest[n_p + 4 + len(names):]
        j = pl.program_id(0)

        @pl.when(j == 0)
        def _():
            z_cur[...] = jnp.zeros_like(z_cur)
            z_halo[...] = jnp.zeros_like(z_halo)

        vecs_v = v_ref[...]
        _, _, hn = _normed(xm_ref[...], vecs_v)
        hb = (hn * (1.0 + vecs_v[1:2, :]) + vecs_v[0:1, :]).astype(BF16)
        for k in range(N_CHIP):
            part = _dot(hb, w_ref[k])
            z_ref[:, PIECE * k:PIECE * (k + 1)] = part
            z_next[:, PIECE * k:PIECE * (k + 1)] = part

        tile = jnp.maximum(j - 1, 0)
        mixed = []
        _mix_forward(z_cur, z_halo, tile == 0, tile * tm, tm, p, s, lambda k, f, yk: mixed.append((f, yk)))
        y = None
        for k, (_, yk) in enumerate(mixed):
            part = _dot(yk, p['wout'][k])
            y = part if y is None else y + part
        y_ref[...] = y
        xn_ref[...] = xr_ref[...] + vecs_v[2:3, :] * y
        conv_ref[...] = mixed[2][0]['conv_c']

        z_halo[...] = z_cur[tm - HALO:tm, :]
        z_cur[...] = z_next[...]

    def ahead(j):
        return jnp.minimum(j, nt - 1)

    def behind(j):
        return jnp.maximum(j - 1, 0)

    return pl.pallas_call(
        body, name=f"fwd_layer_{l}", grid=(nt + 1,),
        in_specs=[pl.BlockSpec((tm, D), lambda j: (ahead(j), 0)), pl.BlockSpec((tm, D), lambda j: (behind(j), 0)),
                  pl.BlockSpec((8, D), lambda j: (0, 0)),
                  pl.BlockSpec((None, N_CHIP, D, PIECE), lambda j: (0, 0, 0, 0))] + _mix_param_specs(l),
        out_specs=(pl.BlockSpec((tm, DIN), lambda j: (ahead(j), 0)), pl.BlockSpec((tm, D), lambda j: (behind(j), 0)),
                   pl.BlockSpec((tm, D), lambda j: (behind(j), 0)), pl.BlockSpec((tm, GW), lambda j: (behind(j), 0))),
        out_shape=(jax.ShapeDtypeStruct((t, DIN), F32), jax.ShapeDtypeStruct((t, D), F32),
                   jax.ShapeDtypeStruct((t, D), F32), jax.ShapeDtypeStruct((t, GW), F32)),
        scratch_shapes=list(_mix_scratch(tm).values())
        + [pltpu.VMEM((tm, DIN), F32), pltpu.VMEM((tm, DIN), F32), pltpu.VMEM((HALO, DIN), F32)],
        compiler_params=_params(("arbitrary",)),
    )(x, x, vecs, w, *_mix_param_arrays(mp, w_out_l))


def _bwd_mix(dxo, y, conv, z, vecs, mp, w_out_l, wst, l, tm, gwout_all=None):
    t = dxo.shape[0]
    nt = t // tm
    nl = wst.shape[0]
    carried = [] if gwout_all is None else [gwout_all]
    per_halo = tm // HALO
    fwd_names = list(_mix_scratch(tm))
    ext = pltpu.VMEM((tm + HALO, GW), F32)
    carry = pltpu.VMEM((HALO, GW), F32)
    tile = pltpu.VMEM((tm, GW), F32)
    bwd_scratch = dict(ea=ext, eb=ext, eb2=ext, ec=ext, ca=carry, cb=carry, cc=carry, dua=tile, dhc=tile,
                       gbacc=pltpu.VMEM((CHUNK, GW), F32), ecs=pltpu.VMEM((8, tm + HALO, GW), F32),
                       dys=pltpu.VMEM((1, tm, GW), F32))
    bwd_names = list(bwd_scratch)
    n_p = len(_MIX_PARAMS)

    def body(dxo_ref, y_ref, conv_ref, zc, zp, v_ref, *rest):
        p = dict(zip(_MIX_PARAMS, rest[:n_p]))
        wst_ref = rest[n_p]
        first_out = n_p + 1 + len(carried)
        dz_ref, gwout, gwbd, gwpw, gws, gbs, gwdw, g256, g1024 = rest[first_out:first_out + 9]
        s = dict(zip(fwd_names + bwd_names, rest[first_out + 9:]))
        i = pl.program_id(0)
        ti = nt - 1 - i
        group = _lane_group()

        @pl.when(i == 0)
        def _():
            for ref in (gwout, gwbd, gwpw, gws, gbs, gwdw, g256, g1024, s['ca'], s['cb'], s['cc'], s['gbacc']):
                ref[...] = jnp.zeros_like(ref)

        def put(k, val):
            dz_ref[:, GW * k:GW * (k + 1)] = val.astype(BF16)

        def bwd_a(f, dya):
            taps = [None] * CONV_A
            for r in range(0, tm, ROW_BLOCK):
                rs = slice(r, r + ROW_BLOCK)
                a_b, a_g = zc[rs, 0:GW], zc[rs, 3 * GW:4 * GW]
                dy_a, cv = s['dys'][0, rs, :], s['cv'][rs, :]
                sg = _sig(a_g)
                silu = a_g * sg
                t = dy_a * cv
                dz_ref[rs, 0:GW] = (t * silu).astype(BF16)
                dz_ref[rs, 3 * GW:4 * GW] = (t * a_b * (sg * (1.0 + a_g * (1.0 - sg)))).astype(BF16)
                d_cv = dy_a * a_b * silu
                s['ea'][rs, :] = d_cv
                for k in range(CONV_A):
                    first = HALO - (CONV_A - 1) + k + r
                    term = d_cv * s['ua'][first:first + ROW_BLOCK, :]
                    taps[k] = term if taps[k] is None else taps[k] + term
            for k in range(CONV_A):
                g256[8 + k:9 + k, :] += _rowsum(taps[k])
            s['ea'][tm:tm + HALO, :] = s['ca'][...]
            s['ca'][...] = s['ea'][0:HALO, :]
            _conv_taps(s['ea'], p['wa'], CONV_A, 0, tm, s['dua'], flip=True)
            for r in range(0, tm, ROW_BLOCK):
                rs = slice(r, r + ROW_BLOCK)
                d_u = s['dua'][rs, :]
                dz_ref[rs, GW:2 * GW] = (d_u * zc[rs, 2 * GW:3 * GW]).astype(BF16)
                dz_ref[rs, 2 * GW:3 * GW] = (d_u * zc[rs, GW:2 * GW]).astype(BF16)

        def bwd_b(f, dyb):
            b_g, q, sg_b, inv_cnt = f['b_g'], f['q'], f['sg_b'], f['inv_cnt']
            scale_b = p['v256'][0:1, :]
            silu_b = b_g * sg_b
            g256[0:1, :] += _rowsum(dyb * q * silu_b)
            put(5, dyb * q * scale_b * (sg_b * (1.0 + b_g * (1.0 - sg_b))))
            d_q = (dyb * scale_b * silu_b).astype(BF16)
            gwbd[...] += _dot_tn(f['pooled'], d_q)
            d_pooled = _dot_nt(d_q, p['wbd'][...])
            gp = d_pooled * inv_cnt
            n = tm + HALO
            s['eb'][0:tm, :] = gp
            s['eb'][tm:n, :] = s['cb'][...]
            s['cb'][...] = gp[0:HALO, :]
            s['eb2'][0:n - 8, :] = s['eb'][0:n - 8, :] + s['eb'][1:n - 7, :]
            r2 = s['eb2'][0:tm, :]
            s['eb'][0:n - 16, :] = s['eb2'][0:n - 16, :] + s['eb2'][2:n - 14, :]
            r4 = s['eb'][0:tm, :]
            s['eb2'][0:n - 24, :] = s['eb'][0:n - 24, :] + s['eb'][4:n - 20, :]
            r8 = s['eb2'][0:tm, :]
            r16 = r8 + s['eb2'][8:tm + 8, :]
            put(4, _by_group(group, r2, r4, r8, r16) - d_pooled)

        def bwd_c(f, dyc):
            c_a, c_g, sg_gl, ln_c, xh_c, rstd_c, sg_ln, pw, sg_c = (
                f['c_a'], f['c_g'], f['sg_gl'], f['ln_c'], f['xh_c'], f['rstd_c'], f['sg_ln'], f['pw'], f['sg_c'])
            put(8, dyc * pw * (sg_c * (1.0 + c_g * (1.0 - sg_c))))
            d_pw = dyc * (c_g * sg_c)
            g256[4:5, :] += _rowsum(d_pw)
            d_pwb = d_pw.astype(BF16)
            gwpw[...] += _dot_tn(f['sc'], d_pwb)
            d_ln = _dot_nt(d_pwb, p['wpw'][...]) * (sg_ln * (1.0 + ln_c * (1.0 - sg_ln)))
            g256[2:3, :] += _rowsum(d_ln * xh_c)
            g256[3:4, :] += _rowsum(d_ln)
            d_co = _layer_norm_grad(d_ln, xh_c, rstd_c, p['v256'][2:3, :])
            g256[1:2, :] += _rowsum(d_co)
            for k in range(CONV_C):
                gwdw[k:k + 1, :] += _rowsum(d_co * _rows_at(s['hc'], s['hcs'], HALO - (CONV_C - 1) + k, tm))
            s['ec'][0:tm, :] = d_co
            s['ec'][tm:tm + HALO, :] = s['cc'][...]
            s['cc'][...] = d_co[0:HALO, :]
            _shifted_copies(s['ec'], s['ecs'], tm + HALO)
            _conv_taps(s['ec'], p['wdw'], CONV_C, 0, tm, s['dhc'], flip=True, sh_ref=s['ecs'])
            d_hc = s['dhc'][...]
            put(6, d_hc * sg_gl)
            put(7, d_hc * c_a * sg_gl * (1.0 - sg_gl))

        def bwd_d(f, dyd):
            d_uu, d_vv, d_g, gu, th_u, th_v, xh_d, rstd_d, vb, mx, sg_d = (
                f['d_u'], f['d_v'], f['d_g'], f['gu'], f['th_u'], f['th_v'], f['xh_d'], f['rstd_d'], f['vb'],
                f['mx'], f['sg_d'])
            silu_d = d_g * sg_d
            put(9, dyd * mx * silu_d * _gelu_grad(d_uu, th_u))
            put(11, dyd * gu * mx * (sg_d * (1.0 + d_g * (1.0 - sg_d))))
            d_mx = dyd * gu * silu_d
            dv_parts = []
            gb = None
            for ch in range(tm // CHUNK):
                dmc = d_mx[ch * CHUNK:(ch + 1) * CHUNK, :]
                gb = dmc if gb is None else gb + dmc
                dmb = dmc.astype(BF16)
                vc = vb[ch * CHUNK:(ch + 1) * CHUNK, :]
                dvc = None
                for h in range(4):
                    gws[h] += _dot_nt(jnp.where(group == h, dmb, jnp.zeros_like(dmb)), vc)
                    part = jnp.where(group == h, _dot(wst_ref[h], dmb), 0.0)
                    dvc = part if dvc is None else dvc + part
                dv_parts.append(dvc)
            s['gbacc'][...] += gb
            d_v = dv_parts[0] if len(dv_parts) == 1 else jnp.concatenate(dv_parts, axis=0)
            g256[5:6, :] += _rowsum(d_v * xh_d)
            g256[6:7, :] += _rowsum(d_v)
            d_gv = _layer_norm_grad(d_v, xh_d, rstd_d, p['v256'][5:6, :])
            put(10, d_gv * _gelu_grad(d_vv, th_v))

        mixed = []
        _mix_forward(zc, zp, ti == 0, ti * tm, tm, p, s, lambda j, f, yj: mixed.append((f, yj)), conv_c=conv_ref)
        dxo_v = dxo_ref[...]
        g1024[0:1, :] += _rowsum(dxo_v * y_ref[...])
        dy = (dxo_v * v_ref[2:3, :]).astype(BF16)
        s['dys'][0] = _dot_nt(dy, p['wout'][0])
        dys = [None] + [_dot_nt(dy, p['wout'][j]) for j in range(1, 4)]
        for j, (f, yj) in enumerate(mixed):
            gwout[j] += _dot_tn(yj, dy)
        for (f, _), backward, dyj in zip(mixed, (bwd_a, bwd_b, bwd_c, bwd_d), dys):
            backward(f, dyj)

        @pl.when(i == nt - 1)
        def _():
            row = lax.broadcasted_iota(jnp.int32, (CHUNK, CHUNK), 0)
            col = lax.broadcasted_iota(jnp.int32, (CHUNK, CHUNK), 1)
            for h in range(4):
                gws[h] = jnp.where(col <= row, gws[h], 0.0)
            head_of_lane = lax.broadcasted_iota(jnp.int32, (GW, CHUNK), 0) // (GW // 4)
            sel = jnp.where(head_of_lane == lax.broadcasted_iota(jnp.int32, (GW, CHUNK), 1), 1.0, 0.0)
            gbs[...] = jnp.dot(s['gbacc'][...], sel, preferred_element_type=F32, precision=lax.Precision.HIGHEST)

    def const(*shape):
        nd = len(shape)
        return pl.BlockSpec(shape, lambda i, _nd=nd: (0,) * _nd)

    def rev(i):
        return nt - 1 - i

    out_shapes = (jax.ShapeDtypeStruct((t, DIN), BF16), jax.ShapeDtypeStruct((nl, N_CHIP, GW, D), F32),
                  jax.ShapeDtypeStruct((GW, GW), F32), jax.ShapeDtypeStruct((GW, GW), F32),
                  jax.ShapeDtypeStruct((4, CHUNK, CHUNK), F32), jax.ShapeDtypeStruct((CHUNK, CHUNK), F32),
                  jax.ShapeDtypeStruct((32, GW), F32), jax.ShapeDtypeStruct((16, GW), F32),
                  jax.ShapeDtypeStruct((8, D), F32))
    out_specs = (pl.BlockSpec((tm, DIN), lambda i: (rev(i), 0)),
                 pl.BlockSpec((None, N_CHIP, GW, D), lambda i: (l, 0, 0, 0)), const(GW, GW),
                 const(GW, GW), const(4, CHUNK, CHUNK), const(CHUNK, CHUNK), const(32, GW), const(16, GW),
                 const(8, D))
    scratch = list(_mix_scratch(tm).values()) + list(bwd_scratch.values())
    n_in = 6 + n_p + 1
    return pl.pallas_call(
        body, name=f"bwd_mix_{l}", grid=(nt,),
        in_specs=[pl.BlockSpec((tm, D), lambda i: (rev(i), 0)), pl.BlockSpec((tm, D), lambda i: (rev(i), 0)),
                  pl.BlockSpec((tm, GW), lambda i: (rev(i), 0)), pl.BlockSpec((tm, DIN), lambda i: (rev(i), 0)),
                  pl.BlockSpec((HALO, DIN), lambda i: (jnp.maximum(rev(i) * per_halo - 1, 0), 0)),
                  pl.BlockSpec((8, D), lambda i: (0, 0))]
        + _mix_param_specs(l)
        + [pl.BlockSpec((None, 4, CHUNK, CHUNK), lambda i: (l, 0, 0, 0))]
        + [pl.BlockSpec(memory_space=pl.ANY)] * len(carried),
        out_specs=out_specs, out_shape=out_shapes, scratch_shapes=scratch,
        input_output_aliases={n_in: 1} if carried else {},
        compiler_params=_params(("arbitrary",)),
    )(dxo, y, conv, z, z, vecs, *_mix_param_arrays(mp, w_out_l), wst, *carried)


def _final(x, target, fg, tt):
    t = x.shape[0]
    nt = t // tt

    def body(x_ref, t_ref, g_ref, dx_ref, sums_ref):
        i = pl.program_id(0)

        @pl.when(i == 0)
        def _():
            sums_ref[...] = jnp.zeros_like(sums_ref)

        xv = x_ref[...]
        g = g_ref[0:1, :]
        r = lax.rsqrt(jnp.mean(xv * xv, axis=-1, keepdims=True) + EPS)
        xhat = xv * r
        err = xhat * g - t_ref[...]
        sums_ref[1:2, :] += _rowsum(err * err) * (0.5 / D)
        dyv = err * (1.0 / D)
        sums_ref[0:1, :] += _rowsum(dyv * xhat)
        dxh = dyv * g
        dx_ref[...] = r * (dxh - xhat * jnp.mean(dxh * xhat, axis=-1, keepdims=True))

        @pl.when(i == nt - 1)
        def _():
            sums_ref[2:3, :] = jnp.broadcast_to(jnp.sum(sums_ref[1:2, :], axis=-1, keepdims=True), (1, D))

    return pl.pallas_call(
        body, name="final_loss", grid=(nt,),
        in_specs=[pl.BlockSpec((tt, D), lambda i: (i, 0)), pl.BlockSpec((tt, D), lambda i: (i, 0)),
                  pl.BlockSpec((8, D), lambda i: (0, 0))],
        out_specs=(pl.BlockSpec((tt, D), lambda i: (i, 0)), pl.BlockSpec((8, D), lambda i: (0, 0))),
        out_shape=(jax.ShapeDtypeStruct((t, D), F32), jax.ShapeDtypeStruct((8, D), F32)),
        compiler_params=_params(("arbitrary",)),
    )(x, target, fg)


def _adamw(w, g, m, v, tag):
    rows, cols = w.shape
    rt = rows
    for cand in (512, 256, 128, 64, 32, 16, 8):
        if rows % cand == 0:
            rt = cand
            break

    def body(w_ref, g_ref, m_ref, v_ref, d_ref, nm_ref, nv_ref):
        d_ref[...], nm_ref[...], nv_ref[...] = _adamw_update(g_ref[...], w_ref[...], m_ref[...], v_ref[...])

    spec = pl.BlockSpec((rt, cols), lambda i: (i, 0))
    return pl.pallas_call(
        body, name=f"adamw_{tag}", grid=(rows // rt,), in_specs=[spec] * 4, out_specs=(spec,) * 3,
        out_shape=(jax.ShapeDtypeStruct(w.shape, F32),) * 3, compiler_params=_params(("parallel",)),
    )(w, g, m, v)


def _adamw_update(gv, w, m, v):
    nm = ADAM_B1 * m + (1.0 - ADAM_B1) * gv
    nv = ADAM_B2 * v + (1.0 - ADAM_B2) * (gv * gv)
    m_hat = nm / (1.0 - ADAM_B1 ** ADAM_STEP)
    v_hat = nv / (1.0 - ADAM_B2 ** ADAM_STEP)
    return -ADAM_LR * (m_hat / (jnp.sqrt(v_hat) + ADAM_EPS) + ADAM_WD * w), nm, nv


def _adamw_many(ws, gs, ms, vs):
    n = len(ws)

    def body(*refs):
        w_refs, g_refs, m_refs, v_refs = (refs[k * n:(k + 1) * n] for k in range(4))
        d_refs, nm_refs, nv_refs = (refs[(4 + k) * n:(5 + k) * n] for k in range(3))
        for k in range(n):
            d_refs[k][...], nm_refs[k][...], nv_refs[k][...] = _adamw_update(
                g_refs[k][...], w_refs[k][...], m_refs[k][...], v_refs[k][...])

    shapes = tuple(jax.ShapeDtypeStruct(w.shape, F32) for w in ws)
    out = pl.pallas_call(body, name="adamw_small", out_shape=shapes * 3)(*ws, *gs, *ms, *vs)
    return out[:n], out[n:2 * n], out[2 * n:]


def _adamw_halves(w, mine, theirs, m, v, cidx, tag):
    nl, r, cdim = w.shape
    r2 = r // 2
    rt = min(r2, 512)
    nrt = r2 // rt

    def body(c_ref, a_ref, b_ref, w_ref, m_ref, v_ref, g_ref, d_ref, nm_ref, nv_ref):
        gv = jnp.where(pl.program_id(1) == c_ref[0], a_ref[...], b_ref[...])
        g_ref[...] = gv
        d_ref[...], nm_ref[...], nv_ref[...] = _adamw_update(gv, w_ref[...], m_ref[...], v_ref[...])

    half = pl.BlockSpec((None, rt, cdim), lambda l, h, i, c: (l, i, 0))
    whole = pl.BlockSpec((None, rt, cdim), lambda l, h, i, c: (l, h * nrt + i, 0))
    grid_spec = pltpu.PrefetchScalarGridSpec(
        num_scalar_prefetch=1, grid=(nl, 2, nrt), in_specs=[half, half, whole, whole, whole],
        out_specs=(whole,) * 4)
    return pl.pallas_call(
        body, name=f"adamw_{tag}", grid_spec=grid_spec,
        out_shape=(jax.ShapeDtypeStruct(w.shape, F32),) * 4,
        compiler_params=_params(("parallel", "parallel", "parallel")),
    )(cidx, mine, theirs, w, m, v)


def _pack(arrays, row_multiple=8):
    parts, offsets, off = [], [], 0
    for a in arrays:
        n = int(np.prod(a.shape))
        padded = -(-n // 1024) * 1024
        flat = a.reshape(-1).astype(F32)
        if padded != n:
            flat = jnp.concatenate([flat, jnp.zeros((padded - n,), F32)])
        parts.append(flat.reshape(-1, 128))
        offsets.append((off // 128, n, a.shape))
        off += padded
    tail = -off % (row_multiple * 128)
    if tail:
        parts.append(jnp.zeros((tail // 128, 128), F32))
    return jnp.concatenate(parts, axis=0), offsets


def _unpack(buf, offsets):
    return [buf[row:row + -(-n // 128)].reshape(-1)[:n].reshape(shape) for row, n, shape in offsets]


def _pad_rows(a, rows):
    return jnp.concatenate([a, jnp.zeros((rows - a.shape[0],) + a.shape[1:], a.dtype)], axis=0)


def kernel(x, c, norm_g, w_ada, b_ada, w_in, w_conv_a, w_pool, pool_scale, w_dw_c, b_dw_c, ln_g_c, ln_b_c, w_pw2_c, b_pw2_c, ln_g_d, ln_b_d, w_s_d, b_s_d, w_out, final_g, loss_target, m_norm_g, m_w_ada, m_b_ada, m_w_in, m_w_conv_a, m_w_pool, m_pool_scale, m_w_dw_c, m_b_dw_c, m_ln_g_c, m_ln_b_c, m_w_pw2_c, m_b_pw2_c, m_ln_g_d, m_ln_b_d, m_w_s_d, m_b_s_d, m_w_out, m_final_g, v_norm_g, v_w_ada, v_b_ada, v_w_in, v_w_conv_a, v_w_pool, v_pool_scale, v_w_dw_c, v_b_dw_c, v_ln_g_c, v_ln_b_c, v_w_pw2_c, v_b_pw2_c, v_ln_g_d, v_ln_b_d, v_w_s_d, v_b_s_d, v_w_out, v_final_g):
    env = dict(locals())
    weights = {n: env[n] for n in WEIGHTS}
    mom_m = {n: env["m_" + n] for n in WEIGHTS}
    mom_v = {n: env["v_" + n] for n in WEIGHTS}
    nl = norm_g.shape[0]
    t = x.shape[1]
    tt, tm = _tiles(t)
    ax, ay, ac = lax.axis_index("x"), lax.axis_index("y"), lax.axis_index("c")
    chip = 2 * ax + ay
    dev = 2 * chip + ac
    cidx = jnp.reshape(ac, (1,)).astype(jnp.int32)
    sidx = jnp.reshape(chip, (1,)).astype(jnp.int32)
    xs, target = x[0], loss_target[0]

    shard_small, shard_off = _pack([c, w_conv_a, w_dw_c, w_pw2_c])
    gathered = _allgather8(shard_small, "gather_small").reshape(N_DEV, -1, 128)
    per_dev = [_unpack(gathered[d], shard_off) for d in range(0, N_DEV, 2)]
    c_all = jnp.concatenate([u[0] for d in range(N_DEV)
                             for u in [_unpack(gathered[d], shard_off[:1])]], axis=0)
    w_conv_full = jnp.concatenate([u[1] for u in per_dev], axis=-1)
    w_dw_full = jnp.concatenate([u[2] for u in per_dev], axis=-1)
    w_pw2_full = jnp.concatenate([u[3] for u in per_dev], axis=1)

    b_ada_s = lax.dynamic_slice_in_dim(b_ada, chip * PIECE, PIECE, axis=1)[:, None, :]
    mod_s = _ada_forward(c_all, w_ada, b_ada_s)
    mod_all = _allgather8(mod_s.reshape(nl * N_DEV, PIECE), "gather_mod").reshape(N_DEV, nl, N_DEV, PIECE)
    mod_rows = lax.dynamic_index_in_dim(mod_all, dev, axis=2, keepdims=False)
    mod = jnp.concatenate([mod_rows[2 * s] for s in range(N_CHIP)], axis=-1)

    w_in_layers, w_out_layers = (list(v) for v in zip(_gather_weights(w_in[:1], w_out[:1])))
    if nl > 1:
        w_in_b, w_out_b = _cast_bf16([w_in[1:], w_out[1:]], after=w_out_layers[0])

    tril = jnp.tril(jnp.ones((CHUNK, CHUNK), bool))
    ws_masked = jnp.where(tril[None, None], w_s_d, 0.0)
    wbd = jnp.zeros((nl, GW, GW), F32)
    for g in range(4):
        wbd = wbd.at[:, 64 * g:64 * (g + 1), 64 * g:64 * (g + 1)].set(w_pool[:, g])
    v256 = jnp.stack([pool_scale, b_dw_c, ln_g_c, ln_b_c, b_pw2_c, ln_g_d, ln_b_d], axis=1)
    mp = dict(
        wa=_pad_rows(jnp.swapaxes(w_conv_full, 0, 1), 8).swapaxes(0, 1),
        wdw=_pad_rows(jnp.swapaxes(w_dw_full, 0, 1), 32).swapaxes(0, 1),
        v256=_pad_rows(jnp.swapaxes(v256, 0, 1), 16).swapaxes(0, 1),
        wbd=wbd.astype(BF16), wpw=w_pw2_full.astype(BF16), ws=ws_masked.astype(BF16),
        bsd=jnp.repeat(jnp.swapaxes(b_s_d, 1, 2), GW // 4, axis=2))
    wst = jnp.swapaxes(ws_masked, 2, 3).astype(BF16)

    def vec_rows(l):
        rows = jnp.stack([mod[l, 0:D], mod[l, D:2 * D], mod[l, 2 * D:3 * D], norm_g[l]], axis=0)
        return _pad_rows(rows, 8)

    xin, zs, ys, convs, vecs = [xs], [], [], [], []
    zero = jnp.zeros((), jnp.int32)
    for l in range(nl):
        vecs.append(vec_rows(l))
        started, token = None, 0.0
        if l + 1 < nl:
            started = _gather_shards_start(w_in_b[l:l + 1], w_out_b[l:l + 1])
            token = started[-1][0, 0]
        zl, xn, yl, cl = _fwd_layer(xin[l], vecs[l] + token, w_in_layers[l], w_out_layers[l], mp, l, tm)
        zs.append(zl)
        convs.append(cl)
        xin.append(xn)
        ys.append(yl)
        if started is not None:
            own_in, own_out, land_in, land_out = _gather_shards_wait(*started[:6], cl)
            w_in_layers.append(lax.dynamic_update_slice(land_in, own_in[:, None], (zero, chip, zero, zero)))
            w_out_layers.append(lax.dynamic_update_slice(land_out, own_out[:, None], (zero, chip, zero, zero)))
    dx, fin_sums = _final(xin[nl], target, _pad_rows(final_g[None, :], 8), tt)

    g_in, g_out, small, dmod = None, None, [None] * nl, [None] * nl
    for l in reversed(range(nl)):
        dz, g_out, gwbd, gwpw, gws, gbs, gwdw, g256, g1024 = _bwd_mix(
            dx, ys[l], convs[l], zs[l], vecs[l], mp, w_out_layers[l], wst, l, tm, gwout_all=g_out)
        dx, g_in, sums = _bwd_in(dz, xin[l], dx, vecs[l], w_in_layers[l], l, nl, tt, gw_all=g_in)
        dmod[l] = jnp.concatenate([sums[0], sums[1], g1024[0]])
        small[l] = dict(
            norm_g=sums[2], w_conv_a=g256[8:8 + CONV_A],
            w_pool=jnp.stack([gwbd[64 * g:64 * (g + 1), 64 * g:64 * (g + 1)] for g in range(4)]),
            pool_scale=g256[0], w_dw_c=gwdw[:CONV_C], b_dw_c=g256[1], ln_g_c=g256[2], ln_b_c=g256[3],
            w_pw2_c=gwpw, b_pw2_c=g256[4], ln_g_d=g256[5], ln_b_d=g256[6], w_s_d=gws, b_s_d=gbs[:, :4].T)
    grad_x = dx[None]

    rs_in = _reduce_scatter_start(g_in, cidx, "in")
    rs_out = _reduce_scatter_start(g_out, cidx, "out")
    token = rs_in[-1][0, 0] + rs_out[-1][0, 0]

    dmod_all, dmod_sum = _allgather8(
        (jnp.stack(dmod) + token).reshape(nl * 3 * D // 128, 128), "gather_dmod", reduce=True)
    dmod_all = dmod_all.reshape(N_DEV, nl, 3 * D)
    grad_b_ada = dmod_sum.reshape(nl, 3 * D)
    dmod_s = jnp.swapaxes(lax.dynamic_slice_in_dim(dmod_all, chip * PIECE, PIECE, axis=2), 0, 1)
    grad_w_ada = _ada_backward(c_all, dmod_s)

    small_names = [n for n in WEIGHTS if n not in BIG and n not in ('b_ada', 'final_g')]
    stacked = [jnp.stack([small[l][n] for l in range(nl)]) for n in small_names]
    small_buf, small_off = _pack(stacked + [fin_sums[0], fin_sums[2, 0:1]], 8 * N_DEV)
    reduced = _unpack(_allreduce8(small_buf, "allreduce_small"), small_off)
    grads = dict(zip(small_names, reduced[:len(small_names)]))
    grads['final_g'] = reduced[-2]
    loss = reduced[-1][0]
    grads['b_ada'] = grad_b_ada
    grads['w_ada'] = grad_w_ada
    grads['w_conv_a'] = lax.dynamic_slice_in_dim(grads['w_conv_a'], chip * 64, 64, axis=2)
    grads['w_dw_c'] = lax.dynamic_slice_in_dim(grads['w_dw_c'], chip * 64, 64, axis=2)
    grads['w_pw2_c'] = lax.dynamic_slice_in_dim(grads['w_pw2_c'], chip * 64, 64, axis=1)

    delta, new_m, new_v = {}, {}, {}
    shape = w_ada.shape
    as2d = lambda a: a.reshape(-1, shape[-1])
    d2, m2, v2 = _adamw(as2d(w_ada), as2d(grads['w_ada']), as2d(m_w_ada), as2d(v_w_ada), 'w_ada')
    delta['w_ada'], new_m['w_ada'], new_v['w_ada'] = d2.reshape(shape), m2.reshape(shape), v2.reshape(shape)
    rest = [n for n in WEIGHTS if n not in BIG]
    ds, nms, nvs = _adamw_many([weights[n] for n in rest], [grads[n] for n in rest],
                               [mom_m[n] for n in rest], [mom_v[n] for n in rest])
    for n, dn, mn, vn in zip(rest, ds, nms, nvs):
        delta[n], new_m[n], new_v[n] = dn, mn, vn
    halves = {'w_in': _reduce_scatter_finish(rs_in, nvs[-1], sidx, "in"),
              'w_out': _reduce_scatter_finish(rs_out, nvs[-1], sidx, "out")}
    for n in ('w_in', 'w_out'):
        grads[n], delta[n], new_m[n], new_v[n] = _adamw_halves(
            weights[n], *halves[n], mom_m[n], mom_v[n], cidx, n)

    return (loss, grad_x, *[grads[n] for n in WEIGHTS], *[delta[n] for n in WEIGHTS],
            *[new_m[n] for n in WEIGHTS], *[new_v[n] for n in WEIGHTS])
```

```python
import math

import jax
import jax.numpy as jnp
import numpy as np
from jax import lax
from jax.experimental import pallas as pl
from jax.experimental.pallas import tpu as pltpu

F32 = jnp.float32
BF16 = jnp.bfloat16
MESH = pl.DeviceIdType.MESH

D = 1024
DIN = 3072
GW = 256
PIECE = 768
N_CHIP = 4
N_DEV = 8
HALO = 32
CONV_A = 3
CONV_C = 31
CHUNK = 128
CONV_ROWS = 64
ROW_BLOCK = 32
EPS = 1e-6
VMEM_LIMIT = 56 * 1024 * 1024

ADAM_LR, ADAM_B1, ADAM_B2, ADAM_EPS, ADAM_WD, ADAM_STEP = 0.001, 0.9, 0.999, 1e-08, 0.01, 10
GELU_K = math.sqrt(2.0 / math.pi)
GELU_C = 0.044715

WEIGHTS = ['norm_g', 'w_ada', 'b_ada', 'w_in', 'w_conv_a', 'w_pool', 'pool_scale', 'w_dw_c', 'b_dw_c', 'ln_g_c',
           'ln_b_c', 'w_pw2_c', 'b_pw2_c', 'ln_g_d', 'ln_b_d', 'w_s_d', 'b_s_d', 'w_out', 'final_g']
BIG = ('w_ada', 'w_in', 'w_out')


def _tiles(t):
    if t % 512 == 0 and t >= 2048:
        return 512, 256
    return 128, 128


def _params(sem, limit=VMEM_LIMIT):
    return pltpu.CompilerParams(dimension_semantics=sem, vmem_limit_bytes=limit)


def _sig(x):
    return jax.nn.sigmoid(x)


def _gelu(x):
    th = jnp.tanh(GELU_K * (x + GELU_C * x * x * x))
    return 0.5 * x * (1.0 + th), th


def _gelu_grad(x, th):
    return 0.5 * (1.0 + th) + 0.5 * x * (1.0 - th * th) * GELU_K * (1.0 + 3.0 * GELU_C * x * x)


def _dot(a, b):
    return jnp.dot(a, b, preferred_element_type=F32)


def _dot_nt(a, b):
    return lax.dot_general(a, b, (((1,), (1,)), ((), ())), preferred_element_type=F32)


def _dot_tn(a, b):
    return lax.dot_general(a, b, (((0,), (0,)), ((), ())), preferred_element_type=F32)


def _rowsum(x):
    return jnp.sum(x, axis=0, keepdims=True)


def _allgather8_steps(x_ref, out_ref, send_sems, recv_sems, local_sem):
    m_per = x_ref.shape[0]
    x, y, c = lax.axis_index("x"), lax.axis_index("y"), lax.axis_index("c")
    me, sibling = (x, y, c), (x, y, 1 - c)
    chips = [(1 - x, y), (x, 1 - y), (1 - x, 1 - y)]

    def rows(px, py, pc):
        return out_ref.at[pl.ds((4 * px + 2 * py + pc) * m_per, m_per), :]

    def copy(k, block, to, src=None):
        return pltpu.make_async_remote_copy(
            src_ref=rows(*block) if src is None else src, dst_ref=rows(*block),
            send_sem=send_sems.at[k], recv_sem=recv_sems.at[k], device_id=to, device_id_type=MESH)

    mine = pltpu.make_async_copy(x_ref, rows(*me), local_sem)
    first = [copy(0, me, sibling, src=x_ref)]
    first += [copy(1 + j, me, (*chip, c), src=x_ref) for j, chip in enumerate(chips)]

    def begin():
        mine.start()
        for cp in first:
            cp.start()

    def finish():
        passed = [copy(4 + j, (*chip, c), sibling) for j, chip in enumerate(chips)]
        for j, chip in enumerate(chips):
            copy(1 + j, (*chip, c), me).wait_recv()
            passed[j].start()
        copy(0, sibling, me).wait_recv()
        for j, chip in enumerate(chips):
            copy(4 + j, (*chip, 1 - c), me).wait_recv()
        for cp in first + passed:
            cp.wait_send()
        mine.wait()

    return begin, finish


_ALLGATHER8_SEMS = [pltpu.SemaphoreType.DMA((7,)), pltpu.SemaphoreType.DMA((7,)), pltpu.SemaphoreType.DMA]


def _allgather8(v, name, reduce=False):
    m_per, n = v.shape

    def body(x_ref, out_ref, *rest):
        sum_ref = rest[0] if reduce else None
        begin, finish = _allgather8_steps(x_ref, out_ref, *rest[-3:])
        begin()
        finish()
        if reduce:
            acc = out_ref[0:m_per, :]
            for d in range(1, N_DEV):
                acc = acc + out_ref[d * m_per:(d + 1) * m_per, :]
            sum_ref[...] = acc

    out_shape = [jax.ShapeDtypeStruct((N_DEV * m_per, n), v.dtype)]
    out_specs = [pl.BlockSpec(memory_space=pltpu.VMEM)]
    if reduce:
        out_shape.append(jax.ShapeDtypeStruct((m_per, n), v.dtype))
        out_specs.append(pl.BlockSpec(memory_space=pltpu.VMEM))
    res = pl.pallas_call(
        body, name=name, out_shape=tuple(out_shape),
        in_specs=[pl.BlockSpec(memory_space=pltpu.VMEM)], out_specs=tuple(out_specs),
        scratch_shapes=list(_ALLGATHER8_SEMS),
        compiler_params=pltpu.CompilerParams(vmem_limit_bytes=VMEM_LIMIT),
    )(v)
    return res if reduce else res[0]


def _allreduce8(v, name):
    rows, n = v.shape
    rc = rows // N_DEV

    def body(x_ref, sum_ref, stage, send1, recv1, send2, recv2):
        x, y, c = lax.axis_index("x"), lax.axis_index("y"), lax.axis_index("c")
        me = 4 * x + 2 * y + c
        peers = []
        for k in range(1, N_DEV):
            kx, ky, kc = (k >> 2) & 1, (k >> 1) & 1, k & 1
            peers.append((1 - x if kx else x, 1 - y if ky else y, 1 - c if kc else c))

        def chunk(ref, d):
            return ref.at[pl.ds(d * rc, rc), :]

        scatter, gather = [], []
        for k, (px, py, pc) in enumerate(peers):
            scatter.append(pltpu.make_async_remote_copy(
                src_ref=chunk(x_ref, 4 * px + 2 * py + pc), dst_ref=stage.at[me], send_sem=send1.at[k],
                recv_sem=recv1.at[k], device_id=(px, py, pc), device_id_type=MESH))
        for cp in scatter:
            cp.start()
        stage[me] = x_ref[pl.ds(me * rc, rc), :]
        for k, (px, py, pc) in enumerate(peers):
            pltpu.make_async_remote_copy(
                src_ref=chunk(x_ref, me), dst_ref=stage.at[4 * px + 2 * py + pc], send_sem=send1.at[k],
                recv_sem=recv1.at[k], device_id=(px, py, pc), device_id_type=MESH).wait_recv()
        total = stage[0]
        for d in range(1, N_DEV):
            total = total + stage[d]
        sum_ref[pl.ds(me * rc, rc), :] = total
        for k, (px, py, pc) in enumerate(peers):
            gather.append(pltpu.make_async_remote_copy(
                src_ref=chunk(sum_ref, me), dst_ref=chunk(sum_ref, me), send_sem=send2.at[k],
                recv_sem=recv2.at[k], device_id=(px, py, pc), device_id_type=MESH))
        for cp in gather:
            cp.start()
        for k, (px, py, pc) in enumerate(peers):
            theirs = 4 * px + 2 * py + pc
            pltpu.make_async_remote_copy(
                src_ref=chunk(sum_ref, theirs), dst_ref=chunk(sum_ref, theirs), send_sem=send2.at[k],
                recv_sem=recv2.at[k], device_id=(px, py, pc), device_id_type=MESH).wait_recv()
        for cp in scatter + gather:
            cp.wait_send()

    return pl.pallas_call(
        body, name=name, out_shape=jax.ShapeDtypeStruct((rows, n), v.dtype),
        in_specs=[pl.BlockSpec(memory_space=pltpu.VMEM)], out_specs=pl.BlockSpec(memory_space=pltpu.VMEM),
        scratch_shapes=[pltpu.VMEM((N_DEV, rc, n), v.dtype)] + [pltpu.SemaphoreType.DMA((N_DEV - 1,))] * 4,
        compiler_params=pltpu.CompilerParams(vmem_limit_bytes=VMEM_LIMIT),
    )(v)


def _gather_weights(w_in_s, w_out_s, small):
    nl = w_in_s.shape[0]
    shapes = ((nl, N_CHIP) + w_in_s.shape[1:], (nl, N_CHIP) + w_out_s.shape[1:])
    n_sem = 2 * 3 * 2

    def body(win_ref, wout_ref, small_ref, win_o, wout_o, small_o, send_sems, recv_sems, *small_sems):
        begin_small, finish_small = _allgather8_steps(small_ref, small_o, *small_sems)
        begin_small()
        x, y, c = lax.axis_index("x"), lax.axis_index("y"), lax.axis_index("c")
        s = 2 * x + y
        sibling = (x, y, 1 - c)
        chips = [(1 - x, y), (x, 1 - y), (1 - x, 1 - y)]
        outs = (win_o, wout_o)
        for src, dst in ((win_ref, win_o), (wout_ref, wout_o)):
            rows = src.shape[1]
            step = min(rows, 256)
            for l in range(nl):
                for r in range(0, rows, step):
                    dst[l, s, r:r + step, :] = src[l, r:r + step, :].astype(BF16)

        def copy(k, ref, piece, half, to):
            r2 = ref.shape[2] // 2
            rows = ref.at[:, piece, pl.ds(half * r2, r2), :]
            return pltpu.make_async_remote_copy(
                src_ref=rows, dst_ref=rows, send_sem=send_sems.at[k], recv_sem=recv_sems.at[k],
                device_id=to, device_id_type=MESH)

        sends = [copy(2 * j + a, ref, s, c, (px, py, c)) for j, (px, py) in enumerate(chips)
                 for a, ref in enumerate(outs)]
        for cp in sends:
            cp.start()
        finish_small()
        passed = []
        for j, (px, py) in enumerate(chips):
            for a, ref in enumerate(outs):
                copy(2 * j + a, ref, 2 * px + py, c, (px, py, c)).wait_recv()
                passed.append(copy(6 + 2 * j + a, ref, 2 * px + py, c, sibling))
                passed[-1].start()
        for j, (px, py) in enumerate(chips):
            for a, ref in enumerate(outs):
                copy(6 + 2 * j + a, ref, 2 * px + py, 1 - c, sibling).wait_recv()
        for cp in sends + passed:
            cp.wait_send()

    vmem = pl.BlockSpec(memory_space=pltpu.VMEM)
    return pl.pallas_call(
        body, name="gather_weights",
        out_shape=tuple(jax.ShapeDtypeStruct(s, BF16) for s in shapes)
        + (jax.ShapeDtypeStruct((N_DEV * small.shape[0], small.shape[1]), small.dtype),),
        in_specs=[vmem] * 3, out_specs=(vmem,) * 3,
        scratch_shapes=[pltpu.SemaphoreType.DMA((n_sem,)), pltpu.SemaphoreType.DMA((n_sem,))] + list(_ALLGATHER8_SEMS),
        compiler_params=pltpu.CompilerParams(vmem_limit_bytes=VMEM_LIMIT),
    )(w_in_s, w_out_s, small)


def _cast_bf16(arrays, after):
    n = len(arrays)

    def body(*refs):
        for src, dst in zip(refs[:n], refs[n + 1:]):
            rows = src.shape[-2]
            step = min(rows, 256)
            for l in range(src.shape[0]):
                for r in range(0, rows, step):
                    dst[l, r:r + step, :] = src[l, r:r + step, :].astype(BF16)

    vmem = pl.BlockSpec(memory_space=pltpu.VMEM)
    return pl.pallas_call(
        body, name="cast_weights", out_shape=tuple(jax.ShapeDtypeStruct(a.shape, BF16) for a in arrays),
        in_specs=[vmem] * n + [pl.BlockSpec(memory_space=pl.ANY)], out_specs=(vmem,) * n,
        compiler_params=pltpu.CompilerParams(vmem_limit_bytes=VMEM_LIMIT),
    )(*arrays, after)


def _shard_copies(a_ref, b_ref, la_ref, lb_ref, send_sems, recv_sems):
    x, y, c = lax.axis_index("x"), lax.axis_index("y"), lax.axis_index("c")
    s = 2 * x + y
    sends, recvs = [], []
    for j, (px, py) in enumerate([(1 - x, y), (x, 1 - y), (1 - x, 1 - y)]):
        for a, (src, land) in enumerate(((a_ref, la_ref), (b_ref, lb_ref))):
            k = 2 * j + a
            for slot, into in ((s, sends), (2 * px + py, recvs)):
                into.append(pltpu.make_async_remote_copy(
                    src_ref=src.at[0], dst_ref=land.at[0, slot], send_sem=send_sems.at[k], recv_sem=recv_sems.at[k],
                    device_id=(px, py, c), device_id_type=MESH))
    return sends, recvs


def _gather_shards_start(a, b):
    lands = ((1, N_CHIP) + a.shape[1:], (1, N_CHIP) + b.shape[1:])
    hbm = pl.BlockSpec(memory_space=pltpu.HBM)
    sem = pl.BlockSpec(memory_space=pltpu.SEMAPHORE)

    def body(a_ref, b_ref, la_ref, lb_ref, send_sems, recv_sems, a_thru, b_thru, la_thru, lb_thru, token):
        for cp in _shard_copies(a_ref, b_ref, la_ref, lb_ref, send_sems, recv_sems)[0]:
            cp.start()
        token[...] = jnp.zeros_like(token)

    return pl.pallas_call(
        body, name="gather_weights_start",
        out_shape=(pltpu.SemaphoreType.DMA((6,)), pltpu.SemaphoreType.DMA((6,)), pltpu.HBM(a.shape, BF16),
                   pltpu.HBM(b.shape, BF16), pltpu.HBM(lands[0], BF16), pltpu.HBM(lands[1], BF16),
                   jax.ShapeDtypeStruct((8, 128), F32)),
        in_specs=(hbm,) * 4, out_specs=(sem, sem, hbm, hbm, hbm, hbm, pl.BlockSpec(memory_space=pltpu.VMEM)),
        input_output_aliases={0: 2, 1: 3, 2: 4, 3: 5},
        compiler_params=pltpu.CompilerParams(has_side_effects=pltpu.SideEffectType.DATAFLOW_SIDE_EFFECTING),
    )(pltpu.with_memory_space_constraint(a, pltpu.HBM), pltpu.with_memory_space_constraint(b, pltpu.HBM),
      pltpu.with_memory_space_constraint(lax.empty(lands[0], BF16), pltpu.HBM),
      pltpu.with_memory_space_constraint(lax.empty(lands[1], BF16), pltpu.HBM))


def _gather_shards_wait(send_sems, recv_sems, a, b, la, lb, after):
    hbm = pl.BlockSpec(memory_space=pltpu.HBM)
    sem = pl.BlockSpec(memory_space=pltpu.SEMAPHORE)

    def body(a_ref, b_ref, la_ref, lb_ref, send_sems, recv_sems, after_ref, a_dead, b_dead, la_out, lb_out):
        sends, recvs = _shard_copies(a_ref, b_ref, la_ref, lb_ref, send_sems, recv_sems)
        for cp in sends:
            cp.wait_send()
        for cp in recvs:
            cp.wait_recv()

    out = pl.pallas_call(
        body, name="gather_weights_wait",
        out_shape=tuple(pltpu.HBM(v.shape, v.dtype) for v in (a, b, la, lb)),
        in_specs=(hbm, hbm, hbm, hbm, sem, sem, pl.BlockSpec(memory_space=pl.ANY)), out_specs=(hbm,) * 4,
        input_output_aliases={0: 0, 1: 1, 2: 2, 3: 3},
        compiler_params=pltpu.CompilerParams(has_side_effects=pltpu.SideEffectType.DATAFLOW_SIDE_EFFECTING),
    )(a, b, la, lb, send_sems, recv_sems, after)
    return out


def _send_half_to_sibling(g, tag):
    nl, nc, r, cdim = g.shape
    half = r // 2

    def body(g_ref, recv_ref, send_sem, recv_sem):
        x, y, c = lax.axis_index("x"), lax.axis_index("y"), lax.axis_index("c")
        cp = pltpu.make_async_remote_copy(
            src_ref=g_ref.at[:, :, pl.ds((1 - c) * half, half), :], dst_ref=recv_ref,
            send_sem=send_sem, recv_sem=recv_sem, device_id=(x, y, 1 - c), device_id_type=MESH)
        cp.start()
        cp.wait()

    return pl.pallas_call(
        body, name=f"rs_sibling_{tag}",
        out_shape=jax.ShapeDtypeStruct((nl, nc, half, cdim), g.dtype),
        in_specs=[pl.BlockSpec(memory_space=pl.ANY)], out_specs=pl.BlockSpec(memory_space=pl.ANY),
        scratch_shapes=[pltpu.SemaphoreType.DMA, pltpu.SemaphoreType.DMA],
    )(g)


def _add_half(g, recv, cidx, tag):
    nl, nc, r, cdim = g.shape
    half = r // 2
    rt = min(half, 512)
    nrt = half // rt

    def body(c_ref, a_ref, b_ref, o_ref, ob_ref):
        c = c_ref[0]
        first = jnp.where(c == 0, a_ref[...], b_ref[...])
        second = jnp.where(c == 0, b_ref[...], a_ref[...])
        total = first + second
        o_ref[...] = total
        ob_ref[...] = total.astype(BF16)

    out_spec = pl.BlockSpec((None, None, rt, cdim), lambda l, j, i, c: (l, j, i, 0))
    grid_spec = pltpu.PrefetchScalarGridSpec(
        num_scalar_prefetch=1, grid=(nl, nc, nrt),
        in_specs=[pl.BlockSpec((None, None, rt, cdim), lambda l, j, i, c: (l, j, c[0] * nrt + i, 0)),
                  pl.BlockSpec((None, None, rt, cdim), lambda l, j, i, c: (l, j, i, 0))],
        out_specs=(out_spec, out_spec))
    return pl.pallas_call(
        body, name=f"rs_add_sibling_{tag}", grid_spec=grid_spec,
        out_shape=(jax.ShapeDtypeStruct((nl, nc, half, cdim), g.dtype),
                   jax.ShapeDtypeStruct((nl, nc, half, cdim), BF16)),
        compiler_params=_params(("parallel", "parallel", "parallel")),
    )(cidx, g, recv)


def _chip_copies(h_ref, land_ref, send_sems, recv_sems):
    x, y, c = lax.axis_index("x"), lax.axis_index("y"), lax.axis_index("c")
    chips = [(1 - x, y), (x, 1 - y), (1 - x, 1 - y)]
    return [pltpu.make_async_remote_copy(
        src_ref=h_ref.at[:, 2 * px + py], dst_ref=land_ref.at[k], send_sem=send_sems.at[k],
        recv_sem=recv_sems.at[k], device_id=(px, py, c), device_id_type=MESH) for k, (px, py) in enumerate(chips)]


def _exchange_chips_start(h, tag):
    nl, nc, r2, cdim = h.shape
    land_shape = (3, nl, r2, cdim)
    hbm = pl.BlockSpec(memory_space=pltpu.HBM)
    sem = pl.BlockSpec(memory_space=pltpu.SEMAPHORE)

    def body(h_ref, land_ref, send_sems, recv_sems, h_thru, land_thru, token):
        for cp in _chip_copies(h_ref, land_ref, send_sems, recv_sems):
            cp.start()
        token[...] = jnp.zeros_like(token)

    return pl.pallas_call(
        body, name=f"rs_chips_start_{tag}",
        out_shape=(pltpu.SemaphoreType.DMA((3,)), pltpu.SemaphoreType.DMA((3,)), pltpu.HBM(h.shape, h.dtype),
                   pltpu.HBM(land_shape, h.dtype), jax.ShapeDtypeStruct((8, 128), F32)),
        in_specs=(hbm, hbm), out_specs=(sem, sem, hbm, hbm, pl.BlockSpec(memory_space=pltpu.VMEM)),
        input_output_aliases={0: 2, 1: 3},
        compiler_params=pltpu.CompilerParams(has_side_effects=pltpu.SideEffectType.DATAFLOW_SIDE_EFFECTING),
    )(pltpu.with_memory_space_constraint(h, pltpu.HBM),
      pltpu.with_memory_space_constraint(lax.empty(land_shape, h.dtype), pltpu.HBM))


def _exchange_chips_wait(send_sems, recv_sems, h_thru, land_thru, after, tag):
    hbm = pl.BlockSpec(memory_space=pltpu.HBM)
    sem = pl.BlockSpec(memory_space=pltpu.SEMAPHORE)

    def body(h_ref, land_ref, send_sems, recv_sems, after_ref, h_dead, got_ref):
        for cp in _chip_copies(h_ref, land_ref, send_sems, recv_sems):
            cp.wait_send()
            cp.wait_recv()

    return pl.pallas_call(
        body, name=f"rs_chips_wait_{tag}",
        out_shape=(pltpu.HBM(h_thru.shape, h_thru.dtype), pltpu.HBM(land_thru.shape, land_thru.dtype)),
        in_specs=(hbm, hbm, sem, sem, pl.BlockSpec(memory_space=pl.ANY)), out_specs=(hbm, hbm),
        input_output_aliases={0: 0, 1: 1},
        compiler_params=pltpu.CompilerParams(has_side_effects=pltpu.SideEffectType.DATAFLOW_SIDE_EFFECTING),
    )(h_thru, land_thru, send_sems, recv_sems, after)[1]


def _add_chips(h, recv, sidx, tag):
    nl, nc, r2, cdim = h.shape
    rt = min(r2, 512)

    def body(s_ref, own_ref, r_ref, o_ref):
        s = s_ref[0]
        got = [r_ref[k].astype(F32) for k in range(3)]
        total = None
        for chip in range(N_CHIP):
            term = jnp.where(s == chip, own_ref[...],
                             jnp.where((s ^ 2) == chip, got[0], jnp.where((s ^ 1) == chip, got[1], got[2])))
            total = term if total is None else total + term
        o_ref[...] = total

    grid_spec = pltpu.PrefetchScalarGridSpec(
        num_scalar_prefetch=1, grid=(nl, r2 // rt),
        in_specs=[pl.BlockSpec((None, None, rt, cdim), lambda l, i, s: (l, s[0], i, 0)),
                  pl.BlockSpec((3, None, rt, cdim), lambda l, i, s: (0, l, i, 0))],
        out_specs=pl.BlockSpec((None, rt, cdim), lambda l, i, s: (l, i, 0)))
    return pl.pallas_call(
        body, name=f"rs_add_chips_{tag}", grid_spec=grid_spec,
        out_shape=jax.ShapeDtypeStruct((nl, r2, cdim), h.dtype),
        compiler_params=_params(("parallel", "parallel")),
    )(sidx, h, recv)


def _swap_halves(red, tag):
    def body(red_ref, out_ref, send_sem, recv_sem):
        x, y, c = lax.axis_index("x"), lax.axis_index("y"), lax.axis_index("c")
        cp = pltpu.make_async_remote_copy(
            src_ref=red_ref, dst_ref=out_ref, send_sem=send_sem, recv_sem=recv_sem,
            device_id=(x, y, 1 - c), device_id_type=MESH)
        cp.start()
        cp.wait()

    return pl.pallas_call(
        body, name=f"rs_swap_{tag}", out_shape=jax.ShapeDtypeStruct(red.shape, red.dtype),
        in_specs=[pl.BlockSpec(memory_space=pl.ANY)], out_specs=pl.BlockSpec(memory_space=pl.ANY),
        scratch_shapes=[pltpu.SemaphoreType.DMA, pltpu.SemaphoreType.DMA],
    )(red)


def _reduce_scatter_start(g, cidx, tag):
    recv = _send_half_to_sibling(g, tag)
    chip_sum, chip_sum_bf16 = _add_half(g, recv, cidx, tag)
    return (chip_sum,) + tuple(_exchange_chips_start(chip_sum_bf16, tag))


def _reduce_scatter_finish(started, after, sidx, tag):
    chip_sum, send_sems, recv_sems, h_thru, land_thru, _ = started
    got = _exchange_chips_wait(send_sems, recv_sems, h_thru, land_thru, after, tag)
    red = _add_chips(chip_sum, got, sidx, tag)
    return red, _swap_halves(red, tag)


def _ada_forward(c_all, w_ada_s, b_s):
    nl = w_ada_s.shape[0]

    def body(c_ref, w_ref, b_ref, o_ref):
        cv = c_ref[...]
        ca = (cv * _sig(cv)).astype(BF16)
        for l in range(nl):
            o_ref[l] = _dot(ca, w_ref[l].astype(BF16)) + b_ref[l]

    return pl.pallas_call(
        body, name="ada_forward", out_shape=jax.ShapeDtypeStruct((nl, N_DEV, PIECE), F32),
        compiler_params=pltpu.CompilerParams(vmem_limit_bytes=VMEM_LIMIT),
    )(c_all, w_ada_s, b_s)


def _ada_backward(c_all, dmod_s):
    nl = dmod_s.shape[0]

    def body(c_ref, d_ref, o_ref):
        cv = c_ref[...]
        ca = (cv * _sig(cv)).astype(BF16)
        for l in range(nl):
            o_ref[l] = _dot_tn(ca, d_ref[l].astype(BF16))

    return pl.pallas_call(
        body, name="ada_backward", out_shape=jax.ShapeDtypeStruct((nl, D, PIECE), F32),
        compiler_params=pltpu.CompilerParams(vmem_limit_bytes=VMEM_LIMIT),
    )(c_all, dmod_s)


def _normed(xv, vecs):
    r = lax.rsqrt(jnp.mean(xv * xv, axis=-1, keepdims=True) + EPS)
    xhat = xv * r
    hn = xhat * vecs[3:4, :]
    return r, xhat, hn


def _bwd_in(dz, x, dxo, vecs, w, l, nl, tt, gw_all=None):
    t = x.shape[0]
    nt = t // tt

    def body(dz_ref, x_ref, dxo_ref, v_ref, w_any, *rest):
        dx_ref, gw_any, sums_ref, w_vmem, gw_acc, sem = rest[-6:]
        i = pl.program_id(0)

        @pl.when(i == 0)
        def _():
            cp = pltpu.make_async_copy(w_any.at[0], w_vmem, sem)
            cp.start()
            cp.wait()
            gw_acc[...] = jnp.zeros_like(gw_acc)
            sums_ref[...] = jnp.zeros_like(sums_ref)

        vecs_v = v_ref[...]
        r, xhat, hn = _normed(x_ref[...], vecs_v)
        one_scale = 1.0 + vecs_v[1:2, :]
        hb = (hn * one_scale + vecs_v[0:1, :]).astype(BF16)
        dh = None
        for j in range(N_CHIP):
            dzj = dz_ref[:, PIECE * j:PIECE * (j + 1)]
            part = _dot_nt(dzj, w_vmem[j])
            dh = part if dh is None else dh + part
            gw_acc[j] += _dot_tn(hb, dzj)
        sums_ref[0:1, :] += _rowsum(dh)
        sums_ref[1:2, :] += _rowsum(dh * hn)
        sums_ref[2:3, :] += _rowsum(dh * one_scale * xhat)
        dxh = dh * (vecs_v[3:4, :] * one_scale)
        dx_ref[...] = dxo_ref[...] + r * (dxh - xhat * jnp.mean(dxh * xhat, axis=-1, keepdims=True))

        @pl.when(i == nt - 1)
        def _():
            cp = pltpu.make_async_copy(gw_acc, gw_any.at[l], sem)
            cp.start()
            cp.wait()

    carried = [] if gw_all is None else [gw_all]
    return pl.pallas_call(
        body, name=f"bwd_in_{l}", grid=(nt,),
        in_specs=[pl.BlockSpec((tt, DIN), lambda i: (i, 0)), pl.BlockSpec((tt, D), lambda i: (i, 0)),
                  pl.BlockSpec((tt, D), lambda i: (i, 0)), pl.BlockSpec((8, D), lambda i: (0, 0)),
                  pl.BlockSpec(memory_space=pl.ANY)] + [pl.BlockSpec(memory_space=pl.ANY)] * len(carried),
        out_specs=(pl.BlockSpec((tt, D), lambda i: (i, 0)), pl.BlockSpec(memory_space=pl.ANY),
                   pl.BlockSpec((8, D), lambda i: (0, 0))),
        out_shape=(jax.ShapeDtypeStruct((t, D), F32), jax.ShapeDtypeStruct((nl, N_CHIP, D, PIECE), F32),
                   jax.ShapeDtypeStruct((8, D), F32)),
        scratch_shapes=[pltpu.VMEM((N_CHIP, D, PIECE), BF16), pltpu.VMEM((N_CHIP, D, PIECE), F32),
                        pltpu.SemaphoreType.DMA],
        input_output_aliases={5: 1} if carried else {},
        compiler_params=_params(("arbitrary",)),
    )(dz, x, dxo, vecs, w, *carried)


def _col(ref, k):
    return ref[:, GW * k:GW * (k + 1)]


def _lane_group():
    return lax.broadcasted_iota(jnp.int32, (1, GW), 1) // (GW // 4)


def _pool_window(group):
    return jnp.where(group == 0, 2.0, jnp.where(group == 1, 4.0, jnp.where(group == 2, 8.0, 16.0)))


def _by_group(group, a, b, c, d):
    return jnp.where(group == 0, a, jnp.where(group == 1, b, jnp.where(group == 2, c, d)))


def _layer_norm(x, g, b):
    mu = jnp.mean(x, axis=-1, keepdims=True)
    xc = x - mu
    rstd = lax.rsqrt(jnp.mean(xc * xc, axis=-1, keepdims=True) + EPS)
    xh = xc * rstd
    return xh * g + b, xh, rstd


def _layer_norm_grad(d_out, xh, rstd, g):
    dxh = d_out * g
    return rstd * (dxh - jnp.mean(dxh, axis=-1, keepdims=True) - xh * jnp.mean(dxh * xh, axis=-1, keepdims=True))


def _shifted_copies(ext_ref, sh_ref, n):
    for b in range(1, 8):
        sh_ref[b, 0:n - 8, :] = ext_ref[b:b + n - 8, :]


def _rows_at(ext_ref, sh_ref, start, rows):
    b = start % 8
    if b == 0 or sh_ref is None:
        return ext_ref[start:start + rows, :]
    return sh_ref[b, start - b:start - b + rows, :]


def _conv_taps(ext_ref, w_ref, n_taps, first_row, tm, out_ref, flip=False, sh_ref=None):
    for rb in range(0, tm, CONV_ROWS):
        acc = None
        for k in range(n_taps):
            wk = n_taps - 1 - k if flip else k
            term = w_ref[wk:wk + 1, :] * _rows_at(ext_ref, sh_ref, first_row + rb + k, CONV_ROWS)
            acc = term if acc is None else acc + term
        out_ref[rb:rb + CONV_ROWS, :] = acc


def _mix_forward(zc, zp, first, row0, tm, p, s, after, conv_c=None):
    keep = jnp.where(first, 0.0, 1.0)
    group = _lane_group()
    a_b, a_c, a_x, a_g = _col(zc, 0), _col(zc, 1), _col(zc, 2), _col(zc, 3)
    s['ua'][0:HALO, :] = _col(zp, 1) * _col(zp, 2) * keep
    s['ua'][HALO:HALO + tm, :] = a_c * a_x
    _conv_taps(s['ua'], p['wa'], CONV_A, HALO - (CONV_A - 1), tm, s['cv'])
    cv = s['cv'][...]
    sg_a = _sig(a_g)
    f = dict(a_b=a_b, a_c=a_c, a_x=a_x, a_g=a_g, cv=cv, sg_a=sg_a)
    after(0, f, (a_b * cv * (a_g * sg_a)).astype(BF16))
    b_p, b_g = _col(zc, 4), _col(zc, 5)
    s['p0'][0:HALO, :] = _col(zp, 4) * keep
    s['p0'][HALO:HALO + tm, :] = b_p
    n = HALO + tm
    s['p1'][8:n, :] = s['p0'][8:n, :] + s['p0'][7:n - 1, :]
    w2 = s['p1'][HALO:n, :]
    s['p0'][16:n, :] = s['p1'][16:n, :] + s['p1'][14:n - 2, :]
    w4 = s['p0'][HALO:n, :]
    s['p1'][24:n, :] = s['p0'][24:n, :] + s['p0'][20:n - 4, :]
    w8 = s['p1'][HALO:n, :]
    w16 = w8 + s['p1'][HALO - 8:n - 8, :]
    tpos = (row0 + lax.broadcasted_iota(jnp.int32, (tm, 1), 0) + 1).astype(F32)
    inv_cnt = 1.0 / jnp.minimum(tpos, _pool_window(group))
    pooled = (_by_group(group, w2, w4, w8, w16) * inv_cnt - b_p).astype(BF16)
    q = _dot(pooled, p['wbd'][...])
    sg_b = _sig(b_g)
    f = dict(b_g=b_g, pooled=pooled, q=q, sg_b=sg_b, inv_cnt=inv_cnt)
    after(1, f, (q * p['v256'][0:1, :] * (b_g * sg_b)).astype(BF16))
    c_a, c_gl, c_g = _col(zc, 6), _col(zc, 7), _col(zc, 8)
    sg_gl = _sig(c_gl)
    zp_gl = _col(zp, 7)
    s['hc'][0:HALO, :] = _col(zp, 6) * _sig(zp_gl) * keep
    s['hc'][HALO:HALO + tm, :] = c_a * sg_gl
    _shifted_copies(s['hc'], s['hcs'], HALO + tm)
    if conv_c is None:
        _conv_taps(s['hc'], p['wdw'], CONV_C, HALO - (CONV_C - 1), tm, s['co'], sh_ref=s['hcs'])
        conv_c = s['co']
    conv_v = conv_c[...]
    co = conv_v + p['v256'][1:2, :]
    ln_c, xh_c, rstd_c = _layer_norm(co, p['v256'][2:3, :], p['v256'][3:4, :])
    sg_ln = _sig(ln_c)
    sc = (ln_c * sg_ln).astype(BF16)
    pw = _dot(sc, p['wpw'][...]) + p['v256'][4:5, :]
    sg_c = _sig(c_g)
    f = dict(c_a=c_a, c_g=c_g, sg_gl=sg_gl, ln_c=ln_c, xh_c=xh_c, rstd_c=rstd_c, sg_ln=sg_ln, sc=sc, pw=pw,
             sg_c=sg_c, conv_c=conv_v)
    after(2, f, (pw * (c_g * sg_c)).astype(BF16))
    d_u, d_v, d_g = _col(zc, 9), _col(zc, 10), _col(zc, 11)
    gu, th_u = _gelu(d_u)
    gv, th_v = _gelu(d_v)
    v, xh_d, rstd_d = _layer_norm(gv, p['v256'][5:6, :], p['v256'][6:7, :])
    vb = v.astype(BF16)
    parts = []
    for ch in range(tm // CHUNK):
        vc = vb[ch * CHUNK:(ch + 1) * CHUNK, :]
        mixed = p['bsd'][...]
        for h in range(4):
            mixed = mixed + jnp.where(group == h, _dot(p['ws'][h], vc), 0.0)
        parts.append(mixed)
    mx = parts[0] if len(parts) == 1 else jnp.concatenate(parts, axis=0)
    sg_d = _sig(d_g)
    f = dict(d_u=d_u, d_v=d_v, d_g=d_g, gu=gu, th_u=th_u, th_v=th_v, xh_d=xh_d, rstd_d=rstd_d, vb=vb, mx=mx,
             sg_d=sg_d)
    after(3, f, (gu * mx * (d_g * sg_d)).astype(BF16))


def _mix_scratch(tm):
    ext = pltpu.VMEM((HALO + tm, GW), F32)
    tile = pltpu.VMEM((tm, GW), F32)
    return dict(ua=ext, cv=tile, p0=ext, p1=ext, hc=ext, co=tile, hcs=pltpu.VMEM((8, HALO + tm, GW), F32))


_MIX_PARAMS = ('wa', 'wdw', 'v256', 'wbd', 'wpw', 'ws', 'bsd', 'wout')


def _mix_param_specs(l):
    def whole(*shape, at=l):
        nd = len(shape)
        return pl.BlockSpec((None,) + shape, lambda i, _nd=nd: (at,) + (0,) * _nd)
    return [whole(8, GW), whole(32, GW), whole(16, GW), whole(GW, GW), whole(GW, GW), whole(4, CHUNK, CHUNK),
            whole(CHUNK, GW), whole(N_CHIP, GW, D, at=0)]


def _mix_param_arrays(mp, w_out_l):
    return [mp[k] for k in _MIX_PARAMS[:-1]] + [w_out_l]


def _fwd_layer(x, vecs, w, w_out_l, mp, l, tm):
    t = x.shape[0]
    nt = t // tm
    names = list(_mix_scratch(tm))
    n_p = len(_MIX_PARAMS)

    def body(xm_ref, xr_ref, v_ref, w_ref, *rest):
        p = dict(zip(_MIX_PARAMS, rest[:n_p]))
        z_ref, xn_ref, y_ref, conv_ref = rest[n_p:n_p + 4]
        s = dict(zip(names, rest[n_p + 4:n_p + 4 + len(names)]))
        z_next, z_cur, z_halo = rest[n_p + 4 + len(names):]
        j = pl.program_id(0)

        @pl.when(j == 0)
        def _():
            z_cur[...] = jnp.zeros_like(z_cur)
            z_halo[...] = jnp.zeros_like(z_halo)

        vecs_v = v_ref[...]
        _, _, hn = _normed(xm_ref[...], vecs_v)
        hb = (hn * (1.0 + vecs_v[1:2, :]) + vecs_v[0:1, :]).astype(BF16)
        for k in range(N_CHIP):
            part = _dot(hb, w_ref[k])
            z_ref[:, PIECE * k:PIECE * (k + 1)] = part
            z_next[:, PIECE * k:PIECE * (k + 1)] = part

        tile = jnp.maximum(j - 1, 0)
        mixed = []
        _mix_forward(z_cur, z_halo, tile == 0, tile * tm, tm, p, s, lambda k, f, yk: mixed.append((f, yk)))
        y = None
        for k, (_, yk) in enumerate(mixed):
            part = _dot(yk, p['wout'][k])
            y = part if y is None else y + part
        y_ref[...] = y
        xn_ref[...] = xr_ref[...] + vecs_v[2:3, :] * y
        conv_ref[...] = mixed[2][0]['conv_c']

        z_halo[...] = z_cur[tm - HALO:tm, :]
        z_cur[...] = z_next[...]

    def ahead(j):
        return jnp.minimum(j, nt - 1)

    def behind(j):
        return jnp.maximum(j - 1, 0)

    return pl.pallas_call(
        body, name=f"fwd_layer_{l}", grid=(nt + 1,),
        in_specs=[pl.BlockSpec((tm, D), lambda j: (ahead(j), 0)), pl.BlockSpec((tm, D), lambda j: (behind(j), 0)),
                  pl.BlockSpec((8, D), lambda j: (0, 0)),
                  pl.BlockSpec((None, N_CHIP, D, PIECE), lambda j: (0, 0, 0, 0))] + _mix_param_specs(l),
        out_specs=(pl.BlockSpec((tm, DIN), lambda j: (ahead(j), 0)), pl.BlockSpec((tm, D), lambda j: (behind(j), 0)),
                   pl.BlockSpec((tm, D), lambda j: (behind(j), 0)), pl.BlockSpec((tm, GW), lambda j: (behind(j), 0))),
        out_shape=(jax.ShapeDtypeStruct((t, DIN), F32), jax.ShapeDtypeStruct((t, D), F32),
                   jax.ShapeDtypeStruct((t, D), F32), jax.ShapeDtypeStruct((t, GW), F32)),
        scratch_shapes=list(_mix_scratch(tm).values())
        + [pltpu.VMEM((tm, DIN), F32), pltpu.VMEM((tm, DIN), F32), pltpu.VMEM((HALO, DIN), F32)],
        compiler_params=_params(("arbitrary",)),
    )(x, x, vecs, w, *_mix_param_arrays(mp, w_out_l))


def _bwd_mix(dxo, y, conv, z, vecs, mp, w_out_l, wst, l, tm, gwout_all=None):
    t = dxo.shape[0]
    nt = t // tm
    nl = wst.shape[0]
    carried = [] if gwout_all is None else [gwout_all]
    per_halo = tm // HALO
    fwd_names = list(_mix_scratch(tm))
    ext = pltpu.VMEM((tm + HALO, GW), F32)
    carry = pltpu.VMEM((HALO, GW), F32)
    tile = pltpu.VMEM((tm, GW), F32)
    bwd_scratch = dict(ea=ext, eb=ext, eb2=ext, ec=ext, ca=carry, cb=carry, cc=carry, dua=tile, dhc=tile,
                       gbacc=pltpu.VMEM((CHUNK, GW), F32), ecs=pltpu.VMEM((8, tm + HALO, GW), F32),
                       dys=pltpu.VMEM((1, tm, GW), F32))
    bwd_names = list(bwd_scratch)
    n_p = len(_MIX_PARAMS)

    def body(dxo_ref, y_ref, conv_ref, zc, zp, v_ref, *rest):
        p = dict(zip(_MIX_PARAMS, rest[:n_p]))
        wst_ref = rest[n_p]
        first_out = n_p + 1 + len(carried)
        dz_ref, gwout, gwbd, gwpw, gws, gbs, gwdw, g256, g1024 = rest[first_out:first_out + 9]
        s = dict(zip(fwd_names + bwd_names, rest[first_out + 9:]))
        i = pl.program_id(0)
        ti = nt - 1 - i
        group = _lane_group()

        @pl.when(i == 0)
        def _():
            for ref in (gwout, gwbd, gwpw, gws, gbs, gwdw, g256, g1024, s['ca'], s['cb'], s['cc'], s['gbacc']):
                ref[...] = jnp.zeros_like(ref)

        def put(k, val):
            dz_ref[:, GW * k:GW * (k + 1)] = val.astype(BF16)

        def bwd_a(f, dya):
            taps = [None] * CONV_A
            for r in range(0, tm, ROW_BLOCK):
                rs = slice(r, r + ROW_BLOCK)
                a_b, a_g = zc[rs, 0:GW], zc[rs, 3 * GW:4 * GW]
                dy_a, cv = s['dys'][0, rs, :], s['cv'][rs, :]
                sg = _sig(a_g)
                silu = a_g * sg
                t = dy_a * cv
                dz_ref[rs, 0:GW] = (t * silu).astype(BF16)
                dz_ref[rs, 3 * GW:4 * GW] = (t * a_b * (sg * (1.0 + a_g * (1.0 - sg)))).astype(BF16)
                d_cv = dy_a * a_b * silu
                s['ea'][rs, :] = d_cv
                for k in range(CONV_A):
                    first = HALO - (CONV_A - 1) + k + r
                    term = d_cv * s['ua'][first:first + ROW_BLOCK, :]
                    taps[k] = term if taps[k] is None else taps[k] + term
            for k in range(CONV_A):
                g256[8 + k:9 + k, :] += _rowsum(taps[k])
            s['ea'][tm:tm + HALO, :] = s['ca'][...]
            s['ca'][...] = s['ea'][0:HALO, :]
            _conv_taps(s['ea'], p['wa'], CONV_A, 0, tm, s['dua'], flip=True)
            for r in range(0, tm, ROW_BLOCK):
                rs = slice(r, r + ROW_BLOCK)
                d_u = s['dua'][rs, :]
                dz_ref[rs, GW:2 * GW] = (d_u * zc[rs, 2 * GW:3 * GW]).astype(BF16)
                dz_ref[rs, 2 * GW:3 * GW] = (d_u * zc[rs, GW:2 * GW]).astype(BF16)

        def bwd_b(f, dyb):
            b_g, q, sg_b, inv_cnt = f['b_g'], f['q'], f['sg_b'], f['inv_cnt']
            scale_b = p['v256'][0:1, :]
            silu_b = b_g * sg_b
            g256[0:1, :] += _rowsum(dyb * q * silu_b)
            put(5, dyb * q * scale_b * (sg_b * (1.0 + b_g * (1.0 - sg_b))))
            d_q = (dyb * scale_b * silu_b).astype(BF16)
            gwbd[...] += _dot_tn(f['pooled'], d_q)
            d_pooled = _dot_nt(d_q, p['wbd'][...])
            gp = d_pooled * inv_cnt
            n = tm + HALO
            s['eb'][0:tm, :] = gp
            s['eb'][tm:n, :] = s['cb'][...]
            s['cb'][...] = gp[0:HALO, :]
            s['eb2'][0:n - 8, :] = s['eb'][0:n - 8, :] + s['eb'][1:n - 7, :]
            r2 = s['eb2'][0:tm, :]
            s['eb'][0:n - 16, :] = s['eb2'][0:n - 16, :] + s['eb2'][2:n - 14, :]
            r4 = s['eb'][0:tm, :]
            s['eb2'][0:n - 24, :] = s['eb'][0:n - 24, :] + s['eb'][4:n - 20, :]
            r8 = s['eb2'][0:tm, :]
            r16 = r8 + s['eb2'][8:tm + 8, :]
            put(4, _by_group(group, r2, r4, r8, r16) - d_pooled)

        def bwd_c(f, dyc):
            c_a, c_g, sg_gl, ln_c, xh_c, rstd_c, sg_ln, pw, sg_c = (
                f['c_a'], f['c_g'], f['sg_gl'], f['ln_c'], f['xh_c'], f['rstd_c'], f['sg_ln'], f['pw'], f['sg_c'])
            put(8, dyc * pw * (sg_c * (1.0 + c_g * (1.0 - sg_c))))
            d_pw = dyc * (c_g * sg_c)
            g256[4:5, :] += _rowsum(d_pw)
            d_pwb = d_pw.astype(BF16)
            gwpw[...] += _dot_tn(f['sc'], d_pwb)
            d_ln = _dot_nt(d_pwb, p['wpw'][...]) * (sg_ln * (1.0 + ln_c * (1.0 - sg_ln)))
            g256[2:3, :] += _rowsum(d_ln * xh_c)
            g256[3:4, :] += _rowsum(d_ln)
            d_co = _layer_norm_grad(d_ln, xh_c, rstd_c, p['v256'][2:3, :])
            g256[1:2, :] += _rowsum(d_co)
            for k in range(CONV_C):
                gwdw[k:k + 1, :] += _rowsum(d_co * _rows_at(s['hc'], s['hcs'], HALO - (CONV_C - 1) + k, tm))
            s['ec'][0:tm, :] = d_co
            s['ec'][tm:tm + HALO, :] = s['cc'][...]
            s['cc'][...] = d_co[0:HALO, :]
            _shifted_copies(s['ec'], s['ecs'], tm + HALO)
            _conv_taps(s['ec'], p['wdw'], CONV_C, 0, tm, s['dhc'], flip=True, sh_ref=s['ecs'])
            d_hc = s['dhc'][...]
            put(6, d_hc * sg_gl)
            put(7, d_hc * c_a * sg_gl * (1.0 - sg_gl))

        def bwd_d(f, dyd):
            d_uu, d_vv, d_g, gu, th_u, th_v, xh_d, rstd_d, vb, mx, sg_d = (
                f['d_u'], f['d_v'], f['d_g'], f['gu'], f['th_u'], f['th_v'], f['xh_d'], f['rstd_d'], f['vb'],
                f['mx'], f['sg_d'])
            silu_d = d_g * sg_d
            put(9, dyd * mx * silu_d * _gelu_grad(d_uu, th_u))
            put(11, dyd * gu * mx * (sg_d * (1.0 + d_g * (1.0 - sg_d))))
            d_mx = dyd * gu * silu_d
            dv_parts = []
            gb = None
            for ch in range(tm // CHUNK):
                dmc = d_mx[ch * CHUNK:(ch + 1) * CHUNK, :]
                gb = dmc if gb is None else gb + dmc
                dmb = dmc.astype(BF16)
                vc = vb[ch * CHUNK:(ch + 1) * CHUNK, :]
                dvc = None
                for h in range(4):
                    gws[h] += _dot_nt(jnp.where(group == h, dmb, jnp.zeros_like(dmb)), vc)
                    part = jnp.where(group == h, _dot(wst_ref[h], dmb), 0.0)
                    dvc = part if dvc is None else dvc + part
                dv_parts.append(dvc)
            s['gbacc'][...] += gb
            d_v = dv_parts[0] if len(dv_parts) == 1 else jnp.concatenate(dv_parts, axis=0)
            g256[5:6, :] += _rowsum(d_v * xh_d)
            g256[6:7, :] += _rowsum(d_v)
            d_gv = _layer_norm_grad(d_v, xh_d, rstd_d, p['v256'][5:6, :])
            put(10, d_gv * _gelu_grad(d_vv, th_v))

        mixed = []
        _mix_forward(zc, zp, ti == 0, ti * tm, tm, p, s, lambda j, f, yj: mixed.append((f, yj)), conv_c=conv_ref)
        dxo_v = dxo_ref[...]
        g1024[0:1, :] += _rowsum(dxo_v * y_ref[...])
        dy = (dxo_v * v_ref[2:3, :]).astype(BF16)
        s['dys'][0] = _dot_nt(dy, p['wout'][0])
        dys = [None] + [_dot_nt(dy, p['wout'][j]) for j in range(1, 4)]
        for j, (f, yj) in enumerate(mixed):
            gwout[j] += _dot_tn(yj, dy)
        for (f, _), backward, dyj in zip(mixed, (bwd_a, bwd_b, bwd_c, bwd_d), dys):
            backward(f, dyj)

        @pl.when(i == nt - 1)
        def _():
            row = lax.broadcasted_iota(jnp.int32, (CHUNK, CHUNK), 0)
            col = lax.broadcasted_iota(jnp.int32, (CHUNK, CHUNK), 1)
            for h in range(4):
                gws[h] = jnp.where(col <= row, gws[h], 0.0)
            head_of_lane = lax.broadcasted_iota(jnp.int32, (GW, CHUNK), 0) // (GW // 4)
            sel = jnp.where(head_of_lane == lax.broadcasted_iota(jnp.int32, (GW, CHUNK), 1), 1.0, 0.0)
            gbs[...] = jnp.dot(s['gbacc'][...], sel, preferred_element_type=F32, precision=lax.Precision.HIGHEST)

    def const(*shape):
        nd = len(shape)
        return pl.BlockSpec(shape, lambda i, _nd=nd: (0,) * _nd)

    def rev(i):
        return nt - 1 - i

    out_shapes = (jax.ShapeDtypeStruct((t, DIN), BF16), jax.ShapeDtypeStruct((nl, N_CHIP, GW, D), F32),
                  jax.ShapeDtypeStruct((GW, GW), F32), jax.ShapeDtypeStruct((GW, GW), F32),
                  jax.ShapeDtypeStruct((4, CHUNK, CHUNK), F32), jax.ShapeDtypeStruct((CHUNK, CHUNK), F32),
                  jax.ShapeDtypeStruct((32, GW), F32), jax.ShapeDtypeStruct((16, GW), F32),
                  jax.ShapeDtypeStruct((8, D), F32))
    out_specs = (pl.BlockSpec((tm, DIN), lambda i: (rev(i), 0)),
                 pl.BlockSpec((None, N_CHIP, GW, D), lambda i: (l, 0, 0, 0)), const(GW, GW),
                 const(GW, GW), const(4, CHUNK, CHUNK), const(CHUNK, CHUNK), const(32, GW), const(16, GW),
                 const(8, D))
    scratch = list(_mix_scratch(tm).values()) + list(bwd_scratch.values())
    n_in = 6 + n_p + 1
    return pl.pallas_call(
        body, name=f"bwd_mix_{l}", grid=(nt,),
        in_specs=[pl.BlockSpec((tm, D), lambda i: (rev(i), 0)), pl.BlockSpec((tm, D), lambda i: (rev(i), 0)),
                  pl.BlockSpec((tm, GW), lambda i: (rev(i), 0)), pl.BlockSpec((tm, DIN), lambda i: (rev(i), 0)),
                  pl.BlockSpec((HALO, DIN), lambda i: (jnp.maximum(rev(i) * per_halo - 1, 0), 0)),
                  pl.BlockSpec((8, D), lambda i: (0, 0))]
        + _mix_param_specs(l)
        + [pl.BlockSpec((None, 4, CHUNK, CHUNK), lambda i: (l, 0, 0, 0))]
        + [pl.BlockSpec(memory_space=pl.ANY)] * len(carried),
        out_specs=out_specs, out_shape=out_shapes, scratch_shapes=scratch,
        input_output_aliases={n_in: 1} if carried else {},
        compiler_params=_params(("arbitrary",)),
    )(dxo, y, conv, z, z, vecs, *_mix_param_arrays(mp, w_out_l), wst, *carried)


def _final(x, target, fg, tt):
    t = x.shape[0]
    nt = t // tt

    def body(x_ref, t_ref, g_ref, dx_ref, sums_ref):
        i = pl.program_id(0)

        @pl.when(i == 0)
        def _():
            sums_ref[...] = jnp.zeros_like(sums_ref)

        xv = x_ref[...]
        g = g_ref[0:1, :]
        r = lax.rsqrt(jnp.mean(xv * xv, axis=-1, keepdims=True) + EPS)
        xhat = xv * r
        err = xhat * g - t_ref[...]
        sums_ref[1:2, :] += _rowsum(err * err) * (0.5 / D)
        dyv = err * (1.0 / D)
        sums_ref[0:1, :] += _rowsum(dyv * xhat)
        dxh = dyv * g
        dx_ref[...] = r * (dxh - xhat * jnp.mean(dxh * xhat, axis=-1, keepdims=True))

        @pl.when(i == nt - 1)
        def _():
            sums_ref[2:3, :] = jnp.broadcast_to(jnp.sum(sums_ref[1:2, :], axis=-1, keepdims=True), (1, D))

    return pl.pallas_call(
        body, name="final_loss", grid=(nt,),
        in_specs=[pl.BlockSpec((tt, D), lambda i: (i, 0)), pl.BlockSpec((tt, D), lambda i: (i, 0)),
                  pl.BlockSpec((8, D), lambda i: (0, 0))],
        out_specs=(pl.BlockSpec((tt, D), lambda i: (i, 0)), pl.BlockSpec((8, D), lambda i: (0, 0))),
        out_shape=(jax.ShapeDtypeStruct((t, D), F32), jax.ShapeDtypeStruct((8, D), F32)),
        compiler_params=_params(("arbitrary",)),
    )(x, target, fg)


def _adamw(w, g, m, v, tag):
    rows, cols = w.shape
    rt = rows
    for cand in (512, 256, 128, 64, 32, 16, 8):
        if rows % cand == 0:
            rt = cand
            break

    def body(w_ref, g_ref, m_ref, v_ref, d_ref, nm_ref, nv_ref):
        d_ref[...], nm_ref[...], nv_ref[...] = _adamw_update(g_ref[...], w_ref[...], m_ref[...], v_ref[...])

    spec = pl.BlockSpec((rt, cols), lambda i: (i, 0))
    return pl.pallas_call(
        body, name=f"adamw_{tag}", grid=(rows // rt,), in_specs=[spec] * 4, out_specs=(spec,) * 3,
        out_shape=(jax.ShapeDtypeStruct(w.shape, F32),) * 3, compiler_params=_params(("parallel",)),
    )(w, g, m, v)


def _adamw_update(gv, w, m, v):
    nm = ADAM_B1 * m + (1.0 - ADAM_B1) * gv
    nv = ADAM_B2 * v + (1.0 - ADAM_B2) * (gv * gv)
    m_hat = nm / (1.0 - ADAM_B1 ** ADAM_STEP)
    v_hat = nv / (1.0 - ADAM_B2 ** ADAM_STEP)
    return -ADAM_LR * (m_hat / (jnp.sqrt(v_hat) + ADAM_EPS) + ADAM_WD * w), nm, nv


def _adamw_many(ws, gs, ms, vs):
    n = len(ws)

    def body(*refs):
        w_refs, g_refs, m_refs, v_refs = (refs[k * n:(k + 1) * n] for k in range(4))
        d_refs, nm_refs, nv_refs = (refs[(4 + k) * n:(5 + k) * n] for k in range(3))
        for k in range(n):
            d_refs[k][...], nm_refs[k][...], nv_refs[k][...] = _adamw_update(
                g_refs[k][...], w_refs[k][...], m_refs[k][...], v_refs[k][...])

    shapes = tuple(jax.ShapeDtypeStruct(w.shape, F32) for w in ws)
    out = pl.pallas_call(body, name="adamw_small", out_shape=shapes * 3)(*ws, *gs, *ms, *vs)
    return out[:n], out[n:2 * n], out[2 * n:]


def _adamw_halves(w, mine, theirs, m, v, cidx, tag):
    nl, r, cdim = w.shape
    r2 = r // 2
    rt = min(r2, 512)
    nrt = r2 // rt

    def body(c_ref, a_ref, b_ref, w_ref, m_ref, v_ref, g_ref, d_ref, nm_ref, nv_ref):
        gv = jnp.where(pl.program_id(1) == c_ref[0], a_ref[...], b_ref[...])
        g_ref[...] = gv
        d_ref[...], nm_ref[...], nv_ref[...] = _adamw_update(gv, w_ref[...], m_ref[...], v_ref[...])

    half = pl.BlockSpec((None, rt, cdim), lambda l, h, i, c: (l, i, 0))
    whole = pl.BlockSpec((None, rt, cdim), lambda l, h, i, c: (l, h * nrt + i, 0))
    grid_spec = pltpu.PrefetchScalarGridSpec(
        num_scalar_prefetch=1, grid=(nl, 2, nrt), in_specs=[half, half, whole, whole, whole],
        out_specs=(whole,) * 4)
    return pl.pallas_call(
        body, name=f"adamw_{tag}", grid_spec=grid_spec,
        out_shape=(jax.ShapeDtypeStruct(w.shape, F32),) * 4,
        compiler_params=_params(("parallel", "parallel", "parallel")),
    )(cidx, mine, theirs, w, m, v)


def _pack(arrays, row_multiple=8):
    parts, offsets, off = [], [], 0
    for a in arrays:
        n = int(np.prod(a.shape))
        padded = -(-n // 1024) * 1024
        flat = a.reshape(-1).astype(F32)
        if padded != n:
            flat = jnp.concatenate([flat, jnp.zeros((padded - n,), F32)])
        parts.append(flat.reshape(-1, 128))
        offsets.append((off // 128, n, a.shape))
        off += padded
    tail = -off % (row_multiple * 128)
    if tail:
        parts.append(jnp.zeros((tail // 128, 128), F32))
    return jnp.concatenate(parts, axis=0), offsets


def _unpack(buf, offsets):
    return [buf[row:row + -(-n // 128)].reshape(-1)[:n].reshape(shape) for row, n, shape in offsets]


def _pad_rows(a, rows):
    return jnp.concatenate([a, jnp.zeros((rows - a.shape[0],) + a.shape[1:], a.dtype)], axis=0)


def kernel(x, c, norm_g, w_ada, b_ada, w_in, w_conv_a, w_pool, pool_scale, w_dw_c, b_dw_c, ln_g_c, ln_b_c, w_pw2_c, b_pw2_c, ln_g_d, ln_b_d, w_s_d, b_s_d, w_out, final_g, loss_target, m_norm_g, m_w_ada, m_b_ada, m_w_in, m_w_conv_a, m_w_pool, m_pool_scale, m_w_dw_c, m_b_dw_c, m_ln_g_c, m_ln_b_c, m_w_pw2_c, m_b_pw2_c, m_ln_g_d, m_ln_b_d, m_w_s_d, m_b_s_d, m_w_out, m_final_g, v_norm_g, v_w_ada, v_b_ada, v_w_in, v_w_conv_a, v_w_pool, v_pool_scale, v_w_dw_c, v_b_dw_c, v_ln_g_c, v_ln_b_c, v_w_pw2_c, v_b_pw2_c, v_ln_g_d, v_ln_b_d, v_w_s_d, v_b_s_d, v_w_out, v_final_g):
    env = dict(locals())
    weights = {n: env[n] for n in WEIGHTS}
    mom_m = {n: env["m_" + n] for n in WEIGHTS}
    mom_v = {n: env["v_" + n] for n in WEIGHTS}
    nl = norm_g.shape[0]
    t = x.shape[1]
    tt, tm = _tiles(t)
    ax, ay, ac = lax.axis_index("x"), lax.axis_index("y"), lax.axis_index("c")
    chip = 2 * ax + ay
    dev = 2 * chip + ac
    cidx = jnp.reshape(ac, (1,)).astype(jnp.int32)
    sidx = jnp.reshape(chip, (1,)).astype(jnp.int32)
    xs, target = x[0], loss_target[0]

    shard_small, shard_off = _pack([c, w_conv_a, w_dw_c, w_pw2_c])
    w_in_0, w_out_0, gathered = _gather_weights(w_in[:1], w_out[:1], shard_small)
    w_in_layers, w_out_layers = [w_in_0], [w_out_0]
    gathered = gathered.reshape(N_DEV, -1, 128)
    per_dev = [_unpack(gathered[d], shard_off) for d in range(0, N_DEV, 2)]
    c_all = jnp.concatenate([u[0] for d in range(N_DEV)
                             for u in [_unpack(gathered[d], shard_off[:1])]], axis=0)
    w_conv_full = jnp.concatenate([u[1] for u in per_dev], axis=-1)
    w_dw_full = jnp.concatenate([u[2] for u in per_dev], axis=-1)
    w_pw2_full = jnp.concatenate([u[3] for u in per_dev], axis=1)

    b_ada_s = lax.dynamic_slice_in_dim(b_ada, chip * PIECE, PIECE, axis=1)[:, None, :]
    mod_s = _ada_forward(c_all, w_ada, b_ada_s)
    mod_all = _allgather8(mod_s.reshape(nl * N_DEV, PIECE), "gather_mod").reshape(N_DEV, nl, N_DEV, PIECE)
    mod_rows = lax.dynamic_index_in_dim(mod_all, dev, axis=2, keepdims=False)
    mod = jnp.concatenate([mod_rows[2 * s] for s in range(N_CHIP)], axis=-1)

    if nl > 1:
        w_in_b, w_out_b = _cast_bf16([w_in[1:], w_out[1:]], after=w_out_layers[0])

    tril = jnp.tril(jnp.ones((CHUNK, CHUNK), bool))
    ws_masked = jnp.where(tril[None, None], w_s_d, 0.0)
    wbd = jnp.zeros((nl, GW, GW), F32)
    for g in range(4):
        wbd = wbd.at[:, 64 * g:64 * (g + 1), 64 * g:64 * (g + 1)].set(w_pool[:, g])
    v256 = jnp.stack([pool_scale, b_dw_c, ln_g_c, ln_b_c, b_pw2_c, ln_g_d, ln_b_d], axis=1)
    mp = dict(
        wa=_pad_rows(jnp.swapaxes(w_conv_full, 0, 1), 8).swapaxes(0, 1),
        wdw=_pad_rows(jnp.swapaxes(w_dw_full, 0, 1), 32).swapaxes(0, 1),
        v256=_pad_rows(jnp.swapaxes(v256, 0, 1), 16).swapaxes(0, 1),
        wbd=wbd.astype(BF16), wpw=w_pw2_full.astype(BF16), ws=ws_masked.astype(BF16),
        bsd=jnp.repeat(jnp.swapaxes(b_s_d, 1, 2), GW // 4, axis=2))
    wst = jnp.swapaxes(ws_masked, 2, 3).astype(BF16)

    def vec_rows(l):
        rows = jnp.stack([mod[l, 0:D], mod[l, D:2 * D], mod[l, 2 * D:3 * D], norm_g[l]], axis=0)
        return _pad_rows(rows, 8)

    xin, zs, ys, convs, vecs = [xs], [], [], [], []
    zero = jnp.zeros((), jnp.int32)
    for l in range(nl):
        vecs.append(vec_rows(l))
        started, token = None, 0.0
        if l + 1 < nl:
            started = _gather_shards_start(w_in_b[l:l + 1], w_out_b[l:l + 1])
            token = started[-1][0, 0]
        zl, xn, yl, cl = _fwd_layer(xin[l], vecs[l] + token, w_in_layers[l], w_out_layers[l], mp, l, tm)
        zs.append(zl)
        convs.append(cl)
        xin.append(xn)
        ys.append(yl)
        if started is not None:
            own_in, own_out, land_in, land_out = _gather_shards_wait(*started[:6], cl)
            w_in_layers.append(lax.dynamic_update_slice(land_in, own_in[:, None], (zero, chip, zero, zero)))
            w_out_layers.append(lax.dynamic_update_slice(land_out, own_out[:, None], (zero, chip, zero, zero)))
    dx, fin_sums = _final(xin[nl], target, _pad_rows(final_g[None, :], 8), tt)

    g_in, g_out, small, dmod = None, None, [None] * nl, [None] * nl
    for l in reversed(range(nl)):
        dz, g_out, gwbd, gwpw, gws, gbs, gwdw, g256, g1024 = _bwd_mix(
            dx, ys[l], convs[l], zs[l], vecs[l], mp, w_out_layers[l], wst, l, tm, gwout_all=g_out)
        dx, g_in, sums = _bwd_in(dz, xin[l], dx, vecs[l], w_in_layers[l], l, nl, tt, gw_all=g_in)
        dmod[l] = jnp.concatenate([sums[0], sums[1], g1024[0]])
        small[l] = dict(
            norm_g=sums[2], w_conv_a=g256[8:8 + CONV_A],
            w_pool=jnp.stack([gwbd[64 * g:64 * (g + 1), 64 * g:64 * (g + 1)] for g in range(4)]),
            pool_scale=g256[0], w_dw_c=gwdw[:CONV_C], b_dw_c=g256[1], ln_g_c=g256[2], ln_b_c=g256[3],
            w_pw2_c=gwpw, b_pw2_c=g256[4], ln_g_d=g256[5], ln_b_d=g256[6], w_s_d=gws, b_s_d=gbs[:, :4].T)
    grad_x = dx[None]

    rs_in = _reduce_scatter_start(g_in, cidx, "in")
    rs_out = _reduce_scatter_start(g_out, cidx, "out")
    token = rs_in[-1][0, 0] + rs_out[-1][0, 0]

    dmod_all, dmod_sum = _allgather8(
        (jnp.stack(dmod) + token).reshape(nl * 3 * D // 128, 128), "gather_dmod", reduce=True)
    dmod_all = dmod_all.reshape(N_DEV, nl, 3 * D)
    grad_b_ada = dmod_sum.reshape(nl, 3 * D)
    dmod_s = jnp.swapaxes(lax.dynamic_slice_in_dim(dmod_all, chip * PIECE, PIECE, axis=2), 0, 1)
    grad_w_ada = _ada_backward(c_all, dmod_s)

    small_names = [n for n in WEIGHTS if n not in BIG and n not in ('b_ada', 'final_g')]
    stacked = [jnp.stack([small[l][n] for l in range(nl)]) for n in small_names]
    small_buf, small_off = _pack(stacked + [fin_sums[0], fin_sums[2, 0:1]], 8 * N_DEV)
    reduced = _unpack(_allreduce8(small_buf, "allreduce_small"), small_off)
    grads = dict(zip(small_names, reduced[:len(small_names)]))
    grads['final_g'] = reduced[-2]
    loss = reduced[-1][0]
    grads['b_ada'] = grad_b_ada
    grads['w_ada'] = grad_w_ada
    grads['w_conv_a'] = lax.dynamic_slice_in_dim(grads['w_conv_a'], chip * 64, 64, axis=2)
    grads['w_dw_c'] = lax.dynamic_slice_in_dim(grads['w_dw_c'], chip * 64, 64, axis=2)
    grads['w_pw2_c'] = lax.dynamic_slice_in_dim(grads['w_pw2_c'], chip * 64, 64, axis=1)

    delta, new_m, new_v = {}, {}, {}
    shape = w_ada.shape
    as2d = lambda a: a.reshape(-1, shape[-1])
    d2, m2, v2 = _adamw(as2d(w_ada), as2d(grads['w_ada']), as2d(m_w_ada), as2d(v_w_ada), 'w_ada')
    delta['w_ada'], new_m['w_ada'], new_v['w_ada'] = d2.reshape(shape), m2.reshape(shape), v2.reshape(shape)
    rest = [n for n in WEIGHTS if n not in BIG]
    ds, nms, nvs = _adamw_many([weights[n] for n in rest], [grads[n] for n in rest],
                               [mom_m[n] for n in rest], [mom_v[n] for n in rest])
    for n, dn, mn, vn in zip(rest, ds, nms, nvs):
        delta[n], new_m[n], new_v[n] = dn, mn, vn
    halves = {'w_in': _reduce_scatter_finish(rs_in, nvs[-1], sidx, "in"),
              'w_out': _reduce_scatter_finish(rs_out, nvs[-1], sidx, "out")}
    for n in ('w_in', 'w_out'):
        grads[n], delta[n], new_m[n], new_v[n] = _adamw_halves(
            weights[n], *halves[n], mom_m[n], mom_v[n], cidx, n)

    return (loss, grad_x, *[grads[n] for n in WEIGHTS], *[delta[n] for n in WEIGHTS],
            *[new_m[n] for n in WEIGHTS], *[new_v[n] for n in WEIGHTS])
```

```python
import math

import jax
import jax.numpy as jnp
import numpy as np
from jax import lax
from jax.experimental import pallas as pl
from jax.experimental.pallas import tpu as pltpu

F32 = jnp.float32
BF16 = jnp.bfloat16
MESH = pl.DeviceIdType.MESH

D = 1024
DIN = 3072
GW = 256
PIECE = 768
N_CHIP = 4
N_DEV = 8
HALO = 32
CONV_A = 3
CONV_C = 31
CHUNK = 128
CONV_ROWS = 64
ROW_BLOCK = 32
EPS = 1e-6
VMEM_LIMIT = 56 * 1024 * 1024

ADAM_LR, ADAM_B1, ADAM_B2, ADAM_EPS, ADAM_WD, ADAM_STEP = 0.001, 0.9, 0.999, 1e-08, 0.01, 10
GELU_K = math.sqrt(2.0 / math.pi)
GELU_C = 0.044715

WEIGHTS = ['norm_g', 'w_ada', 'b_ada', 'w_in', 'w_conv_a', 'w_pool', 'pool_scale', 'w_dw_c', 'b_dw_c', 'ln_g_c',
           'ln_b_c', 'w_pw2_c', 'b_pw2_c', 'ln_g_d', 'ln_b_d', 'w_s_d', 'b_s_d', 'w_out', 'final_g']
BIG = ('w_ada', 'w_in', 'w_out')


def _tiles(t):
    if t % 512 == 0 and t >= 2048:
        return 512, 256
    return 128, 128


def _params(sem, limit=VMEM_LIMIT):
    return pltpu.CompilerParams(dimension_semantics=sem, vmem_limit_bytes=limit)


def _sig(x):
    return jax.nn.sigmoid(x)


def _gelu(x):
    th = jnp.tanh(GELU_K * (x + GELU_C * x * x * x))
    return 0.5 * x * (1.0 + th), th


def _gelu_grad(x, th):
    return 0.5 * (1.0 + th) + 0.5 * x * (1.0 - th * th) * GELU_K * (1.0 + 3.0 * GELU_C * x * x)


def _dot(a, b):
    return jnp.dot(a, b, preferred_element_type=F32)


def _dot_nt(a, b):
    return lax.dot_general(a, b, (((1,), (1,)), ((), ())), preferred_element_type=F32)


def _dot_tn(a, b):
    return lax.dot_general(a, b, (((0,), (0,)), ((), ())), preferred_element_type=F32)


def _rowsum(x):
    return jnp.sum(x, axis=0, keepdims=True)


def _allgather8_steps(x_ref, out_ref, send_sems, recv_sems, local_sem):
    m_per = x_ref.shape[0]
    x, y, c = lax.axis_index("x"), lax.axis_index("y"), lax.axis_index("c")
    me, sibling = (x, y, c), (x, y, 1 - c)
    chips = [(1 - x, y), (x, 1 - y), (1 - x, 1 - y)]

    def rows(px, py, pc):
        return out_ref.at[pl.ds((4 * px + 2 * py + pc) * m_per, m_per), :]

    def copy(k, block, to, src=None):
        return pltpu.make_async_remote_copy(
            src_ref=rows(*block) if src is None else src, dst_ref=rows(*block),
            send_sem=send_sems.at[k], recv_sem=recv_sems.at[k], device_id=to, device_id_type=MESH)

    mine = pltpu.make_async_copy(x_ref, rows(*me), local_sem)
    first = [copy(0, me, sibling, src=x_ref)]
    first += [copy(1 + j, me, (*chip, c), src=x_ref) for j, chip in enumerate(chips)]

    def begin():
        mine.start()
        for cp in first:
            cp.start()

    def finish():
        passed = [copy(4 + j, (*chip, c), sibling) for j, chip in enumerate(chips)]
        for j, chip in enumerate(chips):
            copy(1 + j, (*chip, c), me).wait_recv()
            passed[j].start()
        copy(0, sibling, me).wait_recv()
        for j, chip in enumerate(chips):
            copy(4 + j, (*chip, 1 - c), me).wait_recv()
        for cp in first + passed:
            cp.wait_send()
        mine.wait()

    return begin, finish


_ALLGATHER8_SEMS = [pltpu.SemaphoreType.DMA((7,)), pltpu.SemaphoreType.DMA((7,)), pltpu.SemaphoreType.DMA]


def _allgather8(v, name, reduce=False):
    m_per, n = v.shape

    def body(x_ref, out_ref, *rest):
        sum_ref = rest[0] if reduce else None
        begin, finish = _allgather8_steps(x_ref, out_ref, *rest[-3:])
        begin()
        finish()
        if reduce:
            acc = out_ref[0:m_per, :]
            for d in range(1, N_DEV):
                acc = acc + out_ref[d * m_per:(d + 1) * m_per, :]
            sum_ref[...] = acc

    out_shape = [jax.ShapeDtypeStruct((N_DEV * m_per, n), v.dtype)]
    out_specs = [pl.BlockSpec(memory_space=pltpu.VMEM)]
    if reduce:
        out_shape.append(jax.ShapeDtypeStruct((m_per, n), v.dtype))
        out_specs.append(pl.BlockSpec(memory_space=pltpu.VMEM))
    res = pl.pallas_call(
        body, name=name, out_shape=tuple(out_shape),
        in_specs=[pl.BlockSpec(memory_space=pltpu.VMEM)], out_specs=tuple(out_specs),
        scratch_shapes=list(_ALLGATHER8_SEMS),
        compiler_params=pltpu.CompilerParams(vmem_limit_bytes=VMEM_LIMIT),
    )(v)
    return res if reduce else res[0]


def _allreduce8(v, name):
    rows, n = v.shape
    rc = rows // N_DEV

    def body(x_ref, sum_ref, stage, send1, recv1, send2, recv2):
        x, y, c = lax.axis_index("x"), lax.axis_index("y"), lax.axis_index("c")
        me = 4 * x + 2 * y + c
        peers = []
        for k in range(1, N_DEV):
            kx, ky, kc = (k >> 2) & 1, (k >> 1) & 1, k & 1
            peers.append((1 - x if kx else x, 1 - y if ky else y, 1 - c if kc else c))

        def chunk(ref, d):
            return ref.at[pl.ds(d * rc, rc), :]

        scatter, gather = [], []
        for k, (px, py, pc) in enumerate(peers):
            scatter.append(pltpu.make_async_remote_copy(
                src_ref=chunk(x_ref, 4 * px + 2 * py + pc), dst_ref=stage.at[me], send_sem=send1.at[k],
                recv_sem=recv1.at[k], device_id=(px, py, pc), device_id_type=MESH))
        for cp in scatter:
            cp.start()
        stage[me] = x_ref[pl.ds(me * rc, rc), :]
        for k, (px, py, pc) in enumerate(peers):
            pltpu.make_async_remote_copy(
                src_ref=chunk(x_ref, me), dst_ref=stage.at[4 * px + 2 * py + pc], send_sem=send1.at[k],
                recv_sem=recv1.at[k], device_id=(px, py, pc), device_id_type=MESH).wait_recv()
        total = stage[0]
        for d in range(1, N_DEV):
            total = total + stage[d]
        sum_ref[pl.ds(me * rc, rc), :] = total
        for k, (px, py, pc) in enumerate(peers):
            gather.append(pltpu.make_async_remote_copy(
                src_ref=chunk(sum_ref, me), dst_ref=chunk(sum_ref, me), send_sem=send2.at[k],
                recv_sem=recv2.at[k], device_id=(px, py, pc), device_id_type=MESH))
        for cp in gather:
            cp.start()
        for k, (px, py, pc) in enumerate(peers):
            theirs = 4 * px + 2 * py + pc
            pltpu.make_async_remote_copy(
                src_ref=chunk(sum_ref, theirs), dst_ref=chunk(sum_ref, theirs), send_sem=send2.at[k],
                recv_sem=recv2.at[k], device_id=(px, py, pc), device_id_type=MESH).wait_recv()
        for cp in scatter + gather:
            cp.wait_send()

    return pl.pallas_call(
        body, name=name, out_shape=jax.ShapeDtypeStruct((rows, n), v.dtype),
        in_specs=[pl.BlockSpec(memory_space=pltpu.VMEM)], out_specs=pl.BlockSpec(memory_space=pltpu.VMEM),
        scratch_shapes=[pltpu.VMEM((N_DEV, rc, n), v.dtype)] + [pltpu.SemaphoreType.DMA((N_DEV - 1,))] * 4,
        compiler_params=pltpu.CompilerParams(vmem_limit_bytes=VMEM_LIMIT),
    )(v)


def _gather_weights(w_in_s, w_out_s, small):
    nl = w_in_s.shape[0]
    shapes = ((nl, N_CHIP) + w_in_s.shape[1:], (nl, N_CHIP) + w_out_s.shape[1:])
    n_sem = 2 * 3 * 2

    def body(win_ref, wout_ref, small_ref, win_o, wout_o, small_o, send_sems, recv_sems, *small_sems):
        begin_small, finish_small = _allgather8_steps(small_ref, small_o, *small_sems)
        begin_small()
        x, y, c = lax.axis_index("x"), lax.axis_index("y"), lax.axis_index("c")
        s = 2 * x + y
        sibling = (x, y, 1 - c)
        chips = [(1 - x, y), (x, 1 - y), (1 - x, 1 - y)]
        outs = (win_o, wout_o)
        for src, dst in ((win_ref, win_o), (wout_ref, wout_o)):
            rows = src.shape[1]
            step = min(rows, 256)
            for l in range(nl):
                for r in range(0, rows, step):
                    dst[l, s, r:r + step, :] = src[l, r:r + step, :].astype(BF16)

        def copy(k, ref, piece, half, to):
            r2 = ref.shape[2] // 2
            rows = ref.at[:, piece, pl.ds(half * r2, r2), :]
            return pltpu.make_async_remote_copy(
                src_ref=rows, dst_ref=rows, send_sem=send_sems.at[k], recv_sem=recv_sems.at[k],
                device_id=to, device_id_type=MESH)

        sends = [copy(2 * j + a, ref, s, c, (px, py, c)) for j, (px, py) in enumerate(chips)
                 for a, ref in enumerate(outs)]
        for cp in sends:
            cp.start()
        finish_small()
        passed = []
        for j, (px, py) in enumerate(chips):
            for a, ref in enumerate(outs):
                copy(2 * j + a, ref, 2 * px + py, c, (px, py, c)).wait_recv()
                passed.append(copy(6 + 2 * j + a, ref, 2 * px + py, c, sibling))
                passed[-1].start()
        for j, (px, py) in enumerate(chips):
            for a, ref in enumerate(outs):
                copy(6 + 2 * j + a, ref, 2 * px + py, 1 - c, sibling).wait_recv()
        for cp in sends + passed:
            cp.wait_send()

    vmem = pl.BlockSpec(memory_space=pltpu.VMEM)
    return pl.pallas_call(
        body, name="gather_weights",
        out_shape=tuple(jax.ShapeDtypeStruct(s, BF16) for s in shapes)
        + (jax.ShapeDtypeStruct((N_DEV * small.shape[0], small.shape[1]), small.dtype),),
        in_specs=[vmem] * 3, out_specs=(vmem,) * 3,
        scratch_shapes=[pltpu.SemaphoreType.DMA((n_sem,)), pltpu.SemaphoreType.DMA((n_sem,))] + list(_ALLGATHER8_SEMS),
        compiler_params=pltpu.CompilerParams(vmem_limit_bytes=VMEM_LIMIT),
    )(w_in_s, w_out_s, small)


def _cast_bf16(arrays, after):
    n = len(arrays)

    def body(*refs):
        for src, dst in zip(refs[:n], refs[n + 1:]):
            rows = src.shape[-2]
            step = min(rows, 256)
            for l in range(src.shape[0]):
                for r in range(0, rows, step):
                    dst[l, r:r + step, :] = src[l, r:r + step, :].astype(BF16)

    vmem = pl.BlockSpec(memory_space=pltpu.VMEM)
    return pl.pallas_call(
        body, name="cast_weights", out_shape=tuple(jax.ShapeDtypeStruct(a.shape, BF16) for a in arrays),
        in_specs=[vmem] * n + [pl.BlockSpec(memory_space=pl.ANY)], out_specs=(vmem,) * n,
        compiler_params=pltpu.CompilerParams(vmem_limit_bytes=VMEM_LIMIT),
    )(*arrays, after)


def _shard_copies(a_ref, b_ref, la_ref, lb_ref, send_sems, recv_sems):
    x, y, c = lax.axis_index("x"), lax.axis_index("y"), lax.axis_index("c")
    s = 2 * x + y
    sends, recvs = [], []
    for j, (px, py) in enumerate([(1 - x, y), (x, 1 - y), (1 - x, 1 - y)]):
        for a, (src, land) in enumerate(((a_ref, la_ref), (b_ref, lb_ref))):
            k = 2 * j + a
            for slot, into in ((s, sends), (2 * px + py, recvs)):
                into.append(pltpu.make_async_remote_copy(
                    src_ref=src.at[0], dst_ref=land.at[0, slot], send_sem=send_sems.at[k], recv_sem=recv_sems.at[k],
                    device_id=(px, py, c), device_id_type=MESH))
    return sends, recvs


def _gather_shards_start(a, b):
    lands = ((1, N_CHIP) + a.shape[1:], (1, N_CHIP) + b.shape[1:])
    hbm = pl.BlockSpec(memory_space=pltpu.HBM)
    sem = pl.BlockSpec(memory_space=pltpu.SEMAPHORE)

    def body(a_ref, b_ref, la_ref, lb_ref, send_sems, recv_sems, a_thru, b_thru, la_thru, lb_thru, token):
        for cp in _shard_copies(a_ref, b_ref, la_ref, lb_ref, send_sems, recv_sems)[0]:
            cp.start()
        token[...] = jnp.zeros_like(token)

    return pl.pallas_call(
        body, name="gather_weights_start",
        out_shape=(pltpu.SemaphoreType.DMA((6,)), pltpu.SemaphoreType.DMA((6,)), pltpu.HBM(a.shape, BF16),
                   pltpu.HBM(b.shape, BF16), pltpu.HBM(lands[0], BF16), pltpu.HBM(lands[1], BF16),
                   jax.ShapeDtypeStruct((8, 128), F32)),
        in_specs=(hbm,) * 4, out_specs=(sem, sem, hbm, hbm, hbm, hbm, pl.BlockSpec(memory_space=pltpu.VMEM)),
        input_output_aliases={0: 2, 1: 3, 2: 4, 3: 5},
        compiler_params=pltpu.CompilerParams(has_side_effects=pltpu.SideEffectType.DATAFLOW_SIDE_EFFECTING),
    )(pltpu.with_memory_space_constraint(a, pltpu.HBM), pltpu.with_memory_space_constraint(b, pltpu.HBM),
      pltpu.with_memory_space_constraint(lax.empty(lands[0], BF16), pltpu.HBM),
      pltpu.with_memory_space_constraint(lax.empty(lands[1], BF16), pltpu.HBM))


def _gather_shards_wait(send_sems, recv_sems, a, b, la, lb, after):
    hbm = pl.BlockSpec(memory_space=pltpu.HBM)
    sem = pl.BlockSpec(memory_space=pltpu.SEMAPHORE)

    def body(a_ref, b_ref, la_ref, lb_ref, send_sems, recv_sems, after_ref, a_dead, b_dead, la_out, lb_out):
        sends, recvs = _shard_copies(a_ref, b_ref, la_ref, lb_ref, send_sems, recv_sems)
        for cp in sends:
            cp.wait_send()
        for cp in recvs:
            cp.wait_recv()

    out = pl.pallas_call(
        body, name="gather_weights_wait",
        out_shape=tuple(pltpu.HBM(v.shape, v.dtype) for v in (a, b, la, lb)),
        in_specs=(hbm, hbm, hbm, hbm, sem, sem, pl.BlockSpec(memory_space=pl.ANY)), out_specs=(hbm,) * 4,
        input_output_aliases={0: 0, 1: 1, 2: 2, 3: 3},
        compiler_params=pltpu.CompilerParams(has_side_effects=pltpu.SideEffectType.DATAFLOW_SIDE_EFFECTING),
    )(a, b, la, lb, send_sems, recv_sems, after)
    return out


def _send_halves_to_sibling(gs):
    n = len(gs)
    shapes = [g.shape[:2] + (g.shape[2] // 2, g.shape[3]) for g in gs]

    def body(*refs):
        g_refs, recv_refs, (send_sems, recv_sems) = refs[:n], refs[n:2 * n], refs[2 * n:]
        x, y, c = lax.axis_index("x"), lax.axis_index("y"), lax.axis_index("c")
        cps = []
        for a, (g_ref, recv_ref) in enumerate(zip(g_refs, recv_refs)):
            half = recv_ref.shape[2]
            cps.append(pltpu.make_async_remote_copy(
                src_ref=g_ref.at[:, :, pl.ds((1 - c) * half, half), :], dst_ref=recv_ref,
                send_sem=send_sems.at[a], recv_sem=recv_sems.at[a], device_id=(x, y, 1 - c), device_id_type=MESH))
        for cp in cps:
            cp.start()
        for cp in cps:
            cp.wait()

    return pl.pallas_call(
        body, name="rs_sibling",
        out_shape=tuple(jax.ShapeDtypeStruct(s, g.dtype) for s, g in zip(shapes, gs)),
        in_specs=[pl.BlockSpec(memory_space=pl.ANY)] * n, out_specs=(pl.BlockSpec(memory_space=pl.ANY),) * n,
        scratch_shapes=[pltpu.SemaphoreType.DMA((n,)), pltpu.SemaphoreType.DMA((n,))],
    )(*gs)


def _add_half(g, recv, cidx, tag):
    nl, nc, r, cdim = g.shape
    half = r // 2
    rt = min(half, 512)
    nrt = half // rt

    def body(c_ref, a_ref, b_ref, o_ref, ob_ref):
        c = c_ref[0]
        first = jnp.where(c == 0, a_ref[...], b_ref[...])
        second = jnp.where(c == 0, b_ref[...], a_ref[...])
        total = first + second
        o_ref[...] = total
        ob_ref[...] = total.astype(BF16)

    out_spec = pl.BlockSpec((None, None, rt, cdim), lambda l, j, i, c: (l, j, i, 0))
    grid_spec = pltpu.PrefetchScalarGridSpec(
        num_scalar_prefetch=1, grid=(nl, nc, nrt),
        in_specs=[pl.BlockSpec((None, None, rt, cdim), lambda l, j, i, c: (l, j, c[0] * nrt + i, 0)),
                  pl.BlockSpec((None, None, rt, cdim), lambda l, j, i, c: (l, j, i, 0))],
        out_specs=(out_spec, out_spec))
    return pl.pallas_call(
        body, name=f"rs_add_sibling_{tag}", grid_spec=grid_spec,
        out_shape=(jax.ShapeDtypeStruct((nl, nc, half, cdim), g.dtype),
                   jax.ShapeDtypeStruct((nl, nc, half, cdim), BF16)),
        compiler_params=_params(("parallel", "parallel", "parallel")),
    )(cidx, g, recv)


def _chip_copies(h_ref, land_ref, send_sems, recv_sems):
    x, y, c = lax.axis_index("x"), lax.axis_index("y"), lax.axis_index("c")
    chips = [(1 - x, y), (x, 1 - y), (1 - x, 1 - y)]
    return [pltpu.make_async_remote_copy(
        src_ref=h_ref.at[:, 2 * px + py], dst_ref=land_ref.at[k], send_sem=send_sems.at[k],
        recv_sem=recv_sems.at[k], device_id=(px, py, c), device_id_type=MESH) for k, (px, py) in enumerate(chips)]


def _exchange_chips_start(h, tag):
    nl, nc, r2, cdim = h.shape
    land_shape = (3, nl, r2, cdim)
    hbm = pl.BlockSpec(memory_space=pltpu.HBM)
    sem = pl.BlockSpec(memory_space=pltpu.SEMAPHORE)

    def body(h_ref, land_ref, send_sems, recv_sems, h_thru, land_thru, token):
        for cp in _chip_copies(h_ref, land_ref, send_sems, recv_sems):
            cp.start()
        token[...] = jnp.zeros_like(token)

    return pl.pallas_call(
        body, name=f"rs_chips_start_{tag}",
        out_shape=(pltpu.SemaphoreType.DMA((3,)), pltpu.SemaphoreType.DMA((3,)), pltpu.HBM(h.shape, h.dtype),
                   pltpu.HBM(land_shape, h.dtype), jax.ShapeDtypeStruct((8, 128), F32)),
        in_specs=(hbm, hbm), out_specs=(sem, sem, hbm, hbm, pl.BlockSpec(memory_space=pltpu.VMEM)),
        input_output_aliases={0: 2, 1: 3},
        compiler_params=pltpu.CompilerParams(has_side_effects=pltpu.SideEffectType.DATAFLOW_SIDE_EFFECTING),
    )(pltpu.with_memory_space_constraint(h, pltpu.HBM),
      pltpu.with_memory_space_constraint(lax.empty(land_shape, h.dtype), pltpu.HBM))


def _exchange_chips_wait(send_sems, recv_sems, h_thru, land_thru, after, tag):
    hbm = pl.BlockSpec(memory_space=pltpu.HBM)
    sem = pl.BlockSpec(memory_space=pltpu.SEMAPHORE)

    def body(h_ref, land_ref, send_sems, recv_sems, after_ref, h_dead, got_ref):
        for cp in _chip_copies(h_ref, land_ref, send_sems, recv_sems):
            cp.wait_send()
            cp.wait_recv()

    return pl.pallas_call(
        body, name=f"rs_chips_wait_{tag}",
        out_shape=(pltpu.HBM(h_thru.shape, h_thru.dtype), pltpu.HBM(land_thru.shape, land_thru.dtype)),
        in_specs=(hbm, hbm, sem, sem, pl.BlockSpec(memory_space=pl.ANY)), out_specs=(hbm, hbm),
        input_output_aliases={0: 0, 1: 1},
        compiler_params=pltpu.CompilerParams(has_side_effects=pltpu.SideEffectType.DATAFLOW_SIDE_EFFECTING),
    )(h_thru, land_thru, send_sems, recv_sems, after)[1]


def _add_chips(h, recv, sidx, tag):
    nl, nc, r2, cdim = h.shape
    rt = min(r2, 512)

    def body(s_ref, own_ref, r_ref, o_ref):
        s = s_ref[0]
        got = [r_ref[k].astype(F32) for k in range(3)]
        total = None
        for chip in range(N_CHIP):
            term = jnp.where(s == chip, own_ref[...],
                             jnp.where((s ^ 2) == chip, got[0], jnp.where((s ^ 1) == chip, got[1], got[2])))
            total = term if total is None else total + term
        o_ref[...] = total

    grid_spec = pltpu.PrefetchScalarGridSpec(
        num_scalar_prefetch=1, grid=(nl, r2 // rt),
        in_specs=[pl.BlockSpec((None, None, rt, cdim), lambda l, i, s: (l, s[0], i, 0)),
                  pl.BlockSpec((3, None, rt, cdim), lambda l, i, s: (0, l, i, 0))],
        out_specs=pl.BlockSpec((None, rt, cdim), lambda l, i, s: (l, i, 0)))
    return pl.pallas_call(
        body, name=f"rs_add_chips_{tag}", grid_spec=grid_spec,
        out_shape=jax.ShapeDtypeStruct((nl, r2, cdim), h.dtype),
        compiler_params=_params(("parallel", "parallel")),
    )(sidx, h, recv)


def _swap_halves(reds):
    n = len(reds)

    def body(*refs):
        send_sems, recv_sems = refs[2 * n:]
        x, y, c = lax.axis_index("x"), lax.axis_index("y"), lax.axis_index("c")
        cps = [pltpu.make_async_remote_copy(
            src_ref=refs[a], dst_ref=refs[n + a], send_sem=send_sems.at[a], recv_sem=recv_sems.at[a],
            device_id=(x, y, 1 - c), device_id_type=MESH) for a in range(n)]
        for cp in cps:
            cp.start()
        for cp in cps:
            cp.wait()

    return pl.pallas_call(
        body, name="rs_swap", out_shape=tuple(jax.ShapeDtypeStruct(r.shape, r.dtype) for r in reds),
        in_specs=[pl.BlockSpec(memory_space=pl.ANY)] * n, out_specs=(pl.BlockSpec(memory_space=pl.ANY),) * n,
        scratch_shapes=[pltpu.SemaphoreType.DMA((n,)), pltpu.SemaphoreType.DMA((n,))],
    )(*reds)


def _reduce_scatter_start(gs, tags, cidx):
    recvs = _send_halves_to_sibling(gs)
    started = []
    for g, recv, tag in zip(gs, recvs, tags):
        chip_sum, chip_sum_bf16 = _add_half(g, recv, cidx, tag)
        started.append((chip_sum, _exchange_chips_start(chip_sum_bf16, tag)))
    return started


def _reduce_scatter_finish(started, tags, after, sidx):
    reds = []
    for (chip_sum, (send_sems, recv_sems, h_thru, land_thru, _)), tag in zip(started, tags):
        got = _exchange_chips_wait(send_sems, recv_sems, h_thru, land_thru, after, tag)
        reds.append(_add_chips(chip_sum, got, sidx, tag))
    return list(zip(reds, _swap_halves(reds)))


def _ada_forward(c_all, w_ada_s, b_s):
    nl = w_ada_s.shape[0]

    def body(c_ref, w_ref, b_ref, o_ref):
        cv = c_ref[...]
        ca = (cv * _sig(cv)).astype(BF16)
        for l in range(nl):
            o_ref[l] = _dot(ca, w_ref[l].astype(BF16)) + b_ref[l]

    return pl.pallas_call(
        body, name="ada_forward", out_shape=jax.ShapeDtypeStruct((nl, N_DEV, PIECE), F32),
        compiler_params=pltpu.CompilerParams(vmem_limit_bytes=VMEM_LIMIT),
    )(c_all, w_ada_s, b_s)


def _ada_backward(c_all, dmod_s):
    nl = dmod_s.shape[0]

    def body(c_ref, d_ref, o_ref):
        cv = c_ref[...]
        ca = (cv * _sig(cv)).astype(BF16)
        for l in range(nl):
            o_ref[l] = _dot_tn(ca, d_ref[l].astype(BF16))

    return pl.pallas_call(
        body, name="ada_backward", out_shape=jax.ShapeDtypeStruct((nl, D, PIECE), F32),
        compiler_params=pltpu.CompilerParams(vmem_limit_bytes=VMEM_LIMIT),
    )(c_all, dmod_s)


def _normed(xv, vecs):
    r = lax.rsqrt(jnp.mean(xv * xv, axis=-1, keepdims=True) + EPS)
    xhat = xv * r
    hn = xhat * vecs[3:4, :]
    return r, xhat, hn


def _bwd_in(dz, x, dxo, vecs, w, l, nl, tt, gw_all=None):
    t = x.shape[0]
    nt = t // tt

    def body(dz_ref, x_ref, dxo_ref, v_ref, w_any, *rest):
        dx_ref, gw_any, sums_ref, w_vmem, gw_acc, sem = rest[-6:]
        i = pl.program_id(0)

        @pl.when(i == 0)
        def _():
            cp = pltpu.make_async_copy(w_any.at[0], w_vmem, sem)
            cp.start()
            cp.wait()
            gw_acc[...] = jnp.zeros_like(gw_acc)
            sums_ref[...] = jnp.zeros_like(sums_ref)

        vecs_v = v_ref[...]
        r, xhat, hn = _normed(x_ref[...], vecs_v)
        one_scale = 1.0 + vecs_v[1:2, :]
        hb = (hn * one_scale + vecs_v[0:1, :]).astype(BF16)
        dh = None
        for j in range(N_CHIP):
            dzj = dz_ref[:, PIECE * j:PIECE * (j + 1)]
            part = _dot_nt(dzj, w_vmem[j])
            dh = part if dh is None else dh + part
            gw_acc[j] += _dot_tn(hb, dzj)
        gain = vecs_v[3:4, :] * one_scale
        dh_xhat = dh * xhat
        sums_ref[0:1, :] += _rowsum(dh)
        sums_ref[1:2, :] += _rowsum(dh_xhat)
        dx_ref[...] = dxo_ref[...] + r * (dh * gain - xhat * jnp.mean(dh_xhat * gain, axis=-1, keepdims=True))

        @pl.when(i == nt - 1)
        def _():
            cp = pltpu.make_async_copy(gw_acc, gw_any.at[l], sem)
            cp.start()
            dh_xhat_sum = sums_ref[1:2, :]
            sums_ref[2:3, :] = dh_xhat_sum * one_scale
            sums_ref[1:2, :] = dh_xhat_sum * vecs_v[3:4, :]
            cp.wait()

    carried = [] if gw_all is None else [gw_all]
    return pl.pallas_call(
        body, name=f"bwd_in_{l}", grid=(nt,),
        in_specs=[pl.BlockSpec((tt, DIN), lambda i: (i, 0)), pl.BlockSpec((tt, D), lambda i: (i, 0)),
                  pl.BlockSpec((tt, D), lambda i: (i, 0)), pl.BlockSpec((8, D), lambda i: (0, 0)),
                  pl.BlockSpec(memory_space=pl.ANY)] + [pl.BlockSpec(memory_space=pl.ANY)] * len(carried),
        out_specs=(pl.BlockSpec((tt, D), lambda i: (i, 0)), pl.BlockSpec(memory_space=pl.ANY),
                   pl.BlockSpec((8, D), lambda i: (0, 0))),
        out_shape=(jax.ShapeDtypeStruct((t, D), F32), jax.ShapeDtypeStruct((nl, N_CHIP, D, PIECE), F32),
                   jax.ShapeDtypeStruct((8, D), F32)),
        scratch_shapes=[pltpu.VMEM((N_CHIP, D, PIECE), BF16), pltpu.VMEM((N_CHIP, D, PIECE), F32),
                        pltpu.SemaphoreType.DMA],
        input_output_aliases={5: 1} if carried else {},
        compiler_params=_params(("arbitrary",)),
    )(dz, x, dxo, vecs, w, *carried)


def _col(ref, k):
    return ref[:, GW * k:GW * (k + 1)]


def _lane_group():
    return lax.broadcasted_iota(jnp.int32, (1, GW), 1) // (GW // 4)


def _pool_window(group):
    return jnp.where(group == 0, 2.0, jnp.where(group == 1, 4.0, jnp.where(group == 2, 8.0, 16.0)))


def _by_group(group, a, b, c, d):
    return jnp.where(group == 0, a, jnp.where(group == 1, b, jnp.where(group == 2, c, d)))


def _layer_norm(x, g, b):
    mu = jnp.mean(x, axis=-1, keepdims=True)
    xc = x - mu
    rstd = lax.rsqrt(jnp.mean(xc * xc, axis=-1, keepdims=True) + EPS)
    xh = xc * rstd
    return xh * g + b, xh, rstd


def _layer_norm_grad(d_out, xh, rstd, g):
    dxh = d_out * g
    return rstd * (dxh - jnp.mean(dxh, axis=-1, keepdims=True) - xh * jnp.mean(dxh * xh, axis=-1, keepdims=True))


def _shifted_copies(ext_ref, sh_ref, n):
    for b in range(1, 8):
        sh_ref[b, 0:n - 8, :] = ext_ref[b:b + n - 8, :]


def _rows_at(ext_ref, sh_ref, start, rows):
    b = start % 8
    if b == 0 or sh_ref is None:
        return ext_ref[start:start + rows, :]
    return sh_ref[b, start - b:start - b + rows, :]


def _conv_taps(ext_ref, w_ref, n_taps, first_row, tm, out_ref, flip=False, sh_ref=None):
    for rb in range(0, tm, CONV_ROWS):
        acc = None
        for k in range(n_taps):
            wk = n_taps - 1 - k if flip else k
            term = w_ref[wk:wk + 1, :] * _rows_at(ext_ref, sh_ref, first_row + rb + k, CONV_ROWS)
            acc = term if acc is None else acc + term
        out_ref[rb:rb + CONV_ROWS, :] = acc


def _mix_forward(zc, zp, first, row0, tm, p, s, after, conv_c=None):
    keep = jnp.where(first, 0.0, 1.0)
    group = _lane_group()
    a_b, a_c, a_x, a_g = _col(zc, 0), _col(zc, 1), _col(zc, 2), _col(zc, 3)
    s['ua'][0:HALO, :] = _col(zp, 1) * _col(zp, 2) * keep
    s['ua'][HALO:HALO + tm, :] = a_c * a_x
    _conv_taps(s['ua'], p['wa'], CONV_A, HALO - (CONV_A - 1), tm, s['cv'])
    cv = s['cv'][...]
    sg_a = _sig(a_g)
    f = dict(a_b=a_b, a_c=a_c, a_x=a_x, a_g=a_g, cv=cv, sg_a=sg_a)
    after(0, f, (a_b * cv * (a_g * sg_a)).astype(BF16))
    b_p, b_g = _col(zc, 4), _col(zc, 5)
    s['p0'][0:HALO, :] = _col(zp, 4) * keep
    s['p0'][HALO:HALO + tm, :] = b_p
    n = HALO + tm
    s['p1'][8:n, :] = s['p0'][8:n, :] + s['p0'][7:n - 1, :]
    w2 = s['p1'][HALO:n, :]
    s['p0'][16:n, :] = s['p1'][16:n, :] + s['p1'][14:n - 2, :]
    w4 = s['p0'][HALO:n, :]
    s['p1'][24:n, :] = s['p0'][24:n, :] + s['p0'][20:n - 4, :]
    w8 = s['p1'][HALO:n, :]
    w16 = w8 + s['p1'][HALO - 8:n - 8, :]
    tpos = (row0 + lax.broadcasted_iota(jnp.int32, (tm, 1), 0) + 1).astype(F32)
    inv_cnt = 1.0 / jnp.minimum(tpos, _pool_window(group))
    pooled = (_by_group(group, w2, w4, w8, w16) * inv_cnt - b_p).astype(BF16)
    q = _dot(pooled, p['wbd'][...])
    sg_b = _sig(b_g)
    f = dict(b_g=b_g, pooled=pooled, q=q, sg_b=sg_b, inv_cnt=inv_cnt)
    after(1, f, (q * p['v256'][0:1, :] * (b_g * sg_b)).astype(BF16))
    c_a, c_gl, c_g = _col(zc, 6), _col(zc, 7), _col(zc, 8)
    sg_gl = _sig(c_gl)
    zp_gl = _col(zp, 7)
    s['hc'][0:HALO, :] = _col(zp, 6) * _sig(zp_gl) * keep
    s['hc'][HALO:HALO + tm, :] = c_a * sg_gl
    _shifted_copies(s['hc'], s['hcs'], HALO + tm)
    if conv_c is None:
        _conv_taps(s['hc'], p['wdw'], CONV_C, HALO - (CONV_C - 1), tm, s['co'], sh_ref=s['hcs'])
        conv_c = s['co']
    conv_v = conv_c[...]
    co = conv_v + p['v256'][1:2, :]
    ln_c, xh_c, rstd_c = _layer_norm(co, p['v256'][2:3, :], p['v256'][3:4, :])
    sg_ln = _sig(ln_c)
    sc = (ln_c * sg_ln).astype(BF16)
    pw = _dot(sc, p['wpw'][...]) + p['v256'][4:5, :]
    sg_c = _sig(c_g)
    f = dict(c_a=c_a, c_g=c_g, sg_gl=sg_gl, ln_c=ln_c, xh_c=xh_c, rstd_c=rstd_c, sg_ln=sg_ln, sc=sc, pw=pw,
             sg_c=sg_c, conv_c=conv_v)
    after(2, f, (pw * (c_g * sg_c)).astype(BF16))
    d_u, d_v, d_g = _col(zc, 9), _col(zc, 10), _col(zc, 11)
    gu, th_u = _gelu(d_u)
    gv, th_v = _gelu(d_v)
    v, xh_d, rstd_d = _layer_norm(gv, p['v256'][5:6, :], p['v256'][6:7, :])
    vb = v.astype(BF16)
    parts = []
    for ch in range(tm // CHUNK):
        vc = vb[ch * CHUNK:(ch + 1) * CHUNK, :]
        parts.append(_by_group(group, *[_dot(p['ws'][h], vc) for h in range(4)]) + p['bsd'][...])
    mx = parts[0] if len(parts) == 1 else jnp.concatenate(parts, axis=0)
    sg_d = _sig(d_g)
    f = dict(d_u=d_u, d_v=d_v, d_g=d_g, gu=gu, th_u=th_u, th_v=th_v, xh_d=xh_d, rstd_d=rstd_d, vb=vb, mx=mx,
             sg_d=sg_d)
    after(3, f, (gu * mx * (d_g * sg_d)).astype(BF16))


def _mix_scratch(tm):
    ext = pltpu.VMEM((HALO + tm, GW), F32)
    tile = pltpu.VMEM((tm, GW), F32)
    return dict(ua=ext, cv=tile, p0=ext, p1=ext, hc=ext, co=tile, hcs=pltpu.VMEM((8, HALO + tm, GW), F32))


_MIX_PARAMS = ('wa', 'wdw', 'v256', 'wbd', 'wpw', 'ws', 'bsd', 'wout')


def _mix_param_specs(l):
    def whole(*shape, at=l):
        nd = len(shape)
        return pl.BlockSpec((None,) + shape, lambda i, _nd=nd: (at,) + (0,) * _nd)
    return [whole(8, GW), whole(32, GW), whole(16, GW), whole(GW, GW), whole(GW, GW), whole(4, CHUNK, CHUNK),
            whole(CHUNK, GW), whole(N_CHIP, GW, D, at=0)]


def _mix_param_arrays(mp, w_out_l):
    return [mp[k] for k in _MIX_PARAMS[:-1]] + [w_out_l]


def _fwd_layer(x, vecs, w, w_out_l, mp, l, tm):
    t = x.shape[0]
    nt = t // tm
    names = list(_mix_scratch(tm))
    n_p = len(_MIX_PARAMS)

    def body(xm_ref, xr_ref, v_ref, w_ref, *rest):
        p = dict(zip(_MIX_PARAMS, rest[:n_p]))
        z_ref, xn_ref, y_ref, conv_ref = rest[n_p:n_p + 4]
        s = dict(zip(names, rest[n_p + 4:n_p + 4 + len(names)]))
        z_next, z_cur, z_halo = rest[n_p + 4 + len(names):]
        j = pl.program_id(0)

        @pl.when(j == 0)
        def _():
            z_cur[...] = jnp.zeros_like(z_cur)
            z_halo[...] = jnp.zeros_like(z_halo)

        vecs_v = v_ref[...]
        _, _, hn = _normed(xm_ref[...], vecs_v)
        hb = (hn * (1.0 + vecs_v[1:2, :]) + vecs_v[0:1, :]).astype(BF16)
        for k in range(N_CHIP):
            part = _dot(hb, w_ref[k])
            z_ref[:, PIECE * k:PIECE * (k + 1)] = part
            z_next[:, PIECE * k:PIECE * (k + 1)] = part

        tile = jnp.maximum(j - 1, 0)
        mixed = []
        _mix_forward(z_cur, z_halo, tile == 0, tile * tm, tm, p, s, lambda k, f, yk: mixed.append((f, yk)))
        y = None
        for k, (_, yk) in enumerate(mixed):
            part = _dot(yk, p['wout'][k])
            y = part if y is None else y + part
        y_ref[...] = y
        xn_ref[...] = xr_ref[...] + vecs_v[2:3, :] * y
        conv_ref[...] = mixed[2][0]['conv_c']

        z_halo[...] = z_cur[tm - HALO:tm, :]
        z_cur[...] = z_next[...]

    def ahead(j):
        return jnp.minimum(j, nt - 1)

    def behind(j):
        return jnp.maximum(j - 1, 0)

    return pl.pallas_call(
        body, name=f"fwd_layer_{l}", grid=(nt + 1,),
        in_specs=[pl.BlockSpec((tm, D), lambda j: (ahead(j), 0)), pl.BlockSpec((tm, D), lambda j: (behind(j), 0)),
                  pl.BlockSpec((8, D), lambda j: (0, 0)),
                  pl.BlockSpec((None, N_CHIP, D, PIECE), lambda j: (0, 0, 0, 0))] + _mix_param_specs(l),
        out_specs=(pl.BlockSpec((tm, DIN), lambda j: (ahead(j), 0)), pl.BlockSpec((tm, D), lambda j: (behind(j), 0)),
                   pl.BlockSpec((tm, D), lambda j: (behind(j), 0)), pl.BlockSpec((tm, GW), lambda j: (behind(j), 0))),
        out_shape=(jax.ShapeDtypeStruct((t, DIN), F32), jax.ShapeDtypeStruct((t, D), F32),
                   jax.ShapeDtypeStruct((t, D), F32), jax.ShapeDtypeStruct((t, GW), F32)),
        scratch_shapes=list(_mix_scratch(tm).values())
        + [pltpu.VMEM((tm, DIN), F32), pltpu.VMEM((tm, DIN), F32), pltpu.VMEM((HALO, DIN), F32)],
        compiler_params=_params(("arbitrary",)),
    )(x, x, vecs, w, *_mix_param_arrays(mp, w_out_l))


def _bwd_mix(dxo, y, conv, z, vecs, mp, w_out_l, wst, l, tm, gwout_all=None):
    t = dxo.shape[0]
    nt = t // tm
    nl = wst.shape[0]
    carried = [] if gwout_all is None else [gwout_all]
    per_halo = tm // HALO
    fwd_names = list(_mix_scratch(tm))
    ext = pltpu.VMEM((tm + HALO, GW), F32)
    carry = pltpu.VMEM((HALO, GW), F32)
    tile = pltpu.VMEM((tm, GW), F32)
    bwd_scratch = dict(ea=ext, eb=ext, eb2=ext, ec=ext, ca=carry, cb=carry, cc=carry, dua=tile, dhc=tile,
                       gbacc=pltpu.VMEM((CHUNK, GW), F32), ecs=pltpu.VMEM((8, tm + HALO, GW), F32),
                       dys=pltpu.VMEM((1, tm, GW), F32))
    bwd_names = list(bwd_scratch)
    n_p = len(_MIX_PARAMS)

    def body(dxo_ref, y_ref, conv_ref, zc, zp, v_ref, *rest):
        p = dict(zip(_MIX_PARAMS, rest[:n_p]))
        wst_ref = rest[n_p]
        first_out = n_p + 1 + len(carried)
        dz_ref, gwout, gwbd, gwpw, gws, gbs, gwdw, g256, g1024 = rest[first_out:first_out + 9]
        s = dict(zip(fwd_names + bwd_names, rest[first_out + 9:]))
        i = pl.program_id(0)
        ti = nt - 1 - i
        group = _lane_group()

        @pl.when(i == 0)
        def _():
            for ref in (gwout, gwbd, gwpw, gws, gbs, gwdw, g256, g1024, s['ca'], s['cb'], s['cc'], s['gbacc']):
                ref[...] = jnp.zeros_like(ref)

        def put(k, val):
            dz_ref[:, GW * k:GW * (k + 1)] = val.astype(BF16)

        def bwd_a(f, dya):
            taps = [None] * CONV_A
            for r in range(0, tm, ROW_BLOCK):
                rs = slice(r, r + ROW_BLOCK)
                a_b, a_g = zc[rs, 0:GW], zc[rs, 3 * GW:4 * GW]
                dy_a, cv = s['dys'][0, rs, :], s['cv'][rs, :]
                sg = _sig(a_g)
                silu = a_g * sg
                t = dy_a * cv
                dz_ref[rs, 0:GW] = (t * silu).astype(BF16)
                dz_ref[rs, 3 * GW:4 * GW] = (t * a_b * (sg * (1.0 + a_g * (1.0 - sg)))).astype(BF16)
                d_cv = dy_a * a_b * silu
                s['ea'][rs, :] = d_cv
                for k in range(CONV_A):
                    first = HALO - (CONV_A - 1) + k + r
                    term = d_cv * s['ua'][first:first + ROW_BLOCK, :]
                    taps[k] = term if taps[k] is None else taps[k] + term
            for k in range(CONV_A):
                g256[8 + k:9 + k, :] += _rowsum(taps[k])
            s['ea'][tm:tm + HALO, :] = s['ca'][...]
            s['ca'][...] = s['ea'][0:HALO, :]
            _conv_taps(s['ea'], p['wa'], CONV_A, 0, tm, s['dua'], flip=True)
            for r in range(0, tm, ROW_BLOCK):
                rs = slice(r, r + ROW_BLOCK)
                d_u = s['dua'][rs, :]
                dz_ref[rs, GW:2 * GW] = (d_u * zc[rs, 2 * GW:3 * GW]).astype(BF16)
                dz_ref[rs, 2 * GW:3 * GW] = (d_u * zc[rs, GW:2 * GW]).astype(BF16)

        def bwd_b(f, dyb):
            b_g, q, sg_b, inv_cnt = f['b_g'], f['q'], f['sg_b'], f['inv_cnt']
            scale_b = p['v256'][0:1, :]
            silu_b = b_g * sg_b
            g256[0:1, :] += _rowsum(dyb * q * silu_b)
            put(5, dyb * q * scale_b * (sg_b * (1.0 + b_g * (1.0 - sg_b))))
            d_q = (dyb * scale_b * silu_b).astype(BF16)
            gwbd[...] += _dot_tn(f['pooled'], d_q)
            d_pooled = _dot_nt(d_q, p['wbd'][...])
            gp = d_pooled * inv_cnt
            n = tm + HALO
            s['eb'][0:tm, :] = gp
            s['eb'][tm:n, :] = s['cb'][...]
            s['cb'][...] = gp[0:HALO, :]
            s['eb2'][0:n - 8, :] = s['eb'][0:n - 8, :] + s['eb'][1:n - 7, :]
            r2 = s['eb2'][0:tm, :]
            s['eb'][0:n - 16, :] = s['eb2'][0:n - 16, :] + s['eb2'][2:n - 14, :]
            r4 = s['eb'][0:tm, :]
            s['eb2'][0:n - 24, :] = s['eb'][0:n - 24, :] + s['eb'][4:n - 20, :]
            r8 = s['eb2'][0:tm, :]
            r16 = r8 + s['eb2'][8:tm + 8, :]
            put(4, _by_group(group, r2, r4, r8, r16) - d_pooled)

        def bwd_c(f, dyc):
            c_a, c_g, sg_gl, ln_c, xh_c, rstd_c, sg_ln, pw, sg_c = (
                f['c_a'], f['c_g'], f['sg_gl'], f['ln_c'], f['xh_c'], f['rstd_c'], f['sg_ln'], f['pw'], f['sg_c'])
            put(8, dyc * pw * (sg_c * (1.0 + c_g * (1.0 - sg_c))))
            d_pw = dyc * (c_g * sg_c)
            g256[4:5, :] += _rowsum(d_pw)
            d_pwb = d_pw.astype(BF16)
            gwpw[...] += _dot_tn(f['sc'], d_pwb)
            d_ln = _dot_nt(d_pwb, p['wpw'][...]) * (sg_ln * (1.0 + ln_c * (1.0 - sg_ln)))
            g256[2:3, :] += _rowsum(d_ln * xh_c)
            g256[3:4, :] += _rowsum(d_ln)
            d_co = _layer_norm_grad(d_ln, xh_c, rstd_c, p['v256'][2:3, :])
            g256[1:2, :] += _rowsum(d_co)
            for k in range(CONV_C):
                gwdw[k:k + 1, :] += _rowsum(d_co * _rows_at(s['hc'], s['hcs'], HALO - (CONV_C - 1) + k, tm))
            s['ec'][0:tm, :] = d_co
            s['ec'][tm:tm + HALO, :] = s['cc'][...]
            s['cc'][...] = d_co[0:HALO, :]
            _shifted_copies(s['ec'], s['ecs'], tm + HALO)
            _conv_taps(s['ec'], p['wdw'], CONV_C, 0, tm, s['dhc'], flip=True, sh_ref=s['ecs'])
            d_hc = s['dhc'][...]
            put(6, d_hc * sg_gl)
            put(7, d_hc * c_a * sg_gl * (1.0 - sg_gl))

        def bwd_d(f, dyd):
            d_uu, d_vv, d_g, gu, th_u, th_v, xh_d, rstd_d, vb, mx, sg_d = (
                f['d_u'], f['d_v'], f['d_g'], f['gu'], f['th_u'], f['th_v'], f['xh_d'], f['rstd_d'], f['vb'],
                f['mx'], f['sg_d'])
            silu_d = d_g * sg_d
            put(9, dyd * mx * silu_d * _gelu_grad(d_uu, th_u))
            put(11, dyd * gu * mx * (sg_d * (1.0 + d_g * (1.0 - sg_d))))
            d_mx = dyd * gu * silu_d
            dv_parts = []
            gb = None
            for ch in range(tm // CHUNK):
                dmc = d_mx[ch * CHUNK:(ch + 1) * CHUNK, :]
                gb = dmc if gb is None else gb + dmc
                dmb = dmc.astype(BF16)
                vc = vb[ch * CHUNK:(ch + 1) * CHUNK, :]
                for h in range(4):
                    gws[h] += _dot_nt(jnp.where(group == h, dmb, jnp.zeros_like(dmb)), vc)
                dv_parts.append(_by_group(group, *[_dot(wst_ref[h], dmb) for h in range(4)]))
            s['gbacc'][...] += gb
            d_v = dv_parts[0] if len(dv_parts) == 1 else jnp.concatenate(dv_parts, axis=0)
            g256[5:6, :] += _rowsum(d_v * xh_d)
            g256[6:7, :] += _rowsum(d_v)
            d_gv = _layer_norm_grad(d_v, xh_d, rstd_d, p['v256'][5:6, :])
            put(10, d_gv * _gelu_grad(d_vv, th_v))

        mixed = []
        _mix_forward(zc, zp, ti == 0, ti * tm, tm, p, s, lambda j, f, yj: mixed.append((f, yj)), conv_c=conv_ref)
        dxo_v = dxo_ref[...]
        g1024[0:1, :] += _rowsum(dxo_v * y_ref[...])
        dy = (dxo_v * v_ref[2:3, :]).astype(BF16)
        s['dys'][0] = _dot_nt(dy, p['wout'][0])
        dys = [None] + [_dot_nt(dy, p['wout'][j]) for j in range(1, 4)]
        for j, (f, yj) in enumerate(mixed):
            gwout[j] += _dot_tn(yj, dy)
        for (f, _), backward, dyj in zip(mixed, (bwd_a, bwd_b, bwd_c, bwd_d), dys):
            backward(f, dyj)

        @pl.when(i == nt - 1)
        def _():
            row = lax.broadcasted_iota(jnp.int32, (CHUNK, CHUNK), 0)
            col = lax.broadcasted_iota(jnp.int32, (CHUNK, CHUNK), 1)
            for h in range(4):
                gws[h] = jnp.where(col <= row, gws[h], 0.0)
            head_of_lane = lax.broadcasted_iota(jnp.int32, (GW, CHUNK), 0) // (GW // 4)
            sel = jnp.where(head_of_lane == lax.broadcasted_iota(jnp.int32, (GW, CHUNK), 1), 1.0, 0.0)
            gbs[...] = jnp.dot(s['gbacc'][...], sel, preferred_element_type=F32, precision=lax.Precision.HIGHEST)

    def const(*shape):
        nd = len(shape)
        return pl.BlockSpec(shape, lambda i, _nd=nd: (0,) * _nd)

    def rev(i):
        return nt - 1 - i

    out_shapes = (jax.ShapeDtypeStruct((t, DIN), BF16), jax.ShapeDtypeStruct((nl, N_CHIP, GW, D), F32),
                  jax.ShapeDtypeStruct((GW, GW), F32), jax.ShapeDtypeStruct((GW, GW), F32),
                  jax.ShapeDtypeStruct((4, CHUNK, CHUNK), F32), jax.ShapeDtypeStruct((CHUNK, CHUNK), F32),
                  jax.ShapeDtypeStruct((32, GW), F32), jax.ShapeDtypeStruct((16, GW), F32),
                  jax.ShapeDtypeStruct((8, D), F32))
    out_specs = (pl.BlockSpec((tm, DIN), lambda i: (rev(i), 0)),
                 pl.BlockSpec((None, N_CHIP, GW, D), lambda i: (l, 0, 0, 0)), const(GW, GW),
                 const(GW, GW), const(4, CHUNK, CHUNK), const(CHUNK, CHUNK), const(32, GW), const(16, GW),
                 const(8, D))
    scratch = list(_mix_scratch(tm).values()) + list(bwd_scratch.values())
    n_in = 6 + n_p + 1
    return pl.pallas_call(
        body, name=f"bwd_mix_{l}", grid=(nt,),
        in_specs=[pl.BlockSpec((tm, D), lambda i: (rev(i), 0)), pl.BlockSpec((tm, D), lambda i: (rev(i), 0)),
                  pl.BlockSpec((tm, GW), lambda i: (rev(i), 0)), pl.BlockSpec((tm, DIN), lambda i: (rev(i), 0)),
                  pl.BlockSpec((HALO, DIN), lambda i: (jnp.maximum(rev(i) * per_halo - 1, 0), 0)),
                  pl.BlockSpec((8, D), lambda i: (0, 0))]
        + _mix_param_specs(l)
        + [pl.BlockSpec((None, 4, CHUNK, CHUNK), lambda i: (l, 0, 0, 0))]
        + [pl.BlockSpec(memory_space=pl.ANY)] * len(carried),
        out_specs=out_specs, out_shape=out_shapes, scratch_shapes=scratch,
        input_output_aliases={n_in: 1} if carried else {},
        compiler_params=_params(("arbitrary",)),
    )(dxo, y, conv, z, z, vecs, *_mix_param_arrays(mp, w_out_l), wst, *carried)


def _final(x, target, fg, tt):
    t = x.shape[0]
    nt = t // tt

    def body(x_ref, t_ref, g_ref, dx_ref, sums_ref):
        i = pl.program_id(0)

        @pl.when(i == 0)
        def _():
            sums_ref[...] = jnp.zeros_like(sums_ref)

        xv = x_ref[...]
        g = g_ref[0:1, :]
        r = lax.rsqrt(jnp.mean(xv * xv, axis=-1, keepdims=True) + EPS)
        xhat = xv * r
        err = xhat * g - t_ref[...]
        sums_ref[1:2, :] += _rowsum(err * err) * (0.5 / D)
        dyv = err * (1.0 / D)
        sums_ref[0:1, :] += _rowsum(dyv * xhat)
        dxh = dyv * g
        dx_ref[...] = r * (dxh - xhat * jnp.mean(dxh * xhat, axis=-1, keepdims=True))

        @pl.when(i == nt - 1)
        def _():
            sums_ref[2:3, :] = jnp.broadcast_to(jnp.sum(sums_ref[1:2, :], axis=-1, keepdims=True), (1, D))

    return pl.pallas_call(
        body, name="final_loss", grid=(nt,),
        in_specs=[pl.BlockSpec((tt, D), lambda i: (i, 0)), pl.BlockSpec((tt, D), lambda i: (i, 0)),
                  pl.BlockSpec((8, D), lambda i: (0, 0))],
        out_specs=(pl.BlockSpec((tt, D), lambda i: (i, 0)), pl.BlockSpec((8, D), lambda i: (0, 0))),
        out_shape=(jax.ShapeDtypeStruct((t, D), F32), jax.ShapeDtypeStruct((8, D), F32)),
        compiler_params=_params(("arbitrary",)),
    )(x, target, fg)


def _adamw(w, g, m, v, tag):
    rows, cols = w.shape
    rt = rows
    for cand in (512, 256, 128, 64, 32, 16, 8):
        if rows % cand == 0:
            rt = cand
            break

    def body(w_ref, g_ref, m_ref, v_ref, d_ref, nm_ref, nv_ref):
        d_ref[...], nm_ref[...], nv_ref[...] = _adamw_update(g_ref[...], w_ref[...], m_ref[...], v_ref[...])

    spec = pl.BlockSpec((rt, cols), lambda i: (i, 0))
    return pl.pallas_call(
        body, name=f"adamw_{tag}", grid=(rows // rt,), in_specs=[spec] * 4, out_specs=(spec,) * 3,
        out_shape=(jax.ShapeDtypeStruct(w.shape, F32),) * 3, compiler_params=_params(("parallel",)),
    )(w, g, m, v)


def _adamw_update(gv, w, m, v):
    nm = ADAM_B1 * m + (1.0 - ADAM_B1) * gv
    nv = ADAM_B2 * v + (1.0 - ADAM_B2) * (gv * gv)
    m_hat = nm / (1.0 - ADAM_B1 ** ADAM_STEP)
    v_hat = nv / (1.0 - ADAM_B2 ** ADAM_STEP)
    return -ADAM_LR * (m_hat / (jnp.sqrt(v_hat) + ADAM_EPS) + ADAM_WD * w), nm, nv


def _adamw_many(ws, gs, ms, vs):
    n = len(ws)

    def body(*refs):
        w_refs, g_refs, m_refs, v_refs = (refs[k * n:(k + 1) * n] for k in range(4))
        d_refs, nm_refs, nv_refs = (refs[(4 + k) * n:(5 + k) * n] for k in range(3))
        for k in range(n):
            d_refs[k][...], nm_refs[k][...], nv_refs[k][...] = _adamw_update(
                g_refs[k][...], w_refs[k][...], m_refs[k][...], v_refs[k][...])

    shapes = tuple(jax.ShapeDtypeStruct(w.shape, F32) for w in ws)
    out = pl.pallas_call(body, name="adamw_small", out_shape=shapes * 3)(*ws, *gs, *ms, *vs)
    return out[:n], out[n:2 * n], out[2 * n:]


def _adamw_halves(w, mine, theirs, m, v, cidx, tag):
    nl, r, cdim = w.shape
    r2 = r // 2
    rt = min(r2, 512)
    nrt = r2 // rt

    def body(c_ref, a_ref, b_ref, w_ref, m_ref, v_ref, g_ref, d_ref, nm_ref, nv_ref):
        gv = jnp.where(pl.program_id(1) == c_ref[0], a_ref[...], b_ref[...])
        g_ref[...] = gv
        d_ref[...], nm_ref[...], nv_ref[...] = _adamw_update(gv, w_ref[...], m_ref[...], v_ref[...])

    half = pl.BlockSpec((None, rt, cdim), lambda l, h, i, c: (l, i, 0))
    whole = pl.BlockSpec((None, rt, cdim), lambda l, h, i, c: (l, h * nrt + i, 0))
    grid_spec = pltpu.PrefetchScalarGridSpec(
        num_scalar_prefetch=1, grid=(nl, 2, nrt), in_specs=[half, half, whole, whole, whole],
        out_specs=(whole,) * 4)
    return pl.pallas_call(
        body, name=f"adamw_{tag}", grid_spec=grid_spec,
        out_shape=(jax.ShapeDtypeStruct(w.shape, F32),) * 4,
        compiler_params=_params(("parallel", "parallel", "parallel")),
    )(cidx, mine, theirs, w, m, v)


def _pack(arrays, row_multiple=8):
    parts, offsets, off = [], [], 0
    for a in arrays:
        n = int(np.prod(a.shape))
        padded = -(-n // 1024) * 1024
        flat = a.reshape(-1).astype(F32)
        if padded != n:
            flat = jnp.concatenate([flat, jnp.zeros((padded - n,), F32)])
        parts.append(flat.reshape(-1, 128))
        offsets.append((off // 128, n, a.shape))
        off += padded
    tail = -off % (row_multiple * 128)
    if tail:
        parts.append(jnp.zeros((tail // 128, 128), F32))
    return jnp.concatenate(parts, axis=0), offsets


def _unpack(buf, offsets):
    return [buf[row:row + -(-n // 128)].reshape(-1)[:n].reshape(shape) for row, n, shape in offsets]


def _pad_rows(a, rows):
    return jnp.concatenate([a, jnp.zeros((rows - a.shape[0],) + a.shape[1:], a.dtype)], axis=0)


def kernel(x, c, norm_g, w_ada, b_ada, w_in, w_conv_a, w_pool, pool_scale, w_dw_c, b_dw_c, ln_g_c, ln_b_c, w_pw2_c, b_pw2_c, ln_g_d, ln_b_d, w_s_d, b_s_d, w_out, final_g, loss_target, m_norm_g, m_w_ada, m_b_ada, m_w_in, m_w_conv_a, m_w_pool, m_pool_scale, m_w_dw_c, m_b_dw_c, m_ln_g_c, m_ln_b_c, m_w_pw2_c, m_b_pw2_c, m_ln_g_d, m_ln_b_d, m_w_s_d, m_b_s_d, m_w_out, m_final_g, v_norm_g, v_w_ada, v_b_ada, v_w_in, v_w_conv_a, v_w_pool, v_pool_scale, v_w_dw_c, v_b_dw_c, v_ln_g_c, v_ln_b_c, v_w_pw2_c, v_b_pw2_c, v_ln_g_d, v_ln_b_d, v_w_s_d, v_b_s_d, v_w_out, v_final_g):
    env = dict(locals())
    weights = {n: env[n] for n in WEIGHTS}
    mom_m = {n: env["m_" + n] for n in WEIGHTS}
    mom_v = {n: env["v_" + n] for n in WEIGHTS}
    nl = norm_g.shape[0]
    t = x.shape[1]
    tt, tm = _tiles(t)
    ax, ay, ac = lax.axis_index("x"), lax.axis_index("y"), lax.axis_index("c")
    chip = 2 * ax + ay
    dev = 2 * chip + ac
    cidx = jnp.reshape(ac, (1,)).astype(jnp.int32)
    sidx = jnp.reshape(chip, (1,)).astype(jnp.int32)
    xs, target = x[0], loss_target[0]

    shard_small, shard_off = _pack([c, w_conv_a, w_dw_c, w_pw2_c])
    w_in_0, w_out_0, gathered = _gather_weights(w_in[:1], w_out[:1], shard_small)
    w_in_layers, w_out_layers = [w_in_0], [w_out_0]
    gathered = gathered.reshape(N_DEV, -1, 128)
    per_dev = [_unpack(gathered[d], shard_off) for d in range(0, N_DEV, 2)]
    c_all = jnp.concatenate([u[0] for d in range(N_DEV)
                             for u in [_unpack(gathered[d], shard_off[:1])]], axis=0)
    w_conv_full = jnp.concatenate([u[1] for u in per_dev], axis=-1)
    w_dw_full = jnp.concatenate([u[2] for u in per_dev], axis=-1)
    w_pw2_full = jnp.concatenate([u[3] for u in per_dev], axis=1)

    b_ada_s = lax.dynamic_slice_in_dim(b_ada, chip * PIECE, PIECE, axis=1)[:, None, :]
    mod_s = _ada_forward(c_all, w_ada, b_ada_s)
    mod_all = _allgather8(mod_s.reshape(nl * N_DEV, PIECE), "gather_mod").reshape(N_DEV, nl, N_DEV, PIECE)
    mod_rows = lax.dynamic_index_in_dim(mod_all, dev, axis=2, keepdims=False)
    mod = jnp.concatenate([mod_rows[2 * s] for s in range(N_CHIP)], axis=-1)

    if nl > 1:
        w_in_b, w_out_b = _cast_bf16([w_in[1:], w_out[1:]], after=w_out_layers[0])

    tril = jnp.tril(jnp.ones((CHUNK, CHUNK), bool))
    ws_masked = jnp.where(tril[None, None], w_s_d, 0.0)
    wbd = jnp.zeros((nl, GW, GW), F32)
    for g in range(4):
        wbd = wbd.at[:, 64 * g:64 * (g + 1), 64 * g:64 * (g + 1)].set(w_pool[:, g])
    v256 = jnp.stack([pool_scale, b_dw_c, ln_g_c, ln_b_c, b_pw2_c, ln_g_d, ln_b_d], axis=1)
    mp = dict(
        wa=_pad_rows(jnp.swapaxes(w_conv_full, 0, 1), 8).swapaxes(0, 1),
        wdw=_pad_rows(jnp.swapaxes(w_dw_full, 0, 1), 32).swapaxes(0, 1),
        v256=_pad_rows(jnp.swapaxes(v256, 0, 1), 16).swapaxes(0, 1),
        wbd=wbd.astype(BF16), wpw=w_pw2_full.astype(BF16), ws=ws_masked.astype(BF16),
        bsd=jnp.repeat(jnp.swapaxes(b_s_d, 1, 2), GW // 4, axis=2))
    wst = jnp.swapaxes(ws_masked, 2, 3).astype(BF16)

    def vec_rows(l):
        rows = jnp.stack([mod[l, 0:D], mod[l, D:2 * D], mod[l, 2 * D:3 * D], norm_g[l]], axis=0)
        return _pad_rows(rows, 8)

    xin, zs, ys, convs, vecs = [xs], [], [], [], []
    zero = jnp.zeros((), jnp.int32)
    for l in range(nl):
        vecs.append(vec_rows(l))
        started, token = None, 0.0
        if l + 1 < nl:
            started = _gather_shards_start(w_in_b[l:l + 1], w_out_b[l:l + 1])
            token = started[-1][0, 0]
        zl, xn, yl, cl = _fwd_layer(xin[l], vecs[l] + token, w_in_layers[l], w_out_layers[l], mp, l, tm)
        zs.append(zl)
        convs.append(cl)
        xin.append(xn)
        ys.append(yl)
        if started is not None:
            own_in, own_out, land_in, land_out = _gather_shards_wait(*started[:6], cl)
            w_in_layers.append(lax.dynamic_update_slice(land_in, own_in[:, None], (zero, chip, zero, zero)))
            w_out_layers.append(lax.dynamic_update_slice(land_out, own_out[:, None], (zero, chip, zero, zero)))
    dx, fin_sums = _final(xin[nl], target, _pad_rows(final_g[None, :], 8), tt)

    g_in, g_out, small, dmod = None, None, [None] * nl, [None] * nl
    for l in reversed(range(nl)):
        dz, g_out, gwbd, gwpw, gws, gbs, gwdw, g256, g1024 = _bwd_mix(
            dx, ys[l], convs[l], zs[l], vecs[l], mp, w_out_layers[l], wst, l, tm, gwout_all=g_out)
        dx, g_in, sums = _bwd_in(dz, xin[l], dx, vecs[l], w_in_layers[l], l, nl, tt, gw_all=g_in)
        dmod[l] = jnp.concatenate([sums[0], sums[1], g1024[0]])
        small[l] = dict(
            norm_g=sums[2], w_conv_a=g256[8:8 + CONV_A],
            w_pool=jnp.stack([gwbd[64 * g:64 * (g + 1), 64 * g:64 * (g + 1)] for g in range(4)]),
            pool_scale=g256[0], w_dw_c=gwdw[:CONV_C], b_dw_c=g256[1], ln_g_c=g256[2], ln_b_c=g256[3],
            w_pw2_c=gwpw, b_pw2_c=g256[4], ln_g_d=g256[5], ln_b_d=g256[6], w_s_d=gws, b_s_d=gbs[:, :4].T)
    grad_x = dx[None]

    rs = _reduce_scatter_start([g_in, g_out], ("in", "out"), cidx)
    token = rs[0][1][-1][0, 0] + rs[1][1][-1][0, 0]

    dmod_all, dmod_sum = _allgather8(
        (jnp.stack(dmod) + token).reshape(nl * 3 * D // 128, 128), "gather_dmod", reduce=True)
    dmod_all = dmod_all.reshape(N_DEV, nl, 3 * D)
    grad_b_ada = dmod_sum.reshape(nl, 3 * D)
    dmod_s = jnp.swapaxes(lax.dynamic_slice_in_dim(dmod_all, chip * PIECE, PIECE, axis=2), 0, 1)
    grad_w_ada = _ada_backward(c_all, dmod_s)

    small_names = [n for n in WEIGHTS if n not in BIG and n not in ('b_ada', 'final_g')]
    stacked = [jnp.stack([small[l][n] for l in range(nl)]) for n in small_names]
    small_buf, small_off = _pack(stacked + [fin_sums[0], fin_sums[2, 0:1]], 8 * N_DEV)
    reduced = _unpack(_allreduce8(small_buf, "allreduce_small"), small_off)
    grads = dict(zip(small_names, reduced[:len(small_names)]))
    grads['final_g'] = reduced[-2]
    loss = reduced[-1][0]
    grads['b_ada'] = grad_b_ada
    grads['w_ada'] = grad_w_ada
    grads['w_conv_a'] = lax.dynamic_slice_in_dim(grads['w_conv_a'], chip * 64, 64, axis=2)
    grads['w_dw_c'] = lax.dynamic_slice_in_dim(grads['w_dw_c'], chip * 64, 64, axis=2)
    grads['w_pw2_c'] = lax.dynamic_slice_in_dim(grads['w_pw2_c'], chip * 64, 64, axis=1)

    delta, new_m, new_v = {}, {}, {}
    shape = w_ada.shape
    as2d = lambda a: a.reshape(-1, shape[-1])
    d2, m2, v2 = _adamw(as2d(w_ada), as2d(grads['w_ada']), as2d(m_w_ada), as2d(v_w_ada), 'w_ada')
    delta['w_ada'], new_m['w_ada'], new_v['w_ada'] = d2.reshape(shape), m2.reshape(shape), v2.reshape(shape)
    rest = [n for n in WEIGHTS if n not in BIG]
    ds, nms, nvs = _adamw_many([weights[n] for n in rest], [grads[n] for n in rest],
                               [mom_m[n] for n in rest], [mom_v[n] for n in rest])
    for n, dn, mn, vn in zip(rest, ds, nms, nvs):
        delta[n], new_m[n], new_v[n] = dn, mn, vn
    halves = dict(zip(('w_in', 'w_out'), _reduce_scatter_finish(rs, ("in", "out"), nvs[-1], sidx)))
    for n in ('w_in', 'w_out'):
        grads[n], delta[n], new_m[n], new_v[n] = _adamw_halves(
            weights[n], *halves[n], mom_m[n], mom_v[n], cidx, n)

    return (loss, grad_x, *[grads[n] for n in WEIGHTS], *[delta[n] for n in WEIGHTS],
            *[new_m[n] for n in WEIGHTS], *[new_v[n] for n in WEIGHTS])
```

```python
import math

import jax
import jax.numpy as jnp
import numpy as np
from jax import lax
from jax.experimental import pallas as pl
from jax.experimental.pallas import tpu as pltpu

F32 = jnp.float32
BF16 = jnp.bfloat16
MESH = pl.DeviceIdType.MESH

D = 1024
DIN = 3072
GW = 256
PIECE = 768
N_CHIP = 4
N_DEV = 8
HALO = 32
CONV_A = 3
CONV_C = 31
CHUNK = 128
CONV_ROWS = 64
ROW_BLOCK = 32
EPS = 1e-6
VMEM_LIMIT = 56 * 1024 * 1024

ADAM_LR, ADAM_B1, ADAM_B2, ADAM_EPS, ADAM_WD, ADAM_STEP = 0.001, 0.9, 0.999, 1e-08, 0.01, 10
GELU_K = math.sqrt(2.0 / math.pi)
GELU_C = 0.044715

WEIGHTS = ['norm_g', 'w_ada', 'b_ada', 'w_in', 'w_conv_a', 'w_pool', 'pool_scale', 'w_dw_c', 'b_dw_c', 'ln_g_c',
           'ln_b_c', 'w_pw2_c', 'b_pw2_c', 'ln_g_d', 'ln_b_d', 'w_s_d', 'b_s_d', 'w_out', 'final_g']
BIG = ('w_ada', 'w_in', 'w_out')


def _tiles(t):
    if t % 512 == 0 and t >= 2048:
        return 512, 256
    return 128, 128


def _params(sem, limit=VMEM_LIMIT):
    return pltpu.CompilerParams(dimension_semantics=sem, vmem_limit_bytes=limit)


def _sig(x):
    return jax.nn.sigmoid(x)


def _gelu(x):
    th = jnp.tanh(GELU_K * (x + GELU_C * x * x * x))
    return 0.5 * x * (1.0 + th), th


def _gelu_grad(x, th):
    return 0.5 * (1.0 + th) + 0.5 * x * (1.0 - th * th) * GELU_K * (1.0 + 3.0 * GELU_C * x * x)


def _dot(a, b):
    return jnp.dot(a, b, preferred_element_type=F32)


def _dot_nt(a, b):
    return lax.dot_general(a, b, (((1,), (1,)), ((), ())), preferred_element_type=F32)


def _dot_tn(a, b):
    return lax.dot_general(a, b, (((0,), (0,)), ((), ())), preferred_element_type=F32)


def _rowsum(x):
    return jnp.sum(x, axis=0, keepdims=True)


def _allgather8_steps(x_ref, out_ref, send_sems, recv_sems, local_sem):
    m_per = x_ref.shape[0]
    x, y, c = lax.axis_index("x"), lax.axis_index("y"), lax.axis_index("c")
    me, sibling = (x, y, c), (x, y, 1 - c)
    chips = [(1 - x, y), (x, 1 - y), (1 - x, 1 - y)]

    def rows(px, py, pc):
        return out_ref.at[pl.ds((4 * px + 2 * py + pc) * m_per, m_per), :]

    def copy(k, block, to, src=None):
        return pltpu.make_async_remote_copy(
            src_ref=rows(*block) if src is None else src, dst_ref=rows(*block),
            send_sem=send_sems.at[k], recv_sem=recv_sems.at[k], device_id=to, device_id_type=MESH)

    mine = pltpu.make_async_copy(x_ref, rows(*me), local_sem)
    first = [copy(0, me, sibling, src=x_ref)]
    first += [copy(1 + j, me, (*chip, c), src=x_ref) for j, chip in enumerate(chips)]

    def begin():
        mine.start()
        for cp in first:
            cp.start()

    def finish():
        passed = [copy(4 + j, (*chip, c), sibling) for j, chip in enumerate(chips)]
        for j, chip in enumerate(chips):
            copy(1 + j, (*chip, c), me).wait_recv()
            passed[j].start()
        copy(0, sibling, me).wait_recv()
        for j, chip in enumerate(chips):
            copy(4 + j, (*chip, 1 - c), me).wait_recv()
        for cp in first + passed:
            cp.wait_send()
        mine.wait()

    return begin, finish


_ALLGATHER8_SEMS = [pltpu.SemaphoreType.DMA((7,)), pltpu.SemaphoreType.DMA((7,)), pltpu.SemaphoreType.DMA]


def _allgather8(v, name, reduce=False):
    m_per, n = v.shape

    def body(x_ref, out_ref, *rest):
        sum_ref = rest[0] if reduce else None
        begin, finish = _allgather8_steps(x_ref, out_ref, *rest[-3:])
        begin()
        finish()
        if reduce:
            acc = out_ref[0:m_per, :]
            for d in range(1, N_DEV):
                acc = acc + out_ref[d * m_per:(d + 1) * m_per, :]
            sum_ref[...] = acc

    out_shape = [jax.ShapeDtypeStruct((N_DEV * m_per, n), v.dtype)]
    out_specs = [pl.BlockSpec(memory_space=pltpu.VMEM)]
    if reduce:
        out_shape.append(jax.ShapeDtypeStruct((m_per, n), v.dtype))
        out_specs.append(pl.BlockSpec(memory_space=pltpu.VMEM))
    res = pl.pallas_call(
        body, name=name, out_shape=tuple(out_shape),
        in_specs=[pl.BlockSpec(memory_space=pltpu.VMEM)], out_specs=tuple(out_specs),
        scratch_shapes=list(_ALLGATHER8_SEMS),
        compiler_params=pltpu.CompilerParams(vmem_limit_bytes=VMEM_LIMIT),
    )(v)
    return res if reduce else res[0]


def _allreduce8(v, name):
    rows, n = v.shape
    rc = rows // N_DEV

    def body(x_ref, sum_ref, stage, send1, recv1, send2, recv2):
        x, y, c = lax.axis_index("x"), lax.axis_index("y"), lax.axis_index("c")
        me = 4 * x + 2 * y + c
        peers = []
        for k in range(1, N_DEV):
            kx, ky, kc = (k >> 2) & 1, (k >> 1) & 1, k & 1
            peers.append((1 - x if kx else x, 1 - y if ky else y, 1 - c if kc else c))

        def chunk(ref, d):
            return ref.at[pl.ds(d * rc, rc), :]

        scatter, gather = [], []
        for k, (px, py, pc) in enumerate(peers):
            scatter.append(pltpu.make_async_remote_copy(
                src_ref=chunk(x_ref, 4 * px + 2 * py + pc), dst_ref=stage.at[me], send_sem=send1.at[k],
                recv_sem=recv1.at[k], device_id=(px, py, pc), device_id_type=MESH))
        for cp in scatter:
            cp.start()
        stage[me] = x_ref[pl.ds(me * rc, rc), :]
        for k, (px, py, pc) in enumerate(peers):
            pltpu.make_async_remote_copy(
                src_ref=chunk(x_ref, me), dst_ref=stage.at[4 * px + 2 * py + pc], send_sem=send1.at[k],
                recv_sem=recv1.at[k], device_id=(px, py, pc), device_id_type=MESH).wait_recv()
        total = stage[0]
        for d in range(1, N_DEV):
            total = total + stage[d]
        sum_ref[pl.ds(me * rc, rc), :] = total
        for k, (px, py, pc) in enumerate(peers):
            gather.append(pltpu.make_async_remote_copy(
                src_ref=chunk(sum_ref, me), dst_ref=chunk(sum_ref, me), send_sem=send2.at[k],
                recv_sem=recv2.at[k], device_id=(px, py, pc), device_id_type=MESH))
        for cp in gather:
            cp.start()
        for k, (px, py, pc) in enumerate(peers):
            theirs = 4 * px + 2 * py + pc
            pltpu.make_async_remote_copy(
                src_ref=chunk(sum_ref, theirs), dst_ref=chunk(sum_ref, theirs), send_sem=send2.at[k],
                recv_sem=recv2.at[k], device_id=(px, py, pc), device_id_type=MESH).wait_recv()
        for cp in scatter + gather:
            cp.wait_send()

    return pl.pallas_call(
        body, name=name, out_shape=jax.ShapeDtypeStruct((rows, n), v.dtype),
        in_specs=[pl.BlockSpec(memory_space=pltpu.VMEM)], out_specs=pl.BlockSpec(memory_space=pltpu.VMEM),
        scratch_shapes=[pltpu.VMEM((N_DEV, rc, n), v.dtype)] + [pltpu.SemaphoreType.DMA((N_DEV - 1,))] * 4,
        compiler_params=pltpu.CompilerParams(vmem_limit_bytes=VMEM_LIMIT),
    )(v)


def _gather_weights(w_in_s, w_out_s, small):
    nl = w_in_s.shape[0]
    shapes = ((nl, N_CHIP) + w_in_s.shape[1:], (nl, N_CHIP) + w_out_s.shape[1:])
    n_sem = 2 * 3 * 2

    def body(win_ref, wout_ref, small_ref, win_o, wout_o, small_o, send_sems, recv_sems, *small_sems):
        begin_small, finish_small = _allgather8_steps(small_ref, small_o, *small_sems)
        begin_small()
        x, y, c = lax.axis_index("x"), lax.axis_index("y"), lax.axis_index("c")
        s = 2 * x + y
        sibling = (x, y, 1 - c)
        chips = [(1 - x, y), (x, 1 - y), (1 - x, 1 - y)]
        outs = (win_o, wout_o)
        for src, dst in ((win_ref, win_o), (wout_ref, wout_o)):
            rows = src.shape[1]
            step = min(rows, 256)
            for l in range(nl):
                for r in range(0, rows, step):
                    dst[l, s, r:r + step, :] = src[l, r:r + step, :].astype(BF16)

        def copy(k, ref, piece, half, to):
            r2 = ref.shape[2] // 2
            rows = ref.at[:, piece, pl.ds(half * r2, r2), :]
            return pltpu.make_async_remote_copy(
                src_ref=rows, dst_ref=rows, send_sem=send_sems.at[k], recv_sem=recv_sems.at[k],
                device_id=to, device_id_type=MESH)

        sends = [copy(2 * j + a, ref, s, c, (px, py, c)) for j, (px, py) in enumerate(chips)
                 for a, ref in enumerate(outs)]
        for cp in sends:
            cp.start()
        finish_small()
        passed = []
        for j, (px, py) in enumerate(chips):
            for a, ref in enumerate(outs):
                copy(2 * j + a, ref, 2 * px + py, c, (px, py, c)).wait_recv()
                passed.append(copy(6 + 2 * j + a, ref, 2 * px + py, c, sibling))
                passed[-1].start()
        for j, (px, py) in enumerate(chips):
            for a, ref in enumerate(outs):
                copy(6 + 2 * j + a, ref, 2 * px + py, 1 - c, sibling).wait_recv()
        for cp in sends + passed:
            cp.wait_send()

    vmem = pl.BlockSpec(memory_space=pltpu.VMEM)
    return pl.pallas_call(
        body, name="gather_weights",
        out_shape=tuple(jax.ShapeDtypeStruct(s, BF16) for s in shapes)
        + (jax.ShapeDtypeStruct((N_DEV * small.shape[0], small.shape[1]), small.dtype),),
        in_specs=[vmem] * 3, out_specs=(vmem,) * 3,
        scratch_shapes=[pltpu.SemaphoreType.DMA((n_sem,)), pltpu.SemaphoreType.DMA((n_sem,))] + list(_ALLGATHER8_SEMS),
        compiler_params=pltpu.CompilerParams(vmem_limit_bytes=VMEM_LIMIT),
    )(w_in_s, w_out_s, small)


def _cast_bf16(arrays, after):
    n = len(arrays)

    def body(*refs):
        for src, dst in zip(refs[:n], refs[n + 1:]):
            rows = src.shape[-2]
            step = min(rows, 256)
            for l in range(src.shape[0]):
                for r in range(0, rows, step):
                    dst[l, r:r + step, :] = src[l, r:r + step, :].astype(BF16)

    vmem = pl.BlockSpec(memory_space=pltpu.VMEM)
    return pl.pallas_call(
        body, name="cast_weights", out_shape=tuple(jax.ShapeDtypeStruct(a.shape, BF16) for a in arrays),
        in_specs=[vmem] * n + [pl.BlockSpec(memory_space=pl.ANY)], out_specs=(vmem,) * n,
        compiler_params=pltpu.CompilerParams(vmem_limit_bytes=VMEM_LIMIT),
    )(*arrays, after)


def _shard_copies(a_ref, b_ref, la_ref, lb_ref, send_sems, recv_sems):
    x, y, c = lax.axis_index("x"), lax.axis_index("y"), lax.axis_index("c")
    s = 2 * x + y
    sends, recvs = [], []
    for j, (px, py) in enumerate([(1 - x, y), (x, 1 - y), (1 - x, 1 - y)]):
        for a, (src, land) in enumerate(((a_ref, la_ref), (b_ref, lb_ref))):
            k = 2 * j + a
            for slot, into in ((s, sends), (2 * px + py, recvs)):
                into.append(pltpu.make_async_remote_copy(
                    src_ref=src.at[0], dst_ref=land.at[0, slot], send_sem=send_sems.at[k], recv_sem=recv_sems.at[k],
                    device_id=(px, py, c), device_id_type=MESH))
    return sends, recvs


def _gather_shards_start(a, b):
    lands = ((1, N_CHIP) + a.shape[1:], (1, N_CHIP) + b.shape[1:])
    hbm = pl.BlockSpec(memory_space=pltpu.HBM)
    sem = pl.BlockSpec(memory_space=pltpu.SEMAPHORE)

    def body(a_ref, b_ref, la_ref, lb_ref, send_sems, recv_sems, a_thru, b_thru, la_thru, lb_thru, token):
        for cp in _shard_copies(a_ref, b_ref, la_ref, lb_ref, send_sems, recv_sems)[0]:
            cp.start()
        token[...] = jnp.zeros_like(token)

    return pl.pallas_call(
        body, name="gather_weights_start",
        out_shape=(pltpu.SemaphoreType.DMA((6,)), pltpu.SemaphoreType.DMA((6,)), pltpu.HBM(a.shape, BF16),
                   pltpu.HBM(b.shape, BF16), pltpu.HBM(lands[0], BF16), pltpu.HBM(lands[1], BF16),
                   jax.ShapeDtypeStruct((8, 128), F32)),
        in_specs=(hbm,) * 4, out_specs=(sem, sem, hbm, hbm, hbm, hbm, pl.BlockSpec(memory_space=pltpu.VMEM)),
        input_output_aliases={0: 2, 1: 3, 2: 4, 3: 5},
        compiler_params=pltpu.CompilerParams(has_side_effects=pltpu.SideEffectType.DATAFLOW_SIDE_EFFECTING),
    )(pltpu.with_memory_space_constraint(a, pltpu.HBM), pltpu.with_memory_space_constraint(b, pltpu.HBM),
      pltpu.with_memory_space_constraint(lax.empty(lands[0], BF16), pltpu.HBM),
      pltpu.with_memory_space_constraint(lax.empty(lands[1], BF16), pltpu.HBM))


def _gather_shards_wait(send_sems, recv_sems, a, b, la, lb, after):
    hbm = pl.BlockSpec(memory_space=pltpu.HBM)
    sem = pl.BlockSpec(memory_space=pltpu.SEMAPHORE)

    def body(a_ref, b_ref, la_ref, lb_ref, send_sems, recv_sems, after_ref, a_dead, b_dead, la_out, lb_out):
        sends, recvs = _shard_copies(a_ref, b_ref, la_ref, lb_ref, send_sems, recv_sems)
        for cp in sends:
            cp.wait_send()
        for cp in recvs:
            cp.wait_recv()

    out = pl.pallas_call(
        body, name="gather_weights_wait",
        out_shape=tuple(pltpu.HBM(v.shape, v.dtype) for v in (a, b, la, lb)),
        in_specs=(hbm, hbm, hbm, hbm, sem, sem, pl.BlockSpec(memory_space=pl.ANY)), out_specs=(hbm,) * 4,
        input_output_aliases={0: 0, 1: 1, 2: 2, 3: 3},
        compiler_params=pltpu.CompilerParams(has_side_effects=pltpu.SideEffectType.DATAFLOW_SIDE_EFFECTING),
    )(a, b, la, lb, send_sems, recv_sems, after)
    return out


def _send_halves_to_sibling(gs):
    n = len(gs)
    shapes = [g.shape[:2] + (g.shape[2] // 2, g.shape[3]) for g in gs]

    def body(*refs):
        g_refs, recv_refs, (send_sems, recv_sems) = refs[:n], refs[n:2 * n], refs[2 * n:]
        x, y, c = lax.axis_index("x"), lax.axis_index("y"), lax.axis_index("c")
        cps = []
        for a, (g_ref, recv_ref) in enumerate(zip(g_refs, recv_refs)):
            half = recv_ref.shape[2]
            cps.append(pltpu.make_async_remote_copy(
                src_ref=g_ref.at[:, :, pl.ds((1 - c) * half, half), :], dst_ref=recv_ref,
                send_sem=send_sems.at[a], recv_sem=recv_sems.at[a], device_id=(x, y, 1 - c), device_id_type=MESH))
        for cp in cps:
            cp.start()
        for cp in cps:
            cp.wait()

    return pl.pallas_call(
        body, name="rs_sibling",
        out_shape=tuple(jax.ShapeDtypeStruct(s, g.dtype) for s, g in zip(shapes, gs)),
        in_specs=[pl.BlockSpec(memory_space=pl.ANY)] * n, out_specs=(pl.BlockSpec(memory_space=pl.ANY),) * n,
        scratch_shapes=[pltpu.SemaphoreType.DMA((n,)), pltpu.SemaphoreType.DMA((n,))],
    )(*gs)


def _add_half(g, recv, cidx, tag):
    nl, nc, r, cdim = g.shape
    half = r // 2
    rt = min(half, 512)
    nrt = half // rt

    def body(c_ref, a_ref, b_ref, o_ref, ob_ref):
        c = c_ref[0]
        first = jnp.where(c == 0, a_ref[...], b_ref[...])
        second = jnp.where(c == 0, b_ref[...], a_ref[...])
        total = first + second
        o_ref[...] = total
        ob_ref[...] = total.astype(BF16)

    out_spec = pl.BlockSpec((None, None, rt, cdim), lambda l, j, i, c: (l, j, i, 0))
    grid_spec = pltpu.PrefetchScalarGridSpec(
        num_scalar_prefetch=1, grid=(nl, nc, nrt),
        in_specs=[pl.BlockSpec((None, None, rt, cdim), lambda l, j, i, c: (l, j, c[0] * nrt + i, 0)),
                  pl.BlockSpec((None, None, rt, cdim), lambda l, j, i, c: (l, j, i, 0))],
        out_specs=(out_spec, out_spec))
    return pl.pallas_call(
        body, name=f"rs_add_sibling_{tag}", grid_spec=grid_spec,
        out_shape=(jax.ShapeDtypeStruct((nl, nc, half, cdim), g.dtype),
                   jax.ShapeDtypeStruct((nl, nc, half, cdim), BF16)),
        compiler_params=_params(("parallel", "parallel", "parallel")),
    )(cidx, g, recv)


def _chip_copies(h_ref, land_ref, send_sems, recv_sems):
    x, y, c = lax.axis_index("x"), lax.axis_index("y"), lax.axis_index("c")
    chips = [(1 - x, y), (x, 1 - y), (1 - x, 1 - y)]
    return [pltpu.make_async_remote_copy(
        src_ref=h_ref.at[:, 2 * px + py], dst_ref=land_ref.at[k], send_sem=send_sems.at[k],
        recv_sem=recv_sems.at[k], device_id=(px, py, c), device_id_type=MESH) for k, (px, py) in enumerate(chips)]


def _exchange_chips_start(h, tag):
    nl, nc, r2, cdim = h.shape
    land_shape = (3, nl, r2, cdim)
    hbm = pl.BlockSpec(memory_space=pltpu.HBM)
    sem = pl.BlockSpec(memory_space=pltpu.SEMAPHORE)

    def body(h_ref, land_ref, send_sems, recv_sems, h_thru, land_thru, token):
        for cp in _chip_copies(h_ref, land_ref, send_sems, recv_sems):
            cp.start()
        token[...] = jnp.zeros_like(token)

    return pl.pallas_call(
        body, name=f"rs_chips_start_{tag}",
        out_shape=(pltpu.SemaphoreType.DMA((3,)), pltpu.SemaphoreType.DMA((3,)), pltpu.HBM(h.shape, h.dtype),
                   pltpu.HBM(land_shape, h.dtype), jax.ShapeDtypeStruct((8, 128), F32)),
        in_specs=(hbm, hbm), out_specs=(sem, sem, hbm, hbm, pl.BlockSpec(memory_space=pltpu.VMEM)),
        input_output_aliases={0: 2, 1: 3},
        compiler_params=pltpu.CompilerParams(has_side_effects=pltpu.SideEffectType.DATAFLOW_SIDE_EFFECTING),
    )(pltpu.with_memory_space_constraint(h, pltpu.HBM),
      pltpu.with_memory_space_constraint(lax.empty(land_shape, h.dtype), pltpu.HBM))


def _exchange_chips_wait(send_sems, recv_sems, h_thru, land_thru, after, tag):
    hbm = pl.BlockSpec(memory_space=pltpu.HBM)
    sem = pl.BlockSpec(memory_space=pltpu.SEMAPHORE)

    def body(h_ref, land_ref, send_sems, recv_sems, after_ref, h_dead, got_ref):
        for cp in _chip_copies(h_ref, land_ref, send_sems, recv_sems):
            cp.wait_send()
            cp.wait_recv()

    return pl.pallas_call(
        body, name=f"rs_chips_wait_{tag}",
        out_shape=(pltpu.HBM(h_thru.shape, h_thru.dtype), pltpu.HBM(land_thru.shape, land_thru.dtype)),
        in_specs=(hbm, hbm, sem, sem, pl.BlockSpec(memory_space=pl.ANY)), out_specs=(hbm, hbm),
        input_output_aliases={0: 0, 1: 1},
        compiler_params=pltpu.CompilerParams(has_side_effects=pltpu.SideEffectType.DATAFLOW_SIDE_EFFECTING),
    )(h_thru, land_thru, send_sems, recv_sems, after)[1]


def _add_chips(h, recv, sidx, tag):
    nl, nc, r2, cdim = h.shape
    rt = min(r2, 512)

    def body(s_ref, own_ref, r_ref, o_ref):
        s = s_ref[0]
        got = [r_ref[k].astype(F32) for k in range(3)]
        total = None
        for chip in range(N_CHIP):
            term = jnp.where(s == chip, own_ref[...],
                             jnp.where((s ^ 2) == chip, got[0], jnp.where((s ^ 1) == chip, got[1], got[2])))
            total = term if total is None else total + term
        o_ref[...] = total

    grid_spec = pltpu.PrefetchScalarGridSpec(
        num_scalar_prefetch=1, grid=(nl, r2 // rt),
        in_specs=[pl.BlockSpec((None, None, rt, cdim), lambda l, i, s: (l, s[0], i, 0)),
                  pl.BlockSpec((3, None, rt, cdim), lambda l, i, s: (0, l, i, 0))],
        out_specs=pl.BlockSpec((None, rt, cdim), lambda l, i, s: (l, i, 0)))
    return pl.pallas_call(
        body, name=f"rs_add_chips_{tag}", grid_spec=grid_spec,
        out_shape=jax.ShapeDtypeStruct((nl, r2, cdim), h.dtype),
        compiler_params=_params(("parallel", "parallel")),
    )(sidx, h, recv)


def _swap_halves(reds):
    n = len(reds)

    def body(*refs):
        send_sems, recv_sems = refs[2 * n:]
        x, y, c = lax.axis_index("x"), lax.axis_index("y"), lax.axis_index("c")
        cps = [pltpu.make_async_remote_copy(
            src_ref=refs[a], dst_ref=refs[n + a], send_sem=send_sems.at[a], recv_sem=recv_sems.at[a],
            device_id=(x, y, 1 - c), device_id_type=MESH) for a in range(n)]
        for cp in cps:
            cp.start()
        for cp in cps:
            cp.wait()

    return pl.pallas_call(
        body, name="rs_swap", out_shape=tuple(jax.ShapeDtypeStruct(r.shape, r.dtype) for r in reds),
        in_specs=[pl.BlockSpec(memory_space=pl.ANY)] * n, out_specs=(pl.BlockSpec(memory_space=pl.ANY),) * n,
        scratch_shapes=[pltpu.SemaphoreType.DMA((n,)), pltpu.SemaphoreType.DMA((n,))],
    )(*reds)


def _reduce_scatter_start(gs, tags, cidx):
    recvs = _send_halves_to_sibling(gs)
    started = []
    for g, recv, tag in zip(gs, recvs, tags):
        chip_sum, chip_sum_bf16 = _add_half(g, recv, cidx, tag)
        started.append((chip_sum, _exchange_chips_start(chip_sum_bf16, tag)))
    return started


def _reduce_scatter_finish(started, tags, after, sidx):
    reds = []
    for (chip_sum, (send_sems, recv_sems, h_thru, land_thru, _)), tag in zip(started, tags):
        got = _exchange_chips_wait(send_sems, recv_sems, h_thru, land_thru, after, tag)
        reds.append(_add_chips(chip_sum, got, sidx, tag))
    return list(zip(reds, _swap_halves(reds)))


def _ada_forward(c_all, w_ada_s, b_s):
    nl = w_ada_s.shape[0]

    def body(c_ref, w_ref, b_ref, o_ref):
        cv = c_ref[...]
        ca = (cv * _sig(cv)).astype(BF16)
        for l in range(nl):
            o_ref[l] = _dot(ca, w_ref[l].astype(BF16)) + b_ref[l]

    return pl.pallas_call(
        body, name="ada_forward", out_shape=jax.ShapeDtypeStruct((nl, N_DEV, PIECE), F32),
        compiler_params=pltpu.CompilerParams(vmem_limit_bytes=VMEM_LIMIT),
    )(c_all, w_ada_s, b_s)


def _ada_backward(c_all, dmod_s):
    nl = dmod_s.shape[0]

    def body(c_ref, d_ref, o_ref):
        cv = c_ref[...]
        ca = (cv * _sig(cv)).astype(BF16)
        for l in range(nl):
            o_ref[l] = _dot_tn(ca, d_ref[l].astype(BF16))

    return pl.pallas_call(
        body, name="ada_backward", out_shape=jax.ShapeDtypeStruct((nl, D, PIECE), F32),
        compiler_params=pltpu.CompilerParams(vmem_limit_bytes=VMEM_LIMIT),
    )(c_all, dmod_s)


def _normed(xv, vecs):
    r = lax.rsqrt(jnp.mean(xv * xv, axis=-1, keepdims=True) + EPS)
    xhat = xv * r
    hn = xhat * vecs[3:4, :]
    return r, xhat, hn


def _bwd_in(dz, x, dxo, vecs, w, l, nl, tt, gw_all=None):
    t = x.shape[0]
    nt = t // tt

    def body(dz_ref, x_ref, dxo_ref, v_ref, w_any, *rest):
        dx_ref, gw_any, sums_ref, w_vmem, gw_acc, sem = rest[-6:]
        i = pl.program_id(0)

        @pl.when(i == 0)
        def _():
            cp = pltpu.make_async_copy(w_any.at[0], w_vmem, sem)
            cp.start()
            cp.wait()
            gw_acc[...] = jnp.zeros_like(gw_acc)
            sums_ref[...] = jnp.zeros_like(sums_ref)

        vecs_v = v_ref[...]
        r, xhat, hn = _normed(x_ref[...], vecs_v)
        one_scale = 1.0 + vecs_v[1:2, :]
        hb = (hn * one_scale + vecs_v[0:1, :]).astype(BF16)
        dh = None
        for j in range(N_CHIP):
            dzj = dz_ref[:, PIECE * j:PIECE * (j + 1)]
            part = _dot_nt(dzj, w_vmem[j])
            dh = part if dh is None else dh + part
            gw_acc[j] += _dot_tn(hb, dzj)
        gain = vecs_v[3:4, :] * one_scale
        dh_xhat = dh * xhat
        sums_ref[0:1, :] += _rowsum(dh)
        sums_ref[1:2, :] += _rowsum(dh_xhat)
        dx_ref[...] = dxo_ref[...] + r * (dh * gain - xhat * jnp.mean(dh_xhat * gain, axis=-1, keepdims=True))

        @pl.when(i == nt - 1)
        def _():
            cp = pltpu.make_async_copy(gw_acc, gw_any.at[l], sem)
            cp.start()
            dh_xhat_sum = sums_ref[1:2, :]
            sums_ref[2:3, :] = dh_xhat_sum * one_scale
            sums_ref[1:2, :] = dh_xhat_sum * vecs_v[3:4, :]
            cp.wait()

    carried = [] if gw_all is None else [gw_all]
    return pl.pallas_call(
        body, name=f"bwd_in_{l}", grid=(nt,),
        in_specs=[pl.BlockSpec((tt, DIN), lambda i: (i, 0)), pl.BlockSpec((tt, D), lambda i: (i, 0)),
                  pl.BlockSpec((tt, D), lambda i: (i, 0)), pl.BlockSpec((8, D), lambda i: (0, 0)),
                  pl.BlockSpec(memory_space=pl.ANY)] + [pl.BlockSpec(memory_space=pl.ANY)] * len(carried),
        out_specs=(pl.BlockSpec((tt, D), lambda i: (i, 0)), pl.BlockSpec(memory_space=pl.ANY),
                   pl.BlockSpec((8, D), lambda i: (0, 0))),
        out_shape=(jax.ShapeDtypeStruct((t, D), F32), jax.ShapeDtypeStruct((nl, N_CHIP, D, PIECE), F32),
                   jax.ShapeDtypeStruct((8, D), F32)),
        scratch_shapes=[pltpu.VMEM((N_CHIP, D, PIECE), BF16), pltpu.VMEM((N_CHIP, D, PIECE), F32),
                        pltpu.SemaphoreType.DMA],
        input_output_aliases={5: 1} if carried else {},
        compiler_params=_params(("arbitrary",)),
    )(dz, x, dxo, vecs, w, *carried)


def _col(ref, k):
    return ref[:, GW * k:GW * (k + 1)]


def _lane_group():
    return lax.broadcasted_iota(jnp.int32, (1, GW), 1) // (GW // 4)


def _pool_window(group):
    return jnp.where(group == 0, 2.0, jnp.where(group == 1, 4.0, jnp.where(group == 2, 8.0, 16.0)))


def _by_group(group, a, b, c, d):
    return jnp.where(group == 0, a, jnp.where(group == 1, b, jnp.where(group == 2, c, d)))


def _layer_norm(x, g, b):
    mu = jnp.mean(x, axis=-1, keepdims=True)
    xc = x - mu
    rstd = lax.rsqrt(jnp.mean(xc * xc, axis=-1, keepdims=True) + EPS)
    xh = xc * rstd
    return xh * g + b, xh, rstd


def _layer_norm_grad(d_out, xh, rstd, g):
    dxh = d_out * g
    return rstd * (dxh - jnp.mean(dxh, axis=-1, keepdims=True) - xh * jnp.mean(dxh * xh, axis=-1, keepdims=True))


def _shifted_copies(ext_ref, sh_ref, n):
    for b in range(1, 8):
        sh_ref[b, 0:n - 8, :] = ext_ref[b:b + n - 8, :]


def _rows_at(ext_ref, sh_ref, start, rows):
    b = start % 8
    if b == 0 or sh_ref is None:
        return ext_ref[start:start + rows, :]
    return sh_ref[b, start - b:start - b + rows, :]


def _conv_taps(ext_ref, w_ref, n_taps, first_row, tm, out_ref, flip=False, sh_ref=None):
    for rb in range(0, tm, CONV_ROWS):
        acc = None
        for k in range(n_taps):
            wk = n_taps - 1 - k if flip else k
            term = w_ref[wk:wk + 1, :] * _rows_at(ext_ref, sh_ref, first_row + rb + k, CONV_ROWS)
            acc = term if acc is None else acc + term
        out_ref[rb:rb + CONV_ROWS, :] = acc


def _mix_forward(zc, zp, first, row0, tm, p, s, after, conv_c=None):
    keep = jnp.where(first, 0.0, 1.0)
    group = _lane_group()
    a_b, a_c, a_x, a_g = _col(zc, 0), _col(zc, 1), _col(zc, 2), _col(zc, 3)
    s['ua'][0:HALO, :] = _col(zp, 1) * _col(zp, 2) * keep
    s['ua'][HALO:HALO + tm, :] = a_c * a_x
    _conv_taps(s['ua'], p['wa'], CONV_A, HALO - (CONV_A - 1), tm, s['cv'])
    cv = s['cv'][...]
    sg_a = _sig(a_g)
    f = dict(a_b=a_b, a_c=a_c, a_x=a_x, a_g=a_g, cv=cv, sg_a=sg_a)
    after(0, f, (a_b * cv * (a_g * sg_a)).astype(BF16))
    b_p, b_g = _col(zc, 4), _col(zc, 5)
    s['p0'][0:HALO, :] = _col(zp, 4) * keep
    s['p0'][HALO:HALO + tm, :] = b_p
    n = HALO + tm
    s['p1'][8:n, :] = s['p0'][8:n, :] + s['p0'][7:n - 1, :]
    w2 = s['p1'][HALO:n, :]
    s['p0'][16:n, :] = s['p1'][16:n, :] + s['p1'][14:n - 2, :]
    w4 = s['p0'][HALO:n, :]
    s['p1'][24:n, :] = s['p0'][24:n, :] + s['p0'][20:n - 4, :]
    w8 = s['p1'][HALO:n, :]
    w16 = w8 + s['p1'][HALO - 8:n - 8, :]
    tpos = (row0 + lax.broadcasted_iota(jnp.int32, (tm, 1), 0) + 1).astype(F32)
    inv_cnt = 1.0 / jnp.minimum(tpos, _pool_window(group))
    pooled = (_by_group(group, w2, w4, w8, w16) * inv_cnt - b_p).astype(BF16)
    q = _dot(pooled, p['wbd'][...])
    sg_b = _sig(b_g)
    f = dict(b_g=b_g, pooled=pooled, q=q, sg_b=sg_b, inv_cnt=inv_cnt)
    after(1, f, (q * p['v256'][0:1, :] * (b_g * sg_b)).astype(BF16))
    c_a, c_gl, c_g = _col(zc, 6), _col(zc, 7), _col(zc, 8)
    sg_gl = _sig(c_gl)
    zp_gl = _col(zp, 7)
    s['hc'][0:HALO, :] = _col(zp, 6) * _sig(zp_gl) * keep
    s['hc'][HALO:HALO + tm, :] = c_a * sg_gl
    _shifted_copies(s['hc'], s['hcs'], HALO + tm)
    if conv_c is None:
        _conv_taps(s['hc'], p['wdw'], CONV_C, HALO - (CONV_C - 1), tm, s['co'], sh_ref=s['hcs'])
        conv_c = s['co']
    conv_v = conv_c[...]
    co = conv_v + p['v256'][1:2, :]
    ln_c, xh_c, rstd_c = _layer_norm(co, p['v256'][2:3, :], p['v256'][3:4, :])
    sg_ln = _sig(ln_c)
    sc = (ln_c * sg_ln).astype(BF16)
    pw = _dot(sc, p['wpw'][...]) + p['v256'][4:5, :]
    sg_c = _sig(c_g)
    f = dict(c_a=c_a, c_g=c_g, sg_gl=sg_gl, ln_c=ln_c, xh_c=xh_c, rstd_c=rstd_c, sg_ln=sg_ln, sc=sc, pw=pw,
             sg_c=sg_c, conv_c=conv_v)
    after(2, f, (pw * (c_g * sg_c)).astype(BF16))
    d_u, d_v, d_g = _col(zc, 9), _col(zc, 10), _col(zc, 11)
    gu, th_u = _gelu(d_u)
    gv, th_v = _gelu(d_v)
    v, xh_d, rstd_d = _layer_norm(gv, p['v256'][5:6, :], p['v256'][6:7, :])
    vb = v.astype(BF16)
    parts = []
    for ch in range(tm // CHUNK):
        vc = vb[ch * CHUNK:(ch + 1) * CHUNK, :]
        parts.append(_by_group(group, *[_dot(p['ws'][h], vc) for h in range(4)]) + p['bsd'][...])
    mx = parts[0] if len(parts) == 1 else jnp.concatenate(parts, axis=0)
    sg_d = _sig(d_g)
    f = dict(d_u=d_u, d_v=d_v, d_g=d_g, gu=gu, th_u=th_u, th_v=th_v, xh_d=xh_d, rstd_d=rstd_d, vb=vb, mx=mx,
             sg_d=sg_d)
    after(3, f, (gu * mx * (d_g * sg_d)).astype(BF16))


def _mix_scratch(tm):
    ext = pltpu.VMEM((HALO + tm, GW), F32)
    tile = pltpu.VMEM((tm, GW), F32)
    return dict(ua=ext, cv=tile, p0=ext, p1=ext, hc=ext, co=tile, hcs=pltpu.VMEM((8, HALO + tm, GW), F32))


_MIX_PARAMS = ('wa', 'wdw', 'v256', 'wbd', 'wpw', 'ws', 'bsd', 'wout')


def _mix_param_specs(l):
    def whole(*shape, at=l):
        nd = len(shape)
        return pl.BlockSpec((None,) + shape, lambda i, _nd=nd: (at,) + (0,) * _nd)
    return [whole(8, GW), whole(32, GW), whole(16, GW), whole(GW, GW), whole(GW, GW), whole(4, CHUNK, CHUNK),
            whole(CHUNK, GW), whole(N_CHIP, GW, D, at=0)]


def _mix_param_arrays(mp, w_out_l):
    return [mp[k] for k in _MIX_PARAMS[:-1]] + [w_out_l]


def _fwd_layer(x, vecs, w, w_out_l, mp, l, tm):
    t = x.shape[0]
    nt = t // tm
    names = list(_mix_scratch(tm))
    n_p = len(_MIX_PARAMS)

    def body(xm_ref, xr_ref, v_ref, w_ref, *rest):
        p = dict(zip(_MIX_PARAMS, rest[:n_p]))
        z_ref, xn_ref, y_ref, conv_ref = rest[n_p:n_p + 4]
        s = dict(zip(names, rest[n_p + 4:n_p + 4 + len(names)]))
        z_next, z_cur, z_halo = rest[n_p + 4 + len(names):]
        j = pl.program_id(0)

        @pl.when(j == 0)
        def _():
            z_cur[...] = jnp.zeros_like(z_cur)
            z_halo[...] = jnp.zeros_like(z_halo)

        vecs_v = v_ref[...]
        _, _, hn = _normed(xm_ref[...], vecs_v)
        hb = (hn * (1.0 + vecs_v[1:2, :]) + vecs_v[0:1, :]).astype(BF16)
        for k in range(N_CHIP):
            part = _dot(hb, w_ref[k])
            z_ref[:, PIECE * k:PIECE * (k + 1)] = part
            z_next[:, PIECE * k:PIECE * (k + 1)] = part

        tile = jnp.maximum(j - 1, 0)
        mixed = []
        _mix_forward(z_cur, z_halo, tile == 0, tile * tm, tm, p, s, lambda k, f, yk: mixed.append((f, yk)))
        y = None
        for k, (_, yk) in enumerate(mixed):
            part = _dot(yk, p['wout'][k])
            y = part if y is None else y + part
        y_ref[...] = y
        xn_ref[...] = xr_ref[...] + vecs_v[2:3, :] * y
        conv_ref[...] = mixed[2][0]['conv_c']

        z_halo[...] = z_cur[tm - HALO:tm, :]
        z_cur[...] = z_next[...]

    def ahead(j):
        return jnp.minimum(j, nt - 1)

    def behind(j):
        return jnp.maximum(j - 1, 0)

    return pl.pallas_call(
        body, name=f"fwd_layer_{l}", grid=(nt + 1,),
        in_specs=[pl.BlockSpec((tm, D), lambda j: (ahead(j), 0)), pl.BlockSpec((tm, D), lambda j: (behind(j), 0)),
                  pl.BlockSpec((8, D), lambda j: (0, 0)),
                  pl.BlockSpec((None, N_CHIP, D, PIECE), lambda j: (0, 0, 0, 0))] + _mix_param_specs(l),
        out_specs=(pl.BlockSpec((tm, DIN), lambda j: (ahead(j), 0)), pl.BlockSpec((tm, D), lambda j: (behind(j), 0)),
                   pl.BlockSpec((tm, D), lambda j: (behind(j), 0)), pl.BlockSpec((tm, GW), lambda j: (behind(j), 0))),
        out_shape=(jax.ShapeDtypeStruct((t, DIN), F32), jax.ShapeDtypeStruct((t, D), F32),
                   jax.ShapeDtypeStruct((t, D), F32), jax.ShapeDtypeStruct((t, GW), F32)),
        scratch_shapes=list(_mix_scratch(tm).values())
        + [pltpu.VMEM((tm, DIN), F32), pltpu.VMEM((tm, DIN), F32), pltpu.VMEM((HALO, DIN), F32)],
        compiler_params=_params(("arbitrary",)),
    )(x, x, vecs, w, *_mix_param_arrays(mp, w_out_l))


def _bwd_mix(dxo, y, conv, z, vecs, mp, w_out_l, wst, l, tm, gwout_all=None):
    t = dxo.shape[0]
    nt = t // tm
    nl = wst.shape[0]
    carried = [] if gwout_all is None else [gwout_all]
    per_halo = tm // HALO
    fwd_names = list(_mix_scratch(tm))
    ext = pltpu.VMEM((tm + HALO, GW), F32)
    carry = pltpu.VMEM((HALO, GW), F32)
    tile = pltpu.VMEM((tm, GW), F32)
    bwd_scratch = dict(ea=ext, eb=ext, eb2=ext, ec=ext, ca=carry, cb=carry, cc=carry, dua=tile, dhc=tile,
                       gbacc=pltpu.VMEM((CHUNK, GW), F32), ecs=pltpu.VMEM((8, tm + HALO, GW), F32),
                       dys=pltpu.VMEM((1, tm, GW), F32), zbuf=pltpu.VMEM((3, tm, DIN), F32),
                       zsem=pltpu.SemaphoreType.DMA((3,)))
    bwd_names = list(bwd_scratch)
    n_p = len(_MIX_PARAMS)

    def body(dxo_ref, y_ref, conv_ref, z_any, zp, v_ref, *rest):
        p = dict(zip(_MIX_PARAMS, rest[:n_p]))
        wst_ref = rest[n_p]
        first_out = n_p + 1 + len(carried)
        dz_ref, gwout, gwbd, gwpw, gws, gbs, gwdw, g256, g1024 = rest[first_out:first_out + 9]
        s = dict(zip(fwd_names + bwd_names, rest[first_out + 9:]))
        i = pl.program_id(0)
        ti = nt - 1 - i
        group = _lane_group()

        def z_copy(step):
            rows = pl.ds(pl.multiple_of((nt - 1 - step) * tm, tm), tm)
            return pltpu.make_async_copy(z_any.at[rows, :], s['zbuf'].at[step % 3], s['zsem'].at[step % 3])

        @pl.when(i == 0)
        def _():
            z_copy(0).start()
            if nt > 1:
                z_copy(1).start()
            for ref in (gwout, gwbd, gwpw, gws, gbs, gwdw, g256, g1024, s['ca'], s['cb'], s['cc'], s['gbacc']):
                ref[...] = jnp.zeros_like(ref)

        @pl.when(i + 2 < nt)
        def _():
            z_copy(i + 2).start()

        z_copy(i).wait()
        zc = s['zbuf'].at[i % 3]

        def put(k, val):
            dz_ref[:, GW * k:GW * (k + 1)] = val.astype(BF16)

        def bwd_a(f, dya):
            taps = [None] * CONV_A
            for r in range(0, tm, ROW_BLOCK):
                rs = slice(r, r + ROW_BLOCK)
                a_b, a_g = zc[rs, 0:GW], zc[rs, 3 * GW:4 * GW]
                dy_a, cv = s['dys'][0, rs, :], s['cv'][rs, :]
                sg = _sig(a_g)
                silu = a_g * sg
                t = dy_a * cv
                dz_ref[rs, 0:GW] = (t * silu).astype(BF16)
                dz_ref[rs, 3 * GW:4 * GW] = (t * a_b * (sg * (1.0 + a_g * (1.0 - sg)))).astype(BF16)
                d_cv = dy_a * a_b * silu
                s['ea'][rs, :] = d_cv
                for k in range(CONV_A):
                    first = HALO - (CONV_A - 1) + k + r
                    term = d_cv * s['ua'][first:first + ROW_BLOCK, :]
                    taps[k] = term if taps[k] is None else taps[k] + term
            for k in range(CONV_A):
                g256[8 + k:9 + k, :] += _rowsum(taps[k])
            s['ea'][tm:tm + HALO, :] = s['ca'][...]
            s['ca'][...] = s['ea'][0:HALO, :]
            _conv_taps(s['ea'], p['wa'], CONV_A, 0, tm, s['dua'], flip=True)
            for r in range(0, tm, ROW_BLOCK):
                rs = slice(r, r + ROW_BLOCK)
                d_u = s['dua'][rs, :]
                dz_ref[rs, GW:2 * GW] = (d_u * zc[rs, 2 * GW:3 * GW]).astype(BF16)
                dz_ref[rs, 2 * GW:3 * GW] = (d_u * zc[rs, GW:2 * GW]).astype(BF16)

        def bwd_b(f, dyb):
            b_g, q, sg_b, inv_cnt = f['b_g'], f['q'], f['sg_b'], f['inv_cnt']
            scale_b = p['v256'][0:1, :]
            silu_b = b_g * sg_b
            g256[0:1, :] += _rowsum(dyb * q * silu_b)
            put(5, dyb * q * scale_b * (sg_b * (1.0 + b_g * (1.0 - sg_b))))
            d_q = (dyb * scale_b * silu_b).astype(BF16)
            gwbd[...] += _dot_tn(f['pooled'], d_q)
            d_pooled = _dot_nt(d_q, p['wbd'][...])
            gp = d_pooled * inv_cnt
            n = tm + HALO
            s['eb'][0:tm, :] = gp
            s['eb'][tm:n, :] = s['cb'][...]
            s['cb'][...] = gp[0:HALO, :]
            s['eb2'][0:n - 8, :] = s['eb'][0:n - 8, :] + s['eb'][1:n - 7, :]
            r2 = s['eb2'][0:tm, :]
            s['eb'][0:n - 16, :] = s['eb2'][0:n - 16, :] + s['eb2'][2:n - 14, :]
            r4 = s['eb'][0:tm, :]
            s['eb2'][0:n - 24, :] = s['eb'][0:n - 24, :] + s['eb'][4:n - 20, :]
            r8 = s['eb2'][0:tm, :]
            r16 = r8 + s['eb2'][8:tm + 8, :]
            put(4, _by_group(group, r2, r4, r8, r16) - d_pooled)

        def bwd_c(f, dyc):
            c_a, c_g, sg_gl, ln_c, xh_c, rstd_c, sg_ln, pw, sg_c = (
                f['c_a'], f['c_g'], f['sg_gl'], f['ln_c'], f['xh_c'], f['rstd_c'], f['sg_ln'], f['pw'], f['sg_c'])
            put(8, dyc * pw * (sg_c * (1.0 + c_g * (1.0 - sg_c))))
            d_pw = dyc * (c_g * sg_c)
            g256[4:5, :] += _rowsum(d_pw)
            d_pwb = d_pw.astype(BF16)
            gwpw[...] += _dot_tn(f['sc'], d_pwb)
            d_ln = _dot_nt(d_pwb, p['wpw'][...]) * (sg_ln * (1.0 + ln_c * (1.0 - sg_ln)))
            g256[2:3, :] += _rowsum(d_ln * xh_c)
            g256[3:4, :] += _rowsum(d_ln)
            d_co = _layer_norm_grad(d_ln, xh_c, rstd_c, p['v256'][2:3, :])
            g256[1:2, :] += _rowsum(d_co)
            for k in range(CONV_C):
                gwdw[k:k + 1, :] += _rowsum(d_co * _rows_at(s['hc'], s['hcs'], HALO - (CONV_C - 1) + k, tm))
            s['ec'][0:tm, :] = d_co
            s['ec'][tm:tm + HALO, :] = s['cc'][...]
            s['cc'][...] = d_co[0:HALO, :]
            _shifted_copies(s['ec'], s['ecs'], tm + HALO)
            _conv_taps(s['ec'], p['wdw'], CONV_C, 0, tm, s['dhc'], flip=True, sh_ref=s['ecs'])
            d_hc = s['dhc'][...]
            put(6, d_hc * sg_gl)
            put(7, d_hc * c_a * sg_gl * (1.0 - sg_gl))

        def bwd_d(f, dyd):
            d_uu, d_vv, d_g, gu, th_u, th_v, xh_d, rstd_d, vb, mx, sg_d = (
                f['d_u'], f['d_v'], f['d_g'], f['gu'], f['th_u'], f['th_v'], f['xh_d'], f['rstd_d'], f['vb'],
                f['mx'], f['sg_d'])
            silu_d = d_g * sg_d
            put(9, dyd * mx * silu_d * _gelu_grad(d_uu, th_u))
            put(11, dyd * gu * mx * (sg_d * (1.0 + d_g * (1.0 - sg_d))))
            d_mx = dyd * gu * silu_d
            dv_parts = []
            gb = None
            for ch in range(tm // CHUNK):
                dmc = d_mx[ch * CHUNK:(ch + 1) * CHUNK, :]
                gb = dmc if gb is None else gb + dmc
                dmb = dmc.astype(BF16)
                vc = vb[ch * CHUNK:(ch + 1) * CHUNK, :]
                for h in range(4):
                    gws[h] += _dot_nt(jnp.where(group == h, dmb, jnp.zeros_like(dmb)), vc)
                dv_parts.append(_by_group(group, *[_dot(wst_ref[h], dmb) for h in range(4)]))
            s['gbacc'][...] += gb
            d_v = dv_parts[0] if len(dv_parts) == 1 else jnp.concatenate(dv_parts, axis=0)
            g256[5:6, :] += _rowsum(d_v * xh_d)
            g256[6:7, :] += _rowsum(d_v)
            d_gv = _layer_norm_grad(d_v, xh_d, rstd_d, p['v256'][5:6, :])
            put(10, d_gv * _gelu_grad(d_vv, th_v))

        mixed = []
        _mix_forward(zc, zp, ti == 0, ti * tm, tm, p, s, lambda j, f, yj: mixed.append((f, yj)), conv_c=conv_ref)
        dxo_v = dxo_ref[...]
        g1024[0:1, :] += _rowsum(dxo_v * y_ref[...])
        dy = (dxo_v * v_ref[2:3, :]).astype(BF16)
        s['dys'][0] = _dot_nt(dy, p['wout'][0])
        dys = [None] + [_dot_nt(dy, p['wout'][j]) for j in range(1, 4)]
        for j, (f, yj) in enumerate(mixed):
            gwout[j] += _dot_tn(yj, dy)
        for (f, _), backward, dyj in zip(mixed, (bwd_a, bwd_b, bwd_c, bwd_d), dys):
            backward(f, dyj)

        @pl.when(i == nt - 1)
        def _():
            row = lax.broadcasted_iota(jnp.int32, (CHUNK, CHUNK), 0)
            col = lax.broadcasted_iota(jnp.int32, (CHUNK, CHUNK), 1)
            for h in range(4):
                gws[h] = jnp.where(col <= row, gws[h], 0.0)
            head_of_lane = lax.broadcasted_iota(jnp.int32, (GW, CHUNK), 0) // (GW // 4)
            sel = jnp.where(head_of_lane == lax.broadcasted_iota(jnp.int32, (GW, CHUNK), 1), 1.0, 0.0)
            gbs[...] = jnp.dot(s['gbacc'][...], sel, preferred_element_type=F32, precision=lax.Precision.HIGHEST)

    def const(*shape):
        nd = len(shape)
        return pl.BlockSpec(shape, lambda i, _nd=nd: (0,) * _nd)

    def rev(i):
        return nt - 1 - i

    out_shapes = (jax.ShapeDtypeStruct((t, DIN), BF16), jax.ShapeDtypeStruct((nl, N_CHIP, GW, D), F32),
                  jax.ShapeDtypeStruct((GW, GW), F32), jax.ShapeDtypeStruct((GW, GW), F32),
                  jax.ShapeDtypeStruct((4, CHUNK, CHUNK), F32), jax.ShapeDtypeStruct((CHUNK, CHUNK), F32),
                  jax.ShapeDtypeStruct((32, GW), F32), jax.ShapeDtypeStruct((16, GW), F32),
                  jax.ShapeDtypeStruct((8, D), F32))
    out_specs = (pl.BlockSpec((tm, DIN), lambda i: (rev(i), 0)),
                 pl.BlockSpec((None, N_CHIP, GW, D), lambda i: (l, 0, 0, 0)), const(GW, GW),
                 const(GW, GW), const(4, CHUNK, CHUNK), const(CHUNK, CHUNK), const(32, GW), const(16, GW),
                 const(8, D))
    scratch = list(_mix_scratch(tm).values()) + list(bwd_scratch.values())
    n_in = 6 + n_p + 1
    return pl.pallas_call(
        body, name=f"bwd_mix_{l}", grid=(nt,),
        in_specs=[pl.BlockSpec((tm, D), lambda i: (rev(i), 0)), pl.BlockSpec((tm, D), lambda i: (rev(i), 0)),
                  pl.BlockSpec((tm, GW), lambda i: (rev(i), 0)), pl.BlockSpec(memory_space=pl.ANY),
                  pl.BlockSpec((HALO, DIN), lambda i: (jnp.maximum(rev(i) * per_halo - 1, 0), 0)),
                  pl.BlockSpec((8, D), lambda i: (0, 0))]
        + _mix_param_specs(l)
        + [pl.BlockSpec((None, 4, CHUNK, CHUNK), lambda i: (l, 0, 0, 0))]
        + [pl.BlockSpec(memory_space=pl.ANY)] * len(carried),
        out_specs=out_specs, out_shape=out_shapes, scratch_shapes=scratch,
        input_output_aliases={n_in: 1} if carried else {},
        compiler_params=_params(("arbitrary",)),
    )(dxo, y, conv, z, z, vecs, *_mix_param_arrays(mp, w_out_l), wst, *carried)


def _final(x, target, fg, tt):
    t = x.shape[0]
    nt = t // tt

    def body(x_ref, t_ref, g_ref, dx_ref, sums_ref):
        i = pl.program_id(0)

        @pl.when(i == 0)
        def _():
            sums_ref[...] = jnp.zeros_like(sums_ref)

        xv = x_ref[...]
        g = g_ref[0:1, :]
        r = lax.rsqrt(jnp.mean(xv * xv, axis=-1, keepdims=True) + EPS)
        xhat = xv * r
        err = xhat * g - t_ref[...]
        sums_ref[1:2, :] += _rowsum(err * err) * (0.5 / D)
        dyv = err * (1.0 / D)
        sums_ref[0:1, :] += _rowsum(dyv * xhat)
        dxh = dyv * g
        dx_ref[...] = r * (dxh - xhat * jnp.mean(dxh * xhat, axis=-1, keepdims=True))

        @pl.when(i == nt - 1)
        def _():
            sums_ref[2:3, :] = jnp.broadcast_to(jnp.sum(sums_ref[1:2, :], axis=-1, keepdims=True), (1, D))

    return pl.pallas_call(
        body, name="final_loss", grid=(nt,),
        in_specs=[pl.BlockSpec((tt, D), lambda i: (i, 0)), pl.BlockSpec((tt, D), lambda i: (i, 0)),
                  pl.BlockSpec((8, D), lambda i: (0, 0))],
        out_specs=(pl.BlockSpec((tt, D), lambda i: (i, 0)), pl.BlockSpec((8, D), lambda i: (0, 0))),
        out_shape=(jax.ShapeDtypeStruct((t, D), F32), jax.ShapeDtypeStruct((8, D), F32)),
        compiler_params=_params(("arbitrary",)),
    )(x, target, fg)


def _adamw(w, g, m, v, tag):
    rows, cols = w.shape
    rt = rows
    for cand in (512, 256, 128, 64, 32, 16, 8):
        if rows % cand == 0:
            rt = cand
            break

    def body(w_ref, g_ref, m_ref, v_ref, d_ref, nm_ref, nv_ref):
        d_ref[...], nm_ref[...], nv_ref[...] = _adamw_update(g_ref[...], w_ref[...], m_ref[...], v_ref[...])

    spec = pl.BlockSpec((rt, cols), lambda i: (i, 0))
    return pl.pallas_call(
        body, name=f"adamw_{tag}", grid=(rows // rt,), in_specs=[spec] * 4, out_specs=(spec,) * 3,
        out_shape=(jax.ShapeDtypeStruct(w.shape, F32),) * 3, compiler_params=_params(("parallel",)),
    )(w, g, m, v)


def _adamw_update(gv, w, m, v):
    nm = ADAM_B1 * m + (1.0 - ADAM_B1) * gv
    nv = ADAM_B2 * v + (1.0 - ADAM_B2) * (gv * gv)
    m_hat = nm / (1.0 - ADAM_B1 ** ADAM_STEP)
    v_hat = nv / (1.0 - ADAM_B2 ** ADAM_STEP)
    return -ADAM_LR * (m_hat / (jnp.sqrt(v_hat) + ADAM_EPS) + ADAM_WD * w), nm, nv


def _adamw_many(ws, gs, ms, vs):
    n = len(ws)

    def body(*refs):
        w_refs, g_refs, m_refs, v_refs = (refs[k * n:(k + 1) * n] for k in range(4))
        d_refs, nm_refs, nv_refs = (refs[(4 + k) * n:(5 + k) * n] for k in range(3))
        for k in range(n):
            d_refs[k][...], nm_refs[k][...], nv_refs[k][...] = _adamw_update(
                g_refs[k][...], w_refs[k][...], m_refs[k][...], v_refs[k][...])

    shapes = tuple(jax.ShapeDtypeStruct(w.shape, F32) for w in ws)
    out = pl.pallas_call(body, name="adamw_small", out_shape=shapes * 3)(*ws, *gs, *ms, *vs)
    return out[:n], out[n:2 * n], out[2 * n:]


def _adamw_halves(w, mine, theirs, m, v, cidx, tag):
    nl, r, cdim = w.shape
    r2 = r // 2
    rt = min(r2, 512)
    nrt = r2 // rt

    def body(c_ref, a_ref, b_ref, w_ref, m_ref, v_ref, g_ref, d_ref, nm_ref, nv_ref):
        gv = jnp.where(pl.program_id(1) == c_ref[0], a_ref[...], b_ref[...])
        g_ref[...] = gv
        d_ref[...], nm_ref[...], nv_ref[...] = _adamw_update(gv, w_ref[...], m_ref[...], v_ref[...])

    half = pl.BlockSpec((None, rt, cdim), lambda l, h, i, c: (l, i, 0))
    whole = pl.BlockSpec((None, rt, cdim), lambda l, h, i, c: (l, h * nrt + i, 0))
    grid_spec = pltpu.PrefetchScalarGridSpec(
        num_scalar_prefetch=1, grid=(nl, 2, nrt), in_specs=[half, half, whole, whole, whole],
        out_specs=(whole,) * 4)
    return pl.pallas_call(
        body, name=f"adamw_{tag}", grid_spec=grid_spec,
        out_shape=(jax.ShapeDtypeStruct(w.shape, F32),) * 4,
        compiler_params=_params(("parallel", "parallel", "parallel")),
    )(cidx, mine, theirs, w, m, v)


def _pack(arrays, row_multiple=8):
    parts, offsets, off = [], [], 0
    for a in arrays:
        n = int(np.prod(a.shape))
        padded = -(-n // 1024) * 1024
        flat = a.reshape(-1).astype(F32)
        if padded != n:
            flat = jnp.concatenate([flat, jnp.zeros((padded - n,), F32)])
        parts.append(flat.reshape(-1, 128))
        offsets.append((off // 128, n, a.shape))
        off += padded
    tail = -off % (row_multiple * 128)
    if tail:
        parts.append(jnp.zeros((tail // 128, 128), F32))
    return jnp.concatenate(parts, axis=0), offsets


def _unpack(buf, offsets):
    return [buf[row:row + -(-n // 128)].reshape(-1)[:n].reshape(shape) for row, n, shape in offsets]


def _pad_rows(a, rows):
    return jnp.concatenate([a, jnp.zeros((rows - a.shape[0],) + a.shape[1:], a.dtype)], axis=0)


def kernel(x, c, norm_g, w_ada, b_ada, w_in, w_conv_a, w_pool, pool_scale, w_dw_c, b_dw_c, ln_g_c, ln_b_c, w_pw2_c, b_pw2_c, ln_g_d, ln_b_d, w_s_d, b_s_d, w_out, final_g, loss_target, m_norm_g, m_w_ada, m_b_ada, m_w_in, m_w_conv_a, m_w_pool, m_pool_scale, m_w_dw_c, m_b_dw_c, m_ln_g_c, m_ln_b_c, m_w_pw2_c, m_b_pw2_c, m_ln_g_d, m_ln_b_d, m_w_s_d, m_b_s_d, m_w_out, m_final_g, v_norm_g, v_w_ada, v_b_ada, v_w_in, v_w_conv_a, v_w_pool, v_pool_scale, v_w_dw_c, v_b_dw_c, v_ln_g_c, v_ln_b_c, v_w_pw2_c, v_b_pw2_c, v_ln_g_d, v_ln_b_d, v_w_s_d, v_b_s_d, v_w_out, v_final_g):
    env = dict(locals())
    weights = {n: env[n] for n in WEIGHTS}
    mom_m = {n: env["m_" + n] for n in WEIGHTS}
    mom_v = {n: env["v_" + n] for n in WEIGHTS}
    nl = norm_g.shape[0]
    t = x.shape[1]
    tt, tm = _tiles(t)
    ax, ay, ac = lax.axis_index("x"), lax.axis_index("y"), lax.axis_index("c")
    chip = 2 * ax + ay
    dev = 2 * chip + ac
    cidx = jnp.reshape(ac, (1,)).astype(jnp.int32)
    sidx = jnp.reshape(chip, (1,)).astype(jnp.int32)
    xs, target = x[0], loss_target[0]

    shard_small, shard_off = _pack([c, w_conv_a, w_dw_c, w_pw2_c])
    w_in_0, w_out_0, gathered = _gather_weights(w_in[:1], w_out[:1], shard_small)
    w_in_layers, w_out_layers = [w_in_0], [w_out_0]
    gathered = gathered.reshape(N_DEV, -1, 128)
    per_dev = [_unpack(gathered[d], shard_off) for d in range(0, N_DEV, 2)]
    c_all = jnp.concatenate([u[0] for d in range(N_DEV)
                             for u in [_unpack(gathered[d], shard_off[:1])]], axis=0)
    w_conv_full = jnp.concatenate([u[1] for u in per_dev], axis=-1)
    w_dw_full = jnp.concatenate([u[2] for u in per_dev], axis=-1)
    w_pw2_full = jnp.concatenate([u[3] for u in per_dev], axis=1)

    b_ada_s = lax.dynamic_slice_in_dim(b_ada, chip * PIECE, PIECE, axis=1)[:, None, :]
    mod_s = _ada_forward(c_all, w_ada, b_ada_s)
    mod_all = _allgather8(mod_s.reshape(nl * N_DEV, PIECE), "gather_mod").reshape(N_DEV, nl, N_DEV, PIECE)
    mod_rows = lax.dynamic_index_in_dim(mod_all, dev, axis=2, keepdims=False)
    mod = jnp.concatenate([mod_rows[2 * s] for s in range(N_CHIP)], axis=-1)

    if nl > 1:
        w_in_b, w_out_b = _cast_bf16([w_in[1:], w_out[1:]], after=w_out_layers[0])

    tril = jnp.tril(jnp.ones((CHUNK, CHUNK), bool))
    ws_masked = jnp.where(tril[None, None], w_s_d, 0.0)
    wbd = jnp.zeros((nl, GW, GW), F32)
    for g in range(4):
        wbd = wbd.at[:, 64 * g:64 * (g + 1), 64 * g:64 * (g + 1)].set(w_pool[:, g])
    v256 = jnp.stack([pool_scale, b_dw_c, ln_g_c, ln_b_c, b_pw2_c, ln_g_d, ln_b_d], axis=1)
    mp = dict(
        wa=_pad_rows(jnp.swapaxes(w_conv_full, 0, 1), 8).swapaxes(0, 1),
        wdw=_pad_rows(jnp.swapaxes(w_dw_full, 0, 1), 32).swapaxes(0, 1),
        v256=_pad_rows(jnp.swapaxes(v256, 0, 1), 16).swapaxes(0, 1),
        wbd=wbd.astype(BF16), wpw=w_pw2_full.astype(BF16), ws=ws_masked.astype(BF16),
        bsd=jnp.repeat(jnp.swapaxes(b_s_d, 1, 2), GW // 4, axis=2))
    wst = jnp.swapaxes(ws_masked, 2, 3).astype(BF16)

    def vec_rows(l):
        rows = jnp.stack([mod[l, 0:D], mod[l, D:2 * D], mod[l, 2 * D:3 * D], norm_g[l]], axis=0)
        return _pad_rows(rows, 8)

    xin, zs, ys, convs, vecs = [xs], [], [], [], []
    zero = jnp.zeros((), jnp.int32)
    for l in range(nl):
        vecs.append(vec_rows(l))
        started, token = None, 0.0
        if l + 1 < nl:
            started = _gather_shards_start(w_in_b[l:l + 1], w_out_b[l:l + 1])
            token = started[-1][0, 0]
        zl, xn, yl, cl = _fwd_layer(xin[l], vecs[l] + token, w_in_layers[l], w_out_layers[l], mp, l, tm)
        zs.append(zl)
        convs.append(cl)
        xin.append(xn)
        ys.append(yl)
        if started is not None:
            own_in, own_out, land_in, land_out = _gather_shards_wait(*started[:6], cl)
            w_in_layers.append(lax.dynamic_update_slice(land_in, own_in[:, None], (zero, chip, zero, zero)))
            w_out_layers.append(lax.dynamic_update_slice(land_out, own_out[:, None], (zero, chip, zero, zero)))
    dx, fin_sums = _final(xin[nl], target, _pad_rows(final_g[None, :], 8), tt)

    g_in, g_out, small, dmod = None, None, [None] * nl, [None] * nl
    for l in reversed(range(nl)):
        dz, g_out, gwbd, gwpw, gws, gbs, gwdw, g256, g1024 = _bwd_mix(
            dx, ys[l], convs[l], zs[l], vecs[l], mp, w_out_layers[l], wst, l, tm, gwout_all=g_out)
        dx, g_in, sums = _bwd_in(dz, xin[l], dx, vecs[l], w_in_layers[l], l, nl, tt, gw_all=g_in)
        dmod[l] = jnp.concatenate([sums[0], sums[1], g1024[0]])
        small[l] = dict(
            norm_g=sums[2], w_conv_a=g256[8:8 + CONV_A],
            w_pool=jnp.stack([gwbd[64 * g:64 * (g + 1), 64 * g:64 * (g + 1)] for g in range(4)]),
            pool_scale=g256[0], w_dw_c=gwdw[:CONV_C], b_dw_c=g256[1], ln_g_c=g256[2], ln_b_c=g256[3],
            w_pw2_c=gwpw, b_pw2_c=g256[4], ln_g_d=g256[5], ln_b_d=g256[6], w_s_d=gws, b_s_d=gbs[:, :4].T)
    grad_x = dx[None]

    rs = _reduce_scatter_start([g_in, g_out], ("in", "out"), cidx)
    token = rs[0][1][-1][0, 0] + rs[1][1][-1][0, 0]

    dmod_all, dmod_sum = _allgather8(
        (jnp.stack(dmod) + token).reshape(nl * 3 * D // 128, 128), "gather_dmod", reduce=True)
    dmod_all = dmod_all.reshape(N_DEV, nl, 3 * D)
    grad_b_ada = dmod_sum.reshape(nl, 3 * D)
    dmod_s = jnp.swapaxes(lax.dynamic_slice_in_dim(dmod_all, chip * PIECE, PIECE, axis=2), 0, 1)
    grad_w_ada = _ada_backward(c_all, dmod_s)

    small_names = [n for n in WEIGHTS if n not in BIG and n not in ('b_ada', 'final_g')]
    stacked = [jnp.stack([small[l][n] for l in range(nl)]) for n in small_names]
    small_buf, small_off = _pack(stacked + [fin_sums[0], fin_sums[2, 0:1]], 8 * N_DEV)
    reduced = _unpack(_allreduce8(small_buf, "allreduce_small"), small_off)
    grads = dict(zip(small_names, reduced[:len(small_names)]))
    grads['final_g'] = reduced[-2]
    loss = reduced[-1][0]
    grads['b_ada'] = grad_b_ada
    grads['w_ada'] = grad_w_ada
    grads['w_conv_a'] = lax.dynamic_slice_in_dim(grads['w_conv_a'], chip * 64, 64, axis=2)
    grads['w_dw_c'] = lax.dynamic_slice_in_dim(grads['w_dw_c'], chip * 64, 64, axis=2)
    grads['w_pw2_c'] = lax.dynamic_slice_in_dim(grads['w_pw2_c'], chip * 64, 64, axis=1)

    delta, new_m, new_v = {}, {}, {}
    shape = w_ada.shape
    as2d = lambda a: a.reshape(-1, shape[-1])
    d2, m2, v2 = _adamw(as2d(w_ada), as2d(grads['w_ada']), as2d(m_w_ada), as2d(v_w_ada), 'w_ada')
    delta['w_ada'], new_m['w_ada'], new_v['w_ada'] = d2.reshape(shape), m2.reshape(shape), v2.reshape(shape)
    rest = [n for n in WEIGHTS if n not in BIG]
    ds, nms, nvs = _adamw_many([weights[n] for n in rest], [grads[n] for n in rest],
                               [mom_m[n] for n in rest], [mom_v[n] for n in rest])
    for n, dn, mn, vn in zip(rest, ds, nms, nvs):
        delta[n], new_m[n], new_v[n] = dn, mn, vn
    halves = dict(zip(('w_in', 'w_out'), _reduce_scatter_finish(rs, ("in", "out"), nvs[-1], sidx)))
    for n in ('w_in', 'w_out'):
        grads[n], delta[n], new_m[n], new_v[n] = _adamw_halves(
            weights[n], *halves[n], mom_m[n], mom_v[n], cidx, n)

    return (loss, grad_x, *[grads[n] for n in WEIGHTS], *[delta[n] for n in WEIGHTS],
            *[new_m[n] for n in WEIGHTS], *[new_v[n] for n in WEIGHTS])
```
